```python
import jax, jax.numpy as jnp
from jax import lax
import numpy as np

D_MODEL = 1024
BATCH = 8
SEQ = 2048
DEPTH = 1

GRID_W = 64
CTX_LEN = 256
M_HEADS = 8
M_HEAD_DIM = D_MODEL // M_HEADS
M_WIDTH = M_HEADS * M_HEAD_DIM
CHUNK = 128
SC_WIDTH = D_MODEL
FF_HIDDEN = ((8 * D_MODEL // 3 + 127) // 128) * 128
EPS = 1e-6
M_INIT = -1e30
N_GATES = 4 * M_HEADS

OFF_Q = 0
OFF_K = M_WIDTH
OFF_V = 2 * M_WIDTH
OFF_G = 3 * M_WIDTH
OFF_O = OFF_G + N_GATES
OFF_SB = OFF_O + M_WIDTH
OFF_SC = OFF_SB + SC_WIDTH
OFF_SX = OFF_SC + SC_WIDTH
OFF_MG = OFF_SX + SC_WIDTH
IN_COLS = OFF_MG + 2 * D_MODEL
CTX_COLS = OFF_O

kernel_name = "hybrid_mlstm_shortconv_convffn_prefix_block"


def _rmsnorm(x, g):
    x32 = x.astype(jnp.float32)
    y = x32 * lax.rsqrt(jnp.mean(x32 * x32, axis=-1, keepdims=True) + EPS) * g.astype(jnp.float32)
    return y.astype(x.dtype)


def _modulate(x, g, shift, scale):
    return _rmsnorm(x, g) * (1 + scale) + shift


def _conv1d_w3(x, w, b):
    xp = jnp.pad(x, ((0, 0), (1, 1), (0, 0)))
    return xp[:, :-2] * w[0] + xp[:, 1:-1] * w[1] + xp[:, 2:] * w[2] + b


def _conv2d_3x3(x, w, b, rows):
    bsz, t, ch = x.shape
    xg = x.reshape(bsz, rows, GRID_W, ch)
    y = lax.conv_general_dilated(xg, w[:, :, None, :], (1, 1), 'SAME',
                                 dimension_numbers=('NHWC', 'HWIO', 'NHWC'),
                                 feature_group_count=ch)
    return y.reshape(bsz, t, ch) + b


def _mlstm_chunk_scan(q, k, v, ig, lf, state, emit):
    bsz, nh, t, dh = q.shape
    nc = t // CHUNK

    def chunks(a):
        return jnp.moveaxis(a.reshape(bsz, nh, nc, CHUNK, *a.shape[3:]), 2, 0)

    lower_tri = jnp.tril(jnp.ones((CHUNK, CHUNK), dtype=bool))

    def step(carry, inp):
        c_mat, n_vec, m = carry
        qc, kc, vc, ic, fc = inp
        b = jnp.cumsum(fc, axis=-1)
        dmat = b[..., :, None] - b[..., None, :] + ic[..., None, :]
        dmat = jnp.where(lower_tri, dmat, -jnp.inf)
        inter = b + m[..., None]
        m_t = jnp.maximum(inter, jnp.max(dmat, axis=-1))
        w = jnp.exp(dmat - m_t[..., None])
        a_inter = jnp.exp(inter - m_t)
        s = jnp.einsum('bhtd,bhsd->bhts', qc, kc) * w
        num = (a_inter[..., None] * jnp.einsum('bhed,bhtd->bhte', c_mat, qc)
               + jnp.einsum('bhts,bhse->bhte', s, vc))
        den = a_inter * jnp.einsum('bhd,bhtd->bht', n_vec, qc) + jnp.sum(s, axis=-1)
        h = num / jnp.maximum(jnp.abs(den), jnp.exp(-m_t))[..., None]
        b_end = b[..., -1]
        g = b_end[..., None] - b + ic
        m_new = jnp.maximum(b_end + m, jnp.max(g, axis=-1))
        decay = jnp.exp(b_end + m - m_new)
        wk = jnp.exp(g - m_new[..., None])
        c_new = decay[..., None, None] * c_mat + jnp.einsum('bhs,bhse,bhsd->bhed', wk, vc, kc)
        n_new = decay[..., None] * n_vec + jnp.einsum('bhs,bhsd->bhd', wk, kc)
        return (c_new, n_new, m_new), (h if emit else None)

    final, hs = lax.scan(step, state, tuple(chunks(a) for a in (q, k, v, ig, lf)))
    if not emit:
        return final, None
    return final, jnp.moveaxis(hs, 0, 2).reshape(bsz, nh, t, dh)


def _mlstm_prep(p, qk_w, qk_b, g_b):
    bsz, t, _ = p.shape
    qk = jax.nn.silu(_conv1d_w3(p[..., OFF_Q:OFF_V], qk_w, qk_b))

    def heads(a):
        return a.reshape(bsz, t, M_HEADS, M_HEAD_DIM).transpose(0, 2, 1, 3).astype(jnp.float32)

    q = heads(qk[..., :M_WIDTH])
    k = heads(qk[..., M_WIDTH:]) * (M_HEAD_DIM ** -0.5)
    v = heads(p[..., OFF_V:OFF_G])
    g = (p[..., OFF_G:OFF_O] + g_b).astype(jnp.float32).reshape(bsz, t, 4, M_HEADS).transpose(2, 0, 3, 1)
    ig = (g[0], g[2])
    lf = (jax.nn.log_sigmoid(g[1]), jax.nn.log_sigmoid(g[3]))
    return q, k, v, ig, lf


def _mlstm_bidirectional(ctx_parts, lat_parts, emit_ctx):
    qc, kc, vc, igc, lfc = ctx_parts
    ql, kl, vl, igl, lfl = lat_parts
    bsz, nh, _, dh = ql.shape
    h_lat = None
    h_ctx = None
    for d, flip in enumerate((False, True)):
        fl = (lambda a: jnp.flip(a, axis=2)) if flip else (lambda a: a)
        state0 = (jnp.zeros((bsz, nh, dh, dh), jnp.float32),
                  jnp.zeros((bsz, nh, dh), jnp.float32),
                  jnp.full((bsz, nh), M_INIT, jnp.float32))
        st_ctx, hc = _mlstm_chunk_scan(fl(qc), fl(kc), fl(vc), fl(igc[d]), fl(lfc[d]), state0, emit_ctx)
        _, hl = _mlstm_chunk_scan(fl(ql), fl(kl), fl(vl), fl(igl[d]), fl(lfl[d]), st_ctx, True)
        h_lat = fl(hl) if h_lat is None else h_lat + fl(hl)
        if emit_ctx:
            h_ctx = fl(hc) if h_ctx is None else h_ctx + fl(hc)
    return h_lat, h_ctx


def _mlstm_out(h, p, norm_g, w_out):
    bsz, nh, t, dh = h.shape
    h = h * lax.rsqrt(jnp.mean(h * h, axis=-1, keepdims=True) + EPS) \
        * norm_g.astype(jnp.float32).reshape(M_HEADS, 1, M_HEAD_DIM)
    h = h.transpose(0, 2, 1, 3).reshape(bsz, t, M_WIDTH).astype(p.dtype)
    return (h * jax.nn.sigmoid(p[..., OFF_O:OFF_SB])) @ w_out


def _shortconv_branch(p, w, b, w_out):
    bg = p[..., OFF_SB:OFF_SC]
    cg = p[..., OFF_SC:OFF_SX]
    xin = p[..., OFF_SX:OFF_MG]
    return (bg * _conv1d_w3(cg * xin, w, b)) @ w_out


def _merge(p, y_m, y_c, w_o):
    gm = jax.nn.sigmoid(p[..., OFF_MG:OFF_MG + D_MODEL])
    gc = jax.nn.sigmoid(p[..., OFF_MG + D_MODEL:IN_COLS])
    return (gm * y_m + gc * y_c) @ w_o


def _conv_ffn(h, w_up, conv_w, conv_b, w_down, rows):
    u = h @ w_up
    a, g = u[..., :FF_HIDDEN], u[..., FF_HIDDEN:]
    if rows is None:
        a = _conv1d_w3(a, conv_w[1], conv_b)
    else:
        a = _conv2d_3x3(a, conv_w, conv_b, rows)
    return (jax.nn.gelu(a, approximate=True) * g) @ w_down


def setup_inputs(seed: int = 0) -> dict:
    key = jax.random.key(seed)
    ks = jax.random.split(key, 24)
    f32 = jnp.float32

    def nrm(k, shape, s):
        return jax.random.normal(k, shape, f32) * s

    L = DEPTH
    fb = jnp.linspace(3.0, 6.0, M_HEADS, dtype=f32)
    gk = jax.random.split(ks[11], 4)
    gate_b = jnp.concatenate([nrm(gk[0], (L, M_HEADS), 0.1),
                              fb + nrm(gk[1], (L, M_HEADS), 0.1),
                              nrm(gk[2], (L, M_HEADS), 0.1),
                              fb + nrm(gk[3], (L, M_HEADS), 0.1)], axis=-1)
    return {
        "x": nrm(ks[0], (BATCH, SEQ, D_MODEL), 1.0),
        "c": nrm(ks[1], (BATCH, D_MODEL), 1.0),
        "ctx": nrm(ks[2], (BATCH, CTX_LEN, D_MODEL), 1.0),
        "c_ctx": nrm(ks[3], (D_MODEL,), 1.0),
        "ada_w": nrm(ks[4], (L, D_MODEL, 6 * D_MODEL), 0.5 * D_MODEL ** -0.5),
        "ada_b": nrm(ks[5], (L, 6 * D_MODEL), 0.01),
        "norm1_g": 1.0 + nrm(ks[6], (L, D_MODEL), 0.02),
        "norm2_g": 1.0 + nrm(ks[7], (L, D_MODEL), 0.02),
        "w_in": nrm(ks[8], (L, D_MODEL, IN_COLS), D_MODEL ** -0.5),
        "qk_conv_w": nrm(ks[9], (L, 3, 2 * M_WIDTH), 3 ** -0.5),
        "qk_conv_b": nrm(ks[10], (L, 2 * M_WIDTH), 0.01),
        "gate_b": gate_b,
        "mnorm_g": 1.0 + nrm(ks[12], (L, M_WIDTH), 0.02),
        "w_m_out": nrm(ks[13], (L, M_WIDTH, D_MODEL), M_WIDTH ** -0.5),
        "sc_conv_w": nrm(ks[14], (L, 3, SC_WIDTH), 3 ** -0.5),
        "sc_conv_b": nrm(ks[15], (L, SC_WIDTH), 0.01),
        "w_c_out": nrm(ks[16], (L, SC_WIDTH, D_MODEL), SC_WIDTH ** -0.5),
        "w_o": nrm(ks[17], (L, D_MODEL, D_MODEL), D_MODEL ** -0.5),
        "w_up": nrm(ks[18], (L, D_MODEL, 2 * FF_HIDDEN), D_MODEL ** -0.5),
        "ff_conv_w": nrm(ks[19], (L, 3, 3, FF_HIDDEN), 1.0 / 3.0),
        "ff_conv_b": nrm(ks[20], (L, FF_HIDDEN), 0.01),
        "w_down": nrm(ks[21], (L, FF_HIDDEN, D_MODEL), FF_HIDDEN ** -0.5),
        "final_g": 1.0 + nrm(ks[22], (D_MODEL,), 0.02),
    }


def reference(x, c, ctx, c_ctx, ada_w, ada_b, norm1_g, norm2_g, w_in, qk_conv_w, qk_conv_b,
              gate_b, mnorm_g, w_m_out, sc_conv_w, sc_conv_b, w_c_out, w_o, w_up,
              ff_conv_w, ff_conv_b, w_down, final_g):
    rows = x.shape[1] // GRID_W
    for l in range(DEPTH):
        last = l == DEPTH - 1
        mod_x = (jax.nn.silu(c) @ ada_w[l] + ada_b[l]).reshape(-1, 6, D_MODEL)[:, :, None, :]
        mod_c = (jax.nn.silu(c_ctx) @ ada_w[l] + ada_b[l]).reshape(6, D_MODEL)

        hx = _modulate(x, norm1_g[l], mod_x[:, 0], mod_x[:, 1])
        hc = _modulate(ctx, norm1_g[l], mod_c[0], mod_c[1])
        px = hx @ w_in[l]
        pc = hc @ (w_in[l][:, :CTX_COLS] if last else w_in[l])

        lat_parts = _mlstm_prep(px, qk_conv_w[l], qk_conv_b[l], gate_b[l])
        ctx_parts = _mlstm_prep(pc, qk_conv_w[l], qk_conv_b[l], gate_b[l])
        h_lat, h_ctx = _mlstm_bidirectional(ctx_parts, lat_parts, not last)

        y_x = _merge(px,
                     _mlstm_out(h_lat, px, mnorm_g[l], w_m_out[l]),
                     _shortconv_branch(px, sc_conv_w[l], sc_conv_b[l], w_c_out[l]),
                     w_o[l])
        x = x + mod_x[:, 2] * y_x
        if not last:
            y_c = _merge(pc,
                         _mlstm_out(h_ctx, pc, mnorm_g[l], w_m_out[l]),
                         _shortconv_branch(pc, sc_conv_w[l], sc_conv_b[l], w_c_out[l]),
                         w_o[l])
            ctx = ctx + mod_c[2] * y_c

        hx = _modulate(x, norm2_g[l], mod_x[:, 3], mod_x[:, 4])
        x = x + mod_x[:, 5] * _conv_ffn(hx, w_up[l], ff_conv_w[l], ff_conv_b[l], w_down[l], rows)
        if not last:
            hc = _modulate(ctx, norm2_g[l], mod_c[3], mod_c[4])
            ctx = ctx + mod_c[5] * _conv_ffn(hc, w_up[l], ff_conv_w[l], ff_conv_b[l], w_down[l], None)

    return _rmsnorm(x, final_g)
```

```python
import functools

import jax
import jax.numpy as jnp
from jax import lax
from jax.experimental import pallas as pl
from jax.experimental.pallas import tpu as pltpu

F32 = jnp.float32
BF16 = jnp.bfloat16

D_MODEL = 1024
N_HEADS = 8
HEAD_DIM = D_MODEL // N_HEADS
CHUNK = 128
GRID_W = 64
FF_HIDDEN = 2816
EPS = 1e-6
M_INIT = -1e30
N_GATES = 4 * N_HEADS

LANES = 128
BF16_ROWS = 16
VMEM_LIMIT = 56 * 1024 * 1024

PROJ_TM = 256
SEQ_HALO = BF16_ROWS
MERGE_TM = 512
FFN_TM = 512
FFN_CW = 256

F_COLA, F_AINT, F_ENEG, F_WK, F_DECAY = range(5)
N_COLF = 5
N_FIELDS = 2 * N_COLF + 2


def _const_spec(shape):
    nd = len(shape)
    return pl.BlockSpec(shape, lambda *_: (0,) * nd, pipeline_mode=pl.Buffered(1))


def _sigmoid(v):
    return 1.0 / (1.0 + jnp.exp(-v))


def _rms_scale(v):
    return v * lax.rsqrt(jnp.mean(v * v, axis=-1, keepdims=True) + EPS)


def _dot(a, b):
    return jnp.dot(a, b, preferred_element_type=F32)


def _ada_kernel(c_ref, w_ref, b_ref, o_ref):
    cv = c_ref[...]
    s = cv * _sigmoid(cv)
    o_ref[...] = jnp.dot(s, w_ref[...], preferred_element_type=F32,
                         precision=lax.Precision.HIGHEST) + b_ref[...]


def _ada(cc, w, b):
    rows, n = cc.shape[0], w.shape[1]
    tn = 1024
    return pl.pallas_call(
        _ada_kernel,
        grid=(n // tn,),
        in_specs=[pl.BlockSpec((rows, D_MODEL), lambda j: (0, 0)),
                  pl.BlockSpec((D_MODEL, tn), lambda j: (0, j)),
                  pl.BlockSpec((1, tn), lambda j: (0, j))],
        out_specs=pl.BlockSpec((rows, tn), lambda j: (0, j)),
        out_shape=jax.ShapeDtypeStruct((rows, n), F32),
        name="ada",
    )(cc, w, b)


def _conv3_rows(p, w_ref, b_ref, tm):
    n = tm + 2 * SEQ_HALO
    lo, hi = SEQ_HALO, SEQ_HALO + tm
    left = pltpu.roll(p, 1, 0)[lo:hi]
    right = pltpu.roll(p, n - 1, 0)[lo:hi]
    return left * w_ref[0:1, :] + p[lo:hi] * w_ref[1:2, :] + right * w_ref[2:3, :] + b_ref[...]


def _proj_kernel(*refs, tm, full):
    if full:
        (xm_ref, xp_ref, xn_ref, mod_ref, g1_ref, wqk_ref, wv_ref, wg_ref, gb_ref, qkw_ref, qkb_ref,
         wo_ref, wsb_ref, wscx_ref, scw_ref, scb_ref, wco_ref, wmg_ref,
         q_ref, k_ref, v_ref, g_ref, og_ref, gm_ref, zc_ref) = refs
    else:
        (xm_ref, xp_ref, xn_ref, mod_ref, g1_ref, wqk_ref, wv_ref, wg_ref, gb_ref, qkw_ref, qkb_ref,
         q_ref, k_ref, v_ref, g_ref) = refs
    t = pl.program_id(1)
    nt = pl.num_programs(1)
    shift = mod_ref[0, 0:1, :]
    scale1 = 1.0 + mod_ref[0, 1:2, :]
    gain = g1_ref[...]

    def norm_mod(xv):
        return _rms_scale(xv) * gain * scale1 + shift

    hm = norm_mod(xm_ref[0])
    hp = jnp.where(t > 0, norm_mod(xp_ref[0]), 0.0)
    hn = jnp.where(t < nt - 1, norm_mod(xn_ref[0]), 0.0)
    hmb = hm.astype(BF16)
    he = jnp.concatenate([hp, hm, hn], axis=0).astype(BF16)

    qk = _conv3_rows(_dot(he, wqk_ref[...]), qkw_ref, qkb_ref, tm)
    qk = qk * _sigmoid(qk)
    q_ref[0] = qk[:, :D_MODEL].astype(BF16)
    k_ref[0] = (qk[:, D_MODEL:] * (HEAD_DIM ** -0.5)).astype(BF16)
    v_ref[0] = _dot(hmb, wv_ref[...]).astype(BF16)
    g_ref[0] = _dot(hmb, wg_ref[...]) + gb_ref[...]
    if not full:
        return
    og_ref[0] = _sigmoid(_dot(hmb, wo_ref[...])).astype(BF16)
    pcx = _dot(he, wscx_ref[...])
    cu = _conv3_rows(pcx[:, :D_MODEL] * pcx[:, D_MODEL:], scw_ref, scb_ref, tm)
    s = (_dot(hmb, wsb_ref[...]) * cu).astype(BF16)
    yc = _dot(s, wco_ref[...])
    pm = _dot(hmb, wmg_ref[...])
    gm_ref[0] = _sigmoid(pm[:, :D_MODEL]).astype(BF16)
    zc_ref[0] = _sigmoid(pm[:, D_MODEL:]) * yc


def _proj(xs, mod, g1, weights, *, full, per_batch_mod):
    bsz, t_len, _ = xs.shape
    tm = min(PROJ_TM, t_len)
    nt = t_len // tm
    hb = tm // SEQ_HALO
    nhb = t_len // SEQ_HALO
    x_specs = [
        pl.BlockSpec((1, tm, D_MODEL), lambda b, t: (b, t, 0)),
        pl.BlockSpec((1, SEQ_HALO, D_MODEL), lambda b, t: (b, jnp.maximum(t * hb - 1, 0), 0)),
        pl.BlockSpec((1, SEQ_HALO, D_MODEL), lambda b, t: (b, jnp.minimum((t + 1) * hb, nhb - 1), 0)),
    ]
    mod_spec = pl.BlockSpec((1, 8, D_MODEL), (lambda b, t: (b, 0, 0)) if per_batch_mod else (lambda b, t: (0, 0, 0)))
    w_specs = [_const_spec(w.shape) for w in weights]
    tok = lambda n, dt: (pl.BlockSpec((1, tm, n), lambda b, t: (b, t, 0)),
                         jax.ShapeDtypeStruct((bsz, t_len, n), dt))
    outs = [tok(D_MODEL, BF16), tok(D_MODEL, BF16), tok(D_MODEL, BF16), tok(LANES, F32)]
    if full:
        outs += [tok(D_MODEL, BF16), tok(D_MODEL, BF16), tok(D_MODEL, F32)]
    return pl.pallas_call(
        functools.partial(_proj_kernel, tm=tm, full=full),
        grid=(bsz, nt),
        in_specs=x_specs + [mod_spec, _const_spec(g1.shape)] + w_specs,
        out_specs=[o[0] for o in outs],
        out_shape=[o[1] for o in outs],
        compiler_params=pltpu.CompilerParams(dimension_semantics=("parallel", "parallel"),
                                             vmem_limit_bytes=VMEM_LIMIT),
        name="proj_full" if full else "proj_ctx",
    )(xs, xs, xs, mod, g1, *weights)


def _scan_lanes(v, combine, fill, forward):
    lane = lax.broadcasted_iota(jnp.int32, v.shape, 1)
    d = 1
    while d < LANES:
        if forward:
            shifted = jnp.where(lane >= d, pltpu.roll(v, d, 1), fill)
        else:
            shifted = jnp.where(lane < LANES - d, pltpu.roll(v, LANES - d, 1), fill)
        v = combine(v, shifted)
        d *= 2
    return v


def _gates_kernel(gl_ref, gc_ref, out_ref, *, n_ctx, n_lat):
    n_all = n_ctx + n_lat

    def load_t(cg):
        if cg < n_ctx:
            tile = gc_ref[0, cg * CHUNK:(cg + 1) * CHUNK, :]
        else:
            tile = gl_ref[0, (cg - n_ctx) * CHUNK:(cg - n_ctx + 1) * CHUNK, :]
        return tile.T

    for dirn in range(2):
        forward = dirn == 0
        if forward:
            order = list(range(n_all))
        else:
            order = list(range(n_ctx - 1, -1, -1)) + list(range(n_all - 1, n_ctx - 1, -1))
        last = LANES - 1 if forward else 0
        m = jnp.full((N_HEADS, 1), M_INIT, F32)
        for cg in order:
            tt = load_t(cg)
            ig = tt[2 * N_HEADS * dirn: 2 * N_HEADS * dirn + N_HEADS]
            fg = tt[2 * N_HEADS * dirn + N_HEADS: 2 * N_HEADS * (dirn + 1)]
            lf = jnp.minimum(fg, 0.0) - jnp.log1p(jnp.exp(-jnp.abs(fg)))
            b = _scan_lanes(lf, jnp.add, 0.0, forward)
            r = ig - b
            mx = jnp.maximum(m, _scan_lanes(r, jnp.maximum, -jnp.inf, forward))
            mt = b + mx
            m_new = mt[:, last:last + 1]
            b_end = b[:, last:last + 1]
            sl = slice(cg * CHUNK, (cg + 1) * CHUNK)
            base = N_COLF * dirn
            out_ref[0, base + F_COLA, :, sl] = -mx
            out_ref[0, base + F_AINT, :, sl] = jnp.exp(m - mx)
            out_ref[0, base + F_ENEG, :, sl] = jnp.exp(-mt)
            out_ref[0, base + F_WK, :, sl] = jnp.exp(b_end + r - m_new)
            out_ref[0, base + F_DECAY, :, sl] = jnp.broadcast_to(jnp.exp(b_end + m - m_new), (N_HEADS, CHUNK))
            out_ref[0, 2 * N_COLF + dirn, :, sl] = r
            m = m_new


def _gates(g_lat, g_ctx):
    bsz, t_lat, _ = g_lat.shape
    t_ctx = g_ctx.shape[1]
    t_all = t_lat + t_ctx
    return pl.pallas_call(
        functools.partial(_gates_kernel, n_ctx=t_ctx // CHUNK, n_lat=t_lat // CHUNK),
        grid=(bsz,),
        in_specs=[pl.BlockSpec((1, t_lat, LANES), lambda b: (b, 0, 0)),
                  pl.BlockSpec((1, t_ctx, LANES), lambda b: (b, 0, 0))],
        out_specs=pl.BlockSpec((1, N_FIELDS, N_HEADS, t_all), lambda b: (b, 0, 0, 0)),
        out_shape=jax.ShapeDtypeStruct((bsz, N_FIELDS, N_HEADS, t_all), F32),
        compiler_params=pltpu.CompilerParams(dimension_semantics=("parallel",)),
        name="gates",
    )(g_lat, g_ctx)


def _mlstm_kernel(ql_ref, kl_ref, vl_ref, kc_ref, vc_ref, cols_ref, rows_ref, og_ref, ng_ref, out_ref,
                  s_s, kt_s, dc_s, st_s, hf_s, hb_s, *, n_ctx, n_lat):
    lane = lax.broadcasted_iota(jnp.int32, (CHUNK, HEAD_DIM), 1)
    row = lax.broadcasted_iota(jnp.int32, (CHUNK, HEAD_DIM), 0)
    first_lane = lane == 0
    one_col = jnp.where(first_lane, 1.0, 0.0).astype(BF16)
    tri = (lane <= row, lane >= row)

    def col(cg, dirn, field):
        j = N_COLF * dirn + field
        return cols_ref[0, 0, cg, :, j:j + 1]

    def state_increment(kt, v, cg):
        vf = v.astype(F32)
        parts = []
        for dirn in range(2):
            wk = col(cg, dirn, F_WK)
            parts += [wk * vf, jnp.where(first_lane, wk, 0.0)]
        return _dot(kt, jnp.concatenate(parts, axis=1).astype(BF16))

    def transpose_k(k):
        return k.astype(F32).T.astype(BF16)

    def advance(dirn, cg, inc):
        w = 2 * HEAD_DIM
        st_s[dirn] = col(cg, dirn, F_DECAY) * st_s[dirn] + inc[:, dirn * w:(dirn + 1) * w]

    st_s[...] = jnp.zeros_like(st_s)
    ctx_inc = []
    for cg in range(n_ctx):
        sl = slice(cg * CHUNK, (cg + 1) * CHUNK)
        ctx_inc.append(state_increment(transpose_k(kc_ref[0, sl, :]), vc_ref[0, sl, :], cg))
    for cg in range(n_ctx):
        advance(0, cg, ctx_inc[cg])
    for cg in range(n_ctx - 1, -1, -1):
        advance(1, cg, ctx_inc[cg])

    def phase_a(i, carry):
        sl = pl.ds(pl.multiple_of(i * CHUNK, CHUNK), CHUNK)
        kt = transpose_k(kl_ref[0, sl, :])
        kt_s[i] = kt
        s_s[i] = _dot(ql_ref[0, sl, :], kt)
        dc_s[i] = state_increment(kt, vl_ref[0, sl, :], n_ctx + i)
        return carry

    lax.fori_loop(0, n_lat, phase_a, 0)

    def emit(dirn, i, h_s):
        cg = n_ctx + i
        sl = pl.ds(pl.multiple_of(i * CHUNK, CHUNK), CHUNK)
        dmat = col(cg, dirn, F_COLA) + rows_ref[0, 0, cg, dirn:dirn + 1, :]
        w = jnp.exp(jnp.where(tri[dirn], dmat, -jnp.inf))
        p = (s_s[i] * w).astype(BF16)
        aq = (col(cg, dirn, F_AINT) * ql_ref[0, sl, :].astype(F32)).astype(BF16)
        lhs = jnp.concatenate([aq, p], axis=1)
        vaug = jnp.concatenate([vl_ref[0, sl, :], one_col], axis=1)
        rhs = jnp.concatenate([st_s[dirn].astype(BF16), vaug], axis=0)
        res = _dot(lhs, rhs)
        num = res[:, :HEAD_DIM]
        den = res[:, HEAD_DIM:HEAD_DIM + 1]
        h_s[sl, :] = num / jnp.maximum(jnp.abs(den), col(cg, dirn, F_ENEG))
        advance(dirn, cg, dc_s[i])

    def phase_b(i, carry):
        emit(0, i, hf_s)
        emit(1, n_lat - 1 - i, hb_s)
        return carry

    lax.fori_loop(0, n_lat, phase_b, 0)

    piece = 2 * CHUNK
    for j in range(n_lat * CHUNK // piece):
        sl = slice(j * piece, (j + 1) * piece)
        hs = hf_s[sl, :] + hb_s[sl, :]
        out_ref[0, sl, :] = (_rms_scale(hs) * ng_ref[...] * og_ref[0, sl, :].astype(F32)).astype(BF16)


def _mlstm(q_l, k_l, v_l, k_c, v_c, cols, rows, og, ng):
    bsz, t_lat, _ = q_l.shape
    t_ctx = k_c.shape[1]
    n_lat, n_ctx = t_lat // CHUNK, t_ctx // CHUNK
    n_all = n_lat + n_ctx
    head_lat = pl.BlockSpec((1, t_lat, HEAD_DIM), lambda b, h: (b, 0, h))
    head_ctx = pl.BlockSpec((1, t_ctx, HEAD_DIM), lambda b, h: (b, 0, h))
    return pl.pallas_call(
        functools.partial(_mlstm_kernel, n_ctx=n_ctx, n_lat=n_lat),
        grid=(bsz, N_HEADS),
        in_specs=[head_lat, head_lat, head_lat, head_ctx, head_ctx,
                  pl.BlockSpec((1, 1, n_all, CHUNK, cols.shape[-1]), lambda b, h: (b, h, 0, 0, 0)),
                  pl.BlockSpec((1, 1, n_all, 8, CHUNK), lambda b, h: (b, h, 0, 0, 0)),
                  head_lat,
                  pl.BlockSpec((1, HEAD_DIM), lambda b, h: (0, h))],
        out_specs=head_lat,
        out_shape=jax.ShapeDtypeStruct((bsz, t_lat, D_MODEL), BF16),
        scratch_shapes=[pltpu.VMEM((n_lat, CHUNK, CHUNK), F32),
                        pltpu.VMEM((n_lat, HEAD_DIM, CHUNK), BF16),
                        pltpu.VMEM((n_lat, HEAD_DIM, 4 * HEAD_DIM), F32),
                        pltpu.VMEM((2, HEAD_DIM, 2 * HEAD_DIM), F32),
                        pltpu.VMEM((t_lat, HEAD_DIM), F32),
                        pltpu.VMEM((t_lat, HEAD_DIM), F32)],
        compiler_params=pltpu.CompilerParams(dimension_semantics=("parallel", "parallel"),
                                             vmem_limit_bytes=VMEM_LIMIT),
        name="mlstm",
    )(q_l, k_l, v_l, k_c, v_c, cols, rows, og, ng)


def _merge_kernel(hg_ref, gm_ref, zc_ref, x_ref, mod_ref, g2_ref, wmo_ref, wo_ref, x1_ref, h2_ref):
    ym = _dot(hg_ref[0], wmo_ref[...])
    z = (gm_ref[0].astype(F32) * ym + zc_ref[0]).astype(BF16)
    x1 = x_ref[0] + mod_ref[0, 2:3, :] * _dot(z, wo_ref[...])
    x1_ref[0] = x1
    h2 = _rms_scale(x1) * g2_ref[...] * (1.0 + mod_ref[0, 4:5, :]) + mod_ref[0, 3:4, :]
    h2_ref[0] = h2.astype(BF16)


def _merge(hg, gm, zc, x, mod, g2, wmo, wo):
    bsz, t_len, _ = x.shape
    tm = MERGE_TM
    tok = pl.BlockSpec((1, tm, D_MODEL), lambda b, t: (b, t, 0))
    return pl.pallas_call(
        _merge_kernel,
        grid=(bsz, t_len // tm),
        in_specs=[tok, tok, tok, tok, pl.BlockSpec((1, 8, D_MODEL), lambda b, t: (b, 0, 0)),
                  _const_spec(g2.shape), _const_spec(wmo.shape), _const_spec(wo.shape)],
        out_specs=[tok, tok],
        out_shape=[jax.ShapeDtypeStruct(x.shape, F32), jax.ShapeDtypeStruct(x.shape, BF16)],
        compiler_params=pltpu.CompilerParams(dimension_semantics=("parallel", "parallel"),
                                             vmem_limit_bytes=VMEM_LIMIT),
        name="merge",
    )(hg, gm, zc, x, mod, g2, wmo, wo)


def _gelu_tanh(v):
    return 0.5 * v * (1.0 + jnp.tanh(0.7978845608028654 * (v + 0.044715 * (v * v * v))))


def _ffn_kernel(hm_ref, hp_ref, hn_ref, x1_ref, mod_ref, wa_ref, wg_ref, cw_ref, cb_ref, wd_ref, fg_ref,
                out_ref, *, tm):
    t = pl.program_id(1)
    nt = pl.num_programs(1)
    n = tm + 2 * GRID_W
    hm = hm_ref[0]
    hp = jnp.where(t > 0, hp_ref[0], jnp.zeros_like(hp_ref[0]))
    hn = jnp.where(t < nt - 1, hn_ref[0], jnp.zeros_like(hn_ref[0]))
    he = jnp.concatenate([hp, hm, hn], axis=0)
    gcol = lax.broadcasted_iota(jnp.int32, (n, 1), 0) & (GRID_W - 1)
    has_left = gcol != 0
    has_right = gcol != GRID_W - 1
    acc = jnp.zeros((tm, D_MODEL), F32)
    for j in range(FF_HIDDEN // FFN_CW):
        cs = slice(j * FFN_CW, (j + 1) * FFN_CW)
        a = _dot(he, wa_ref[:, cs])
        taps = (jnp.where(has_left, pltpu.roll(a, 1, 0), 0.0), a,
                jnp.where(has_right, pltpu.roll(a, n - 1, 0), 0.0))
        conv = cb_ref[:, cs]
        for dr in range(3):
            for dc in range(3):
                conv = conv + cw_ref[3 * dr + dc: 3 * dr + dc + 1, cs] * taps[dc][GRID_W * dr: GRID_W * dr + tm]
        act = _gelu_tanh(conv) * _dot(hm, wg_ref[:, cs])
        acc = acc + _dot(act.astype(BF16), wd_ref[cs, :])
    x2 = x1_ref[0] + mod_ref[0, 5:6, :] * acc
    out_ref[0] = _rms_scale(x2) * fg_ref[...]


def _ffn(h2, x1, mod, wa, wg, cw, cb, wd, fg):
    bsz, t_len, _ = x1.shape
    tm = FFN_TM
    rb = tm // GRID_W
    nrb = t_len // GRID_W
    tok = pl.BlockSpec((1, tm, D_MODEL), lambda b, t: (b, t, 0))
    return pl.pallas_call(
        functools.partial(_ffn_kernel, tm=tm),
        grid=(bsz, t_len // tm),
        in_specs=[tok,
                  pl.BlockSpec((1, GRID_W, D_MODEL), lambda b, t: (b, jnp.maximum(t * rb - 1, 0), 0)),
                  pl.BlockSpec((1, GRID_W, D_MODEL), lambda b, t: (b, jnp.minimum((t + 1) * rb, nrb - 1), 0)),
                  tok, pl.BlockSpec((1, 8, D_MODEL), lambda b, t: (b, 0, 0)),
                  _const_spec(wa.shape), _const_spec(wg.shape), _const_spec(cw.shape), _const_spec(cb.shape),
                  _const_spec(wd.shape), _const_spec(fg.shape)],
        out_specs=tok,
        out_shape=jax.ShapeDtypeStruct(x1.shape, F32),
        compiler_params=pltpu.CompilerParams(dimension_semantics=("parallel", "parallel"),
                                             vmem_limit_bytes=VMEM_LIMIT),
        name="ffn",
    )(h2, h2, h2, x1, mod, wa, wg, cw, cb, wd, fg)


def kernel(x, c, ctx, c_ctx, ada_w, ada_b, norm1_g, norm2_g, w_in, qk_conv_w, qk_conv_b, gate_b, mnorm_g,
           w_m_out, sc_conv_w, sc_conv_b, w_c_out, w_o, w_up, ff_conv_w, ff_conv_b, w_down, final_g):
    assert ada_w.shape[0] == 1, "single-layer block"
    bsz, t_lat, _ = x.shape
    t_ctx = ctx.shape[1]
    n_all = (t_lat + t_ctx) // CHUNK

    cc = jnp.zeros((2 * 8, D_MODEL), F32).at[:bsz].set(c).at[bsz].set(c_ctx)
    mod = _ada(cc, ada_w[0], ada_b[0][None, :])
    mod_x = jnp.pad(mod[:bsz].reshape(bsz, 6, D_MODEL), ((0, 0), (0, 2), (0, 0)))
    mod_c = jnp.pad(mod[bsz].reshape(1, 6, D_MODEL), ((0, 0), (0, 2), (0, 0)))

    w = w_in[0]
    m_w = D_MODEL
    off_g = 3 * m_w
    off_o = off_g + N_GATES
    off_sb = off_o + m_w
    off_sc = off_sb + D_MODEL
    off_mg = off_sc + 2 * D_MODEL
    cast = lambda a: a.astype(BF16)
    wqk, wv = cast(w[:, :2 * m_w]), cast(w[:, 2 * m_w:off_g])
    wgt = cast(jnp.pad(w[:, off_g:off_o], ((0, 0), (0, LANES - N_GATES))))
    gb = jnp.pad(gate_b[0], (0, LANES - N_GATES))[None, :]
    g1 = norm1_g[0][None, :]
    ctx_w = [wqk, wv, wgt, gb, qk_conv_w[0], qk_conv_b[0][None, :]]
    lat_w = ctx_w + [cast(w[:, off_o:off_sb]), cast(w[:, off_sb:off_sc]), cast(w[:, off_sc:off_mg]),
                     sc_conv_w[0], sc_conv_b[0][None, :], cast(w_c_out[0]), cast(w[:, off_mg:])]

    q_l, k_l, v_l, g_l, og, gm, zc = _proj(x, mod_x, g1, lat_w, full=True, per_batch_mod=True)
    _, k_c, v_c, g_c = _proj(ctx, mod_c, g1, ctx_w, full=False, per_batch_mod=False)

    fields = _gates(g_l, g_c)
    fields = fields.reshape(bsz, N_FIELDS, N_HEADS, n_all, CHUNK)
    cols = jnp.pad(fields[:, :2 * N_COLF].transpose(0, 2, 3, 4, 1), ((0, 0),) * 4 + ((0, 16 - 2 * N_COLF),))
    rows = jnp.pad(fields[:, 2 * N_COLF:].transpose(0, 2, 3, 1, 4), ((0, 0),) * 3 + ((0, 6), (0, 0)))

    hg = _mlstm(q_l, k_l, v_l, k_c, v_c, cols, rows, og, mnorm_g[0][None, :])

    x1, h2 = _merge(hg, gm, zc, x, mod_x, norm2_g[0][None, :], cast(w_m_out[0]), cast(w_o[0]))

    wu = w_up[0]
    out = _ffn(h2, x1, mod_x, cast(wu[:, :FF_HIDDEN]), cast(wu[:, FF_HIDDEN:]),
               ff_conv_w[0].reshape(9, FF_HIDDEN), ff_conv_b[0][None, :], cast(w_down[0]),
               final_g[None, :])
    return out
```

```python
import functools

import jax
import jax.numpy as jnp
from jax import lax
from jax.experimental import pallas as pl
from jax.experimental.pallas import tpu as pltpu

F32 = jnp.float32
BF16 = jnp.bfloat16

D_MODEL = 1024
N_HEADS = 8
HEAD_DIM = D_MODEL // N_HEADS
CHUNK = 128
GRID_W = 64
FF_HIDDEN = 2816
EPS = 1e-6
M_INIT = -1e30
N_GATES = 4 * N_HEADS

LANES = 128
BF16_ROWS = 16
VMEM_LIMIT = 56 * 1024 * 1024

PROJ_TM = 256
SEQ_HALO = BF16_ROWS
MERGE_TM = 512
FFN_TM = 512
FFN_CW = 256

R_ROWB, R_DECAY = 0, 2
C_BASE, C_PER_DIR = 4, 4
F_COLA, F_AINT, F_ENEG, F_WK = range(4)
N_FIELDS = 16


def _const_spec(shape):
    nd = len(shape)
    return pl.BlockSpec(shape, lambda *_: (0,) * nd, pipeline_mode=pl.Buffered(1))


def _sigmoid(v):
    return 1.0 / (1.0 + jnp.exp(-v))


def _rms_scale(v):
    return v * lax.rsqrt(jnp.mean(v * v, axis=-1, keepdims=True) + EPS)


def _dot(a, b):
    return jnp.dot(a, b, preferred_element_type=F32)


def _ada_kernel(c_ref, w_ref, b_ref, o_ref):
    cv = c_ref[...]
    s = cv * _sigmoid(cv)
    o_ref[...] = jnp.dot(s, w_ref[...], preferred_element_type=F32,
                         precision=lax.Precision.HIGHEST) + b_ref[...]


def _ada(cc, w, b):
    rows, n = cc.shape[0], w.shape[1]
    tn = 1024
    return pl.pallas_call(
        _ada_kernel,
        grid=(n // tn,),
        in_specs=[pl.BlockSpec((rows, D_MODEL), lambda j: (0, 0)),
                  pl.BlockSpec((D_MODEL, tn), lambda j: (0, j)),
                  pl.BlockSpec((1, tn), lambda j: (0, j))],
        out_specs=pl.BlockSpec((rows, tn), lambda j: (0, j)),
        out_shape=jax.ShapeDtypeStruct((rows, n), F32),
        name="ada",
    )(cc, w, b)


def _conv3_rows(p, w_ref, b_ref, tm):
    n = tm + 2 * SEQ_HALO
    lo, hi = SEQ_HALO, SEQ_HALO + tm
    left = pltpu.roll(p, 1, 0)[lo:hi]
    right = pltpu.roll(p, n - 1, 0)[lo:hi]
    return left * w_ref[0:1, :] + p[lo:hi] * w_ref[1:2, :] + right * w_ref[2:3, :] + b_ref[...]


def _proj_kernel(*refs, tm, full):
    if full:
        (xm_ref, xp_ref, xn_ref, mod_ref, g1_ref, wqk_ref, wv_ref, wg_ref, gb_ref, qkw_ref, qkb_ref,
         wo_ref, wsb_ref, wscx_ref, scw_ref, scb_ref, wco_ref, wmg_ref,
         q_ref, k_ref, v_ref, g_ref, og_ref, gm_ref, zc_ref) = refs
    else:
        (xm_ref, xp_ref, xn_ref, mod_ref, g1_ref, wqk_ref, wv_ref, wg_ref, gb_ref, qkw_ref, qkb_ref,
         q_ref, k_ref, v_ref, g_ref) = refs
    t = pl.program_id(1)
    nt = pl.num_programs(1)
    shift = mod_ref[0, 0:1, :]
    scale1 = 1.0 + mod_ref[0, 1:2, :]
    gain = g1_ref[...]

    def norm_mod(xv):
        return _rms_scale(xv) * gain * scale1 + shift

    hm = norm_mod(xm_ref[0])
    hp = jnp.where(t > 0, norm_mod(xp_ref[0]), 0.0)
    hn = jnp.where(t < nt - 1, norm_mod(xn_ref[0]), 0.0)
    hmb = hm.astype(BF16)
    he = jnp.concatenate([hp, hm, hn], axis=0).astype(BF16)

    qk = _conv3_rows(_dot(he, wqk_ref[...]), qkw_ref, qkb_ref, tm)
    qk = qk * _sigmoid(qk)
    q_ref[0] = qk[:, :D_MODEL].astype(BF16)
    k_ref[0] = (qk[:, D_MODEL:] * (HEAD_DIM ** -0.5)).astype(BF16)
    v_ref[0] = _dot(hmb, wv_ref[...]).astype(BF16)
    g_ref[0] = _dot(hmb, wg_ref[...]) + gb_ref[...]
    if not full:
        return
    og_ref[0] = _sigmoid(_dot(hmb, wo_ref[...])).astype(BF16)
    pcx = _dot(he, wscx_ref[...])
    cu = _conv3_rows(pcx[:, :D_MODEL] * pcx[:, D_MODEL:], scw_ref, scb_ref, tm)
    s = (_dot(hmb, wsb_ref[...]) * cu).astype(BF16)
    yc = _dot(s, wco_ref[...])
    pm = _dot(hmb, wmg_ref[...])
    gm_ref[0] = _sigmoid(pm[:, :D_MODEL]).astype(BF16)
    zc_ref[0] = _sigmoid(pm[:, D_MODEL:]) * yc


def _proj(xs, mod, g1, weights, *, full, per_batch_mod):
    bsz, t_len, _ = xs.shape
    tm = min(PROJ_TM, t_len)
    nt = t_len // tm
    hb = tm // SEQ_HALO
    nhb = t_len // SEQ_HALO
    x_specs = [
        pl.BlockSpec((1, tm, D_MODEL), lambda b, t: (b, t, 0)),
        pl.BlockSpec((1, SEQ_HALO, D_MODEL), lambda b, t: (b, jnp.maximum(t * hb - 1, 0), 0)),
        pl.BlockSpec((1, SEQ_HALO, D_MODEL), lambda b, t: (b, jnp.minimum((t + 1) * hb, nhb - 1), 0)),
    ]
    mod_spec = pl.BlockSpec((1, 8, D_MODEL), (lambda b, t: (b, 0, 0)) if per_batch_mod else (lambda b, t: (0, 0, 0)))
    w_specs = [_const_spec(w.shape) for w in weights]
    tok = lambda n, dt: (pl.BlockSpec((1, tm, n), lambda b, t: (b, t, 0)),
                         jax.ShapeDtypeStruct((bsz, t_len, n), dt))
    outs = [tok(D_MODEL, BF16), tok(D_MODEL, BF16), tok(D_MODEL, BF16), tok(LANES, F32)]
    if full:
        outs += [tok(D_MODEL, BF16), tok(D_MODEL, BF16), tok(D_MODEL, F32)]
    return pl.pallas_call(
        functools.partial(_proj_kernel, tm=tm, full=full),
        grid=(bsz, nt),
        in_specs=x_specs + [mod_spec, _const_spec(g1.shape)] + w_specs,
        out_specs=[o[0] for o in outs],
        out_shape=[o[1] for o in outs],
        compiler_params=pltpu.CompilerParams(dimension_semantics=("parallel", "parallel"),
                                             vmem_limit_bytes=VMEM_LIMIT),
        name="proj_full" if full else "proj_ctx",
    )(xs, xs, xs, mod, g1, *weights)


def _scan_lanes(v, combine, fill, forward):
    lane = lax.broadcasted_iota(jnp.int32, v.shape, 1)
    d = 1
    while d < LANES:
        if forward:
            shifted = jnp.where(lane >= d, pltpu.roll(v, d, 1), fill)
        else:
            shifted = jnp.where(lane < LANES - d, pltpu.roll(v, LANES - d, 1), fill)
        v = combine(v, shifted)
        d *= 2
    return v


def _gates_kernel(gl_ref, gc_ref, out_ref, *, n_ctx, n_lat):
    n_all = n_ctx + n_lat

    def load_t(cg):
        if cg < n_ctx:
            tile = gc_ref[0, cg * CHUNK:(cg + 1) * CHUNK, :]
        else:
            tile = gl_ref[0, (cg - n_ctx) * CHUNK:(cg - n_ctx + 1) * CHUNK, :]
        return tile.T

    for dirn in range(2):
        forward = dirn == 0
        if forward:
            order = list(range(n_all))
        else:
            order = list(range(n_ctx - 1, -1, -1)) + list(range(n_all - 1, n_ctx - 1, -1))
        last = LANES - 1 if forward else 0
        m = jnp.full((N_HEADS, 1), M_INIT, F32)
        for cg in order:
            tt = load_t(cg)
            ig = tt[2 * N_HEADS * dirn: 2 * N_HEADS * dirn + N_HEADS]
            fg = tt[2 * N_HEADS * dirn + N_HEADS: 2 * N_HEADS * (dirn + 1)]
            lf = jnp.minimum(fg, 0.0) - jnp.log1p(jnp.exp(-jnp.abs(fg)))
            b = _scan_lanes(lf, jnp.add, 0.0, forward)
            r = ig - b
            mx = jnp.maximum(m, _scan_lanes(r, jnp.maximum, -jnp.inf, forward))
            mt = b + mx
            m_new = mt[:, last:last + 1]
            b_end = b[:, last:last + 1]
            cbase = C_BASE + C_PER_DIR * dirn
            fields = {
                R_ROWB + dirn: r,
                R_DECAY + dirn: jnp.broadcast_to(jnp.exp(b_end + m - m_new), (N_HEADS, CHUNK)),
                cbase + F_COLA: -mx,
                cbase + F_AINT: jnp.exp(m - mx),
                cbase + F_ENEG: jnp.exp(-mt),
                cbase + F_WK: jnp.exp(b_end + r - m_new),
            }
            for f, val in fields.items():
                for h in range(N_HEADS):
                    out_ref[0, h, cg, f:f + 1, :] = val[h:h + 1, :]
            m = m_new
    n_used = C_BASE + 2 * C_PER_DIR
    out_ref[0, :, :, n_used:, :] = jnp.zeros((N_HEADS, n_all, N_FIELDS - n_used, CHUNK), F32)


def _gates(g_lat, g_ctx):
    bsz, t_lat, _ = g_lat.shape
    t_ctx = g_ctx.shape[1]
    n_all = (t_lat + t_ctx) // CHUNK
    shape = (bsz, N_HEADS, n_all, N_FIELDS, CHUNK)
    return pl.pallas_call(
        functools.partial(_gates_kernel, n_ctx=t_ctx // CHUNK, n_lat=t_lat // CHUNK),
        grid=(bsz,),
        in_specs=[pl.BlockSpec((1, t_lat, LANES), lambda b: (b, 0, 0)),
                  pl.BlockSpec((1, t_ctx, LANES), lambda b: (b, 0, 0))],
        out_specs=pl.BlockSpec((1,) + shape[1:], lambda b: (b, 0, 0, 0, 0)),
        out_shape=jax.ShapeDtypeStruct(shape, F32),
        compiler_params=pltpu.CompilerParams(dimension_semantics=("parallel",)),
        name="gates",
    )(g_lat, g_ctx)


def _mlstm_kernel(ql_ref, kl_ref, vl_ref, kc_ref, vc_ref, gf_ref, og_ref, ng_ref, out_ref,
                  kt_s, dc_s, cols_s, stf_s, stb_s, st_s, *, n_ctx, n_lat):
    lane = lax.broadcasted_iota(jnp.int32, (CHUNK, HEAD_DIM), 1)
    row = lax.broadcasted_iota(jnp.int32, (CHUNK, HEAD_DIM), 0)
    first_lane = lane == 0
    one_col = jnp.where(first_lane, 1.0, 0.0).astype(BF16)
    tri = (lane <= row, lane >= row)
    wide = 2 * HEAD_DIM

    def col(cg, dirn, field):
        j = C_BASE + C_PER_DIR * dirn + field
        return cols_s[cg, :, j:j + 1]

    def pass_a(cg, k, v):
        cols_s[cg] = gf_ref[0, 0, cg].T
        kt = k.astype(F32).T.astype(BF16)
        vf = v.astype(F32)
        parts = []
        for dirn in range(2):
            wk = col(cg, dirn, F_WK)
            parts += [wk * vf, jnp.where(first_lane, wk, 0.0)]
        dc_s[cg] = _dot(kt, jnp.concatenate(parts, axis=1).astype(BF16))
        return kt

    for cg in range(n_ctx):
        sl = slice(cg * CHUNK, (cg + 1) * CHUNK)
        pass_a(cg, kc_ref[0, sl, :], vc_ref[0, sl, :])

    def pass_a_lat(i, carry):
        sl = pl.ds(pl.multiple_of(i * CHUNK, CHUNK), CHUNK)
        kt_s[i] = pass_a(n_ctx + i, kl_ref[0, sl, :], vl_ref[0, sl, :])
        return carry

    lax.fori_loop(0, n_lat, pass_a_lat, 0, unroll=2)

    def advance(dirn, cg):
        dec = gf_ref[0, 0, cg, R_DECAY + dirn:R_DECAY + dirn + 1, :]
        dec = jnp.concatenate([dec, dec], axis=1)
        st_s[dirn] = dec * st_s[dirn] + dc_s[cg, :, dirn * wide:(dirn + 1) * wide]

    st_s[...] = jnp.zeros_like(st_s)
    for cg in range(n_ctx):
        advance(0, cg)
    for cg in range(n_ctx - 1, -1, -1):
        advance(1, cg)

    def pass_b(i, carry):
        stf_s[i] = st_s[0].astype(BF16)
        advance(0, n_ctx + i)
        j = n_lat - 1 - i
        stb_s[j] = st_s[1].astype(BF16)
        advance(1, n_ctx + j)
        return carry

    lax.fori_loop(0, n_lat, pass_b, 0, unroll=2)

    def pass_c(i, carry):
        cg = n_ctx + i
        sl = pl.ds(pl.multiple_of(i * CHUNK, CHUNK), CHUNK)
        q = ql_ref[0, sl, :]
        qf = q.astype(F32)
        s = _dot(q, kt_s[i])
        vaug = jnp.concatenate([vl_ref[0, sl, :], one_col], axis=1)
        h = None
        for dirn, st_ref in enumerate((stf_s, stb_s)):
            dmat = col(cg, dirn, F_COLA) + gf_ref[0, 0, cg, R_ROWB + dirn:R_ROWB + dirn + 1, :]
            w = jnp.exp(jnp.where(tri[dirn], dmat, -jnp.inf))
            lhs = jnp.concatenate([(col(cg, dirn, F_AINT) * qf).astype(BF16), (s * w).astype(BF16)], axis=1)
            res = _dot(lhs, jnp.concatenate([st_ref[i], vaug], axis=0))
            den = jnp.maximum(jnp.abs(res[:, HEAD_DIM:HEAD_DIM + 1]), col(cg, dirn, F_ENEG))
            hd = res[:, :HEAD_DIM] / den
            h = hd if h is None else h + hd
        out_ref[0, sl, :] = (_rms_scale(h) * ng_ref[...] * og_ref[0, sl, :].astype(F32)).astype(BF16)
        return carry

    lax.fori_loop(0, n_lat, pass_c, 0, unroll=2)


def _mlstm(q_l, k_l, v_l, k_c, v_c, gf, og, ng):
    bsz, t_lat, _ = q_l.shape
    t_ctx = k_c.shape[1]
    n_lat, n_ctx = t_lat // CHUNK, t_ctx // CHUNK
    n_all = n_lat + n_ctx
    head_lat = pl.BlockSpec((1, t_lat, HEAD_DIM), lambda b, h: (b, 0, h))
    head_ctx = pl.BlockSpec((1, t_ctx, HEAD_DIM), lambda b, h: (b, 0, h))
    return pl.pallas_call(
        functools.partial(_mlstm_kernel, n_ctx=n_ctx, n_lat=n_lat),
        grid=(bsz, N_HEADS),
        in_specs=[head_lat, head_lat, head_lat, head_ctx, head_ctx,
                  pl.BlockSpec((1, 1, n_all, N_FIELDS, CHUNK), lambda b, h: (b, h, 0, 0, 0)),
                  head_lat,
                  pl.BlockSpec((1, HEAD_DIM), lambda b, h: (0, h))],
        out_specs=head_lat,
        out_shape=jax.ShapeDtypeStruct((bsz, t_lat, D_MODEL), BF16),
        scratch_shapes=[pltpu.VMEM((n_lat, HEAD_DIM, CHUNK), BF16),
                        pltpu.VMEM((n_all, HEAD_DIM, 4 * HEAD_DIM), F32),
                        pltpu.VMEM((n_all, CHUNK, N_FIELDS), F32),
                        pltpu.VMEM((n_lat, HEAD_DIM, 2 * HEAD_DIM), BF16),
                        pltpu.VMEM((n_lat, HEAD_DIM, 2 * HEAD_DIM), BF16),
                        pltpu.VMEM((2, HEAD_DIM, 2 * HEAD_DIM), F32)],
        compiler_params=pltpu.CompilerParams(dimension_semantics=("parallel", "parallel"),
                                             vmem_limit_bytes=VMEM_LIMIT),
        name="mlstm",
    )(q_l, k_l, v_l, k_c, v_c, gf, og, ng)


def _merge_kernel(hg_ref, gm_ref, zc_ref, x_ref, mod_ref, g2_ref, wmo_ref, wo_ref, x1_ref, h2_ref):
    ym = _dot(hg_ref[0], wmo_ref[...])
    z = (gm_ref[0].astype(F32) * ym + zc_ref[0]).astype(BF16)
    x1 = x_ref[0] + mod_ref[0, 2:3, :] * _dot(z, wo_ref[...])
    x1_ref[0] = x1
    h2 = _rms_scale(x1) * g2_ref[...] * (1.0 + mod_ref[0, 4:5, :]) + mod_ref[0, 3:4, :]
    h2_ref[0] = h2.astype(BF16)


def _merge(hg, gm, zc, x, mod, g2, wmo, wo):
    bsz, t_len, _ = x.shape
    tm = MERGE_TM
    tok = pl.BlockSpec((1, tm, D_MODEL), lambda b, t: (b, t, 0))
    return pl.pallas_call(
        _merge_kernel,
        grid=(bsz, t_len // tm),
        in_specs=[tok, tok, tok, tok, pl.BlockSpec((1, 8, D_MODEL), lambda b, t: (b, 0, 0)),
                  _const_spec(g2.shape), _const_spec(wmo.shape), _const_spec(wo.shape)],
        out_specs=[tok, tok],
        out_shape=[jax.ShapeDtypeStruct(x.shape, F32), jax.ShapeDtypeStruct(x.shape, BF16)],
        compiler_params=pltpu.CompilerParams(dimension_semantics=("parallel", "parallel"),
                                             vmem_limit_bytes=VMEM_LIMIT),
        name="merge",
    )(hg, gm, zc, x, mod, g2, wmo, wo)


def _gelu_tanh(v):
    return 0.5 * v * (1.0 + jnp.tanh(0.7978845608028654 * (v + 0.044715 * (v * v * v))))


def _ffn_kernel(hm_ref, hp_ref, hn_ref, x1_ref, mod_ref, wa_ref, wg_ref, cw_ref, cb_ref, wd_ref, fg_ref,
                out_ref, *, tm):
    t = pl.program_id(1)
    nt = pl.num_programs(1)
    n = tm + 2 * GRID_W
    hm = hm_ref[0]
    hp = jnp.where(t > 0, hp_ref[0], jnp.zeros_like(hp_ref[0]))
    hn = jnp.where(t < nt - 1, hn_ref[0], jnp.zeros_like(hn_ref[0]))
    he = jnp.concatenate([hp, hm, hn], axis=0)
    gcol = lax.broadcasted_iota(jnp.int32, (n, 1), 0) & (GRID_W - 1)
    has_left = gcol != 0
    has_right = gcol != GRID_W - 1
    acc = jnp.zeros((tm, D_MODEL), F32)
    for j in range(FF_HIDDEN // FFN_CW):
        cs = slice(j * FFN_CW, (j + 1) * FFN_CW)
        a = _dot(he, wa_ref[:, cs])
        taps = (jnp.where(has_left, pltpu.roll(a, 1, 0), 0.0), a,
                jnp.where(has_right, pltpu.roll(a, n - 1, 0), 0.0))
        conv = cb_ref[:, cs]
        for dr in range(3):
            for dc in range(3):
                conv = conv + cw_ref[3 * dr + dc: 3 * dr + dc + 1, cs] * taps[dc][GRID_W * dr: GRID_W * dr + tm]
        act = _gelu_tanh(conv) * _dot(hm, wg_ref[:, cs])
        acc = acc + _dot(act.astype(BF16), wd_ref[cs, :])
    x2 = x1_ref[0] + mod_ref[0, 5:6, :] * acc
    out_ref[0] = _rms_scale(x2) * fg_ref[...]


def _ffn(h2, x1, mod, wa, wg, cw, cb, wd, fg):
    bsz, t_len, _ = x1.shape
    tm = FFN_TM
    rb = tm // GRID_W
    nrb = t_len // GRID_W
    tok = pl.BlockSpec((1, tm, D_MODEL), lambda b, t: (b, t, 0))
    return pl.pallas_call(
        functools.partial(_ffn_kernel, tm=tm),
        grid=(bsz, t_len // tm),
        in_specs=[tok,
                  pl.BlockSpec((1, GRID_W, D_MODEL), lambda b, t: (b, jnp.maximum(t * rb - 1, 0), 0)),
                  pl.BlockSpec((1, GRID_W, D_MODEL), lambda b, t: (b, jnp.minimum((t + 1) * rb, nrb - 1), 0)),
                  tok, pl.BlockSpec((1, 8, D_MODEL), lambda b, t: (b, 0, 0)),
                  _const_spec(wa.shape), _const_spec(wg.shape), _const_spec(cw.shape), _const_spec(cb.shape),
                  _const_spec(wd.shape), _const_spec(fg.shape)],
        out_specs=tok,
        out_shape=jax.ShapeDtypeStruct(x1.shape, F32),
        compiler_params=pltpu.CompilerParams(dimension_semantics=("parallel", "parallel"),
                                             vmem_limit_bytes=VMEM_LIMIT),
        name="ffn",
    )(h2, h2, h2, x1, mod, wa, wg, cw, cb, wd, fg)


def kernel(x, c, ctx, c_ctx, ada_w, ada_b, norm1_g, norm2_g, w_in, qk_conv_w, qk_conv_b, gate_b, mnorm_g,
           w_m_out, sc_conv_w, sc_conv_b, w_c_out, w_o, w_up, ff_conv_w, ff_conv_b, w_down, final_g):
    assert ada_w.shape[0] == 1, "single-layer block"
    bsz, t_lat, _ = x.shape
    t_ctx = ctx.shape[1]
    n_all = (t_lat + t_ctx) // CHUNK

    cc = jnp.zeros((2 * 8, D_MODEL), F32).at[:bsz].set(c).at[bsz].set(c_ctx)
    mod = _ada(cc, ada_w[0], ada_b[0][None, :])
    mod_x = jnp.pad(mod[:bsz].reshape(bsz, 6, D_MODEL), ((0, 0), (0, 2), (0, 0)))
    mod_c = jnp.pad(mod[bsz].reshape(1, 6, D_MODEL), ((0, 0), (0, 2), (0, 0)))

    w = w_in[0]
    m_w = D_MODEL
    off_g = 3 * m_w
    off_o = off_g + N_GATES
    off_sb = off_o + m_w
    off_sc = off_sb + D_MODEL
    off_mg = off_sc + 2 * D_MODEL
    cast = lambda a: a.astype(BF16)
    wqk, wv = cast(w[:, :2 * m_w]), cast(w[:, 2 * m_w:off_g])
    wgt = cast(jnp.pad(w[:, off_g:off_o], ((0, 0), (0, LANES - N_GATES))))
    gb = jnp.pad(gate_b[0], (0, LANES - N_GATES))[None, :]
    g1 = norm1_g[0][None, :]
    ctx_w = [wqk, wv, wgt, gb, qk_conv_w[0], qk_conv_b[0][None, :]]
    lat_w = ctx_w + [cast(w[:, off_o:off_sb]), cast(w[:, off_sb:off_sc]), cast(w[:, off_sc:off_mg]),
                     sc_conv_w[0], sc_conv_b[0][None, :], cast(w_c_out[0]), cast(w[:, off_mg:])]

    q_l, k_l, v_l, g_l, og, gm, zc = _proj(x, mod_x, g1, lat_w, full=True, per_batch_mod=True)
    _, k_c, v_c, g_c = _proj(ctx, mod_c, g1, ctx_w, full=False, per_batch_mod=False)

    gf = _gates(g_l, g_c)
    hg = _mlstm(q_l, k_l, v_l, k_c, v_c, gf, og, mnorm_g[0][None, :])

    x1, h2 = _merge(hg, gm, zc, x, mod_x, norm2_g[0][None, :], cast(w_m_out[0]), cast(w_o[0]))

    wu = w_up[0]
    out = _ffn(h2, x1, mod_x, cast(wu[:, :FF_HIDDEN]), cast(wu[:, FF_HIDDEN:]),
               ff_conv_w[0].reshape(9, FF_HIDDEN), ff_conv_b[0][None, :], cast(w_down[0]),
               final_g[None, :])
    return out
```

```python
import functools

import jax
import jax.numpy as jnp
from jax import lax
from jax.experimental import pallas as pl
from jax.experimental.pallas import tpu as pltpu

F32 = jnp.float32
BF16 = jnp.bfloat16

D_MODEL = 1024
N_HEADS = 8
HEAD_DIM = D_MODEL // N_HEADS
CHUNK = 128
GRID_W = 64
FF_HIDDEN = 2816
EPS = 1e-6
M_INIT = -1e30
N_GATES = 4 * N_HEADS

LANES = 128
BF16_ROWS = 16
VMEM_LIMIT = 56 * 1024 * 1024

PROJ_TM = 256
SEQ_HALO = BF16_ROWS
MERGE_TM = 512
FFN_TM = 512
FFN_CW = 256

R_ROWB, R_DECAY = 0, 2
C_BASE, C_PER_DIR = 4, 4
F_COLA, F_AINT, F_ENEG, F_WK = range(4)
N_FIELDS = 16


def _const_spec(shape):
    nd = len(shape)
    return pl.BlockSpec(shape, lambda *_: (0,) * nd, pipeline_mode=pl.Buffered(1))


def _sigmoid(v):
    return 1.0 / (1.0 + jnp.exp(-v))


def _rms_scale(v):
    return v * lax.rsqrt(jnp.mean(v * v, axis=-1, keepdims=True) + EPS)


def _dot(a, b):
    return jnp.dot(a, b, preferred_element_type=F32)


def _ada_kernel(c_ref, w_ref, b_ref, o_ref):
    cv = c_ref[...]
    s = cv * _sigmoid(cv)
    o_ref[...] = jnp.dot(s, w_ref[...], preferred_element_type=F32,
                         precision=lax.Precision.HIGHEST) + b_ref[...]


def _ada(cc, w, b):
    rows, n = cc.shape[0], w.shape[1]
    tn = 1024
    return pl.pallas_call(
        _ada_kernel,
        grid=(n // tn,),
        in_specs=[pl.BlockSpec((rows, D_MODEL), lambda j: (0, 0)),
                  pl.BlockSpec((D_MODEL, tn), lambda j: (0, j)),
                  pl.BlockSpec((1, tn), lambda j: (0, j))],
        out_specs=pl.BlockSpec((rows, tn), lambda j: (0, j)),
        out_shape=jax.ShapeDtypeStruct((rows, n), F32),
        name="ada",
    )(cc, w, b)


def _conv3_rows(p, w_ref, b_ref, tm):
    n = tm + 2 * SEQ_HALO
    lo, hi = SEQ_HALO, SEQ_HALO + tm
    left = pltpu.roll(p, 1, 0)[lo:hi]
    right = pltpu.roll(p, n - 1, 0)[lo:hi]
    return left * w_ref[0:1, :] + p[lo:hi] * w_ref[1:2, :] + right * w_ref[2:3, :] + b_ref[...]


def _store_head_tiles_t(ref, val, tm):
    for h in range(N_HEADS):
        for ci in range(tm // CHUNK):
            tile = val[ci * CHUNK:(ci + 1) * CHUNK, h * HEAD_DIM:(h + 1) * HEAD_DIM]
            ref[0, h, ci] = tile.T.astype(BF16)


def _proj_kernel(*refs, tm, full):
    if full:
        (xm_ref, xp_ref, xn_ref, mod_ref, g1_ref, wqk_ref, wv_ref, wg_ref, gb_ref, qkw_ref, qkb_ref,
         wo_ref, wsb_ref, wscx_ref, scw_ref, scb_ref, wco_ref, wmg_ref,
         k_ref, vt_ref, g_ref, qt_ref, og_ref, gm_ref, zc_ref) = refs
    else:
        (xm_ref, xp_ref, xn_ref, mod_ref, g1_ref, wqk_ref, wv_ref, wg_ref, gb_ref, qkw_ref, qkb_ref,
         k_ref, vt_ref, g_ref) = refs
    t = pl.program_id(1)
    nt = pl.num_programs(1)
    shift = mod_ref[0, 0:1, :]
    scale1 = 1.0 + mod_ref[0, 1:2, :]
    gain = g1_ref[...]

    def norm_mod(xv):
        return _rms_scale(xv) * gain * scale1 + shift

    hm = norm_mod(xm_ref[0])
    hp = jnp.where(t > 0, norm_mod(xp_ref[0]), 0.0)
    hn = jnp.where(t < nt - 1, norm_mod(xn_ref[0]), 0.0)
    hmb = hm.astype(BF16)
    he = jnp.concatenate([hp, hm, hn], axis=0).astype(BF16)

    qk = _conv3_rows(_dot(he, wqk_ref[...]), qkw_ref, qkb_ref, tm)
    qk = qk * _sigmoid(qk)
    k_ref[0] = (qk[:, D_MODEL:] * (HEAD_DIM ** -0.5)).astype(BF16)
    _store_head_tiles_t(vt_ref, _dot(hmb, wv_ref[...]), tm)
    g_ref[0] = _dot(hmb, wg_ref[...]) + gb_ref[...]
    if not full:
        return
    _store_head_tiles_t(qt_ref, qk[:, :D_MODEL], tm)
    og_ref[0] = _sigmoid(_dot(hmb, wo_ref[...])).astype(BF16)
    pcx = _dot(he, wscx_ref[...])
    cu = _conv3_rows(pcx[:, :D_MODEL] * pcx[:, D_MODEL:], scw_ref, scb_ref, tm)
    s = (_dot(hmb, wsb_ref[...]) * cu).astype(BF16)
    yc = _dot(s, wco_ref[...])
    pm = _dot(hmb, wmg_ref[...])
    gm_ref[0] = _sigmoid(pm[:, :D_MODEL]).astype(BF16)
    zc_ref[0] = _sigmoid(pm[:, D_MODEL:]) * yc


def _proj(xs, mod, g1, weights, *, full, per_batch_mod):
    bsz, t_len, _ = xs.shape
    tm = min(PROJ_TM, t_len)
    nt = t_len // tm
    hb = tm // SEQ_HALO
    nhb = t_len // SEQ_HALO
    x_specs = [
        pl.BlockSpec((1, tm, D_MODEL), lambda b, t: (b, t, 0)),
        pl.BlockSpec((1, SEQ_HALO, D_MODEL), lambda b, t: (b, jnp.maximum(t * hb - 1, 0), 0)),
        pl.BlockSpec((1, SEQ_HALO, D_MODEL), lambda b, t: (b, jnp.minimum((t + 1) * hb, nhb - 1), 0)),
    ]
    mod_spec = pl.BlockSpec((1, 8, D_MODEL), (lambda b, t: (b, 0, 0)) if per_batch_mod else (lambda b, t: (0, 0, 0)))
    w_specs = [_const_spec(w.shape) for w in weights]
    tok = lambda n, dt: (pl.BlockSpec((1, tm, n), lambda b, t: (b, t, 0)),
                         jax.ShapeDtypeStruct((bsz, t_len, n), dt))
    tiles_t = (pl.BlockSpec((1, N_HEADS, tm // CHUNK, HEAD_DIM, CHUNK), lambda b, t: (b, 0, t, 0, 0)),
               jax.ShapeDtypeStruct((bsz, N_HEADS, t_len // CHUNK, HEAD_DIM, CHUNK), BF16))
    outs = [tok(D_MODEL, BF16), tiles_t, tok(LANES, F32)]
    if full:
        outs += [tiles_t, tok(D_MODEL, BF16), tok(D_MODEL, BF16), tok(D_MODEL, F32)]
    return pl.pallas_call(
        functools.partial(_proj_kernel, tm=tm, full=full),
        grid=(bsz, nt),
        in_specs=x_specs + [mod_spec, _const_spec(g1.shape)] + w_specs,
        out_specs=[o[0] for o in outs],
        out_shape=[o[1] for o in outs],
        compiler_params=pltpu.CompilerParams(dimension_semantics=("parallel", "parallel"),
                                             vmem_limit_bytes=VMEM_LIMIT),
        name="proj_full" if full else "proj_ctx",
    )(xs, xs, xs, mod, g1, *weights)


def _scan_lanes(v, combine, fill, forward):
    lane = lax.broadcasted_iota(jnp.int32, v.shape, 1)
    d = 1
    while d < LANES:
        if forward:
            shifted = jnp.where(lane >= d, pltpu.roll(v, d, 1), fill)
        else:
            shifted = jnp.where(lane < LANES - d, pltpu.roll(v, LANES - d, 1), fill)
        v = combine(v, shifted)
        d *= 2
    return v


def _gates_kernel(gl_ref, gc_ref, out_ref, *, n_ctx, n_lat):
    n_all = n_ctx + n_lat

    def load_t(cg):
        if cg < n_ctx:
            tile = gc_ref[0, cg * CHUNK:(cg + 1) * CHUNK, :]
        else:
            tile = gl_ref[0, (cg - n_ctx) * CHUNK:(cg - n_ctx + 1) * CHUNK, :]
        return tile.T

    for dirn in range(2):
        forward = dirn == 0
        if forward:
            order = list(range(n_all))
        else:
            order = list(range(n_ctx - 1, -1, -1)) + list(range(n_all - 1, n_ctx - 1, -1))
        last = LANES - 1 if forward else 0
        m = jnp.full((N_HEADS, 1), M_INIT, F32)
        for cg in order:
            tt = load_t(cg)
            ig = tt[2 * N_HEADS * dirn: 2 * N_HEADS * dirn + N_HEADS]
            fg = tt[2 * N_HEADS * dirn + N_HEADS: 2 * N_HEADS * (dirn + 1)]
            lf = jnp.minimum(fg, 0.0) - jnp.log1p(jnp.exp(-jnp.abs(fg)))
            b = _scan_lanes(lf, jnp.add, 0.0, forward)
            r = ig - b
            mx = jnp.maximum(m, _scan_lanes(r, jnp.maximum, -jnp.inf, forward))
            mt = b + mx
            m_new = mt[:, last:last + 1]
            b_end = b[:, last:last + 1]
            cbase = C_BASE + C_PER_DIR * dirn
            fields = {
                R_ROWB + dirn: r,
                R_DECAY + dirn: jnp.broadcast_to(jnp.exp(b_end + m - m_new), (N_HEADS, CHUNK)),
                cbase + F_COLA: -mx,
                cbase + F_AINT: jnp.exp(m - mx),
                cbase + F_ENEG: jnp.exp(-mt),
                cbase + F_WK: jnp.exp(b_end + r - m_new),
            }
            for f, val in fields.items():
                for h in range(N_HEADS):
                    out_ref[0, h, cg, f:f + 1, :] = val[h:h + 1, :]
            m = m_new
    n_used = C_BASE + 2 * C_PER_DIR
    out_ref[0, :, :, n_used:, :] = jnp.zeros((N_HEADS, n_all, N_FIELDS - n_used, CHUNK), F32)


def _gates(g_lat, g_ctx):
    bsz, t_lat, _ = g_lat.shape
    t_ctx = g_ctx.shape[1]
    n_all = (t_lat + t_ctx) // CHUNK
    shape = (bsz, N_HEADS, n_all, N_FIELDS, CHUNK)
    return pl.pallas_call(
        functools.partial(_gates_kernel, n_ctx=t_ctx // CHUNK, n_lat=t_lat // CHUNK),
        grid=(bsz,),
        in_specs=[pl.BlockSpec((1, t_lat, LANES), lambda b: (b, 0, 0)),
                  pl.BlockSpec((1, t_ctx, LANES), lambda b: (b, 0, 0))],
        out_specs=pl.BlockSpec((1,) + shape[1:], lambda b: (b, 0, 0, 0, 0)),
        out_shape=jax.ShapeDtypeStruct(shape, F32),
        compiler_params=pltpu.CompilerParams(dimension_semantics=("parallel",)),
        name="gates",
    )(g_lat, g_ctx)


STATE_ROWS = HEAD_DIM + BF16_ROWS


def _mlstm_kernel(qt_ref, kl_ref, vtl_ref, kc_ref, vtc_ref, gf_ref, og_ref, ng_ref, out_ref,
                  dz_s, zf_s, zb_s, z_s, p_s, h_s, *, n_ctx, n_lat):
    lane = lax.broadcasted_iota(jnp.int32, (CHUNK, CHUNK), 1)
    row = lax.broadcasted_iota(jnp.int32, (CHUNK, CHUNK), 0)
    tri = (row <= lane, row >= lane)
    tail_first = lax.broadcasted_iota(jnp.int32, (BF16_ROWS, CHUNK), 0) == 0
    ones_tail = jnp.where(tail_first, 1.0, 0.0).astype(BF16)

    def field(cg, dirn, f):
        j = C_BASE + C_PER_DIR * dirn + f
        return gf_ref[0, 0, cg, j:j + 1, :]

    def pass_a(cg, k, vt):
        vf = vt.astype(F32)
        parts = []
        for dirn in range(2):
            wk = field(cg, dirn, F_WK)
            parts += [(vf * wk).astype(BF16), jnp.where(tail_first, wk, 0.0).astype(BF16)]
        dz_s[cg] = _dot(jnp.concatenate(parts, axis=0), k)

    for cg in range(n_ctx):
        pass_a(cg, kc_ref[0, cg * CHUNK:(cg + 1) * CHUNK, :], vtc_ref[0, 0, cg])

    def pass_a_lat(i, carry):
        sl = pl.ds(pl.multiple_of(i * CHUNK, CHUNK), CHUNK)
        pass_a(n_ctx + i, kl_ref[0, sl, :], vtl_ref[0, 0, i])
        return carry

    lax.fori_loop(0, n_lat, pass_a_lat, 0, unroll=2)

    def advance(dirn, cg):
        dec = gf_ref[0, 0, cg, R_DECAY + dirn:R_DECAY + dirn + 1, :]
        z_s[dirn] = dec * z_s[dirn] + dz_s[cg, dirn * STATE_ROWS:(dirn + 1) * STATE_ROWS, :]

    z_s[...] = jnp.zeros_like(z_s)
    for cg in range(n_ctx):
        advance(0, cg)
    for cg in range(n_ctx - 1, -1, -1):
        advance(1, cg)

    def pass_b(i, carry):
        zf_s[i] = z_s[0].astype(BF16)
        advance(0, n_ctx + i)
        j = n_lat - 1 - i
        zb_s[j] = z_s[1].astype(BF16)
        advance(1, n_ctx + j)
        return carry

    lax.fori_loop(0, n_lat, pass_b, 0, unroll=2)

    def pass_c1(i, carry):
        cg = n_ctx + i
        sl = pl.ds(pl.multiple_of(i * CHUNK, CHUNK), CHUNK)
        st = _dot(kl_ref[0, sl, :], qt_ref[0, 0, i])
        rowb = gf_ref[0, 0, cg].T
        for dirn in range(2):
            dmat = rowb[:, R_ROWB + dirn:R_ROWB + dirn + 1] + field(cg, dirn, F_COLA)
            w = jnp.exp(jnp.where(tri[dirn], dmat, -jnp.inf))
            p_s[i, dirn] = (st * w).astype(BF16)
        return carry

    lax.fori_loop(0, n_lat, pass_c1, 0, unroll=4)

    def pass_c2(i, carry):
        cg = n_ctx + i
        qtf = qt_ref[0, 0, i].astype(F32)
        vaug = jnp.concatenate([vtl_ref[0, 0, i], ones_tail], axis=0)
        ht = None
        for dirn, z_ref in enumerate((zf_s, zb_s)):
            rhs = jnp.concatenate([(qtf * field(cg, dirn, F_AINT)).astype(BF16), p_s[i, dirn]], axis=0)
            res = _dot(jnp.concatenate([z_ref[i], vaug], axis=1), rhs)
            den = jnp.maximum(jnp.abs(res[HEAD_DIM:HEAD_DIM + 1, :]), field(cg, dirn, F_ENEG))
            hd = res[:HEAD_DIM, :] / den
            ht = hd if ht is None else ht + hd
        h_s[i] = ht * lax.rsqrt(jnp.mean(ht * ht, axis=0, keepdims=True) + EPS)
        return carry

    lax.fori_loop(0, n_lat, pass_c2, 0, unroll=4)

    def pass_c3(i, carry):
        sl = pl.ds(pl.multiple_of(i * CHUNK, CHUNK), CHUNK)
        out_ref[0, sl, :] = (h_s[i].T * ng_ref[...] * og_ref[0, sl, :].astype(F32)).astype(BF16)
        return carry

    lax.fori_loop(0, n_lat, pass_c3, 0, unroll=4)


def _mlstm(qt_l, k_l, vt_l, k_c, vt_c, gf, og, ng):
    bsz, t_lat, _ = k_l.shape
    t_ctx = k_c.shape[1]
    n_lat, n_ctx = t_lat // CHUNK, t_ctx // CHUNK
    n_all = n_lat + n_ctx
    head_lat = pl.BlockSpec((1, t_lat, HEAD_DIM), lambda b, h: (b, 0, h))
    head_ctx = pl.BlockSpec((1, t_ctx, HEAD_DIM), lambda b, h: (b, 0, h))
    tiles = lambda n: pl.BlockSpec((1, 1, n, HEAD_DIM, CHUNK), lambda b, h: (b, h, 0, 0, 0))
    return pl.pallas_call(
        functools.partial(_mlstm_kernel, n_ctx=n_ctx, n_lat=n_lat),
        grid=(bsz, N_HEADS),
        in_specs=[tiles(n_lat), head_lat, tiles(n_lat), head_ctx, tiles(n_ctx),
                  pl.BlockSpec((1, 1, n_all, N_FIELDS, CHUNK), lambda b, h: (b, h, 0, 0, 0)),
                  head_lat,
                  pl.BlockSpec((1, HEAD_DIM), lambda b, h: (0, h))],
        out_specs=head_lat,
        out_shape=jax.ShapeDtypeStruct((bsz, t_lat, D_MODEL), BF16),
        scratch_shapes=[pltpu.VMEM((n_all, 2 * STATE_ROWS, HEAD_DIM), F32),
                        pltpu.VMEM((n_lat, STATE_ROWS, HEAD_DIM), BF16),
                        pltpu.VMEM((n_lat, STATE_ROWS, HEAD_DIM), BF16),
                        pltpu.VMEM((2, STATE_ROWS, HEAD_DIM), F32),
                        pltpu.VMEM((n_lat, 2, CHUNK, CHUNK), BF16),
                        pltpu.VMEM((n_lat, HEAD_DIM, CHUNK), F32)],
        compiler_params=pltpu.CompilerParams(dimension_semantics=("parallel", "parallel"),
                                             vmem_limit_bytes=VMEM_LIMIT),
        name="mlstm",
    )(qt_l, k_l, vt_l, k_c, vt_c, gf, og, ng)


def _merge_kernel(hg_ref, gm_ref, zc_ref, x_ref, mod_ref, g2_ref, wmo_ref, wo_ref, x1_ref, h2_ref):
    ym = _dot(hg_ref[0], wmo_ref[...])
    z = (gm_ref[0].astype(F32) * ym + zc_ref[0]).astype(BF16)
    x1 = x_ref[0] + mod_ref[0, 2:3, :] * _dot(z, wo_ref[...])
    x1_ref[0] = x1
    h2 = _rms_scale(x1) * g2_ref[...] * (1.0 + mod_ref[0, 4:5, :]) + mod_ref[0, 3:4, :]
    h2_ref[0] = h2.astype(BF16)


def _merge(hg, gm, zc, x, mod, g2, wmo, wo):
    bsz, t_len, _ = x.shape
    tm = MERGE_TM
    tok = pl.BlockSpec((1, tm, D_MODEL), lambda b, t: (b, t, 0))
    return pl.pallas_call(
        _merge_kernel,
        grid=(bsz, t_len // tm),
        in_specs=[tok, tok, tok, tok, pl.BlockSpec((1, 8, D_MODEL), lambda b, t: (b, 0, 0)),
                  _const_spec(g2.shape), _const_spec(wmo.shape), _const_spec(wo.shape)],
        out_specs=[tok, tok],
        out_shape=[jax.ShapeDtypeStruct(x.shape, F32), jax.ShapeDtypeStruct(x.shape, BF16)],
        compiler_params=pltpu.CompilerParams(dimension_semantics=("parallel", "parallel"),
                                             vmem_limit_bytes=VMEM_LIMIT),
        name="merge",
    )(hg, gm, zc, x, mod, g2, wmo, wo)


def _gelu_tanh(v):
    return 0.5 * v * (1.0 + jnp.tanh(0.7978845608028654 * (v + 0.044715 * (v * v * v))))


def _ffn_kernel(hm_ref, hp_ref, hn_ref, x1_ref, mod_ref, wa_ref, wg_ref, cw_ref, cb_ref, wd_ref, fg_ref,
                out_ref, *, tm):
    t = pl.program_id(1)
    nt = pl.num_programs(1)
    n = tm + 2 * GRID_W
    hm = hm_ref[0]
    hp = jnp.where(t > 0, hp_ref[0], jnp.zeros_like(hp_ref[0]))
    hn = jnp.where(t < nt - 1, hn_ref[0], jnp.zeros_like(hn_ref[0]))
    he = jnp.concatenate([hp, hm, hn], axis=0)
    gcol = lax.broadcasted_iota(jnp.int32, (n, 1), 0) & (GRID_W - 1)
    has_left = gcol != 0
    has_right = gcol != GRID_W - 1
    acc = jnp.zeros((tm, D_MODEL), F32)
    for j in range(FF_HIDDEN // FFN_CW):
        cs = slice(j * FFN_CW, (j + 1) * FFN_CW)
        a = _dot(he, wa_ref[:, cs])
        taps = (jnp.where(has_left, pltpu.roll(a, 1, 0), 0.0), a,
                jnp.where(has_right, pltpu.roll(a, n - 1, 0), 0.0))
        conv = cb_ref[:, cs]
        for dr in range(3):
            for dc in range(3):
                conv = conv + cw_ref[3 * dr + dc: 3 * dr + dc + 1, cs] * taps[dc][GRID_W * dr: GRID_W * dr + tm]
        act = _gelu_tanh(conv) * _dot(hm, wg_ref[:, cs])
        acc = acc + _dot(act.astype(BF16), wd_ref[cs, :])
    x2 = x1_ref[0] + mod_ref[0, 5:6, :] * acc
    out_ref[0] = _rms_scale(x2) * fg_ref[...]


def _ffn(h2, x1, mod, wa, wg, cw, cb, wd, fg):
    bsz, t_len, _ = x1.shape
    tm = FFN_TM
    rb = tm // GRID_W
    nrb = t_len // GRID_W
    tok = pl.BlockSpec((1, tm, D_MODEL), lambda b, t: (b, t, 0))
    return pl.pallas_call(
        functools.partial(_ffn_kernel, tm=tm),
        grid=(bsz, t_len // tm),
        in_specs=[tok,
                  pl.BlockSpec((1, GRID_W, D_MODEL), lambda b, t: (b, jnp.maximum(t * rb - 1, 0), 0)),
                  pl.BlockSpec((1, GRID_W, D_MODEL), lambda b, t: (b, jnp.minimum((t + 1) * rb, nrb - 1), 0)),
                  tok, pl.BlockSpec((1, 8, D_MODEL), lambda b, t: (b, 0, 0)),
                  _const_spec(wa.shape), _const_spec(wg.shape), _const_spec(cw.shape), _const_spec(cb.shape),
                  _const_spec(wd.shape), _const_spec(fg.shape)],
        out_specs=tok,
        out_shape=jax.ShapeDtypeStruct(x1.shape, F32),
        compiler_params=pltpu.CompilerParams(dimension_semantics=("parallel", "parallel"),
                                             vmem_limit_bytes=VMEM_LIMIT),
        name="ffn",
    )(h2, h2, h2, x1, mod, wa, wg, cw, cb, wd, fg)


def kernel(x, c, ctx, c_ctx, ada_w, ada_b, norm1_g, norm2_g, w_in, qk_conv_w, qk_conv_b, gate_b, mnorm_g,
           w_m_out, sc_conv_w, sc_conv_b, w_c_out, w_o, w_up, ff_conv_w, ff_conv_b, w_down, final_g):
    assert ada_w.shape[0] == 1, "single-layer block"
    bsz, t_lat, _ = x.shape
    t_ctx = ctx.shape[1]
    n_all = (t_lat + t_ctx) // CHUNK

    cc = jnp.zeros((2 * 8, D_MODEL), F32).at[:bsz].set(c).at[bsz].set(c_ctx)
    mod = _ada(cc, ada_w[0], ada_b[0][None, :])
    mod_x = jnp.pad(mod[:bsz].reshape(bsz, 6, D_MODEL), ((0, 0), (0, 2), (0, 0)))
    mod_c = jnp.pad(mod[bsz].reshape(1, 6, D_MODEL), ((0, 0), (0, 2), (0, 0)))

    w = w_in[0]
    m_w = D_MODEL
    off_g = 3 * m_w
    off_o = off_g + N_GATES
    off_sb = off_o + m_w
    off_sc = off_sb + D_MODEL
    off_mg = off_sc + 2 * D_MODEL
    cast = lambda a: a.astype(BF16)
    wqk, wv = cast(w[:, :2 * m_w]), cast(w[:, 2 * m_w:off_g])
    wgt = cast(jnp.pad(w[:, off_g:off_o], ((0, 0), (0, LANES - N_GATES))))
    gb = jnp.pad(gate_b[0], (0, LANES - N_GATES))[None, :]
    g1 = norm1_g[0][None, :]
    ctx_w = [wqk, wv, wgt, gb, qk_conv_w[0], qk_conv_b[0][None, :]]
    lat_w = ctx_w + [cast(w[:, off_o:off_sb]), cast(w[:, off_sb:off_sc]), cast(w[:, off_sc:off_mg]),
                     sc_conv_w[0], sc_conv_b[0][None, :], cast(w_c_out[0]), cast(w[:, off_mg:])]

    k_l, vt_l, g_l, qt_l, og, gm, zc = _proj(x, mod_x, g1, lat_w, full=True, per_batch_mod=True)
    k_c, vt_c, g_c = _proj(ctx, mod_c, g1, ctx_w, full=False, per_batch_mod=False)

    gf = _gates(g_l, g_c)
    hg = _mlstm(qt_l, k_l, vt_l, k_c, vt_c, gf, og, mnorm_g[0][None, :])

    x1, h2 = _merge(hg, gm, zc, x, mod_x, norm2_g[0][None, :], cast(w_m_out[0]), cast(w_o[0]))

    wu = w_up[0]
    out = _ffn(h2, x1, mod_x, cast(wu[:, :FF_HIDDEN]), cast(wu[:, FF_HIDDEN:]),
               ff_conv_w[0].reshape(9, FF_HIDDEN), ff_conv_b[0][None, :], cast(w_down[0]),
               final_g[None, :])
    return out
```

```python
import functools

import jax
import jax.numpy as jnp
from jax import lax
from jax.experimental import pallas as pl
from jax.experimental.pallas import tpu as pltpu

F32 = jnp.float32
BF16 = jnp.bfloat16

D_MODEL = 1024
N_HEADS = 8
HEAD_DIM = D_MODEL // N_HEADS
CHUNK = 128
GRID_W = 64
FF_HIDDEN = 2816
EPS = 1e-6
M_INIT = -1e30
N_GATES = 4 * N_HEADS

LANES = 128
BF16_ROWS = 16
VMEM_LIMIT = 56 * 1024 * 1024

PROJ_TM = 256
SEQ_HALO = BF16_ROWS
MERGE_TM = 512
FFN_TM = 512
FFN_CW = 256
FFN_DOWN_GROUP = 4

R_ROWB, R_DECAY = 0, 2
C_BASE, C_PER_DIR = 4, 4
F_COLA, F_AINT, F_ENEG, F_WK = range(4)
N_FIELDS = 16


def _const_spec(shape):
    nd = len(shape)
    return pl.BlockSpec(shape, lambda *_: (0,) * nd, pipeline_mode=pl.Buffered(1))


def _sigmoid(v):
    return 1.0 / (1.0 + jnp.exp(-v))


def _rms_scale(v):
    return v * lax.rsqrt(jnp.mean(v * v, axis=-1, keepdims=True) + EPS)


def _dot(a, b):
    return jnp.dot(a, b, preferred_element_type=F32)


def _ada_kernel(c_ref, w_ref, b_ref, o_ref):
    cv = c_ref[...]
    s = cv * _sigmoid(cv)
    o_ref[...] = jnp.dot(s, w_ref[...], preferred_element_type=F32,
                         precision=lax.Precision.HIGHEST) + b_ref[...]


def _ada(cc, w, b):
    rows, n = cc.shape[0], w.shape[1]
    tn = 1024
    return pl.pallas_call(
        _ada_kernel,
        grid=(n // tn,),
        in_specs=[pl.BlockSpec((rows, D_MODEL), lambda j: (0, 0)),
                  pl.BlockSpec((D_MODEL, tn), lambda j: (0, j)),
                  pl.BlockSpec((1, tn), lambda j: (0, j))],
        out_specs=pl.BlockSpec((rows, tn), lambda j: (0, j)),
        out_shape=jax.ShapeDtypeStruct((rows, n), F32),
        name="ada",
    )(cc, w, b)


def _conv3_rows(p, w_ref, b_ref, tm):
    n = tm + 2 * SEQ_HALO
    lo, hi = SEQ_HALO, SEQ_HALO + tm
    left = pltpu.roll(p, 1, 0)[lo:hi]
    right = pltpu.roll(p, n - 1, 0)[lo:hi]
    return left * w_ref[0:1, :] + p[lo:hi] * w_ref[1:2, :] + right * w_ref[2:3, :] + b_ref[...]


def _store_head_tiles_t(ref, val, tm):
    for h in range(N_HEADS):
        for ci in range(tm // CHUNK):
            tile = val[ci * CHUNK:(ci + 1) * CHUNK, h * HEAD_DIM:(h + 1) * HEAD_DIM]
            ref[0, h, ci] = tile.T.astype(BF16)


def _proj_kernel(*refs, tm, full):
    if full:
        (xm_ref, xp_ref, xn_ref, mod_ref, g1_ref, wqk_ref, wv_ref, wg_ref, gb_ref, qkw_ref, qkb_ref,
         wo_ref, wsb_ref, wscx_ref, scw_ref, scb_ref, wco_ref, wmg_ref,
         k_ref, vt_ref, g_ref, qt_ref, og_ref, gm_ref, zc_ref) = refs
    else:
        (xm_ref, xp_ref, xn_ref, mod_ref, g1_ref, wqk_ref, wv_ref, wg_ref, gb_ref, qkw_ref, qkb_ref,
         k_ref, vt_ref, g_ref) = refs
    t = pl.program_id(1)
    nt = pl.num_programs(1)
    shift = mod_ref[0, 0:1, :]
    scale1 = 1.0 + mod_ref[0, 1:2, :]
    gain = g1_ref[...]

    def norm_mod(xv):
        return _rms_scale(xv) * gain * scale1 + shift

    hm = norm_mod(xm_ref[0])
    hp = jnp.where(t > 0, norm_mod(xp_ref[0]), 0.0)
    hn = jnp.where(t < nt - 1, norm_mod(xn_ref[0]), 0.0)
    hmb = hm.astype(BF16)
    he = jnp.concatenate([hp, hm, hn], axis=0).astype(BF16)

    qk = _conv3_rows(_dot(he, wqk_ref[...]), qkw_ref, qkb_ref, tm)
    qk = qk * _sigmoid(qk)
    k_ref[0] = (qk[:, D_MODEL:] * (HEAD_DIM ** -0.5)).astype(BF16)
    _store_head_tiles_t(vt_ref, _dot(hmb, wv_ref[...]), tm)
    g_ref[0] = _dot(hmb, wg_ref[...]) + gb_ref[...]
    if not full:
        return
    _store_head_tiles_t(qt_ref, qk[:, :D_MODEL], tm)
    og_ref[0] = _sigmoid(_dot(hmb, wo_ref[...])).astype(BF16)
    pcx = _dot(he, wscx_ref[...])
    cu = _conv3_rows(pcx[:, :D_MODEL] * pcx[:, D_MODEL:], scw_ref, scb_ref, tm)
    s = (_dot(hmb, wsb_ref[...]) * cu).astype(BF16)
    yc = _dot(s, wco_ref[...])
    pm = _dot(hmb, wmg_ref[...])
    gm_ref[0] = _sigmoid(pm[:, :D_MODEL]).astype(BF16)
    zc_ref[0] = _sigmoid(pm[:, D_MODEL:]) * yc


def _proj(xs, mod, g1, weights, *, full, per_batch_mod):
    bsz, t_len, _ = xs.shape
    tm = min(PROJ_TM, t_len)
    nt = t_len // tm
    hb = tm // SEQ_HALO
    nhb = t_len // SEQ_HALO
    x_specs = [
        pl.BlockSpec((1, tm, D_MODEL), lambda b, t: (b, t, 0)),
        pl.BlockSpec((1, SEQ_HALO, D_MODEL), lambda b, t: (b, jnp.maximum(t * hb - 1, 0), 0)),
        pl.BlockSpec((1, SEQ_HALO, D_MODEL), lambda b, t: (b, jnp.minimum((t + 1) * hb, nhb - 1), 0)),
    ]
    mod_spec = pl.BlockSpec((1, 8, D_MODEL), (lambda b, t: (b, 0, 0)) if per_batch_mod else (lambda b, t: (0, 0, 0)))
    w_specs = [_const_spec(w.shape) for w in weights]
    tok = lambda n, dt: (pl.BlockSpec((1, tm, n), lambda b, t: (b, t, 0)),
                         jax.ShapeDtypeStruct((bsz, t_len, n), dt))
    tiles_t = (pl.BlockSpec((1, N_HEADS, tm // CHUNK, HEAD_DIM, CHUNK), lambda b, t: (b, 0, t, 0, 0)),
               jax.ShapeDtypeStruct((bsz, N_HEADS, t_len // CHUNK, HEAD_DIM, CHUNK), BF16))
    outs = [tok(D_MODEL, BF16), tiles_t, tok(LANES, F32)]
    if full:
        outs += [tiles_t, tok(D_MODEL, BF16), tok(D_MODEL, BF16), tok(D_MODEL, F32)]
    return pl.pallas_call(
        functools.partial(_proj_kernel, tm=tm, full=full),
        grid=(bsz, nt),
        in_specs=x_specs + [mod_spec, _const_spec(g1.shape)] + w_specs,
        out_specs=[o[0] for o in outs],
        out_shape=[o[1] for o in outs],
        compiler_params=pltpu.CompilerParams(dimension_semantics=("parallel", "parallel"),
                                             vmem_limit_bytes=VMEM_LIMIT),
        name="proj_full" if full else "proj_ctx",
    )(xs, xs, xs, mod, g1, *weights)


def _scan_lanes(v, combine, fill, forward):
    lane = lax.broadcasted_iota(jnp.int32, v.shape, 1)
    d = 1
    while d < LANES:
        if forward:
            shifted = jnp.where(lane >= d, pltpu.roll(v, d, 1), fill)
        else:
            shifted = jnp.where(lane < LANES - d, pltpu.roll(v, LANES - d, 1), fill)
        v = combine(v, shifted)
        d *= 2
    return v


def _gates_kernel(gl_ref, gc_ref, out_ref, *, n_ctx, n_lat):
    n_all = n_ctx + n_lat

    def load_t(cg):
        if cg < n_ctx:
            tile = gc_ref[0, cg * CHUNK:(cg + 1) * CHUNK, :]
        else:
            tile = gl_ref[0, (cg - n_ctx) * CHUNK:(cg - n_ctx + 1) * CHUNK, :]
        return tile.T

    for dirn in range(2):
        forward = dirn == 0
        if forward:
            order = list(range(n_all))
        else:
            order = list(range(n_ctx - 1, -1, -1)) + list(range(n_all - 1, n_ctx - 1, -1))
        last = LANES - 1 if forward else 0
        m = jnp.full((N_HEADS, 1), M_INIT, F32)
        for cg in order:
            tt = load_t(cg)
            ig = tt[2 * N_HEADS * dirn: 2 * N_HEADS * dirn + N_HEADS]
            fg = tt[2 * N_HEADS * dirn + N_HEADS: 2 * N_HEADS * (dirn + 1)]
            lf = jnp.minimum(fg, 0.0) - jnp.log1p(jnp.exp(-jnp.abs(fg)))
            b = _scan_lanes(lf, jnp.add, 0.0, forward)
            r = ig - b
            mx = jnp.maximum(m, _scan_lanes(r, jnp.maximum, -jnp.inf, forward))
            mt = b + mx
            m_new = mt[:, last:last + 1]
            b_end = b[:, last:last + 1]
            cbase = C_BASE + C_PER_DIR * dirn
            fields = {
                R_ROWB + dirn: r,
                R_DECAY + dirn: jnp.broadcast_to(jnp.exp(b_end + m - m_new), (N_HEADS, CHUNK)),
                cbase + F_COLA: -mx,
                cbase + F_AINT: jnp.exp(m - mx),
                cbase + F_ENEG: jnp.exp(-mt),
                cbase + F_WK: jnp.exp(b_end + r - m_new),
            }
            for f, val in fields.items():
                for h in range(N_HEADS):
                    out_ref[0, h, cg, f:f + 1, :] = val[h:h + 1, :]
            m = m_new
    n_used = C_BASE + 2 * C_PER_DIR
    out_ref[0, :, :, n_used:, :] = jnp.zeros((N_HEADS, n_all, N_FIELDS - n_used, CHUNK), F32)


def _gates(g_lat, g_ctx):
    bsz, t_lat, _ = g_lat.shape
    t_ctx = g_ctx.shape[1]
    n_all = (t_lat + t_ctx) // CHUNK
    shape = (bsz, N_HEADS, n_all, N_FIELDS, CHUNK)
    return pl.pallas_call(
        functools.partial(_gates_kernel, n_ctx=t_ctx // CHUNK, n_lat=t_lat // CHUNK),
        grid=(bsz,),
        in_specs=[pl.BlockSpec((1, t_lat, LANES), lambda b: (b, 0, 0)),
                  pl.BlockSpec((1, t_ctx, LANES), lambda b: (b, 0, 0))],
        out_specs=pl.BlockSpec((1,) + shape[1:], lambda b: (b, 0, 0, 0, 0)),
        out_shape=jax.ShapeDtypeStruct(shape, F32),
        compiler_params=pltpu.CompilerParams(dimension_semantics=("parallel",)),
        name="gates",
    )(g_lat, g_ctx)


STATE_ROWS = HEAD_DIM + BF16_ROWS


def _mlstm_kernel(qt_ref, kl_ref, vtl_ref, kc_ref, vtc_ref, gf_ref, og_ref, ng_ref, out_ref,
                  dz_s, zf_s, zb_s, z_s, p_s, h_s, *, n_ctx, n_lat):
    lane = lax.broadcasted_iota(jnp.int32, (CHUNK, CHUNK), 1)
    row = lax.broadcasted_iota(jnp.int32, (CHUNK, CHUNK), 0)
    tri = (row <= lane, row >= lane)
    tail_first = lax.broadcasted_iota(jnp.int32, (BF16_ROWS, CHUNK), 0) == 0
    ones_tail = jnp.where(tail_first, 1.0, 0.0).astype(BF16)

    def field(cg, dirn, f):
        j = C_BASE + C_PER_DIR * dirn + f
        return gf_ref[0, 0, cg, j:j + 1, :]

    def pass_a(cg, k, vt):
        vf = vt.astype(F32)
        parts = []
        for dirn in range(2):
            wk = field(cg, dirn, F_WK)
            parts += [(vf * wk).astype(BF16), jnp.where(tail_first, wk, 0.0).astype(BF16)]
        dz_s[cg] = _dot(jnp.concatenate(parts, axis=0), k)

    for cg in range(n_ctx):
        pass_a(cg, kc_ref[0, cg * CHUNK:(cg + 1) * CHUNK, :], vtc_ref[0, 0, cg])

    def pass_a_lat(i, carry):
        sl = pl.ds(pl.multiple_of(i * CHUNK, CHUNK), CHUNK)
        pass_a(n_ctx + i, kl_ref[0, sl, :], vtl_ref[0, 0, i])
        return carry

    lax.fori_loop(0, n_lat, pass_a_lat, 0, unroll=2)

    def advance(dirn, cg):
        dec = gf_ref[0, 0, cg, R_DECAY + dirn:R_DECAY + dirn + 1, :]
        z_s[dirn] = dec * z_s[dirn] + dz_s[cg, dirn * STATE_ROWS:(dirn + 1) * STATE_ROWS, :]

    z_s[...] = jnp.zeros_like(z_s)
    for cg in range(n_ctx):
        advance(0, cg)
    for cg in range(n_ctx - 1, -1, -1):
        advance(1, cg)

    def pass_b(i, carry):
        zf_s[i] = z_s[0].astype(BF16)
        advance(0, n_ctx + i)
        j = n_lat - 1 - i
        zb_s[j] = z_s[1].astype(BF16)
        advance(1, n_ctx + j)
        return carry

    lax.fori_loop(0, n_lat, pass_b, 0, unroll=2)

    def pass_c1(i, carry):
        cg = n_ctx + i
        sl = pl.ds(pl.multiple_of(i * CHUNK, CHUNK), CHUNK)
        st = _dot(kl_ref[0, sl, :], qt_ref[0, 0, i])
        rowb = gf_ref[0, 0, cg].T
        for dirn in range(2):
            dmat = rowb[:, R_ROWB + dirn:R_ROWB + dirn + 1] + field(cg, dirn, F_COLA)
            w = jnp.exp(jnp.where(tri[dirn], dmat, -jnp.inf))
            p_s[i, dirn] = (st * w).astype(BF16)
        return carry

    lax.fori_loop(0, n_lat, pass_c1, 0, unroll=4)

    def pass_c2(i, carry):
        cg = n_ctx + i
        qtf = qt_ref[0, 0, i].astype(F32)
        vaug = jnp.concatenate([vtl_ref[0, 0, i], ones_tail], axis=0)
        ht = None
        for dirn, z_ref in enumerate((zf_s, zb_s)):
            rhs = jnp.concatenate([(qtf * field(cg, dirn, F_AINT)).astype(BF16), p_s[i, dirn]], axis=0)
            res = _dot(jnp.concatenate([z_ref[i], vaug], axis=1), rhs)
            den = jnp.maximum(jnp.abs(res[HEAD_DIM:HEAD_DIM + 1, :]), field(cg, dirn, F_ENEG))
            hd = res[:HEAD_DIM, :] / den
            ht = hd if ht is None else ht + hd
        h_s[i] = ht * lax.rsqrt(jnp.mean(ht * ht, axis=0, keepdims=True) + EPS)
        return carry

    lax.fori_loop(0, n_lat, pass_c2, 0, unroll=4)

    def pass_c3(i, carry):
        sl = pl.ds(pl.multiple_of(i * CHUNK, CHUNK), CHUNK)
        out_ref[0, sl, :] = (h_s[i].T * ng_ref[...] * og_ref[0, sl, :].astype(F32)).astype(BF16)
        return carry

    lax.fori_loop(0, n_lat, pass_c3, 0, unroll=4)


def _mlstm(qt_l, k_l, vt_l, k_c, vt_c, gf, og, ng):
    bsz, t_lat, _ = k_l.shape
    t_ctx = k_c.shape[1]
    n_lat, n_ctx = t_lat // CHUNK, t_ctx // CHUNK
    n_all = n_lat + n_ctx
    head_lat = pl.BlockSpec((1, t_lat, HEAD_DIM), lambda b, h: (b, 0, h))
    head_ctx = pl.BlockSpec((1, t_ctx, HEAD_DIM), lambda b, h: (b, 0, h))
    tiles = lambda n: pl.BlockSpec((1, 1, n, HEAD_DIM, CHUNK), lambda b, h: (b, h, 0, 0, 0))
    return pl.pallas_call(
        functools.partial(_mlstm_kernel, n_ctx=n_ctx, n_lat=n_lat),
        grid=(bsz, N_HEADS),
        in_specs=[tiles(n_lat), head_lat, tiles(n_lat), head_ctx, tiles(n_ctx),
                  pl.BlockSpec((1, 1, n_all, N_FIELDS, CHUNK), lambda b, h: (b, h, 0, 0, 0)),
                  head_lat,
                  pl.BlockSpec((1, HEAD_DIM), lambda b, h: (0, h))],
        out_specs=head_lat,
        out_shape=jax.ShapeDtypeStruct((bsz, t_lat, D_MODEL), BF16),
        scratch_shapes=[pltpu.VMEM((n_all, 2 * STATE_ROWS, HEAD_DIM), F32),
                        pltpu.VMEM((n_lat, STATE_ROWS, HEAD_DIM), BF16),
                        pltpu.VMEM((n_lat, STATE_ROWS, HEAD_DIM), BF16),
                        pltpu.VMEM((2, STATE_ROWS, HEAD_DIM), F32),
                        pltpu.VMEM((n_lat, 2, CHUNK, CHUNK), BF16),
                        pltpu.VMEM((n_lat, HEAD_DIM, CHUNK), F32)],
        compiler_params=pltpu.CompilerParams(dimension_semantics=("parallel", "parallel"),
                                             vmem_limit_bytes=VMEM_LIMIT),
        name="mlstm",
    )(qt_l, k_l, vt_l, k_c, vt_c, gf, og, ng)


def _merge_kernel(hg_ref, gm_ref, zc_ref, x_ref, mod_ref, g2_ref, wmo_ref, wo_ref, x1_ref, h2_ref):
    ym = _dot(hg_ref[0], wmo_ref[...])
    z = (gm_ref[0].astype(F32) * ym + zc_ref[0]).astype(BF16)
    x1 = x_ref[0] + mod_ref[0, 2:3, :] * _dot(z, wo_ref[...])
    x1_ref[0] = x1
    h2 = _rms_scale(x1) * g2_ref[...] * (1.0 + mod_ref[0, 4:5, :]) + mod_ref[0, 3:4, :]
    h2_ref[0] = h2.astype(BF16)


def _merge(hg, gm, zc, x, mod, g2, wmo, wo):
    bsz, t_len, _ = x.shape
    tm = MERGE_TM
    tok = pl.BlockSpec((1, tm, D_MODEL), lambda b, t: (b, t, 0))
    return pl.pallas_call(
        _merge_kernel,
        grid=(bsz, t_len // tm),
        in_specs=[tok, tok, tok, tok, pl.BlockSpec((1, 8, D_MODEL), lambda b, t: (b, 0, 0)),
                  _const_spec(g2.shape), _const_spec(wmo.shape), _const_spec(wo.shape)],
        out_specs=[tok, tok],
        out_shape=[jax.ShapeDtypeStruct(x.shape, F32), jax.ShapeDtypeStruct(x.shape, BF16)],
        compiler_params=pltpu.CompilerParams(dimension_semantics=("parallel", "parallel"),
                                             vmem_limit_bytes=VMEM_LIMIT),
        name="merge",
    )(hg, gm, zc, x, mod, g2, wmo, wo)


def _gelu_tanh(v):
    return 0.5 * v * (1.0 + jnp.tanh(0.7978845608028654 * (v + 0.044715 * (v * v * v))))


def _ffn_kernel(hm_ref, hp_ref, hn_ref, x1_ref, mod_ref, wa_ref, wg_ref, cw_ref, cb_ref, wd_ref, fg_ref,
                out_ref, act_s, *, tm):
    t = pl.program_id(1)
    nt = pl.num_programs(1)
    n = tm + 2 * GRID_W
    hm = hm_ref[0]
    hp = jnp.where(t > 0, hp_ref[0], jnp.zeros_like(hp_ref[0]))
    hn = jnp.where(t < nt - 1, hn_ref[0], jnp.zeros_like(hn_ref[0]))
    he = jnp.concatenate([hp, hm, hn], axis=0)
    gcol = lax.broadcasted_iota(jnp.int32, (n, 1), 0) & (GRID_W - 1)
    has_left = gcol != 0
    has_right = gcol != GRID_W - 1
    n_chunks = FF_HIDDEN // FFN_CW
    cols = lambda j: slice(j * FFN_CW, (j + 1) * FFN_CW)
    up = lambda j: (_dot(he, wa_ref[:, cols(j)]), _dot(hm, wg_ref[:, cols(j)]))
    acc = None
    nxt = up(0)
    for j in range(n_chunks):
        cs = cols(j)
        a, g = nxt
        if j + 1 < n_chunks:
            nxt = up(j + 1)
        taps = (jnp.where(has_left, pltpu.roll(a, 1, 0), 0.0), a,
                jnp.where(has_right, pltpu.roll(a, n - 1, 0), 0.0))
        conv = cb_ref[:, cs]
        for dr in range(3):
            for dc in range(3):
                conv = conv + cw_ref[3 * dr + dc: 3 * dr + dc + 1, cs] * taps[dc][GRID_W * dr: GRID_W * dr + tm]
        act_s[:, cs] = (_gelu_tanh(conv) * g).astype(BF16)
        if (j + 1) % FFN_DOWN_GROUP == 0 or j + 1 == n_chunks:
            gs = slice((j // FFN_DOWN_GROUP) * FFN_DOWN_GROUP * FFN_CW, (j + 1) * FFN_CW)
            part = _dot(act_s[:, gs], wd_ref[gs, :])
            acc = part if acc is None else acc + part
    x2 = x1_ref[0] + mod_ref[0, 5:6, :] * acc
    out_ref[0] = _rms_scale(x2) * fg_ref[...]


def _ffn(h2, x1, mod, wa, wg, cw, cb, wd, fg):
    bsz, t_len, _ = x1.shape
    tm = FFN_TM
    rb = tm // GRID_W
    nrb = t_len // GRID_W
    tok = pl.BlockSpec((1, tm, D_MODEL), lambda b, t: (b, t, 0))
    return pl.pallas_call(
        functools.partial(_ffn_kernel, tm=tm),
        grid=(bsz, t_len // tm),
        in_specs=[tok,
                  pl.BlockSpec((1, GRID_W, D_MODEL), lambda b, t: (b, jnp.maximum(t * rb - 1, 0), 0)),
                  pl.BlockSpec((1, GRID_W, D_MODEL), lambda b, t: (b, jnp.minimum((t + 1) * rb, nrb - 1), 0)),
                  tok, pl.BlockSpec((1, 8, D_MODEL), lambda b, t: (b, 0, 0)),
                  _const_spec(wa.shape), _const_spec(wg.shape), _const_spec(cw.shape), _const_spec(cb.shape),
                  _const_spec(wd.shape), _const_spec(fg.shape)],
        out_specs=tok,
        out_shape=jax.ShapeDtypeStruct(x1.shape, F32),
        scratch_shapes=[pltpu.VMEM((tm, FF_HIDDEN), BF16)],
        compiler_params=pltpu.CompilerParams(dimension_semantics=("parallel", "parallel"),
                                             vmem_limit_bytes=VMEM_LIMIT),
        name="ffn",
    )(h2, h2, h2, x1, mod, wa, wg, cw, cb, wd, fg)


def kernel(x, c, ctx, c_ctx, ada_w, ada_b, norm1_g, norm2_g, w_in, qk_conv_w, qk_conv_b, gate_b, mnorm_g,
           w_m_out, sc_conv_w, sc_conv_b, w_c_out, w_o, w_up, ff_conv_w, ff_conv_b, w_down, final_g):
    assert ada_w.shape[0] == 1, "single-layer block"
    bsz, t_lat, _ = x.shape
    t_ctx = ctx.shape[1]
    n_all = (t_lat + t_ctx) // CHUNK

    cc = jnp.zeros((2 * 8, D_MODEL), F32).at[:bsz].set(c).at[bsz].set(c_ctx)
    mod = _ada(cc, ada_w[0], ada_b[0][None, :])
    mod_x = jnp.pad(mod[:bsz].reshape(bsz, 6, D_MODEL), ((0, 0), (0, 2), (0, 0)))
    mod_c = jnp.pad(mod[bsz].reshape(1, 6, D_MODEL), ((0, 0), (0, 2), (0, 0)))

    w = w_in[0]
    m_w = D_MODEL
    off_g = 3 * m_w
    off_o = off_g + N_GATES
    off_sb = off_o + m_w
    off_sc = off_sb + D_MODEL
    off_mg = off_sc + 2 * D_MODEL
    cast = lambda a: a.astype(BF16)
    wqk, wv = cast(w[:, :2 * m_w]), cast(w[:, 2 * m_w:off_g])
    wgt = cast(jnp.pad(w[:, off_g:off_o], ((0, 0), (0, LANES - N_GATES))))
    gb = jnp.pad(gate_b[0], (0, LANES - N_GATES))[None, :]
    g1 = norm1_g[0][None, :]
    ctx_w = [wqk, wv, wgt, gb, qk_conv_w[0], qk_conv_b[0][None, :]]
    lat_w = ctx_w + [cast(w[:, off_o:off_sb]), cast(w[:, off_sb:off_sc]), cast(w[:, off_sc:off_mg]),
                     sc_conv_w[0], sc_conv_b[0][None, :], cast(w_c_out[0]), cast(w[:, off_mg:])]

    k_l, vt_l, g_l, qt_l, og, gm, zc = _proj(x, mod_x, g1, lat_w, full=True, per_batch_mod=True)
    k_c, vt_c, g_c = _proj(ctx, mod_c, g1, ctx_w, full=False, per_batch_mod=False)

    gf = _gates(g_l, g_c)
    hg = _mlstm(qt_l, k_l, vt_l, k_c, vt_c, gf, og, mnorm_g[0][None, :])

    x1, h2 = _merge(hg, gm, zc, x, mod_x, norm2_g[0][None, :], cast(w_m_out[0]), cast(w_o[0]))

    wu = w_up[0]
    out = _ffn(h2, x1, mod_x, cast(wu[:, :FF_HIDDEN]), cast(wu[:, FF_HIDDEN:]),
               ff_conv_w[0].reshape(9, FF_HIDDEN), ff_conv_b[0][None, :], cast(w_down[0]),
               final_g[None, :])
    return out
```

```python
import functools

import jax
import jax.numpy as jnp
from jax import lax
from jax.experimental import pallas as pl
from jax.experimental.pallas import tpu as pltpu

F32 = jnp.float32
BF16 = jnp.bfloat16

D_MODEL = 1024
N_HEADS = 8
HEAD_DIM = D_MODEL // N_HEADS
CHUNK = 128
GRID_W = 64
FF_HIDDEN = 2816
EPS = 1e-6
M_INIT = -1e30
N_GATES = 4 * N_HEADS

LANES = 128
BF16_ROWS = 16
VMEM_LIMIT = 56 * 1024 * 1024

PROJ_TM = 256
SEQ_HALO = BF16_ROWS
MERGE_TM = 512
FFN_TM = 512
FFN_CW = 256
FFN_DOWN_GROUP = 4

R_ROWB, R_DECAY = 0, 2
C_BASE, C_PER_DIR = 4, 4
F_COLA, F_AINT, F_ENEG, F_WK = range(4)
N_FIELDS = 16


def _const_spec(shape):
    nd = len(shape)
    return pl.BlockSpec(shape, lambda *_: (0,) * nd, pipeline_mode=pl.Buffered(1))


def _sigmoid(v):
    return 1.0 / (1.0 + jnp.exp(-v))


def _rms_scale(v):
    return v * lax.rsqrt(jnp.mean(v * v, axis=-1, keepdims=True) + EPS)


def _dot(a, b):
    return jnp.dot(a, b, preferred_element_type=F32)


def _ada_kernel(c_ref, w_ref, b_ref, o_ref):
    cv = c_ref[...]
    s = cv * _sigmoid(cv)
    o_ref[...] = jnp.dot(s, w_ref[...], preferred_element_type=F32,
                         precision=lax.Precision.HIGHEST) + b_ref[...]


def _ada(cc, w, b):
    rows, n = cc.shape[0], w.shape[1]
    tn = 1024
    return pl.pallas_call(
        _ada_kernel,
        grid=(n // tn,),
        in_specs=[pl.BlockSpec((rows, D_MODEL), lambda j: (0, 0)),
                  pl.BlockSpec((D_MODEL, tn), lambda j: (0, j)),
                  pl.BlockSpec((1, tn), lambda j: (0, j))],
        out_specs=pl.BlockSpec((rows, tn), lambda j: (0, j)),
        out_shape=jax.ShapeDtypeStruct((rows, n), F32),
        name="ada",
    )(cc, w, b)


def _conv3_rows(p, w_ref, b_ref, tm):
    n = tm + 2 * SEQ_HALO
    lo, hi = SEQ_HALO, SEQ_HALO + tm
    left = pltpu.roll(p, 1, 0)[lo:hi]
    right = pltpu.roll(p, n - 1, 0)[lo:hi]
    return left * w_ref[0:1, :] + p[lo:hi] * w_ref[1:2, :] + right * w_ref[2:3, :] + b_ref[...]


def _store_head_tiles_t(ref, val, tm):
    for h in range(N_HEADS):
        for ci in range(tm // CHUNK):
            tile = val[ci * CHUNK:(ci + 1) * CHUNK, h * HEAD_DIM:(h + 1) * HEAD_DIM]
            ref[0, h, ci] = tile.T.astype(BF16)


def _proj_kernel(*refs, tm, full):
    if full:
        (xm_ref, xp_ref, xn_ref, mod_ref, g1_ref, wqk_ref, wv_ref, wg_ref, gb_ref, qkw_ref, qkb_ref,
         wo_ref, wsb_ref, wscx_ref, scw_ref, scb_ref, wco_ref, wmg_ref,
         k_ref, vt_ref, g_ref, qt_ref, ogt_ref, gm_ref, zc_ref) = refs
    else:
        (xm_ref, xp_ref, xn_ref, mod_ref, g1_ref, wqk_ref, wv_ref, wg_ref, gb_ref, qkw_ref, qkb_ref,
         k_ref, vt_ref, g_ref) = refs
    t = pl.program_id(1)
    nt = pl.num_programs(1)
    shift = mod_ref[0, 0:1, :]
    scale1 = 1.0 + mod_ref[0, 1:2, :]
    gain = g1_ref[...]

    def norm_mod(xv):
        return _rms_scale(xv) * gain * scale1 + shift

    hm = norm_mod(xm_ref[0])
    hp = jnp.where(t > 0, norm_mod(xp_ref[0]), 0.0)
    hn = jnp.where(t < nt - 1, norm_mod(xn_ref[0]), 0.0)
    hmb = hm.astype(BF16)
    he = jnp.concatenate([hp, hm, hn], axis=0).astype(BF16)

    pqk = _dot(he, wqk_ref[...])
    pv = _dot(hmb, wv_ref[...])
    qk = _conv3_rows(pqk, qkw_ref, qkb_ref, tm)
    qk = qk * _sigmoid(qk)
    k_ref[0] = (qk[:, D_MODEL:] * (HEAD_DIM ** -0.5)).astype(BF16)
    pg = _dot(hmb, wg_ref[...])
    if not full:
        _store_head_tiles_t(vt_ref, pv, tm)
        g_ref[0] = pg + gb_ref[...]
        return
    po = _dot(hmb, wo_ref[...])
    _store_head_tiles_t(qt_ref, qk[:, :D_MODEL], tm)
    _store_head_tiles_t(vt_ref, pv, tm)
    g_ref[0] = pg + gb_ref[...]
    pcx = _dot(he, wscx_ref[...])
    _store_head_tiles_t(ogt_ref, _sigmoid(po), tm)
    psb = _dot(hmb, wsb_ref[...])
    cu = _conv3_rows(pcx[:, :D_MODEL] * pcx[:, D_MODEL:], scw_ref, scb_ref, tm)
    pm = _dot(hmb, wmg_ref[...])
    yc = _dot((psb * cu).astype(BF16), wco_ref[...])
    gm_ref[0] = _sigmoid(pm[:, :D_MODEL]).astype(BF16)
    zc_ref[0] = (_sigmoid(pm[:, D_MODEL:]) * yc).astype(BF16)


def _proj(xs, mod, g1, weights, *, full, per_batch_mod):
    bsz, t_len, _ = xs.shape
    tm = min(PROJ_TM, t_len)
    nt = t_len // tm
    hb = tm // SEQ_HALO
    nhb = t_len // SEQ_HALO
    x_specs = [
        pl.BlockSpec((1, tm, D_MODEL), lambda b, t: (b, t, 0)),
        pl.BlockSpec((1, SEQ_HALO, D_MODEL), lambda b, t: (b, jnp.maximum(t * hb - 1, 0), 0)),
        pl.BlockSpec((1, SEQ_HALO, D_MODEL), lambda b, t: (b, jnp.minimum((t + 1) * hb, nhb - 1), 0)),
    ]
    mod_spec = pl.BlockSpec((1, 8, D_MODEL), (lambda b, t: (b, 0, 0)) if per_batch_mod else (lambda b, t: (0, 0, 0)))
    w_specs = [_const_spec(w.shape) for w in weights]
    tok = lambda n, dt: (pl.BlockSpec((1, tm, n), lambda b, t: (b, t, 0)),
                         jax.ShapeDtypeStruct((bsz, t_len, n), dt))
    tiles_t = (pl.BlockSpec((1, N_HEADS, tm // CHUNK, HEAD_DIM, CHUNK), lambda b, t: (b, 0, t, 0, 0)),
               jax.ShapeDtypeStruct((bsz, N_HEADS, t_len // CHUNK, HEAD_DIM, CHUNK), BF16))
    outs = [tok(D_MODEL, BF16), tiles_t, tok(LANES, F32)]
    if full:
        outs += [tiles_t, tiles_t, tok(D_MODEL, BF16), tok(D_MODEL, BF16)]
    return pl.pallas_call(
        functools.partial(_proj_kernel, tm=tm, full=full),
        grid=(bsz, nt),
        in_specs=x_specs + [mod_spec, _const_spec(g1.shape)] + w_specs,
        out_specs=[o[0] for o in outs],
        out_shape=[o[1] for o in outs],
        compiler_params=pltpu.CompilerParams(dimension_semantics=("parallel", "parallel"),
                                             vmem_limit_bytes=VMEM_LIMIT),
        name="proj_full" if full else "proj_ctx",
    )(xs, xs, xs, mod, g1, *weights)


def _scan_lanes(v, combine, fill, forward):
    lane = lax.broadcasted_iota(jnp.int32, v.shape, 1)
    d = 1
    while d < LANES:
        if forward:
            shifted = jnp.where(lane >= d, pltpu.roll(v, d, 1), fill)
        else:
            shifted = jnp.where(lane < LANES - d, pltpu.roll(v, LANES - d, 1), fill)
        v = combine(v, shifted)
        d *= 2
    return v


def _gates_kernel(gl_ref, gc_ref, out_ref, *, n_ctx, n_lat):
    n_all = n_ctx + n_lat

    def load_t(cg):
        if cg < n_ctx:
            tile = gc_ref[0, cg * CHUNK:(cg + 1) * CHUNK, :]
        else:
            tile = gl_ref[0, (cg - n_ctx) * CHUNK:(cg - n_ctx + 1) * CHUNK, :]
        return tile.T

    for dirn in range(2):
        forward = dirn == 0
        if forward:
            order = list(range(n_all))
        else:
            order = list(range(n_ctx - 1, -1, -1)) + list(range(n_all - 1, n_ctx - 1, -1))
        last = LANES - 1 if forward else 0
        m = jnp.full((N_HEADS, 1), M_INIT, F32)
        for cg in order:
            tt = load_t(cg)
            ig = tt[2 * N_HEADS * dirn: 2 * N_HEADS * dirn + N_HEADS]
            fg = tt[2 * N_HEADS * dirn + N_HEADS: 2 * N_HEADS * (dirn + 1)]
            lf = jnp.minimum(fg, 0.0) - jnp.log1p(jnp.exp(-jnp.abs(fg)))
            b = _scan_lanes(lf, jnp.add, 0.0, forward)
            r = ig - b
            mx = jnp.maximum(m, _scan_lanes(r, jnp.maximum, -jnp.inf, forward))
            mt = b + mx
            m_new = mt[:, last:last + 1]
            b_end = b[:, last:last + 1]
            cbase = C_BASE + C_PER_DIR * dirn
            fields = {
                R_ROWB + dirn: r,
                R_DECAY + dirn: jnp.broadcast_to(jnp.exp(b_end + m - m_new), (N_HEADS, CHUNK)),
                cbase + F_COLA: -mx,
                cbase + F_AINT: jnp.exp(m - mx),
                cbase + F_ENEG: jnp.exp(-mt),
                cbase + F_WK: jnp.exp(b_end + r - m_new),
            }
            for f, val in fields.items():
                for h in range(N_HEADS):
                    out_ref[0, h, cg, f:f + 1, :] = val[h:h + 1, :]
            m = m_new
    n_used = C_BASE + 2 * C_PER_DIR
    out_ref[0, :, :, n_used:, :] = jnp.zeros((N_HEADS, n_all, N_FIELDS - n_used, CHUNK), F32)


def _gates(g_lat, g_ctx):
    bsz, t_lat, _ = g_lat.shape
    t_ctx = g_ctx.shape[1]
    n_all = (t_lat + t_ctx) // CHUNK
    shape = (bsz, N_HEADS, n_all, N_FIELDS, CHUNK)
    return pl.pallas_call(
        functools.partial(_gates_kernel, n_ctx=t_ctx // CHUNK, n_lat=t_lat // CHUNK),
        grid=(bsz,),
        in_specs=[pl.BlockSpec((1, t_lat, LANES), lambda b: (b, 0, 0)),
                  pl.BlockSpec((1, t_ctx, LANES), lambda b: (b, 0, 0))],
        out_specs=pl.BlockSpec((1,) + shape[1:], lambda b: (b, 0, 0, 0, 0)),
        out_shape=jax.ShapeDtypeStruct(shape, F32),
        compiler_params=pltpu.CompilerParams(dimension_semantics=("parallel",)),
        name="gates",
    )(g_lat, g_ctx)


STATE_ROWS = HEAD_DIM + BF16_ROWS


def _mlstm_kernel(qt_ref, kl_ref, vtl_ref, kc_ref, vtc_ref, gf_ref, ogt_ref, ng_ref, out_ref,
                  dz_s, zf_s, zb_s, z_s, p_s, *, n_ctx, n_lat):
    lane = lax.broadcasted_iota(jnp.int32, (CHUNK, CHUNK), 1)
    row = lax.broadcasted_iota(jnp.int32, (CHUNK, CHUNK), 0)
    tri = (row <= lane, row >= lane)
    tail_first = lax.broadcasted_iota(jnp.int32, (BF16_ROWS, CHUNK), 0) == 0
    ones_tail = jnp.where(tail_first, 1.0, 0.0).astype(BF16)

    def field(cg, dirn, f):
        j = C_BASE + C_PER_DIR * dirn + f
        return gf_ref[0, 0, cg, j:j + 1, :]

    def pass_a(cg, k, vt):
        vf = vt.astype(F32)
        parts = []
        for dirn in range(2):
            wk = field(cg, dirn, F_WK)
            parts += [(vf * wk).astype(BF16), jnp.where(tail_first, wk, 0.0).astype(BF16)]
        dz_s[cg] = _dot(jnp.concatenate(parts, axis=0), k)

    for cg in range(n_ctx):
        pass_a(cg, kc_ref[0, cg * CHUNK:(cg + 1) * CHUNK, :], vtc_ref[0, 0, cg])

    def pass_a_lat(i, carry):
        sl = pl.ds(pl.multiple_of(i * CHUNK, CHUNK), CHUNK)
        pass_a(n_ctx + i, kl_ref[0, sl, :], vtl_ref[0, 0, i])
        return carry

    lax.fori_loop(0, n_lat, pass_a_lat, 0, unroll=8)

    def advance(dirn, cg):
        dec = gf_ref[0, 0, cg, R_DECAY + dirn:R_DECAY + dirn + 1, :]
        z_s[dirn] = dec * z_s[dirn] + dz_s[cg, dirn * STATE_ROWS:(dirn + 1) * STATE_ROWS, :]

    z_s[...] = jnp.zeros_like(z_s)
    for cg in range(n_ctx):
        advance(0, cg)
    for cg in range(n_ctx - 1, -1, -1):
        advance(1, cg)

    def pass_b(i, carry):
        zf_s[i] = z_s[0].astype(BF16)
        advance(0, n_ctx + i)
        j = n_lat - 1 - i
        zb_s[j] = z_s[1].astype(BF16)
        advance(1, n_ctx + j)
        return carry

    lax.fori_loop(0, n_lat, pass_b, 0, unroll=2)

    gain = jnp.broadcast_to(ng_ref[...], (CHUNK, HEAD_DIM)).T

    def pass_c1(i, carry):
        cg = n_ctx + i
        sl = pl.ds(pl.multiple_of(i * CHUNK, CHUNK), CHUNK)
        st = _dot(kl_ref[0, sl, :], qt_ref[0, 0, i])
        rowb = gf_ref[0, 0, cg].T
        for dirn in range(2):
            dmat = rowb[:, R_ROWB + dirn:R_ROWB + dirn + 1] + field(cg, dirn, F_COLA)
            w = jnp.exp(jnp.where(tri[dirn], dmat, -jnp.inf))
            p_s[i, dirn] = (st * w).astype(BF16)
        return carry

    lax.fori_loop(0, n_lat, pass_c1, 0, unroll=16)

    def pass_c2(i, carry):
        cg = n_ctx + i
        qtf = qt_ref[0, 0, i].astype(F32)
        vaug = jnp.concatenate([vtl_ref[0, 0, i], ones_tail], axis=0)
        ht = None
        for dirn, z_ref in enumerate((zf_s, zb_s)):
            rhs = jnp.concatenate([(qtf * field(cg, dirn, F_AINT)).astype(BF16), p_s[i, dirn]], axis=0)
            res = _dot(jnp.concatenate([z_ref[i], vaug], axis=1), rhs)
            den = jnp.maximum(jnp.abs(res[HEAD_DIM:HEAD_DIM + 1, :]), field(cg, dirn, F_ENEG))
            hd = res[:HEAD_DIM, :] / den
            ht = hd if ht is None else ht + hd
        ht = ht * lax.rsqrt(jnp.mean(ht * ht, axis=0, keepdims=True) + EPS)
        out_ref[0, 0, i] = (ht * gain * ogt_ref[0, 0, i].astype(F32)).astype(BF16)
        return carry

    lax.fori_loop(0, n_lat, pass_c2, 0, unroll=16)


def _mlstm(qt_l, k_l, vt_l, k_c, vt_c, gf, ogt, ng):
    bsz, t_lat, _ = k_l.shape
    t_ctx = k_c.shape[1]
    n_lat, n_ctx = t_lat // CHUNK, t_ctx // CHUNK
    n_all = n_lat + n_ctx
    head_lat = pl.BlockSpec((1, t_lat, HEAD_DIM), lambda b, h: (b, 0, h))
    head_ctx = pl.BlockSpec((1, t_ctx, HEAD_DIM), lambda b, h: (b, 0, h))
    tiles = lambda n: pl.BlockSpec((1, 1, n, HEAD_DIM, CHUNK), lambda b, h: (b, h, 0, 0, 0))
    return pl.pallas_call(
        functools.partial(_mlstm_kernel, n_ctx=n_ctx, n_lat=n_lat),
        grid=(bsz, N_HEADS),
        in_specs=[tiles(n_lat), head_lat, tiles(n_lat), head_ctx, tiles(n_ctx),
                  pl.BlockSpec((1, 1, n_all, N_FIELDS, CHUNK), lambda b, h: (b, h, 0, 0, 0)),
                  tiles(n_lat),
                  pl.BlockSpec((1, HEAD_DIM), lambda b, h: (0, h))],
        out_specs=tiles(n_lat),
        out_shape=jax.ShapeDtypeStruct(qt_l.shape, BF16),
        scratch_shapes=[pltpu.VMEM((n_all, 2 * STATE_ROWS, HEAD_DIM), F32),
                        pltpu.VMEM((n_lat, STATE_ROWS, HEAD_DIM), BF16),
                        pltpu.VMEM((n_lat, STATE_ROWS, HEAD_DIM), BF16),
                        pltpu.VMEM((2, STATE_ROWS, HEAD_DIM), F32),
                        pltpu.VMEM((n_lat, 2, CHUNK, CHUNK), BF16)],
        compiler_params=pltpu.CompilerParams(dimension_semantics=("parallel", "parallel"),
                                             vmem_limit_bytes=VMEM_LIMIT),
        name="mlstm",
    )(qt_l, k_l, vt_l, k_c, vt_c, gf, ogt, ng)


def _merge_kernel(hgt_ref, gm_ref, zc_ref, x_ref, mod_ref, g2_ref, wmo_ref, wo_ref, x1_ref, h2_ref):
    hg = jnp.concatenate(
        [jnp.concatenate([hgt_ref[0, h, ci].astype(F32).T for h in range(N_HEADS)], axis=1)
         for ci in range(hgt_ref.shape[2])], axis=0).astype(BF16)
    ym = _dot(hg, wmo_ref[...])
    z = (gm_ref[0].astype(F32) * ym + zc_ref[0].astype(F32)).astype(BF16)
    x1 = x_ref[0] + mod_ref[0, 2:3, :] * _dot(z, wo_ref[...])
    x1_ref[0] = x1
    h2 = _rms_scale(x1) * g2_ref[...] * (1.0 + mod_ref[0, 4:5, :]) + mod_ref[0, 3:4, :]
    h2_ref[0] = h2.astype(BF16)


def _merge(hgt, gm, zc, x, mod, g2, wmo, wo):
    bsz, t_len, _ = x.shape
    tm = MERGE_TM
    tok = pl.BlockSpec((1, tm, D_MODEL), lambda b, t: (b, t, 0))
    tiles = pl.BlockSpec((1, N_HEADS, tm // CHUNK, HEAD_DIM, CHUNK), lambda b, t: (b, 0, t, 0, 0))
    return pl.pallas_call(
        _merge_kernel,
        grid=(bsz, t_len // tm),
        in_specs=[tiles, tok, tok, tok, pl.BlockSpec((1, 8, D_MODEL), lambda b, t: (b, 0, 0)),
                  _const_spec(g2.shape), _const_spec(wmo.shape), _const_spec(wo.shape)],
        out_specs=[tok, tok],
        out_shape=[jax.ShapeDtypeStruct(x.shape, F32), jax.ShapeDtypeStruct(x.shape, BF16)],
        compiler_params=pltpu.CompilerParams(dimension_semantics=("parallel", "parallel"),
                                             vmem_limit_bytes=VMEM_LIMIT),
        name="merge",
    )(hgt, gm, zc, x, mod, g2, wmo, wo)


def _gelu_tanh(v):
    return 0.5 * v * (1.0 + jnp.tanh(0.7978845608028654 * (v + 0.044715 * (v * v * v))))


def _ffn_kernel(hm_ref, hp_ref, hn_ref, x1_ref, mod_ref, wa_ref, wg_ref, cw_ref, cb_ref, wd_ref, fg_ref,
                out_ref, act_s, *, tm):
    t = pl.program_id(1)
    nt = pl.num_programs(1)
    n = tm + 2 * GRID_W
    hm = hm_ref[0]
    hp = jnp.where(t > 0, hp_ref[0], jnp.zeros_like(hp_ref[0]))
    hn = jnp.where(t < nt - 1, hn_ref[0], jnp.zeros_like(hn_ref[0]))
    he = jnp.concatenate([hp, hm, hn], axis=0)
    gcol = lax.broadcasted_iota(jnp.int32, (n, 1), 0) & (GRID_W - 1)
    has_left = gcol != 0
    has_right = gcol != GRID_W - 1
    n_chunks = FF_HIDDEN // FFN_CW
    cols = lambda j: slice(j * FFN_CW, (j + 1) * FFN_CW)
    up = lambda j: (_dot(he, wa_ref[:, cols(j)]), _dot(hm, wg_ref[:, cols(j)]))
    acc = None
    nxt = up(0)
    for j in range(n_chunks):
        cs = cols(j)
        a, g = nxt
        if j + 1 < n_chunks:
            nxt = up(j + 1)
        taps = (jnp.where(has_left, pltpu.roll(a, 1, 0), 0.0), a,
                jnp.where(has_right, pltpu.roll(a, n - 1, 0), 0.0))
        conv = cb_ref[:, cs]
        for dr in range(3):
            for dc in range(3):
                conv = conv + cw_ref[3 * dr + dc: 3 * dr + dc + 1, cs] * taps[dc][GRID_W * dr: GRID_W * dr + tm]
        act_s[:, cs] = (_gelu_tanh(conv) * g).astype(BF16)
        if (j + 1) % FFN_DOWN_GROUP == 0 or j + 1 == n_chunks:
            gs = slice((j // FFN_DOWN_GROUP) * FFN_DOWN_GROUP * FFN_CW, (j + 1) * FFN_CW)
            part = _dot(act_s[:, gs], wd_ref[gs, :])
            acc = part if acc is None else acc + part
    x2 = x1_ref[0] + mod_ref[0, 5:6, :] * acc
    out_ref[0] = _rms_scale(x2) * fg_ref[...]


def _ffn(h2, x1, mod, wa, wg, cw, cb, wd, fg):
    bsz, t_len, _ = x1.shape
    tm = FFN_TM
    rb = tm // GRID_W
    nrb = t_len // GRID_W
    tok = pl.BlockSpec((1, tm, D_MODEL), lambda b, t: (b, t, 0))
    return pl.pallas_call(
        functools.partial(_ffn_kernel, tm=tm),
        grid=(bsz, t_len // tm),
        in_specs=[tok,
                  pl.BlockSpec((1, GRID_W, D_MODEL), lambda b, t: (b, jnp.maximum(t * rb - 1, 0), 0)),
                  pl.BlockSpec((1, GRID_W, D_MODEL), lambda b, t: (b, jnp.minimum((t + 1) * rb, nrb - 1), 0)),
                  tok, pl.BlockSpec((1, 8, D_MODEL), lambda b, t: (b, 0, 0)),
                  _const_spec(wa.shape), _const_spec(wg.shape), _const_spec(cw.shape), _const_spec(cb.shape),
                  _const_spec(wd.shape), _const_spec(fg.shape)],
        out_specs=tok,
        out_shape=jax.ShapeDtypeStruct(x1.shape, F32),
        scratch_shapes=[pltpu.VMEM((tm, FF_HIDDEN), BF16)],
        compiler_params=pltpu.CompilerParams(dimension_semantics=("parallel", "parallel"),
                                             vmem_limit_bytes=VMEM_LIMIT),
        name="ffn",
    )(h2, h2, h2, x1, mod, wa, wg, cw, cb, wd, fg)


def kernel(x, c, ctx, c_ctx, ada_w, ada_b, norm1_g, norm2_g, w_in, qk_conv_w, qk_conv_b, gate_b, mnorm_g,
           w_m_out, sc_conv_w, sc_conv_b, w_c_out, w_o, w_up, ff_conv_w, ff_conv_b, w_down, final_g):
    assert ada_w.shape[0] == 1, "single-layer block"
    bsz, t_lat, _ = x.shape
    t_ctx = ctx.shape[1]
    n_all = (t_lat + t_ctx) // CHUNK

    cc = jnp.zeros((2 * 8, D_MODEL), F32).at[:bsz].set(c).at[bsz].set(c_ctx)
    mod = _ada(cc, ada_w[0], ada_b[0][None, :])
    mod_x = jnp.pad(mod[:bsz].reshape(bsz, 6, D_MODEL), ((0, 0), (0, 2), (0, 0)))
    mod_c = jnp.pad(mod[bsz].reshape(1, 6, D_MODEL), ((0, 0), (0, 2), (0, 0)))

    w = w_in[0]
    m_w = D_MODEL
    off_g = 3 * m_w
    off_o = off_g + N_GATES
    off_sb = off_o + m_w
    off_sc = off_sb + D_MODEL
    off_mg = off_sc + 2 * D_MODEL
    cast = lambda a: a.astype(BF16)
    wqk, wv = cast(w[:, :2 * m_w]), cast(w[:, 2 * m_w:off_g])
    wgt = cast(jnp.pad(w[:, off_g:off_o], ((0, 0), (0, LANES - N_GATES))))
    gb = jnp.pad(gate_b[0], (0, LANES - N_GATES))[None, :]
    g1 = norm1_g[0][None, :]
    ctx_w = [wqk, wv, wgt, gb, qk_conv_w[0], qk_conv_b[0][None, :]]
    lat_w = ctx_w + [cast(w[:, off_o:off_sb]), cast(w[:, off_sb:off_sc]), cast(w[:, off_sc:off_mg]),
                     sc_conv_w[0], sc_conv_b[0][None, :], cast(w_c_out[0]), cast(w[:, off_mg:])]

    k_l, vt_l, g_l, qt_l, ogt, gm, zc = _proj(x, mod_x, g1, lat_w, full=True, per_batch_mod=True)
    k_c, vt_c, g_c = _proj(ctx, mod_c, g1, ctx_w, full=False, per_batch_mod=False)

    gf = _gates(g_l, g_c)
    hgt = _mlstm(qt_l, k_l, vt_l, k_c, vt_c, gf, ogt, mnorm_g[0][None, :])

    x1, h2 = _merge(hgt, gm, zc, x, mod_x, norm2_g[0][None, :], cast(w_m_out[0]), cast(w_o[0]))

    wu = w_up[0]
    out = _ffn(h2, x1, mod_x, cast(wu[:, :FF_HIDDEN]), cast(wu[:, FF_HIDDEN:]),
               ff_conv_w[0].reshape(9, FF_HIDDEN), ff_conv_b[0][None, :], cast(w_down[0]),
               final_g[None, :])
    return out
```

```python
import functools

import jax
import jax.numpy as jnp
from jax import lax
from jax.experimental import pallas as pl
from jax.experimental.pallas import tpu as pltpu

F32 = jnp.float32
BF16 = jnp.bfloat16

D_MODEL = 1024
N_HEADS = 8
HEAD_DIM = D_MODEL // N_HEADS
CHUNK = 128
GRID_W = 64
FF_HIDDEN = 2816
EPS = 1e-6
M_INIT = -1e30
N_GATES = 4 * N_HEADS

LANES = 128
BF16_ROWS = 16
VMEM_LIMIT = 56 * 1024 * 1024

PROJ_TM = 256
SEQ_HALO = BF16_ROWS
MERGE_TM = 512
MERGE_SUB = 256
FFN_TM = 512
FFN_CW = 256
FFN_DOWN_GROUP = 4

R_ROWB, R_DECAY = 0, 2
C_BASE, C_PER_DIR = 4, 4
F_COLA, F_AINT, F_ENEG, F_WK = range(4)
N_FIELDS = 16


def _const_spec(shape):
    nd = len(shape)
    return pl.BlockSpec(shape, lambda *_: (0,) * nd, pipeline_mode=pl.Buffered(1))


def _sigmoid(v):
    return 1.0 / (1.0 + jnp.exp(-v))


def _rms_scale(v):
    return v * lax.rsqrt(jnp.mean(v * v, axis=-1, keepdims=True) + EPS)


def _dot(a, b):
    return jnp.dot(a, b, preferred_element_type=F32)


def _ada_kernel(c_ref, w_ref, b_ref, o_ref):
    cv = c_ref[...]
    s = cv * _sigmoid(cv)
    o_ref[...] = jnp.dot(s, w_ref[...], preferred_element_type=F32,
                         precision=lax.Precision.HIGHEST) + b_ref[...]


def _ada(cc, w, b):
    rows, n = cc.shape[0], w.shape[1]
    tn = 1024
    return pl.pallas_call(
        _ada_kernel,
        grid=(n // tn,),
        in_specs=[pl.BlockSpec((rows, D_MODEL), lambda j: (0, 0)),
                  pl.BlockSpec((D_MODEL, tn), lambda j: (0, j)),
                  pl.BlockSpec((1, tn), lambda j: (0, j))],
        out_specs=pl.BlockSpec((rows, tn), lambda j: (0, j)),
        out_shape=jax.ShapeDtypeStruct((rows, n), F32),
        name="ada",
    )(cc, w, b)


def _conv3_rows(p, w_ref, b_ref, tm):
    n = tm + 2 * SEQ_HALO
    lo, hi = SEQ_HALO, SEQ_HALO + tm
    left = pltpu.roll(p, 1, 0)[lo:hi]
    right = pltpu.roll(p, n - 1, 0)[lo:hi]
    return left * w_ref[0:1, :] + p[lo:hi] * w_ref[1:2, :] + right * w_ref[2:3, :] + b_ref[...]


def _store_head_tiles_t(ref, val, tm):
    for h in range(N_HEADS):
        for ci in range(tm // CHUNK):
            tile = val[ci * CHUNK:(ci + 1) * CHUNK, h * HEAD_DIM:(h + 1) * HEAD_DIM]
            ref[0, h, ci] = tile.T.astype(BF16)


def _proj_kernel(*refs, tm, full):
    if full:
        (xm_ref, xp_ref, xn_ref, mod_ref, g1_ref, wqk_ref, wv_ref, wg_ref, gb_ref, qkw_ref, qkb_ref,
         wo_ref, wsb_ref, wscx_ref, scw_ref, scb_ref, wco_ref, wmg_ref,
         k_ref, vt_ref, g_ref, qt_ref, ogt_ref, gm_ref, zc_ref) = refs
    else:
        (xm_ref, xp_ref, xn_ref, mod_ref, g1_ref, wqk_ref, wv_ref, wg_ref, gb_ref, qkw_ref, qkb_ref,
         k_ref, vt_ref, g_ref) = refs
    t = pl.program_id(1)
    nt = pl.num_programs(1)
    shift = mod_ref[0, 0:1, :]
    scale1 = 1.0 + mod_ref[0, 1:2, :]
    gain = g1_ref[...]

    def norm_mod(xv):
        return _rms_scale(xv) * gain * scale1 + shift

    hm = norm_mod(xm_ref[0])
    hp = jnp.where(t > 0, norm_mod(xp_ref[0]), 0.0)
    hn = jnp.where(t < nt - 1, norm_mod(xn_ref[0]), 0.0)
    hmb = hm.astype(BF16)
    he = jnp.concatenate([hp, hm, hn], axis=0).astype(BF16)

    pqk = _dot(he, wqk_ref[...])
    pv = _dot(hmb, wv_ref[...])
    qk = _conv3_rows(pqk, qkw_ref, qkb_ref, tm)
    qk = qk * _sigmoid(qk)
    k_ref[0] = (qk[:, D_MODEL:] * (HEAD_DIM ** -0.5)).astype(BF16)
    pg = lax.dot_general(wg_ref[...], hmb, (((1,), (1,)), ((), ())), preferred_element_type=F32)
    if not full:
        _store_head_tiles_t(vt_ref, pv, tm)
        g_ref[0] = pg + gb_ref[...]
        return
    po = _dot(hmb, wo_ref[...])
    _store_head_tiles_t(qt_ref, qk[:, :D_MODEL], tm)
    _store_head_tiles_t(vt_ref, pv, tm)
    g_ref[0] = pg + gb_ref[...]
    pcx = _dot(he, wscx_ref[...])
    _store_head_tiles_t(ogt_ref, _sigmoid(po), tm)
    psb = _dot(hmb, wsb_ref[...])
    cu = _conv3_rows(pcx[:, :D_MODEL] * pcx[:, D_MODEL:], scw_ref, scb_ref, tm)
    pm = _dot(hmb, wmg_ref[...])
    yc = _dot((psb * cu).astype(BF16), wco_ref[...])
    gm_ref[0] = _sigmoid(pm[:, :D_MODEL]).astype(BF16)
    zc_ref[0] = (_sigmoid(pm[:, D_MODEL:]) * yc).astype(BF16)


def _proj(xs, mod, g1, weights, *, full, per_batch_mod):
    bsz, t_len, _ = xs.shape
    tm = min(PROJ_TM, t_len)
    nt = t_len // tm
    hb = tm // SEQ_HALO
    nhb = t_len // SEQ_HALO
    x_specs = [
        pl.BlockSpec((1, tm, D_MODEL), lambda b, t: (b, t, 0)),
        pl.BlockSpec((1, SEQ_HALO, D_MODEL), lambda b, t: (b, jnp.maximum(t * hb - 1, 0), 0)),
        pl.BlockSpec((1, SEQ_HALO, D_MODEL), lambda b, t: (b, jnp.minimum((t + 1) * hb, nhb - 1), 0)),
    ]
    mod_spec = pl.BlockSpec((1, 8, D_MODEL), (lambda b, t: (b, 0, 0)) if per_batch_mod else (lambda b, t: (0, 0, 0)))
    w_specs = [_const_spec(w.shape) for w in weights]
    tok = lambda n, dt: (pl.BlockSpec((1, tm, n), lambda b, t: (b, t, 0)),
                         jax.ShapeDtypeStruct((bsz, t_len, n), dt))
    tiles_t = (pl.BlockSpec((1, N_HEADS, tm // CHUNK, HEAD_DIM, CHUNK), lambda b, t: (b, 0, t, 0, 0)),
               jax.ShapeDtypeStruct((bsz, N_HEADS, t_len // CHUNK, HEAD_DIM, CHUNK), BF16))
    gates_t = (pl.BlockSpec((1, N_GATES, tm), lambda b, t: (b, 0, t)),
               jax.ShapeDtypeStruct((bsz, N_GATES, t_len), F32))
    outs = [tok(D_MODEL, BF16), tiles_t, gates_t]
    if full:
        outs += [tiles_t, tiles_t, tok(D_MODEL, BF16), tok(D_MODEL, BF16)]
    return pl.pallas_call(
        functools.partial(_proj_kernel, tm=tm, full=full),
        grid=(bsz, nt),
        in_specs=x_specs + [mod_spec, _const_spec(g1.shape)] + w_specs,
        out_specs=[o[0] for o in outs],
        out_shape=[o[1] for o in outs],
        compiler_params=pltpu.CompilerParams(dimension_semantics=("parallel", "parallel"),
                                             vmem_limit_bytes=VMEM_LIMIT),
        name="proj_full" if full else "proj_ctx",
    )(xs, xs, xs, mod, g1, *weights)


def _scan_lanes(v, combine, fill, forward):
    lane = lax.broadcasted_iota(jnp.int32, v.shape, 1)
    d = 1
    while d < LANES:
        if forward:
            shifted = jnp.where(lane >= d, pltpu.roll(v, d, 1), fill)
        else:
            shifted = jnp.where(lane < LANES - d, pltpu.roll(v, LANES - d, 1), fill)
        v = combine(v, shifted)
        d *= 2
    return v


def _gates_kernel(gl_ref, gc_ref, out_ref, *, n_ctx, n_lat):
    n_all = n_ctx + n_lat
    src = lax.broadcasted_iota(jnp.int32, (CHUNK, 2 * CHUNK), 0)
    dst = lax.broadcasted_iota(jnp.int32, (CHUNK, 2 * CHUNK), 1)

    def gate_rows(cg, first):
        if cg < n_ctx:
            return gc_ref[0, first:first + N_HEADS, cg * CHUNK:(cg + 1) * CHUNK]
        return gl_ref[0, first:first + N_HEADS, (cg - n_ctx) * CHUNK:(cg - n_ctx + 1) * CHUNK]

    for dirn in range(2):
        forward = dirn == 0
        if forward:
            order = list(range(n_all))
        else:
            order = list(range(n_ctx - 1, -1, -1)) + list(range(n_all - 1, n_ctx - 1, -1))
        last = LANES - 1 if forward else 0
        first_row = 2 * N_HEADS * dirn
        fg = jnp.concatenate([gate_rows(cg, first_row + N_HEADS) for cg in range(n_all)], axis=0)
        lf = jnp.minimum(fg, 0.0) - jnp.log1p(jnp.exp(-jnp.abs(fg)))
        prefix = (src <= dst) if forward else (src >= dst)
        sel = jnp.where((dst >= CHUNK) | prefix, 1.0, 0.0).astype(BF16)
        hi = lf.astype(BF16)
        rest = lf - hi.astype(F32)
        mid = rest.astype(BF16)
        low = (rest - mid.astype(F32)).astype(BF16)
        sums = _dot(hi, sel) + _dot(mid, sel) + _dot(low, sel)
        m = jnp.full((N_HEADS, 1), M_INIT, F32)
        for cg in order:
            rows = slice(cg * N_HEADS, (cg + 1) * N_HEADS)
            b = sums[rows, :CHUNK]
            b_end = sums[rows, CHUNK:CHUNK + 1]
            r = gate_rows(cg, first_row) - b
            cm = _scan_lanes(r, jnp.maximum, -jnp.inf, forward)
            mx = jnp.maximum(m, cm)
            mt = b + mx
            m_new = b_end + jnp.maximum(m, cm[:, last:last + 1])
            cbase = C_BASE + C_PER_DIR * dirn
            fields = {
                R_ROWB + dirn: r,
                R_DECAY + dirn: jnp.broadcast_to(jnp.exp(b_end + m - m_new), (N_HEADS, CHUNK)),
                cbase + F_COLA: -mx,
                cbase + F_AINT: jnp.exp(m - mx),
                cbase + F_ENEG: jnp.exp(-mt),
                cbase + F_WK: jnp.exp((b_end - m_new) + r),
            }
            for f, val in fields.items():
                for h in range(N_HEADS):
                    out_ref[0, h, cg, f:f + 1, :] = val[h:h + 1, :]
            m = m_new
    n_used = C_BASE + 2 * C_PER_DIR
    out_ref[0, :, :, n_used:, :] = jnp.zeros((N_HEADS, n_all, N_FIELDS - n_used, CHUNK), F32)


def _gates(g_lat, g_ctx):
    bsz, _, t_lat = g_lat.shape
    t_ctx = g_ctx.shape[2]
    n_all = (t_lat + t_ctx) // CHUNK
    shape = (bsz, N_HEADS, n_all, N_FIELDS, CHUNK)
    return pl.pallas_call(
        functools.partial(_gates_kernel, n_ctx=t_ctx // CHUNK, n_lat=t_lat // CHUNK),
        grid=(bsz,),
        in_specs=[pl.BlockSpec((1, N_GATES, t_lat), lambda b: (b, 0, 0)),
                  pl.BlockSpec((1, N_GATES, t_ctx), lambda b: (b, 0, 0))],
        out_specs=pl.BlockSpec((1,) + shape[1:], lambda b: (b, 0, 0, 0, 0)),
        out_shape=jax.ShapeDtypeStruct(shape, F32),
        compiler_params=pltpu.CompilerParams(dimension_semantics=("parallel",)),
        name="gates",
    )(g_lat, g_ctx)


STATE_ROWS = HEAD_DIM + BF16_ROWS


def _mlstm_kernel(qt_ref, kl_ref, vtl_ref, kc_ref, vtc_ref, gf_ref, ogt_ref, ng_ref, out_ref,
                  dz_s, zf_s, zb_s, z_s, p_s, *, n_ctx, n_lat):
    lane = lax.broadcasted_iota(jnp.int32, (CHUNK, CHUNK), 1)
    row = lax.broadcasted_iota(jnp.int32, (CHUNK, CHUNK), 0)
    tri = (row <= lane, row >= lane)
    tail_first = lax.broadcasted_iota(jnp.int32, (BF16_ROWS, CHUNK), 0) == 0
    ones_tail = jnp.where(tail_first, 1.0, 0.0).astype(BF16)

    def field(cg, dirn, f):
        j = C_BASE + C_PER_DIR * dirn + f
        return gf_ref[0, 0, cg, j:j + 1, :]

    def pass_a(cg, k, vt):
        vf = vt.astype(F32)
        parts = []
        for dirn in range(2):
            wk = field(cg, dirn, F_WK)
            parts += [(vf * wk).astype(BF16), jnp.where(tail_first, wk, 0.0).astype(BF16)]
        dz_s[cg] = _dot(jnp.concatenate(parts, axis=0), k)

    for cg in range(n_ctx):
        pass_a(cg, kc_ref[0, cg * CHUNK:(cg + 1) * CHUNK, :], vtc_ref[0, 0, cg])

    def pass_a_lat(i, carry):
        sl = pl.ds(pl.multiple_of(i * CHUNK, CHUNK), CHUNK)
        pass_a(n_ctx + i, kl_ref[0, sl, :], vtl_ref[0, 0, i])
        return carry

    lax.fori_loop(0, n_lat, pass_a_lat, 0, unroll=8)

    def advance(dirn, cg):
        dec = gf_ref[0, 0, cg, R_DECAY + dirn:R_DECAY + dirn + 1, :]
        z_s[dirn] = dec * z_s[dirn] + dz_s[cg, dirn * STATE_ROWS:(dirn + 1) * STATE_ROWS, :]

    z_s[...] = jnp.zeros_like(z_s)
    for cg in range(n_ctx):
        advance(0, cg)
    for cg in range(n_ctx - 1, -1, -1):
        advance(1, cg)

    def pass_b(i, carry):
        zf_s[i] = z_s[0].astype(BF16)
        advance(0, n_ctx + i)
        j = n_lat - 1 - i
        zb_s[j] = z_s[1].astype(BF16)
        advance(1, n_ctx + j)
        return carry

    lax.fori_loop(0, n_lat, pass_b, 0, unroll=2)

    gain = jnp.broadcast_to(ng_ref[...], (CHUNK, HEAD_DIM)).T

    def pass_c1(i, carry):
        cg = n_ctx + i
        sl = pl.ds(pl.multiple_of(i * CHUNK, CHUNK), CHUNK)
        st = _dot(kl_ref[0, sl, :], qt_ref[0, 0, i])
        rowb = gf_ref[0, 0, cg].T
        for dirn in range(2):
            dmat = rowb[:, R_ROWB + dirn:R_ROWB + dirn + 1] + field(cg, dirn, F_COLA)
            w = jnp.exp(jnp.where(tri[dirn], dmat, -jnp.inf))
            p_s[i, dirn] = (st * w).astype(BF16)
        return carry

    lax.fori_loop(0, n_lat, pass_c1, 0, unroll=16)

    def pass_c2(i, carry):
        cg = n_ctx + i
        qtf = qt_ref[0, 0, i].astype(F32)
        vaug = jnp.concatenate([vtl_ref[0, 0, i], ones_tail], axis=0)
        ht = None
        for dirn, z_ref in enumerate((zf_s, zb_s)):
            rhs = jnp.concatenate([(qtf * field(cg, dirn, F_AINT)).astype(BF16), p_s[i, dirn]], axis=0)
            res = _dot(jnp.concatenate([z_ref[i], vaug], axis=1), rhs)
            den = jnp.maximum(jnp.abs(res[HEAD_DIM:HEAD_DIM + 1, :]), field(cg, dirn, F_ENEG))
            hd = res[:HEAD_DIM, :] / den
            ht = hd if ht is None else ht + hd
        ht = ht * lax.rsqrt(jnp.mean(ht * ht, axis=0, keepdims=True) + EPS)
        out_ref[0, 0, i] = (ht * gain * ogt_ref[0, 0, i].astype(F32)).astype(BF16)
        return carry

    lax.fori_loop(0, n_lat, pass_c2, 0, unroll=16)


def _mlstm(qt_l, k_l, vt_l, k_c, vt_c, gf, ogt, ng):
    bsz, t_lat, _ = k_l.shape
    t_ctx = k_c.shape[1]
    n_lat, n_ctx = t_lat // CHUNK, t_ctx // CHUNK
    n_all = n_lat + n_ctx
    head_lat = pl.BlockSpec((1, t_lat, HEAD_DIM), lambda b, h: (b, 0, h))
    head_ctx = pl.BlockSpec((1, t_ctx, HEAD_DIM), lambda b, h: (b, 0, h))
    tiles = lambda n: pl.BlockSpec((1, 1, n, HEAD_DIM, CHUNK), lambda b, h: (b, h, 0, 0, 0))
    return pl.pallas_call(
        functools.partial(_mlstm_kernel, n_ctx=n_ctx, n_lat=n_lat),
        grid=(bsz, N_HEADS),
        in_specs=[tiles(n_lat), head_lat, tiles(n_lat), head_ctx, tiles(n_ctx),
                  pl.BlockSpec((1, 1, n_all, N_FIELDS, CHUNK), lambda b, h: (b, h, 0, 0, 0)),
                  tiles(n_lat),
                  pl.BlockSpec((1, HEAD_DIM), lambda b, h: (0, h))],
        out_specs=tiles(n_lat),
        out_shape=jax.ShapeDtypeStruct(qt_l.shape, BF16),
        scratch_shapes=[pltpu.VMEM((n_all, 2 * STATE_ROWS, HEAD_DIM), F32),
                        pltpu.VMEM((n_lat, STATE_ROWS, HEAD_DIM), BF16),
                        pltpu.VMEM((n_lat, STATE_ROWS, HEAD_DIM), BF16),
                        pltpu.VMEM((2, STATE_ROWS, HEAD_DIM), F32),
                        pltpu.VMEM((n_lat, 2, CHUNK, CHUNK), BF16)],
        compiler_params=pltpu.CompilerParams(dimension_semantics=("parallel", "parallel"),
                                             vmem_limit_bytes=VMEM_LIMIT),
        name="mlstm",
    )(qt_l, k_l, vt_l, k_c, vt_c, gf, ogt, ng)


def _merge_kernel(hgt_ref, gm_ref, zc_ref, x_ref, mod_ref, g2_ref, wmo_ref, wo_ref, x1_ref, h2_ref):
    per_sub = MERGE_SUB // CHUNK
    subs = [slice(i * MERGE_SUB, (i + 1) * MERGE_SUB) for i in range(hgt_ref.shape[2] // per_sub)]
    hg = [jnp.concatenate(
        [jnp.concatenate([hgt_ref[0, h, ci].astype(F32).T for h in range(N_HEADS)], axis=1)
         for ci in range(i * per_sub, (i + 1) * per_sub)], axis=0).astype(BF16) for i in range(len(subs))]
    ym = [_dot(hg_i, wmo_ref[...]) for hg_i in hg]
    y = [_dot((gm_ref[0, sl, :].astype(F32) * ym_i + zc_ref[0, sl, :].astype(F32)).astype(BF16), wo_ref[...])
         for sl, ym_i in zip(subs, ym)]
    for sl, y_i in zip(subs, y):
        x1 = x_ref[0, sl, :] + mod_ref[0, 2:3, :] * y_i
        x1_ref[0, sl, :] = x1
        h2 = _rms_scale(x1) * g2_ref[...] * (1.0 + mod_ref[0, 4:5, :]) + mod_ref[0, 3:4, :]
        h2_ref[0, sl, :] = h2.astype(BF16)


def _merge(hgt, gm, zc, x, mod, g2, wmo, wo):
    bsz, t_len, _ = x.shape
    tm = MERGE_TM
    tok = pl.BlockSpec((1, tm, D_MODEL), lambda b, t: (b, t, 0))
    tiles = pl.BlockSpec((1, N_HEADS, tm // CHUNK, HEAD_DIM, CHUNK), lambda b, t: (b, 0, t, 0, 0))
    return pl.pallas_call(
        _merge_kernel,
        grid=(bsz, t_len // tm),
        in_specs=[tiles, tok, tok, tok, pl.BlockSpec((1, 8, D_MODEL), lambda b, t: (b, 0, 0)),
                  _const_spec(g2.shape), _const_spec(wmo.shape), _const_spec(wo.shape)],
        out_specs=[tok, tok],
        out_shape=[jax.ShapeDtypeStruct(x.shape, F32), jax.ShapeDtypeStruct(x.shape, BF16)],
        compiler_params=pltpu.CompilerParams(dimension_semantics=("parallel", "parallel"),
                                             vmem_limit_bytes=VMEM_LIMIT),
        name="merge",
    )(hgt, gm, zc, x, mod, g2, wmo, wo)


def _gelu_tanh(v):
    return 0.5 * v * (1.0 + jnp.tanh(0.7978845608028654 * (v + 0.044715 * (v * v * v))))


def _ffn_kernel(hm_ref, hp_ref, hn_ref, x1_ref, mod_ref, wa_ref, wg_ref, cw_ref, cb_ref, wd_ref, fg_ref,
                out_ref, act_s, *, tm):
    t = pl.program_id(1)
    nt = pl.num_programs(1)
    n = tm + 2 * GRID_W
    hm = hm_ref[0]
    hp = jnp.where(t > 0, hp_ref[0], jnp.zeros_like(hp_ref[0]))
    hn = jnp.where(t < nt - 1, hn_ref[0], jnp.zeros_like(hn_ref[0]))
    he = jnp.concatenate([hp, hm, hn], axis=0)
    gcol = lax.broadcasted_iota(jnp.int32, (n, 1), 0) & (GRID_W - 1)
    has_left = gcol != 0
    has_right = gcol != GRID_W - 1
    n_chunks = FF_HIDDEN // FFN_CW
    cols = lambda j: slice(j * FFN_CW, (j + 1) * FFN_CW)
    up = lambda j: (_dot(he, wa_ref[:, cols(j)]), _dot(hm, wg_ref[:, cols(j)]))
    acc = None
    nxt = up(0)
    for j in range(n_chunks):
        cs = cols(j)
        a, g = nxt
        if j + 1 < n_chunks:
            nxt = up(j + 1)
        taps = (jnp.where(has_left, pltpu.roll(a, 1, 0), 0.0), a,
                jnp.where(has_right, pltpu.roll(a, n - 1, 0), 0.0))
        conv = cb_ref[:, cs]
        for dr in range(3):
            for dc in range(3):
                conv = conv + cw_ref[3 * dr + dc: 3 * dr + dc + 1, cs] * taps[dc][GRID_W * dr: GRID_W * dr + tm]
        act_s[:, cs] = (_gelu_tanh(conv) * g).astype(BF16)
        if (j + 1) % FFN_DOWN_GROUP == 0 or j + 1 == n_chunks:
            gs = slice((j // FFN_DOWN_GROUP) * FFN_DOWN_GROUP * FFN_CW, (j + 1) * FFN_CW)
            part = _dot(act_s[:, gs], wd_ref[gs, :])
            acc = part if acc is None else acc + part
    x2 = x1_ref[0] + mod_ref[0, 5:6, :] * acc
    out_ref[0] = _rms_scale(x2) * fg_ref[...]


def _ffn(h2, x1, mod, wa, wg, cw, cb, wd, fg):
    bsz, t_len, _ = x1.shape
    tm = FFN_TM
    rb = tm // GRID_W
    nrb = t_len // GRID_W
    tok = pl.BlockSpec((1, tm, D_MODEL), lambda b, t: (b, t, 0))
    return pl.pallas_call(
        functools.partial(_ffn_kernel, tm=tm),
        grid=(bsz, t_len // tm),
        in_specs=[tok,
                  pl.BlockSpec((1, GRID_W, D_MODEL), lambda b, t: (b, jnp.maximum(t * rb - 1, 0), 0)),
                  pl.BlockSpec((1, GRID_W, D_MODEL), lambda b, t: (b, jnp.minimum((t + 1) * rb, nrb - 1), 0)),
                  tok, pl.BlockSpec((1, 8, D_MODEL), lambda b, t: (b, 0, 0)),
                  _const_spec(wa.shape), _const_spec(wg.shape), _const_spec(cw.shape), _const_spec(cb.shape),
                  _const_spec(wd.shape), _const_spec(fg.shape)],
        out_specs=tok,
        out_shape=jax.ShapeDtypeStruct(x1.shape, F32),
        scratch_shapes=[pltpu.VMEM((tm, FF_HIDDEN), BF16)],
        compiler_params=pltpu.CompilerParams(dimension_semantics=("parallel", "parallel"),
                                             vmem_limit_bytes=VMEM_LIMIT),
        name="ffn",
    )(h2, h2, h2, x1, mod, wa, wg, cw, cb, wd, fg)


def kernel(x, c, ctx, c_ctx, ada_w, ada_b, norm1_g, norm2_g, w_in, qk_conv_w, qk_conv_b, gate_b, mnorm_g,
           w_m_out, sc_conv_w, sc_conv_b, w_c_out, w_o, w_up, ff_conv_w, ff_conv_b, w_down, final_g):
    assert ada_w.shape[0] == 1, "single-layer block"
    bsz, t_lat, _ = x.shape
    t_ctx = ctx.shape[1]
    n_all = (t_lat + t_ctx) // CHUNK

    cc = jnp.zeros((2 * 8, D_MODEL), F32).at[:bsz].set(c).at[bsz].set(c_ctx)
    mod = _ada(cc, ada_w[0], ada_b[0][None, :])
    mod_x = jnp.pad(mod[:bsz].reshape(bsz, 6, D_MODEL), ((0, 0), (0, 2), (0, 0)))
    mod_c = jnp.pad(mod[bsz].reshape(1, 6, D_MODEL), ((0, 0), (0, 2), (0, 0)))

    w = w_in[0]
    m_w = D_MODEL
    off_g = 3 * m_w
    off_o = off_g + N_GATES
    off_sb = off_o + m_w
    off_sc = off_sb + D_MODEL
    off_mg = off_sc + 2 * D_MODEL
    cast = lambda a: a.astype(BF16)
    wqk, wv = cast(w[:, :2 * m_w]), cast(w[:, 2 * m_w:off_g])
    wgt = cast(w[:, off_g:off_o].T)
    gb = gate_b[0][:, None]
    g1 = norm1_g[0][None, :]
    ctx_w = [wqk, wv, wgt, gb, qk_conv_w[0], qk_conv_b[0][None, :]]
    lat_w = ctx_w + [cast(w[:, off_o:off_sb]), cast(w[:, off_sb:off_sc]), cast(w[:, off_sc:off_mg]),
                     sc_conv_w[0], sc_conv_b[0][None, :], cast(w_c_out[0]), cast(w[:, off_mg:])]

    k_l, vt_l, g_l, qt_l, ogt, gm, zc = _proj(x, mod_x, g1, lat_w, full=True, per_batch_mod=True)
    k_c, vt_c, g_c = _proj(ctx, mod_c, g1, ctx_w, full=False, per_batch_mod=False)

    gf = _gates(g_l, g_c)
    hgt = _mlstm(qt_l, k_l, vt_l, k_c, vt_c, gf, ogt, mnorm_g[0][None, :])

    x1, h2 = _merge(hgt, gm, zc, x, mod_x, norm2_g[0][None, :], cast(w_m_out[0]), cast(w_o[0]))

    wu = w_up[0]
    out = _ffn(h2, x1, mod_x, cast(wu[:, :FF_HIDDEN]), cast(wu[:, FF_HIDDEN:]),
               ff_conv_w[0].reshape(9, FF_HIDDEN), ff_conv_b[0][None, :], cast(w_down[0]),
               final_g[None, :])
    return out
```

```python
import functools

import jax
import jax.numpy as jnp
from jax import lax
from jax.experimental import pallas as pl
from jax.experimental.pallas import tpu as pltpu

F32 = jnp.float32
BF16 = jnp.bfloat16

D_MODEL = 1024
N_HEADS = 8
HEAD_DIM = D_MODEL // N_HEADS
CHUNK = 128
GRID_W = 64
FF_HIDDEN = 2816
EPS = 1e-6
M_INIT = -1e30
N_GATES = 4 * N_HEADS

LANES = 128
BF16_ROWS = 16
VMEM_LIMIT = 56 * 1024 * 1024

PROJ_TM = 256
SEQ_HALO = BF16_ROWS
MERGE_TM = 512
MERGE_SUB = 256
FFN_TM = 512
FFN_CW = 256
FFN_DOWN_GROUP = 4

R_ROWB, R_DECAY = 0, 2
C_BASE, C_PER_DIR = 4, 4
F_COLA, F_AINT, F_ENEG, F_WK = range(4)
N_FIELDS = 16


def _const_spec(shape):
    nd = len(shape)
    return pl.BlockSpec(shape, lambda *_: (0,) * nd, pipeline_mode=pl.Buffered(1))


def _sigmoid(v):
    return 1.0 / (1.0 + jnp.exp(-v))


def _rms_scale(v):
    return v * lax.rsqrt(jnp.mean(v * v, axis=-1, keepdims=True) + EPS)


def _dot(a, b):
    return jnp.dot(a, b, preferred_element_type=F32)


def _ada_kernel(c_ref, w_ref, b_ref, o_ref):
    cv = c_ref[...]
    s = cv * _sigmoid(cv)
    o_ref[...] = jnp.dot(s, w_ref[...], preferred_element_type=F32,
                         precision=lax.Precision.HIGHEST) + b_ref[...]


def _ada(cc, w, b):
    rows, n = cc.shape[0], w.shape[1]
    tn = 1024
    return pl.pallas_call(
        _ada_kernel,
        grid=(n // tn,),
        in_specs=[pl.BlockSpec((rows, D_MODEL), lambda j: (0, 0)),
                  pl.BlockSpec((D_MODEL, tn), lambda j: (0, j)),
                  pl.BlockSpec((1, tn), lambda j: (0, j))],
        out_specs=pl.BlockSpec((rows, tn), lambda j: (0, j)),
        out_shape=jax.ShapeDtypeStruct((rows, n), F32),
        name="ada",
    )(cc, w, b)


def _conv3_rows(p, w_ref, b_ref, tm):
    n = tm + 2 * SEQ_HALO
    lo, hi = SEQ_HALO, SEQ_HALO + tm
    left = pltpu.roll(p, 1, 0)[lo:hi]
    right = pltpu.roll(p, n - 1, 0)[lo:hi]
    return left * w_ref[0:1, :] + p[lo:hi] * w_ref[1:2, :] + right * w_ref[2:3, :] + b_ref[...]


def _store_head_tiles_t(ref, val, tm):
    for h in range(N_HEADS):
        for ci in range(tm // CHUNK):
            tile = val[ci * CHUNK:(ci + 1) * CHUNK, h * HEAD_DIM:(h + 1) * HEAD_DIM]
            ref[0, h, ci] = tile.T.astype(BF16)


W_QK = slice(0, 2 * D_MODEL)
W_V = slice(2 * D_MODEL, 3 * D_MODEL)
W_CTX_COLS = 3 * D_MODEL
W_O = slice(3 * D_MODEL, 4 * D_MODEL)
W_SB = slice(4 * D_MODEL, 5 * D_MODEL)
W_SCX = slice(5 * D_MODEL, 7 * D_MODEL)
W_MG = slice(7 * D_MODEL, 9 * D_MODEL)


def _proj_kernel(*refs, tm, full):
    if full:
        (xm_ref, xp_ref, xn_ref, mod_ref, g1_ref, w_ref, wg_ref, gb_ref, qkw_ref, qkb_ref,
         scw_ref, scb_ref, wco_ref,
         k_ref, vt_ref, g_ref, qt_ref, ogt_ref, gm_ref, zc_ref) = refs
    else:
        (xm_ref, xp_ref, xn_ref, mod_ref, g1_ref, w_ref, wg_ref, gb_ref, qkw_ref, qkb_ref,
         k_ref, vt_ref, g_ref) = refs
    t = pl.program_id(1)
    nt = pl.num_programs(1)
    shift = mod_ref[0, 0:1, :]
    scale1 = 1.0 + mod_ref[0, 1:2, :]
    gain = g1_ref[...]

    def norm_mod(xv):
        return _rms_scale(xv) * gain * scale1 + shift

    hm = norm_mod(xm_ref[0])
    hp = jnp.where(t > 0, norm_mod(xp_ref[0]), 0.0)
    hn = jnp.where(t < nt - 1, norm_mod(xn_ref[0]), 0.0)
    hmb = hm.astype(BF16)
    he = jnp.concatenate([hp, hm, hn], axis=0).astype(BF16)

    pqk = _dot(he, w_ref[:, W_QK])
    pv = _dot(hmb, w_ref[:, W_V])
    qk = _conv3_rows(pqk, qkw_ref, qkb_ref, tm)
    qk = qk * _sigmoid(qk)
    k_ref[0] = (qk[:, D_MODEL:] * (HEAD_DIM ** -0.5)).astype(BF16)
    pg = lax.dot_general(wg_ref[...], hmb, (((1,), (1,)), ((), ())), preferred_element_type=F32)
    if not full:
        _store_head_tiles_t(vt_ref, pv, tm)
        g_ref[0] = pg + gb_ref[...]
        return
    po = _dot(hmb, w_ref[:, W_O])
    _store_head_tiles_t(qt_ref, qk[:, :D_MODEL], tm)
    _store_head_tiles_t(vt_ref, pv, tm)
    g_ref[0] = pg + gb_ref[...]
    pcx = _dot(he, w_ref[:, W_SCX])
    _store_head_tiles_t(ogt_ref, _sigmoid(po), tm)
    psb = _dot(hmb, w_ref[:, W_SB])
    cu = _conv3_rows(pcx[:, :D_MODEL] * pcx[:, D_MODEL:], scw_ref, scb_ref, tm)
    pm = _dot(hmb, w_ref[:, W_MG])
    yc = _dot((psb * cu).astype(BF16), wco_ref[...])
    gm_ref[0] = _sigmoid(pm[:, :D_MODEL]).astype(BF16)
    zc_ref[0] = (_sigmoid(pm[:, D_MODEL:]) * yc).astype(BF16)


def _proj(xs, mod, g1, weights, *, full, per_batch_mod):
    bsz, t_len, _ = xs.shape
    tm = min(PROJ_TM, t_len)
    nt = t_len // tm
    hb = tm // SEQ_HALO
    nhb = t_len // SEQ_HALO
    x_specs = [
        pl.BlockSpec((1, tm, D_MODEL), lambda b, t: (b, t, 0)),
        pl.BlockSpec((1, SEQ_HALO, D_MODEL), lambda b, t: (b, jnp.maximum(t * hb - 1, 0), 0)),
        pl.BlockSpec((1, SEQ_HALO, D_MODEL), lambda b, t: (b, jnp.minimum((t + 1) * hb, nhb - 1), 0)),
    ]
    mod_spec = pl.BlockSpec((1, 8, D_MODEL), (lambda b, t: (b, 0, 0)) if per_batch_mod else (lambda b, t: (0, 0, 0)))
    w_cols = weights[0].shape[1] if full else W_CTX_COLS
    w_specs = [_const_spec((D_MODEL, w_cols))] + [_const_spec(w.shape) for w in weights[1:]]
    tok = lambda n, dt: (pl.BlockSpec((1, tm, n), lambda b, t: (b, t, 0)),
                         jax.ShapeDtypeStruct((bsz, t_len, n), dt))
    tiles_t = (pl.BlockSpec((1, N_HEADS, tm // CHUNK, HEAD_DIM, CHUNK), lambda b, t: (b, 0, t, 0, 0)),
               jax.ShapeDtypeStruct((bsz, N_HEADS, t_len // CHUNK, HEAD_DIM, CHUNK), BF16))
    gates_t = (pl.BlockSpec((1, N_GATES, tm), lambda b, t: (b, 0, t)),
               jax.ShapeDtypeStruct((bsz, N_GATES, t_len), F32))
    outs = [tok(D_MODEL, BF16), tiles_t, gates_t]
    if full:
        outs += [tiles_t, tiles_t, tok(D_MODEL, BF16), tok(D_MODEL, BF16)]
    return pl.pallas_call(
        functools.partial(_proj_kernel, tm=tm, full=full),
        grid=(bsz, nt),
        in_specs=x_specs + [mod_spec, _const_spec(g1.shape)] + w_specs,
        out_specs=[o[0] for o in outs],
        out_shape=[o[1] for o in outs],
        compiler_params=pltpu.CompilerParams(dimension_semantics=("parallel", "parallel"),
                                             vmem_limit_bytes=VMEM_LIMIT),
        name="proj_full" if full else "proj_ctx",
    )(xs, xs, xs, mod, g1, *weights)


def _scan_lanes(v, combine, fill, forward):
    lane = lax.broadcasted_iota(jnp.int32, v.shape, 1)
    d = 1
    while d < LANES:
        if forward:
            shifted = jnp.where(lane >= d, pltpu.roll(v, d, 1), fill)
        else:
            shifted = jnp.where(lane < LANES - d, pltpu.roll(v, LANES - d, 1), fill)
        v = combine(v, shifted)
        d *= 2
    return v


def _gates_kernel(gl_ref, gc_ref, out_ref, *, n_ctx, n_lat):
    n_all = n_ctx + n_lat
    src = lax.broadcasted_iota(jnp.int32, (CHUNK, 2 * CHUNK), 0)
    dst = lax.broadcasted_iota(jnp.int32, (CHUNK, 2 * CHUNK), 1)

    def gate_rows(cg, first):
        if cg < n_ctx:
            return gc_ref[0, first:first + N_HEADS, cg * CHUNK:(cg + 1) * CHUNK]
        return gl_ref[0, first:first + N_HEADS, (cg - n_ctx) * CHUNK:(cg - n_ctx + 1) * CHUNK]

    for dirn in range(2):
        forward = dirn == 0
        if forward:
            order = list(range(n_all))
        else:
            order = list(range(n_ctx - 1, -1, -1)) + list(range(n_all - 1, n_ctx - 1, -1))
        last = LANES - 1 if forward else 0
        first_row = 2 * N_HEADS * dirn
        fg = jnp.concatenate([gate_rows(cg, first_row + N_HEADS) for cg in range(n_all)], axis=0)
        lf = jnp.minimum(fg, 0.0) - jnp.log1p(jnp.exp(-jnp.abs(fg)))
        prefix = (src <= dst) if forward else (src >= dst)
        sel = jnp.where((dst >= CHUNK) | prefix, 1.0, 0.0).astype(BF16)
        hi = lf.astype(BF16)
        rest = lf - hi.astype(F32)
        mid = rest.astype(BF16)
        low = (rest - mid.astype(F32)).astype(BF16)
        sums = _dot(hi, sel) + _dot(mid, sel) + _dot(low, sel)
        m = jnp.full((N_HEADS, 1), M_INIT, F32)
        for cg in order:
            rows = slice(cg * N_HEADS, (cg + 1) * N_HEADS)
            b = sums[rows, :CHUNK]
            b_end = sums[rows, CHUNK:CHUNK + 1]
            r = gate_rows(cg, first_row) - b
            cm = _scan_lanes(r, jnp.maximum, -jnp.inf, forward)
            mx = jnp.maximum(m, cm)
            mt = b + mx
            m_new = b_end + jnp.maximum(m, cm[:, last:last + 1])
            cbase = C_BASE + C_PER_DIR * dirn
            fields = {
                R_ROWB + dirn: r,
                R_DECAY + dirn: jnp.broadcast_to(jnp.exp(b_end + m - m_new), (N_HEADS, CHUNK)),
                cbase + F_COLA: -mx,
                cbase + F_AINT: jnp.exp(m - mx),
                cbase + F_ENEG: jnp.exp(-mt),
                cbase + F_WK: jnp.exp((b_end - m_new) + r),
            }
            for f, val in fields.items():
                for h in range(N_HEADS):
                    out_ref[0, h, cg, f:f + 1, :] = val[h:h + 1, :]
            m = m_new
    n_used = C_BASE + 2 * C_PER_DIR
    out_ref[0, :, :, n_used:, :] = jnp.zeros((N_HEADS, n_all, N_FIELDS - n_used, CHUNK), F32)


def _gates(g_lat, g_ctx):
    bsz, _, t_lat = g_lat.shape
    t_ctx = g_ctx.shape[2]
    n_all = (t_lat + t_ctx) // CHUNK
    shape = (bsz, N_HEADS, n_all, N_FIELDS, CHUNK)
    return pl.pallas_call(
        functools.partial(_gates_kernel, n_ctx=t_ctx // CHUNK, n_lat=t_lat // CHUNK),
        grid=(bsz,),
        in_specs=[pl.BlockSpec((1, N_GATES, t_lat), lambda b: (b, 0, 0)),
                  pl.BlockSpec((1, N_GATES, t_ctx), lambda b: (b, 0, 0))],
        out_specs=pl.BlockSpec((1,) + shape[1:], lambda b: (b, 0, 0, 0, 0)),
        out_shape=jax.ShapeDtypeStruct(shape, F32),
        compiler_params=pltpu.CompilerParams(dimension_semantics=("parallel",)),
        name="gates",
    )(g_lat, g_ctx)


STATE_ROWS = HEAD_DIM + BF16_ROWS


def _mlstm_kernel(qt_ref, kl_ref, vtl_ref, kc_ref, vtc_ref, gf_ref, ogt_ref, ng_ref, out_ref,
                  dz_s, zf_s, zb_s, z_s, p_s, *, n_ctx, n_lat):
    lane = lax.broadcasted_iota(jnp.int32, (CHUNK, CHUNK), 1)
    row = lax.broadcasted_iota(jnp.int32, (CHUNK, CHUNK), 0)
    tri = (row <= lane, row >= lane)
    tail_first = lax.broadcasted_iota(jnp.int32, (BF16_ROWS, CHUNK), 0) == 0
    ones_tail = jnp.where(tail_first, 1.0, 0.0).astype(BF16)

    def field(cg, dirn, f):
        j = C_BASE + C_PER_DIR * dirn + f
        return gf_ref[0, 0, cg, j:j + 1, :]

    def pass_a(cg, k, vt):
        vf = vt.astype(F32)
        parts = []
        for dirn in range(2):
            wk = field(cg, dirn, F_WK)
            parts += [(vf * wk).astype(BF16), jnp.where(tail_first, wk, 0.0).astype(BF16)]
        dz_s[cg] = _dot(jnp.concatenate(parts, axis=0), k)

    for cg in range(n_ctx):
        pass_a(cg, kc_ref[0, cg * CHUNK:(cg + 1) * CHUNK, :], vtc_ref[0, 0, cg])

    def pass_a_lat(i, carry):
        sl = pl.ds(pl.multiple_of(i * CHUNK, CHUNK), CHUNK)
        pass_a(n_ctx + i, kl_ref[0, sl, :], vtl_ref[0, 0, i])
        return carry

    lax.fori_loop(0, n_lat, pass_a_lat, 0, unroll=8)

    def advance(dirn, cg):
        dec = gf_ref[0, 0, cg, R_DECAY + dirn:R_DECAY + dirn + 1, :]
        z_s[dirn] = dec * z_s[dirn] + dz_s[cg, dirn * STATE_ROWS:(dirn + 1) * STATE_ROWS, :]

    z_s[...] = jnp.zeros_like(z_s)
    for cg in range(n_ctx):
        advance(0, cg)
    for cg in range(n_ctx - 1, -1, -1):
        advance(1, cg)

    def pass_b(i, carry):
        zf_s[i] = z_s[0].astype(BF16)
        advance(0, n_ctx + i)
        j = n_lat - 1 - i
        zb_s[j] = z_s[1].astype(BF16)
        advance(1, n_ctx + j)
        return carry

    lax.fori_loop(0, n_lat, pass_b, 0, unroll=2)

    gain = jnp.broadcast_to(ng_ref[...], (CHUNK, HEAD_DIM)).T

    def pass_c1(i, carry):
        cg = n_ctx + i
        sl = pl.ds(pl.multiple_of(i * CHUNK, CHUNK), CHUNK)
        st = _dot(kl_ref[0, sl, :], qt_ref[0, 0, i])
        rowb = gf_ref[0, 0, cg].T
        for dirn in range(2):
            dmat = rowb[:, R_ROWB + dirn:R_ROWB + dirn + 1] + field(cg, dirn, F_COLA)
            w = jnp.exp(jnp.where(tri[dirn], dmat, -jnp.inf))
            p_s[i, dirn] = (st * w).astype(BF16)
        return carry

    lax.fori_loop(0, n_lat, pass_c1, 0, unroll=16)

    def pass_c2(i, carry):
        cg = n_ctx + i
        qtf = qt_ref[0, 0, i].astype(F32)
        vaug = jnp.concatenate([vtl_ref[0, 0, i], ones_tail], axis=0)
        ht = None
        for dirn, z_ref in enumerate((zf_s, zb_s)):
            rhs = jnp.concatenate([(qtf * field(cg, dirn, F_AINT)).astype(BF16), p_s[i, dirn]], axis=0)
            res = _dot(jnp.concatenate([z_ref[i], vaug], axis=1), rhs)
            den = jnp.maximum(jnp.abs(res[HEAD_DIM:HEAD_DIM + 1, :]), field(cg, dirn, F_ENEG))
            hd = res[:HEAD_DIM, :] / den
            ht = hd if ht is None else ht + hd
        ht = ht * lax.rsqrt(jnp.mean(ht * ht, axis=0, keepdims=True) + EPS)
        out_ref[0, 0, i] = (ht * gain * ogt_ref[0, 0, i].astype(F32)).astype(BF16)
        return carry

    lax.fori_loop(0, n_lat, pass_c2, 0, unroll=16)


def _mlstm(qt_l, k_l, vt_l, k_c, vt_c, gf, ogt, ng):
    bsz, t_lat, _ = k_l.shape
    t_ctx = k_c.shape[1]
    n_lat, n_ctx = t_lat // CHUNK, t_ctx // CHUNK
    n_all = n_lat + n_ctx
    head_lat = pl.BlockSpec((1, t_lat, HEAD_DIM), lambda b, h: (b, 0, h))
    head_ctx = pl.BlockSpec((1, t_ctx, HEAD_DIM), lambda b, h: (b, 0, h))
    tiles = lambda n: pl.BlockSpec((1, 1, n, HEAD_DIM, CHUNK), lambda b, h: (b, h, 0, 0, 0))
    return pl.pallas_call(
        functools.partial(_mlstm_kernel, n_ctx=n_ctx, n_lat=n_lat),
        grid=(bsz, N_HEADS),
        in_specs=[tiles(n_lat), head_lat, tiles(n_lat), head_ctx, tiles(n_ctx),
                  pl.BlockSpec((1, 1, n_all, N_FIELDS, CHUNK), lambda b, h: (b, h, 0, 0, 0)),
                  tiles(n_lat),
                  pl.BlockSpec((1, HEAD_DIM), lambda b, h: (0, h))],
        out_specs=tiles(n_lat),
        out_shape=jax.ShapeDtypeStruct(qt_l.shape, BF16),
        scratch_shapes=[pltpu.VMEM((n_all, 2 * STATE_ROWS, HEAD_DIM), F32),
                        pltpu.VMEM((n_lat, STATE_ROWS, HEAD_DIM), BF16),
                        pltpu.VMEM((n_lat, STATE_ROWS, HEAD_DIM), BF16),
                        pltpu.VMEM((2, STATE_ROWS, HEAD_DIM), F32),
                        pltpu.VMEM((n_lat, 2, CHUNK, CHUNK), BF16)],
        compiler_params=pltpu.CompilerParams(dimension_semantics=("parallel", "parallel"),
                                             vmem_limit_bytes=VMEM_LIMIT),
        name="mlstm",
    )(qt_l, k_l, vt_l, k_c, vt_c, gf, ogt, ng)


def _merge_kernel(hgt_ref, gm_ref, zc_ref, x_ref, mod_ref, g2_ref, wmo_ref, wo_ref, x1_ref, h2_ref):
    per_sub = MERGE_SUB // CHUNK
    subs = [slice(i * MERGE_SUB, (i + 1) * MERGE_SUB) for i in range(hgt_ref.shape[2] // per_sub)]
    hg = [jnp.concatenate(
        [jnp.concatenate([hgt_ref[0, h, ci].astype(F32).T for h in range(N_HEADS)], axis=1)
         for ci in range(i * per_sub, (i + 1) * per_sub)], axis=0).astype(BF16) for i in range(len(subs))]
    ym = [_dot(hg_i, wmo_ref[...]) for hg_i in hg]
    y = [_dot((gm_ref[0, sl, :].astype(F32) * ym_i + zc_ref[0, sl, :].astype(F32)).astype(BF16), wo_ref[...])
         for sl, ym_i in zip(subs, ym)]
    for sl, y_i in zip(subs, y):
        x1 = x_ref[0, sl, :] + mod_ref[0, 2:3, :] * y_i
        x1_ref[0, sl, :] = x1
        h2 = _rms_scale(x1) * g2_ref[...] * (1.0 + mod_ref[0, 4:5, :]) + mod_ref[0, 3:4, :]
        h2_ref[0, sl, :] = h2.astype(BF16)


def _merge(hgt, gm, zc, x, mod, g2, wmo, wo):
    bsz, t_len, _ = x.shape
    tm = MERGE_TM
    tok = pl.BlockSpec((1, tm, D_MODEL), lambda b, t: (b, t, 0))
    tiles = pl.BlockSpec((1, N_HEADS, tm // CHUNK, HEAD_DIM, CHUNK), lambda b, t: (b, 0, t, 0, 0))
    return pl.pallas_call(
        _merge_kernel,
        grid=(bsz, t_len // tm),
        in_specs=[tiles, tok, tok, tok, pl.BlockSpec((1, 8, D_MODEL), lambda b, t: (b, 0, 0)),
                  _const_spec(g2.shape), _const_spec(wmo.shape), _const_spec(wo.shape)],
        out_specs=[tok, tok],
        out_shape=[jax.ShapeDtypeStruct(x.shape, F32), jax.ShapeDtypeStruct(x.shape, BF16)],
        compiler_params=pltpu.CompilerParams(dimension_semantics=("parallel", "parallel"),
                                             vmem_limit_bytes=VMEM_LIMIT),
        name="merge",
    )(hgt, gm, zc, x, mod, g2, wmo, wo)


def _gelu_tanh(v):
    return 0.5 * v * (1.0 + jnp.tanh(0.7978845608028654 * (v + 0.044715 * (v * v * v))))


def _ffn_kernel(hm_ref, hp_ref, hn_ref, x1_ref, mod_ref, wu_ref, cw_ref, cb_ref, wd_ref, fg_ref,
                out_ref, act_s, *, tm):
    t = pl.program_id(1)
    nt = pl.num_programs(1)
    n = tm + 2 * GRID_W
    hm = hm_ref[0]
    hp = jnp.where(t > 0, hp_ref[0], jnp.zeros_like(hp_ref[0]))
    hn = jnp.where(t < nt - 1, hn_ref[0], jnp.zeros_like(hn_ref[0]))
    he = jnp.concatenate([hp, hm, hn], axis=0)
    gcol = lax.broadcasted_iota(jnp.int32, (n, 1), 0) & (GRID_W - 1)
    has_left = gcol != 0
    has_right = gcol != GRID_W - 1
    n_chunks = FF_HIDDEN // FFN_CW
    cols = lambda j: slice(j * FFN_CW, (j + 1) * FFN_CW)
    gate_cols = lambda j: slice(FF_HIDDEN + j * FFN_CW, FF_HIDDEN + (j + 1) * FFN_CW)
    up = lambda j: (_dot(he, wu_ref[:, cols(j)]), _dot(hm, wu_ref[:, gate_cols(j)]))
    acc = None
    nxt = up(0)
    for j in range(n_chunks):
        cs = cols(j)
        a, g = nxt
        if j + 1 < n_chunks:
            nxt = up(j + 1)
        taps = (jnp.where(has_left, pltpu.roll(a, 1, 0), 0.0), a,
                jnp.where(has_right, pltpu.roll(a, n - 1, 0), 0.0))
        conv = cb_ref[:, cs]
        for dr in range(3):
            for dc in range(3):
                conv = conv + cw_ref[3 * dr + dc: 3 * dr + dc + 1, cs] * taps[dc][GRID_W * dr: GRID_W * dr + tm]
        act_s[:, cs] = (_gelu_tanh(conv) * g).astype(BF16)
        if (j + 1) % FFN_DOWN_GROUP == 0 or j + 1 == n_chunks:
            gs = slice((j // FFN_DOWN_GROUP) * FFN_DOWN_GROUP * FFN_CW, (j + 1) * FFN_CW)
            part = _dot(act_s[:, gs], wd_ref[gs, :])
            acc = part if acc is None else acc + part
    x2 = x1_ref[0] + mod_ref[0, 5:6, :] * acc
    out_ref[0] = _rms_scale(x2) * fg_ref[...]


def _ffn(h2, x1, mod, wu, cw, cb, wd, fg):
    bsz, t_len, _ = x1.shape
    tm = FFN_TM
    rb = tm // GRID_W
    nrb = t_len // GRID_W
    tok = pl.BlockSpec((1, tm, D_MODEL), lambda b, t: (b, t, 0))
    return pl.pallas_call(
        functools.partial(_ffn_kernel, tm=tm),
        grid=(bsz, t_len // tm),
        in_specs=[tok,
                  pl.BlockSpec((1, GRID_W, D_MODEL), lambda b, t: (b, jnp.maximum(t * rb - 1, 0), 0)),
                  pl.BlockSpec((1, GRID_W, D_MODEL), lambda b, t: (b, jnp.minimum((t + 1) * rb, nrb - 1), 0)),
                  tok, pl.BlockSpec((1, 8, D_MODEL), lambda b, t: (b, 0, 0)),
                  _const_spec(wu.shape), _const_spec(cw.shape), _const_spec(cb.shape),
                  _const_spec(wd.shape), _const_spec(fg.shape)],
        out_specs=tok,
        out_shape=jax.ShapeDtypeStruct(x1.shape, F32),
        scratch_shapes=[pltpu.VMEM((tm, FF_HIDDEN), BF16)],
        compiler_params=pltpu.CompilerParams(dimension_semantics=("parallel", "parallel"),
                                             vmem_limit_bytes=VMEM_LIMIT),
        name="ffn",
    )(h2, h2, h2, x1, mod, wu, cw, cb, wd, fg)


def kernel(x, c, ctx, c_ctx, ada_w, ada_b, norm1_g, norm2_g, w_in, qk_conv_w, qk_conv_b, gate_b, mnorm_g,
           w_m_out, sc_conv_w, sc_conv_b, w_c_out, w_o, w_up, ff_conv_w, ff_conv_b, w_down, final_g):
    assert ada_w.shape[0] == 1, "single-layer block"
    bsz, t_lat, _ = x.shape
    t_ctx = ctx.shape[1]

    cc = jnp.zeros((2 * 8, D_MODEL), F32).at[:bsz].set(c).at[bsz].set(c_ctx)
    mod = _ada(cc, ada_w[0], ada_b[0][None, :])
    mod_x = jnp.pad(mod[:bsz].reshape(bsz, 6, D_MODEL), ((0, 0), (0, 2), (0, 0)))
    mod_c = jnp.pad(mod[bsz].reshape(1, 6, D_MODEL), ((0, 0), (0, 2), (0, 0)))

    w = w_in[0]
    off_g = 3 * D_MODEL
    off_o = off_g + N_GATES
    cast = lambda a: a.astype(BF16)
    w_main = cast(jnp.concatenate([w[:, :off_g], w[:, off_o:]], axis=1))
    wgt = cast(w[:, off_g:off_o].T)
    gb = gate_b[0][:, None]
    g1 = norm1_g[0][None, :]
    ctx_w = [w_main, wgt, gb, qk_conv_w[0], qk_conv_b[0][None, :]]
    lat_w = ctx_w + [sc_conv_w[0], sc_conv_b[0][None, :], cast(w_c_out[0])]

    k_l, vt_l, g_l, qt_l, ogt, gm, zc = _proj(x, mod_x, g1, lat_w, full=True, per_batch_mod=True)
    k_c, vt_c, g_c = _proj(ctx, mod_c, g1, ctx_w, full=False, per_batch_mod=False)

    gf = _gates(g_l, g_c)
    hgt = _mlstm(qt_l, k_l, vt_l, k_c, vt_c, gf, ogt, mnorm_g[0][None, :])

    x1, h2 = _merge(hgt, gm, zc, x, mod_x, norm2_g[0][None, :], cast(w_m_out[0]), cast(w_o[0]))

    out = _ffn(h2, x1, mod_x, cast(w_up[0]),
               ff_conv_w[0].reshape(9, FF_HIDDEN), ff_conv_b[0][None, :], cast(w_down[0]),
               final_g[None, :])
    return out
```

```python
import functools

import jax
import jax.numpy as jnp
from jax import lax
from jax.experimental import pallas as pl
from jax.experimental.pallas import tpu as pltpu

F32 = jnp.float32
BF16 = jnp.bfloat16

D_MODEL = 1024
N_HEADS = 8
HEAD_DIM = D_MODEL // N_HEADS
CHUNK = 128
GRID_W = 64
FF_HIDDEN = 2816
EPS = 1e-6
M_INIT = -1e30
N_GATES = 4 * N_HEADS

LANES = 128
BF16_ROWS = 16
VMEM_LIMIT = 56 * 1024 * 1024

PROJ_TM = 256
SEQ_HALO = BF16_ROWS
MERGE_TM = 512
MERGE_SUB = 256
FFN_TM = 512
FFN_CW = 256
FFN_DOWN_GROUP = 4

R_ROWB, R_DECAY = 0, 2
C_BASE, C_PER_DIR = 4, 4
F_COLA, F_AINT, F_ENEG, F_WK = range(4)
N_FIELDS = 16


def _const_spec(shape):
    nd = len(shape)
    return pl.BlockSpec(shape, lambda *_: (0,) * nd, pipeline_mode=pl.Buffered(1))


def _sigmoid(v):
    return 1.0 / (1.0 + jnp.exp(-v))


def _rms_scale(v):
    return v * lax.rsqrt(jnp.mean(v * v, axis=-1, keepdims=True) + EPS)


def _dot(a, b):
    return jnp.dot(a, b, preferred_element_type=F32)


def _ada_kernel(c_ref, w_ref, b_ref, o_ref):
    cv = c_ref[...]
    s = cv * _sigmoid(cv)
    o_ref[...] = jnp.dot(s, w_ref[...], preferred_element_type=F32,
                         precision=lax.Precision.HIGHEST) + b_ref[...]


def _ada(cc, w, b):
    rows, n = cc.shape[0], w.shape[1]
    tn = 1024
    return pl.pallas_call(
        _ada_kernel,
        grid=(n // tn,),
        in_specs=[pl.BlockSpec((rows, D_MODEL), lambda j: (0, 0)),
                  pl.BlockSpec((D_MODEL, tn), lambda j: (0, j)),
                  pl.BlockSpec((1, tn), lambda j: (0, j))],
        out_specs=pl.BlockSpec((rows, tn), lambda j: (0, j)),
        out_shape=jax.ShapeDtypeStruct((rows, n), F32),
        name="ada",
    )(cc, w, b)


def _conv3_rows(p, w_ref, b_ref, tm):
    n = tm + 2 * SEQ_HALO
    lo, hi = SEQ_HALO, SEQ_HALO + tm
    left = pltpu.roll(p, 1, 0)[lo:hi]
    right = pltpu.roll(p, n - 1, 0)[lo:hi]
    return left * w_ref[0:1, :] + p[lo:hi] * w_ref[1:2, :] + right * w_ref[2:3, :] + b_ref[...]


def _store_head_tiles_t(ref, val, tm):
    for h in range(N_HEADS):
        for ci in range(tm // CHUNK):
            tile = val[ci * CHUNK:(ci + 1) * CHUNK, h * HEAD_DIM:(h + 1) * HEAD_DIM]
            ref[0, h, ci] = tile.T.astype(BF16)


W_QK = slice(0, 2 * D_MODEL)
W_V = slice(2 * D_MODEL, 3 * D_MODEL)
W_CTX_COLS = 3 * D_MODEL
W_O = slice(3 * D_MODEL, 4 * D_MODEL)
W_SB = slice(4 * D_MODEL, 5 * D_MODEL)
W_SCX = slice(5 * D_MODEL, 7 * D_MODEL)
W_MG = slice(7 * D_MODEL, 9 * D_MODEL)


W_PREP_ROWS = 128


def _wprep_kernel(w_ref, o_ref):
    off_g = W_CTX_COLS
    o_ref[:, :off_g] = w_ref[:, :off_g].astype(BF16)
    o_ref[:, off_g:] = w_ref[:, off_g + N_GATES:].astype(BF16)


def _wprep(w):
    rows, cols = w.shape
    return pl.pallas_call(
        _wprep_kernel,
        grid=(rows // W_PREP_ROWS,),
        in_specs=[pl.BlockSpec((W_PREP_ROWS, cols), lambda i: (i, 0))],
        out_specs=pl.BlockSpec((W_PREP_ROWS, cols - N_GATES), lambda i: (i, 0)),
        out_shape=jax.ShapeDtypeStruct((rows, cols - N_GATES), BF16),
        compiler_params=pltpu.CompilerParams(dimension_semantics=("parallel",)),
        name="wprep",
    )(w)


def _proj_kernel(*refs, tm, full):
    if full:
        (xm_ref, xp_ref, xn_ref, mod_ref, g1_ref, w_ref, wg_ref, gb_ref, qkw_ref, qkb_ref,
         scw_ref, scb_ref, wco_ref,
         k_ref, vt_ref, g_ref, qt_ref, ogt_ref, gm_ref, zc_ref) = refs
    else:
        (xm_ref, xp_ref, xn_ref, mod_ref, g1_ref, w_ref, wg_ref, gb_ref, qkw_ref, qkb_ref,
         k_ref, vt_ref, g_ref) = refs
    t = pl.program_id(1)
    nt = pl.num_programs(1)
    shift = mod_ref[0, 0:1, :]
    scale1 = 1.0 + mod_ref[0, 1:2, :]
    gain = g1_ref[...]

    def norm_mod(xv):
        return _rms_scale(xv) * gain * scale1 + shift

    hm = norm_mod(xm_ref[0])
    hp = jnp.where(t > 0, norm_mod(xp_ref[0]), 0.0)
    hn = jnp.where(t < nt - 1, norm_mod(xn_ref[0]), 0.0)
    hmb = hm.astype(BF16)
    he = jnp.concatenate([hp, hm, hn], axis=0).astype(BF16)

    pqk = _dot(he, w_ref[:, W_QK])
    pv = _dot(hmb, w_ref[:, W_V])
    qk = _conv3_rows(pqk, qkw_ref, qkb_ref, tm)
    qk = qk * _sigmoid(qk)
    k_ref[0] = (qk[:, D_MODEL:] * (HEAD_DIM ** -0.5)).astype(BF16)
    pg = lax.dot_general(wg_ref[...], hmb, (((1,), (1,)), ((), ())), preferred_element_type=F32)
    if not full:
        _store_head_tiles_t(vt_ref, pv, tm)
        g_ref[0] = pg + gb_ref[...]
        return
    po = _dot(hmb, w_ref[:, W_O])
    _store_head_tiles_t(qt_ref, qk[:, :D_MODEL], tm)
    _store_head_tiles_t(vt_ref, pv, tm)
    g_ref[0] = pg + gb_ref[...]
    pcx = _dot(he, w_ref[:, W_SCX])
    _store_head_tiles_t(ogt_ref, _sigmoid(po), tm)
    psb = _dot(hmb, w_ref[:, W_SB])
    cu = _conv3_rows(pcx[:, :D_MODEL] * pcx[:, D_MODEL:], scw_ref, scb_ref, tm)
    pm = _dot(hmb, w_ref[:, W_MG])
    yc = _dot((psb * cu).astype(BF16), wco_ref[...])
    gm_ref[0] = _sigmoid(pm[:, :D_MODEL]).astype(BF16)
    zc_ref[0] = (_sigmoid(pm[:, D_MODEL:]) * yc).astype(BF16)


def _proj(xs, mod, g1, weights, *, full, per_batch_mod):
    bsz, t_len, _ = xs.shape
    tm = min(PROJ_TM, t_len)
    nt = t_len // tm
    hb = tm // SEQ_HALO
    nhb = t_len // SEQ_HALO
    x_specs = [
        pl.BlockSpec((1, tm, D_MODEL), lambda b, t: (b, t, 0)),
        pl.BlockSpec((1, SEQ_HALO, D_MODEL), lambda b, t: (b, jnp.maximum(t * hb - 1, 0), 0)),
        pl.BlockSpec((1, SEQ_HALO, D_MODEL), lambda b, t: (b, jnp.minimum((t + 1) * hb, nhb - 1), 0)),
    ]
    mod_spec = pl.BlockSpec((1, 8, D_MODEL), (lambda b, t: (b, 0, 0)) if per_batch_mod else (lambda b, t: (0, 0, 0)))
    w_cols = weights[0].shape[1] if full else W_CTX_COLS
    w_specs = [_const_spec((D_MODEL, w_cols))] + [_const_spec(w.shape) for w in weights[1:]]
    tok = lambda n, dt: (pl.BlockSpec((1, tm, n), lambda b, t: (b, t, 0)),
                         jax.ShapeDtypeStruct((bsz, t_len, n), dt))
    tiles_t = (pl.BlockSpec((1, N_HEADS, tm // CHUNK, HEAD_DIM, CHUNK), lambda b, t: (b, 0, t, 0, 0)),
               jax.ShapeDtypeStruct((bsz, N_HEADS, t_len // CHUNK, HEAD_DIM, CHUNK), BF16))
    gates_t = (pl.BlockSpec((1, N_GATES, tm), lambda b, t: (b, 0, t)),
               jax.ShapeDtypeStruct((bsz, N_GATES, t_len), F32))
    outs = [tok(D_MODEL, BF16), tiles_t, gates_t]
    if full:
        outs += [tiles_t, tiles_t, tok(D_MODEL, BF16), tok(D_MODEL, BF16)]
    return pl.pallas_call(
        functools.partial(_proj_kernel, tm=tm, full=full),
        grid=(bsz, nt),
        in_specs=x_specs + [mod_spec, _const_spec(g1.shape)] + w_specs,
        out_specs=[o[0] for o in outs],
        out_shape=[o[1] for o in outs],
        compiler_params=pltpu.CompilerParams(dimension_semantics=("parallel", "parallel"),
                                             vmem_limit_bytes=VMEM_LIMIT),
        name="proj_full" if full else "proj_ctx",
    )(xs, xs, xs, mod, g1, *weights)


def _scan_lanes(v, combine, fill, forward):
    lane = lax.broadcasted_iota(jnp.int32, v.shape, 1)
    d = 1
    while d < LANES:
        if forward:
            shifted = jnp.where(lane >= d, pltpu.roll(v, d, 1), fill)
        else:
            shifted = jnp.where(lane < LANES - d, pltpu.roll(v, LANES - d, 1), fill)
        v = combine(v, shifted)
        d *= 2
    return v


def _gates_kernel(gl_ref, gc_ref, out_ref, *, n_ctx, n_lat):
    n_all = n_ctx + n_lat
    src = lax.broadcasted_iota(jnp.int32, (CHUNK, 2 * CHUNK), 0)
    dst = lax.broadcasted_iota(jnp.int32, (CHUNK, 2 * CHUNK), 1)

    def gate_rows(cg, first):
        if cg < n_ctx:
            return gc_ref[0, first:first + N_HEADS, cg * CHUNK:(cg + 1) * CHUNK]
        return gl_ref[0, first:first + N_HEADS, (cg - n_ctx) * CHUNK:(cg - n_ctx + 1) * CHUNK]

    for dirn in range(2):
        forward = dirn == 0
        if forward:
            order = list(range(n_all))
        else:
            order = list(range(n_ctx - 1, -1, -1)) + list(range(n_all - 1, n_ctx - 1, -1))
        last = LANES - 1 if forward else 0
        first_row = 2 * N_HEADS * dirn
        fg = jnp.concatenate([gate_rows(cg, first_row + N_HEADS) for cg in range(n_all)], axis=0)
        lf = jnp.minimum(fg, 0.0) - jnp.log1p(jnp.exp(-jnp.abs(fg)))
        prefix = (src <= dst) if forward else (src >= dst)
        sel = jnp.where((dst >= CHUNK) | prefix, 1.0, 0.0).astype(BF16)
        hi = lf.astype(BF16)
        rest = lf - hi.astype(F32)
        mid = rest.astype(BF16)
        low = (rest - mid.astype(F32)).astype(BF16)
        sums = _dot(hi, sel) + _dot(mid, sel) + _dot(low, sel)
        m = jnp.full((N_HEADS, 1), M_INIT, F32)
        for cg in order:
            rows = slice(cg * N_HEADS, (cg + 1) * N_HEADS)
            b = sums[rows, :CHUNK]
            b_end = sums[rows, CHUNK:CHUNK + 1]
            r = gate_rows(cg, first_row) - b
            cm = _scan_lanes(r, jnp.maximum, -jnp.inf, forward)
            mx = jnp.maximum(m, cm)
            mt = b + mx
            m_new = b_end + jnp.maximum(m, cm[:, last:last + 1])
            cbase = C_BASE + C_PER_DIR * dirn
            fields = {
                R_ROWB + dirn: r,
                R_DECAY + dirn: jnp.broadcast_to(jnp.exp(b_end + m - m_new), (N_HEADS, CHUNK)),
                cbase + F_COLA: -mx,
                cbase + F_AINT: jnp.exp(m - mx),
                cbase + F_ENEG: jnp.exp(-mt),
                cbase + F_WK: jnp.exp((b_end - m_new) + r),
            }
            for f, val in fields.items():
                for h in range(N_HEADS):
                    out_ref[0, h, cg, f:f + 1, :] = val[h:h + 1, :]
            m = m_new
    n_used = C_BASE + 2 * C_PER_DIR
    out_ref[0, :, :, n_used:, :] = jnp.zeros((N_HEADS, n_all, N_FIELDS - n_used, CHUNK), F32)


def _gates(g_lat, g_ctx):
    bsz, _, t_lat = g_lat.shape
    t_ctx = g_ctx.shape[2]
    n_all = (t_lat + t_ctx) // CHUNK
    shape = (bsz, N_HEADS, n_all, N_FIELDS, CHUNK)
    return pl.pallas_call(
        functools.partial(_gates_kernel, n_ctx=t_ctx // CHUNK, n_lat=t_lat // CHUNK),
        grid=(bsz,),
        in_specs=[pl.BlockSpec((1, N_GATES, t_lat), lambda b: (b, 0, 0)),
                  pl.BlockSpec((1, N_GATES, t_ctx), lambda b: (b, 0, 0))],
        out_specs=pl.BlockSpec((1,) + shape[1:], lambda b: (b, 0, 0, 0, 0)),
        out_shape=jax.ShapeDtypeStruct(shape, F32),
        compiler_params=pltpu.CompilerParams(dimension_semantics=("parallel",)),
        name="gates",
    )(g_lat, g_ctx)


STATE_ROWS = HEAD_DIM + BF16_ROWS


def _mlstm_kernel(qt_ref, kl_ref, vtl_ref, kc_ref, vtc_ref, gf_ref, ogt_ref, ng_ref, out_ref,
                  dz_s, zf_s, zb_s, z_s, p_s, *, n_ctx, n_lat):
    lane = lax.broadcasted_iota(jnp.int32, (CHUNK, CHUNK), 1)
    row = lax.broadcasted_iota(jnp.int32, (CHUNK, CHUNK), 0)
    tri = (row <= lane, row >= lane)
    tail_first = lax.broadcasted_iota(jnp.int32, (BF16_ROWS, CHUNK), 0) == 0
    ones_tail = jnp.where(tail_first, 1.0, 0.0).astype(BF16)

    def field(cg, dirn, f):
        j = C_BASE + C_PER_DIR * dirn + f
        return gf_ref[0, 0, cg, j:j + 1, :]

    def pass_a(cg, k, vt):
        vf = vt.astype(F32)
        parts = []
        for dirn in range(2):
            wk = field(cg, dirn, F_WK)
            parts += [(vf * wk).astype(BF16), jnp.where(tail_first, wk, 0.0).astype(BF16)]
        dz_s[cg] = _dot(jnp.concatenate(parts, axis=0), k)

    for cg in range(n_ctx):
        pass_a(cg, kc_ref[0, cg * CHUNK:(cg + 1) * CHUNK, :], vtc_ref[0, 0, cg])

    def pass_a_lat(i, carry):
        sl = pl.ds(pl.multiple_of(i * CHUNK, CHUNK), CHUNK)
        pass_a(n_ctx + i, kl_ref[0, sl, :], vtl_ref[0, 0, i])
        return carry

    lax.fori_loop(0, n_lat, pass_a_lat, 0, unroll=8)

    def advance(dirn, cg):
        dec = gf_ref[0, 0, cg, R_DECAY + dirn:R_DECAY + dirn + 1, :]
        z_s[dirn] = dec * z_s[dirn] + dz_s[cg, dirn * STATE_ROWS:(dirn + 1) * STATE_ROWS, :]

    z_s[...] = jnp.zeros_like(z_s)
    for cg in range(n_ctx):
        advance(0, cg)
    for cg in range(n_ctx - 1, -1, -1):
        advance(1, cg)

    def pass_b(i, carry):
        zf_s[i] = z_s[0].astype(BF16)
        advance(0, n_ctx + i)
        j = n_lat - 1 - i
        zb_s[j] = z_s[1].astype(BF16)
        advance(1, n_ctx + j)
        return carry

    lax.fori_loop(0, n_lat, pass_b, 0, unroll=2)

    gain = jnp.broadcast_to(ng_ref[...], (CHUNK, HEAD_DIM)).T

    def pass_c1(i, carry):
        cg = n_ctx + i
        sl = pl.ds(pl.multiple_of(i * CHUNK, CHUNK), CHUNK)
        st = _dot(kl_ref[0, sl, :], qt_ref[0, 0, i])
        rowb = gf_ref[0, 0, cg].T
        for dirn in range(2):
            dmat = rowb[:, R_ROWB + dirn:R_ROWB + dirn + 1] + field(cg, dirn, F_COLA)
            w = jnp.exp(jnp.where(tri[dirn], dmat, -jnp.inf))
            p_s[i, dirn] = (st * w).astype(BF16)
        return carry

    lax.fori_loop(0, n_lat, pass_c1, 0, unroll=16)

    def pass_c2(i, carry):
        cg = n_ctx + i
        qtf = qt_ref[0, 0, i].astype(F32)
        vaug = jnp.concatenate([vtl_ref[0, 0, i], ones_tail], axis=0)
        ht = None
        for dirn, z_ref in enumerate((zf_s, zb_s)):
            rhs = jnp.concatenate([(qtf * field(cg, dirn, F_AINT)).astype(BF16), p_s[i, dirn]], axis=0)
            res = _dot(jnp.concatenate([z_ref[i], vaug], axis=1), rhs)
            den = jnp.maximum(jnp.abs(res[HEAD_DIM:HEAD_DIM + 1, :]), field(cg, dirn, F_ENEG))
            hd = res[:HEAD_DIM, :] / den
            ht = hd if ht is None else ht + hd
        ht = ht * lax.rsqrt(jnp.mean(ht * ht, axis=0, keepdims=True) + EPS)
        out_ref[0, 0, i] = (ht * gain * ogt_ref[0, 0, i].astype(F32)).astype(BF16)
        return carry

    lax.fori_loop(0, n_lat, pass_c2, 0, unroll=16)


def _mlstm(qt_l, k_l, vt_l, k_c, vt_c, gf, ogt, ng):
    bsz, t_lat, _ = k_l.shape
    t_ctx = k_c.shape[1]
    n_lat, n_ctx = t_lat // CHUNK, t_ctx // CHUNK
    n_all = n_lat + n_ctx
    head_lat = pl.BlockSpec((1, t_lat, HEAD_DIM), lambda b, h: (b, 0, h))
    head_ctx = pl.BlockSpec((1, t_ctx, HEAD_DIM), lambda b, h: (b, 0, h))
    tiles = lambda n: pl.BlockSpec((1, 1, n, HEAD_DIM, CHUNK), lambda b, h: (b, h, 0, 0, 0))
    return pl.pallas_call(
        functools.partial(_mlstm_kernel, n_ctx=n_ctx, n_lat=n_lat),
        grid=(bsz, N_HEADS),
        in_specs=[tiles(n_lat), head_lat, tiles(n_lat), head_ctx, tiles(n_ctx),
                  pl.BlockSpec((1, 1, n_all, N_FIELDS, CHUNK), lambda b, h: (b, h, 0, 0, 0)),
                  tiles(n_lat),
                  pl.BlockSpec((1, HEAD_DIM), lambda b, h: (0, h))],
        out_specs=tiles(n_lat),
        out_shape=jax.ShapeDtypeStruct(qt_l.shape, BF16),
        scratch_shapes=[pltpu.VMEM((n_all, 2 * STATE_ROWS, HEAD_DIM), F32),
                        pltpu.VMEM((n_lat, STATE_ROWS, HEAD_DIM), BF16),
                        pltpu.VMEM((n_lat, STATE_ROWS, HEAD_DIM), BF16),
                        pltpu.VMEM((2, STATE_ROWS, HEAD_DIM), F32),
                        pltpu.VMEM((n_lat, 2, CHUNK, CHUNK), BF16)],
        compiler_params=pltpu.CompilerParams(dimension_semantics=("parallel", "parallel"),
                                             vmem_limit_bytes=VMEM_LIMIT),
        name="mlstm",
    )(qt_l, k_l, vt_l, k_c, vt_c, gf, ogt, ng)


def _merge_kernel(hgt_ref, gm_ref, zc_ref, x_ref, mod_ref, g2_ref, wmo_ref, wo_ref, x1_ref, h2_ref):
    per_sub = MERGE_SUB // CHUNK
    subs = [slice(i * MERGE_SUB, (i + 1) * MERGE_SUB) for i in range(hgt_ref.shape[2] // per_sub)]
    hg = [jnp.concatenate(
        [jnp.concatenate([hgt_ref[0, h, ci].astype(F32).T for h in range(N_HEADS)], axis=1)
         for ci in range(i * per_sub, (i + 1) * per_sub)], axis=0).astype(BF16) for i in range(len(subs))]
    ym = [_dot(hg_i, wmo_ref[...]) for hg_i in hg]
    y = [_dot((gm_ref[0, sl, :].astype(F32) * ym_i + zc_ref[0, sl, :].astype(F32)).astype(BF16), wo_ref[...])
         for sl, ym_i in zip(subs, ym)]
    for sl, y_i in zip(subs, y):
        x1 = x_ref[0, sl, :] + mod_ref[0, 2:3, :] * y_i
        x1_ref[0, sl, :] = x1
        h2 = _rms_scale(x1) * g2_ref[...] * (1.0 + mod_ref[0, 4:5, :]) + mod_ref[0, 3:4, :]
        h2_ref[0, sl, :] = h2.astype(BF16)


def _merge(hgt, gm, zc, x, mod, g2, wmo, wo):
    bsz, t_len, _ = x.shape
    tm = MERGE_TM
    tok = pl.BlockSpec((1, tm, D_MODEL), lambda b, t: (b, t, 0))
    tiles = pl.BlockSpec((1, N_HEADS, tm // CHUNK, HEAD_DIM, CHUNK), lambda b, t: (b, 0, t, 0, 0))
    return pl.pallas_call(
        _merge_kernel,
        grid=(bsz, t_len // tm),
        in_specs=[tiles, tok, tok, tok, pl.BlockSpec((1, 8, D_MODEL), lambda b, t: (b, 0, 0)),
                  _const_spec(g2.shape), _const_spec(wmo.shape), _const_spec(wo.shape)],
        out_specs=[tok, tok],
        out_shape=[jax.ShapeDtypeStruct(x.shape, F32), jax.ShapeDtypeStruct(x.shape, BF16)],
        compiler_params=pltpu.CompilerParams(dimension_semantics=("parallel", "parallel"),
                                             vmem_limit_bytes=VMEM_LIMIT),
        name="merge",
    )(hgt, gm, zc, x, mod, g2, wmo, wo)


def _gelu_tanh(v):
    return 0.5 * v * (1.0 + jnp.tanh(0.7978845608028654 * (v + 0.044715 * (v * v * v))))


def _ffn_kernel(hm_ref, hp_ref, hn_ref, x1_ref, mod_ref, wu_ref, cw_ref, cb_ref, wd_ref, fg_ref,
                out_ref, act_s, *, tm):
    t = pl.program_id(1)
    nt = pl.num_programs(1)
    n = tm + 2 * GRID_W
    hm = hm_ref[0]
    hp = jnp.where(t > 0, hp_ref[0], jnp.zeros_like(hp_ref[0]))
    hn = jnp.where(t < nt - 1, hn_ref[0], jnp.zeros_like(hn_ref[0]))
    he = jnp.concatenate([hp, hm, hn], axis=0)
    gcol = lax.broadcasted_iota(jnp.int32, (n, 1), 0) & (GRID_W - 1)
    has_left = gcol != 0
    has_right = gcol != GRID_W - 1
    n_chunks = FF_HIDDEN // FFN_CW
    cols = lambda j: slice(j * FFN_CW, (j + 1) * FFN_CW)
    gate_cols = lambda j: slice(FF_HIDDEN + j * FFN_CW, FF_HIDDEN + (j + 1) * FFN_CW)
    up = lambda j: (_dot(he, wu_ref[:, cols(j)]), _dot(hm, wu_ref[:, gate_cols(j)]))
    acc = None
    nxt = up(0)
    for j in range(n_chunks):
        cs = cols(j)
        a, g = nxt
        if j + 1 < n_chunks:
            nxt = up(j + 1)
        taps = (jnp.where(has_left, pltpu.roll(a, 1, 0), 0.0), a,
                jnp.where(has_right, pltpu.roll(a, n - 1, 0), 0.0))
        conv = cb_ref[:, cs]
        for dr in range(3):
            for dc in range(3):
                conv = conv + cw_ref[3 * dr + dc: 3 * dr + dc + 1, cs] * taps[dc][GRID_W * dr: GRID_W * dr + tm]
        act_s[:, cs] = (_gelu_tanh(conv) * g).astype(BF16)
        if (j + 1) % FFN_DOWN_GROUP == 0 or j + 1 == n_chunks:
            gs = slice((j // FFN_DOWN_GROUP) * FFN_DOWN_GROUP * FFN_CW, (j + 1) * FFN_CW)
            part = _dot(act_s[:, gs], wd_ref[gs, :])
            acc = part if acc is None else acc + part
    x2 = x1_ref[0] + mod_ref[0, 5:6, :] * acc
    out_ref[0] = _rms_scale(x2) * fg_ref[...]


def _ffn(h2, x1, mod, wu, cw, cb, wd, fg):
    bsz, t_len, _ = x1.shape
    tm = FFN_TM
    rb = tm // GRID_W
    nrb = t_len // GRID_W
    tok = pl.BlockSpec((1, tm, D_MODEL), lambda b, t: (b, t, 0))
    return pl.pallas_call(
        functools.partial(_ffn_kernel, tm=tm),
        grid=(bsz, t_len // tm),
        in_specs=[tok,
                  pl.BlockSpec((1, GRID_W, D_MODEL), lambda b, t: (b, jnp.maximum(t * rb - 1, 0), 0)),
                  pl.BlockSpec((1, GRID_W, D_MODEL), lambda b, t: (b, jnp.minimum((t + 1) * rb, nrb - 1), 0)),
                  tok, pl.BlockSpec((1, 8, D_MODEL), lambda b, t: (b, 0, 0)),
                  _const_spec(wu.shape), _const_spec(cw.shape), _const_spec(cb.shape),
                  _const_spec(wd.shape), _const_spec(fg.shape)],
        out_specs=tok,
        out_shape=jax.ShapeDtypeStruct(x1.shape, F32),
        scratch_shapes=[pltpu.VMEM((tm, FF_HIDDEN), BF16)],
        compiler_params=pltpu.CompilerParams(dimension_semantics=("parallel", "parallel"),
                                             vmem_limit_bytes=VMEM_LIMIT),
        name="ffn",
    )(h2, h2, h2, x1, mod, wu, cw, cb, wd, fg)


def kernel(x, c, ctx, c_ctx, ada_w, ada_b, norm1_g, norm2_g, w_in, qk_conv_w, qk_conv_b, gate_b, mnorm_g,
           w_m_out, sc_conv_w, sc_conv_b, w_c_out, w_o, w_up, ff_conv_w, ff_conv_b, w_down, final_g):
    assert ada_w.shape[0] == 1, "single-layer block"
    bsz, t_lat, _ = x.shape
    t_ctx = ctx.shape[1]

    cc = jnp.zeros((2 * 8, D_MODEL), F32).at[:bsz].set(c).at[bsz].set(c_ctx)
    mod = _ada(cc, ada_w[0], ada_b[0][None, :])
    mod_x = jnp.pad(mod[:bsz].reshape(bsz, 6, D_MODEL), ((0, 0), (0, 2), (0, 0)))
    mod_c = jnp.pad(mod[bsz].reshape(1, 6, D_MODEL), ((0, 0), (0, 2), (0, 0)))

    w = w_in[0]
    off_g = 3 * D_MODEL
    off_o = off_g + N_GATES
    cast = lambda a: a.astype(BF16)
    w_main = _wprep(w)
    wgt = cast(w[:, off_g:off_o].T)
    gb = gate_b[0][:, None]
    g1 = norm1_g[0][None, :]
    ctx_w = [w_main, wgt, gb, qk_conv_w[0], qk_conv_b[0][None, :]]
    lat_w = ctx_w + [sc_conv_w[0], sc_conv_b[0][None, :], cast(w_c_out[0])]

    k_l, vt_l, g_l, qt_l, ogt, gm, zc = _proj(x, mod_x, g1, lat_w, full=True, per_batch_mod=True)
    k_c, vt_c, g_c = _proj(ctx, mod_c, g1, ctx_w, full=False, per_batch_mod=False)

    gf = _gates(g_l, g_c)
    hgt = _mlstm(qt_l, k_l, vt_l, k_c, vt_c, gf, ogt, mnorm_g[0][None, :])

    x1, h2 = _merge(hgt, gm, zc, x, mod_x, norm2_g[0][None, :], cast(w_m_out[0]), cast(w_o[0]))

    out = _ffn(h2, x1, mod_x, cast(w_up[0]),
               ff_conv_w[0].reshape(9, FF_HIDDEN), ff_conv_b[0][None, :], cast(w_down[0]),
               final_g[None, :])
    return out
```

```python
import functools

import jax
import jax.numpy as jnp
from jax import lax
from jax.experimental import pallas as pl
from jax.experimental.pallas import tpu as pltpu

F32 = jnp.float32
BF16 = jnp.bfloat16

D_MODEL = 1024
N_HEADS = 8
HEAD_DIM = D_MODEL // N_HEADS
CHUNK = 128
GRID_W = 64
FF_HIDDEN = 2816
EPS = 1e-6
M_INIT = -1e30
N_GATES = 4 * N_HEADS

LANES = 128
BF16_ROWS = 16
VMEM_LIMIT = 56 * 1024 * 1024

PROJ_TM = 256
SEQ_HALO = BF16_ROWS
MERGE_TM = 512
MERGE_SUB = 256
FFN_TM = 512
FFN_CW = 256
FFN_DOWN_GROUP = 4

R_ROWB, R_DECAY = 0, 2
C_BASE, C_PER_DIR = 4, 4
F_COLA, F_AINT, F_ENEG, F_WK = range(4)
N_FIELDS = 16


def _const_spec(shape):
    nd = len(shape)
    return pl.BlockSpec(shape, lambda *_: (0,) * nd, pipeline_mode=pl.Buffered(1))


def _sigmoid(v):
    return 1.0 / (1.0 + jnp.exp(-v))


def _rms_scale(v):
    return v * lax.rsqrt(jnp.mean(v * v, axis=-1, keepdims=True) + EPS)


def _dot(a, b):
    return jnp.dot(a, b, preferred_element_type=F32)


def _ada_kernel(c_ref, w_ref, b_ref, o_ref):
    cv = c_ref[...]
    s = cv * _sigmoid(cv)
    o_ref[...] = jnp.dot(s, w_ref[...], preferred_element_type=F32,
                         precision=lax.Precision.HIGHEST) + b_ref[...]


def _ada(cc, w, b):
    rows, n = cc.shape[0], w.shape[1]
    tn = 1024
    return pl.pallas_call(
        _ada_kernel,
        grid=(n // tn,),
        in_specs=[pl.BlockSpec((rows, D_MODEL), lambda j: (0, 0)),
                  pl.BlockSpec((D_MODEL, tn), lambda j: (0, j)),
                  pl.BlockSpec((1, tn), lambda j: (0, j))],
        out_specs=pl.BlockSpec((rows, tn), lambda j: (0, j)),
        out_shape=jax.ShapeDtypeStruct((rows, n), F32),
        name="ada",
    )(cc, w, b)


def _conv3_rows(p, w_ref, b_ref, tm):
    n = tm + 2 * SEQ_HALO
    lo, hi = SEQ_HALO, SEQ_HALO + tm
    left = pltpu.roll(p, 1, 0)[lo:hi]
    right = pltpu.roll(p, n - 1, 0)[lo:hi]
    return left * w_ref[0:1, :] + p[lo:hi] * w_ref[1:2, :] + right * w_ref[2:3, :] + b_ref[...]


def _store_head_tiles_t(ref, val, tm):
    for h in range(N_HEADS):
        for ci in range(tm // CHUNK):
            tile = val[ci * CHUNK:(ci + 1) * CHUNK, h * HEAD_DIM:(h + 1) * HEAD_DIM]
            ref[0, h, ci] = tile.T.astype(BF16)


W_QK = slice(0, 2 * D_MODEL)
W_V = slice(2 * D_MODEL, 3 * D_MODEL)
W_G = slice(3 * D_MODEL, 3 * D_MODEL + N_GATES)
W_CTX_ROWS = W_G.stop
W_O = slice(W_G.stop, W_G.stop + D_MODEL)
W_SB = slice(W_O.stop, W_O.stop + D_MODEL)
W_SCX = slice(W_SB.stop, W_SB.stop + 2 * D_MODEL)
W_MG = slice(W_SCX.stop, W_SCX.stop + 2 * D_MODEL)


def _dot_nt(a, b):
    return lax.dot_general(a, b, (((1,), (1,)), ((), ())), preferred_element_type=F32)


def _proj_kernel(*refs, tm, full):
    if full:
        (xm_ref, xp_ref, xn_ref, mod_ref, g1_ref, w_ref, gb_ref, qkw_ref, qkb_ref,
         scw_ref, scb_ref, wco_ref,
         k_ref, vt_ref, g_ref, qt_ref, ogt_ref, gm_ref, zc_ref) = refs
    else:
        (xm_ref, xp_ref, xn_ref, mod_ref, g1_ref, w_ref, gb_ref, qkw_ref, qkb_ref,
         k_ref, vt_ref, g_ref) = refs
    t = pl.program_id(1)
    nt = pl.num_programs(1)
    shift = mod_ref[0, 0:1, :]
    scale1 = 1.0 + mod_ref[0, 1:2, :]
    gain = g1_ref[...]

    def norm_mod(xv):
        return _rms_scale(xv) * gain * scale1 + shift

    hm = norm_mod(xm_ref[0])
    hp = jnp.where(t > 0, norm_mod(xp_ref[0]), 0.0)
    hn = jnp.where(t < nt - 1, norm_mod(xn_ref[0]), 0.0)
    hmb = hm.astype(BF16)
    he = jnp.concatenate([hp, hm, hn], axis=0).astype(BF16)

    pqk = _dot_nt(he, w_ref[W_QK, :])
    pv = _dot_nt(hmb, w_ref[W_V, :])
    qk = _conv3_rows(pqk, qkw_ref, qkb_ref, tm)
    qk = qk * _sigmoid(qk)
    k_ref[0] = (qk[:, D_MODEL:] * (HEAD_DIM ** -0.5)).astype(BF16)
    pg = _dot_nt(w_ref[W_G, :], hmb)
    if not full:
        _store_head_tiles_t(vt_ref, pv, tm)
        g_ref[0] = pg + gb_ref[...]
        return
    po = _dot_nt(hmb, w_ref[W_O, :])
    _store_head_tiles_t(qt_ref, qk[:, :D_MODEL], tm)
    _store_head_tiles_t(vt_ref, pv, tm)
    g_ref[0] = pg + gb_ref[...]
    pcx = _dot_nt(he, w_ref[W_SCX, :])
    _store_head_tiles_t(ogt_ref, _sigmoid(po), tm)
    psb = _dot_nt(hmb, w_ref[W_SB, :])
    cu = _conv3_rows(pcx[:, :D_MODEL] * pcx[:, D_MODEL:], scw_ref, scb_ref, tm)
    pm = _dot_nt(hmb, w_ref[W_MG, :])
    yc = _dot((psb * cu).astype(BF16), wco_ref[...])
    gm_ref[0] = _sigmoid(pm[:, :D_MODEL]).astype(BF16)
    zc_ref[0] = (_sigmoid(pm[:, D_MODEL:]) * yc).astype(BF16)


def _proj(xs, mod, g1, weights, *, full, per_batch_mod):
    bsz, t_len, _ = xs.shape
    tm = min(PROJ_TM, t_len)
    nt = t_len // tm
    hb = tm // SEQ_HALO
    nhb = t_len // SEQ_HALO
    x_specs = [
        pl.BlockSpec((1, tm, D_MODEL), lambda b, t: (b, t, 0)),
        pl.BlockSpec((1, SEQ_HALO, D_MODEL), lambda b, t: (b, jnp.maximum(t * hb - 1, 0), 0)),
        pl.BlockSpec((1, SEQ_HALO, D_MODEL), lambda b, t: (b, jnp.minimum((t + 1) * hb, nhb - 1), 0)),
    ]
    mod_spec = pl.BlockSpec((1, 8, D_MODEL), (lambda b, t: (b, 0, 0)) if per_batch_mod else (lambda b, t: (0, 0, 0)))
    w_rows = weights[0].shape[0] if full else W_CTX_ROWS
    w_specs = [_const_spec((w_rows, D_MODEL))] + [_const_spec(w.shape) for w in weights[1:]]
    tok = lambda n, dt: (pl.BlockSpec((1, tm, n), lambda b, t: (b, t, 0)),
                         jax.ShapeDtypeStruct((bsz, t_len, n), dt))
    tiles_t = (pl.BlockSpec((1, N_HEADS, tm // CHUNK, HEAD_DIM, CHUNK), lambda b, t: (b, 0, t, 0, 0)),
               jax.ShapeDtypeStruct((bsz, N_HEADS, t_len // CHUNK, HEAD_DIM, CHUNK), BF16))
    gates_t = (pl.BlockSpec((1, N_GATES, tm), lambda b, t: (b, 0, t)),
               jax.ShapeDtypeStruct((bsz, N_GATES, t_len), F32))
    outs = [tok(D_MODEL, BF16), tiles_t, gates_t]
    if full:
        outs += [tiles_t, tiles_t, tok(D_MODEL, BF16), tok(D_MODEL, BF16)]
    return pl.pallas_call(
        functools.partial(_proj_kernel, tm=tm, full=full),
        grid=(bsz, nt),
        in_specs=x_specs + [mod_spec, _const_spec(g1.shape)] + w_specs,
        out_specs=[o[0] for o in outs],
        out_shape=[o[1] for o in outs],
        compiler_params=pltpu.CompilerParams(dimension_semantics=("parallel", "parallel"),
                                             vmem_limit_bytes=VMEM_LIMIT),
        name="proj_full" if full else "proj_ctx",
    )(xs, xs, xs, mod, g1, *weights)


def _scan_lanes(v, combine, fill, forward):
    lane = lax.broadcasted_iota(jnp.int32, v.shape, 1)
    d = 1
    while d < LANES:
        if forward:
            shifted = jnp.where(lane >= d, pltpu.roll(v, d, 1), fill)
        else:
            shifted = jnp.where(lane < LANES - d, pltpu.roll(v, LANES - d, 1), fill)
        v = combine(v, shifted)
        d *= 2
    return v


def _gates_kernel(gl_ref, gc_ref, out_ref, *, n_ctx, n_lat):
    n_all = n_ctx + n_lat
    src = lax.broadcasted_iota(jnp.int32, (CHUNK, 2 * CHUNK), 0)
    dst = lax.broadcasted_iota(jnp.int32, (CHUNK, 2 * CHUNK), 1)

    def gate_rows(cg, first):
        if cg < n_ctx:
            return gc_ref[0, first:first + N_HEADS, cg * CHUNK:(cg + 1) * CHUNK]
        return gl_ref[0, first:first + N_HEADS, (cg - n_ctx) * CHUNK:(cg - n_ctx + 1) * CHUNK]

    for dirn in range(2):
        forward = dirn == 0
        if forward:
            order = list(range(n_all))
        else:
            order = list(range(n_ctx - 1, -1, -1)) + list(range(n_all - 1, n_ctx - 1, -1))
        last = LANES - 1 if forward else 0
        first_row = 2 * N_HEADS * dirn
        fg = jnp.concatenate([gate_rows(cg, first_row + N_HEADS) for cg in range(n_all)], axis=0)
        lf = jnp.minimum(fg, 0.0) - jnp.log1p(jnp.exp(-jnp.abs(fg)))
        prefix = (src <= dst) if forward else (src >= dst)
        sel = jnp.where((dst >= CHUNK) | prefix, 1.0, 0.0).astype(BF16)
        hi = lf.astype(BF16)
        rest = lf - hi.astype(F32)
        mid = rest.astype(BF16)
        low = (rest - mid.astype(F32)).astype(BF16)
        sums = _dot(hi, sel) + _dot(mid, sel) + _dot(low, sel)
        m = jnp.full((N_HEADS, 1), M_INIT, F32)
        for cg in order:
            rows = slice(cg * N_HEADS, (cg + 1) * N_HEADS)
            b = sums[rows, :CHUNK]
            b_end = sums[rows, CHUNK:CHUNK + 1]
            r = gate_rows(cg, first_row) - b
            cm = _scan_lanes(r, jnp.maximum, -jnp.inf, forward)
            mx = jnp.maximum(m, cm)
            mt = b + mx
            m_new = b_end + jnp.maximum(m, cm[:, last:last + 1])
            cbase = C_BASE + C_PER_DIR * dirn
            fields = {
                R_ROWB + dirn: r,
                R_DECAY + dirn: jnp.broadcast_to(jnp.exp(b_end + m - m_new), (N_HEADS, CHUNK)),
                cbase + F_COLA: -mx,
                cbase + F_AINT: jnp.exp(m - mx),
                cbase + F_ENEG: jnp.exp(-mt),
                cbase + F_WK: jnp.exp((b_end - m_new) + r),
            }
            for f, val in fields.items():
                for h in range(N_HEADS):
                    out_ref[0, h, cg, f:f + 1, :] = val[h:h + 1, :]
            m = m_new
    n_used = C_BASE + 2 * C_PER_DIR
    out_ref[0, :, :, n_used:, :] = jnp.zeros((N_HEADS, n_all, N_FIELDS - n_used, CHUNK), F32)


def _gates(g_lat, g_ctx):
    bsz, _, t_lat = g_lat.shape
    t_ctx = g_ctx.shape[2]
    n_all = (t_lat + t_ctx) // CHUNK
    shape = (bsz, N_HEADS, n_all, N_FIELDS, CHUNK)
    return pl.pallas_call(
        functools.partial(_gates_kernel, n_ctx=t_ctx // CHUNK, n_lat=t_lat // CHUNK),
        grid=(bsz,),
        in_specs=[pl.BlockSpec((1, N_GATES, t_lat), lambda b: (b, 0, 0)),
                  pl.BlockSpec((1, N_GATES, t_ctx), lambda b: (b, 0, 0))],
        out_specs=pl.BlockSpec((1,) + shape[1:], lambda b: (b, 0, 0, 0, 0)),
        out_shape=jax.ShapeDtypeStruct(shape, F32),
        compiler_params=pltpu.CompilerParams(dimension_semantics=("parallel",)),
        name="gates",
    )(g_lat, g_ctx)


STATE_ROWS = HEAD_DIM + BF16_ROWS


def _mlstm_kernel(qt_ref, kl_ref, vtl_ref, kc_ref, vtc_ref, gf_ref, ogt_ref, ng_ref, out_ref,
                  dz_s, zf_s, zb_s, z_s, p_s, *, n_ctx, n_lat):
    lane = lax.broadcasted_iota(jnp.int32, (CHUNK, CHUNK), 1)
    row = lax.broadcasted_iota(jnp.int32, (CHUNK, CHUNK), 0)
    tri = (row <= lane, row >= lane)
    tail_first = lax.broadcasted_iota(jnp.int32, (BF16_ROWS, CHUNK), 0) == 0
    ones_tail = jnp.where(tail_first, 1.0, 0.0).astype(BF16)

    def field(cg, dirn, f):
        j = C_BASE + C_PER_DIR * dirn + f
        return gf_ref[0, 0, cg, j:j + 1, :]

    def pass_a(cg, k, vt):
        vf = vt.astype(F32)
        parts = []
        for dirn in range(2):
            wk = field(cg, dirn, F_WK)
            parts += [(vf * wk).astype(BF16), jnp.where(tail_first, wk, 0.0).astype(BF16)]
        dz_s[cg] = _dot(jnp.concatenate(parts, axis=0), k)

    for cg in range(n_ctx):
        pass_a(cg, kc_ref[0, cg * CHUNK:(cg + 1) * CHUNK, :], vtc_ref[0, 0, cg])

    def pass_a_lat(i, carry):
        sl = pl.ds(pl.multiple_of(i * CHUNK, CHUNK), CHUNK)
        pass_a(n_ctx + i, kl_ref[0, sl, :], vtl_ref[0, 0, i])
        return carry

    lax.fori_loop(0, n_lat, pass_a_lat, 0, unroll=8)

    def advance(dirn, cg):
        dec = gf_ref[0, 0, cg, R_DECAY + dirn:R_DECAY + dirn + 1, :]
        z_s[dirn] = dec * z_s[dirn] + dz_s[cg, dirn * STATE_ROWS:(dirn + 1) * STATE_ROWS, :]

    z_s[...] = jnp.zeros_like(z_s)
    for cg in range(n_ctx):
        advance(0, cg)
    for cg in range(n_ctx - 1, -1, -1):
        advance(1, cg)

    def pass_b(i, carry):
        zf_s[i] = z_s[0].astype(BF16)
        advance(0, n_ctx + i)
        j = n_lat - 1 - i
        zb_s[j] = z_s[1].astype(BF16)
        advance(1, n_ctx + j)
        return carry

    lax.fori_loop(0, n_lat, pass_b, 0, unroll=2)

    gain = jnp.broadcast_to(ng_ref[...], (CHUNK, HEAD_DIM)).T

    def pass_c1(i, carry):
        cg = n_ctx + i
        sl = pl.ds(pl.multiple_of(i * CHUNK, CHUNK), CHUNK)
        st = _dot(kl_ref[0, sl, :], qt_ref[0, 0, i])
        rowb = gf_ref[0, 0, cg].T
        for dirn in range(2):
            dmat = rowb[:, R_ROWB + dirn:R_ROWB + dirn + 1] + field(cg, dirn, F_COLA)
            w = jnp.exp(jnp.where(tri[dirn], dmat, -jnp.inf))
            p_s[i, dirn] = (st * w).astype(BF16)
        return carry

    lax.fori_loop(0, n_lat, pass_c1, 0, unroll=16)

    def pass_c2(i, carry):
        cg = n_ctx + i
        qtf = qt_ref[0, 0, i].astype(F32)
        vaug = jnp.concatenate([vtl_ref[0, 0, i], ones_tail], axis=0)
        ht = None
        for dirn, z_ref in enumerate((zf_s, zb_s)):
            rhs = jnp.concatenate([(qtf * field(cg, dirn, F_AINT)).astype(BF16), p_s[i, dirn]], axis=0)
            res = _dot(jnp.concatenate([z_ref[i], vaug], axis=1), rhs)
            den = jnp.maximum(jnp.abs(res[HEAD_DIM:HEAD_DIM + 1, :]), field(cg, dirn, F_ENEG))
            hd = res[:HEAD_DIM, :] / den
            ht = hd if ht is None else ht + hd
        ht = ht * lax.rsqrt(jnp.mean(ht * ht, axis=0, keepdims=True) + EPS)
        out_ref[0, 0, i] = (ht * gain * ogt_ref[0, 0, i].astype(F32)).astype(BF16)
        return carry

    lax.fori_loop(0, n_lat, pass_c2, 0, unroll=16)


def _mlstm(qt_l, k_l, vt_l, k_c, vt_c, gf, ogt, ng):
    bsz, t_lat, _ = k_l.shape
    t_ctx = k_c.shape[1]
    n_lat, n_ctx = t_lat // CHUNK, t_ctx // CHUNK
    n_all = n_lat + n_ctx
    head_lat = pl.BlockSpec((1, t_lat, HEAD_DIM), lambda b, h: (b, 0, h))
    head_ctx = pl.BlockSpec((1, t_ctx, HEAD_DIM), lambda b, h: (b, 0, h))
    tiles = lambda n: pl.BlockSpec((1, 1, n, HEAD_DIM, CHUNK), lambda b, h: (b, h, 0, 0, 0))
    return pl.pallas_call(
        functools.partial(_mlstm_kernel, n_ctx=n_ctx, n_lat=n_lat),
        grid=(bsz, N_HEADS),
        in_specs=[tiles(n_lat), head_lat, tiles(n_lat), head_ctx, tiles(n_ctx),
                  pl.BlockSpec((1, 1, n_all, N_FIELDS, CHUNK), lambda b, h: (b, h, 0, 0, 0)),
                  tiles(n_lat),
                  pl.BlockSpec((1, HEAD_DIM), lambda b, h: (0, h))],
        out_specs=tiles(n_lat),
        out_shape=jax.ShapeDtypeStruct(qt_l.shape, BF16),
        scratch_shapes=[pltpu.VMEM((n_all, 2 * STATE_ROWS, HEAD_DIM), F32),
                        pltpu.VMEM((n_lat, STATE_ROWS, HEAD_DIM), BF16),
                        pltpu.VMEM((n_lat, STATE_ROWS, HEAD_DIM), BF16),
                        pltpu.VMEM((2, STATE_ROWS, HEAD_DIM), F32),
                        pltpu.VMEM((n_lat, 2, CHUNK, CHUNK), BF16)],
        compiler_params=pltpu.CompilerParams(dimension_semantics=("parallel", "parallel"),
                                             vmem_limit_bytes=VMEM_LIMIT),
        name="mlstm",
    )(qt_l, k_l, vt_l, k_c, vt_c, gf, ogt, ng)


def _merge_kernel(hgt_ref, gm_ref, zc_ref, x_ref, mod_ref, g2_ref, wmo_ref, wo_ref, x1_ref, h2_ref):
    per_sub = MERGE_SUB // CHUNK
    subs = [slice(i * MERGE_SUB, (i + 1) * MERGE_SUB) for i in range(hgt_ref.shape[2] // per_sub)]
    hg = [jnp.concatenate(
        [jnp.concatenate([hgt_ref[0, h, ci].astype(F32).T for h in range(N_HEADS)], axis=1)
         for ci in range(i * per_sub, (i + 1) * per_sub)], axis=0).astype(BF16) for i in range(len(subs))]
    ym = [_dot(hg_i, wmo_ref[...]) for hg_i in hg]
    y = [_dot((gm_ref[0, sl, :].astype(F32) * ym_i + zc_ref[0, sl, :].astype(F32)).astype(BF16), wo_ref[...])
         for sl, ym_i in zip(subs, ym)]
    for sl, y_i in zip(subs, y):
        x1 = x_ref[0, sl, :] + mod_ref[0, 2:3, :] * y_i
        x1_ref[0, sl, :] = x1
        h2 = _rms_scale(x1) * g2_ref[...] * (1.0 + mod_ref[0, 4:5, :]) + mod_ref[0, 3:4, :]
        h2_ref[0, sl, :] = h2.astype(BF16)


def _merge(hgt, gm, zc, x, mod, g2, wmo, wo):
    bsz, t_len, _ = x.shape
    tm = MERGE_TM
    tok = pl.BlockSpec((1, tm, D_MODEL), lambda b, t: (b, t, 0))
    tiles = pl.BlockSpec((1, N_HEADS, tm // CHUNK, HEAD_DIM, CHUNK), lambda b, t: (b, 0, t, 0, 0))
    return pl.pallas_call(
        _merge_kernel,
        grid=(bsz, t_len // tm),
        in_specs=[tiles, tok, tok, tok, pl.BlockSpec((1, 8, D_MODEL), lambda b, t: (b, 0, 0)),
                  _const_spec(g2.shape), _const_spec(wmo.shape), _const_spec(wo.shape)],
        out_specs=[tok, tok],
        out_shape=[jax.ShapeDtypeStruct(x.shape, F32), jax.ShapeDtypeStruct(x.shape, BF16)],
        compiler_params=pltpu.CompilerParams(dimension_semantics=("parallel", "parallel"),
                                             vmem_limit_bytes=VMEM_LIMIT),
        name="merge",
    )(hgt, gm, zc, x, mod, g2, wmo, wo)


def _gelu_tanh(v):
    return 0.5 * v * (1.0 + jnp.tanh(0.7978845608028654 * (v + 0.044715 * (v * v * v))))


def _ffn_kernel(hm_ref, hp_ref, hn_ref, x1_ref, mod_ref, wu_ref, cw_ref, cb_ref, wd_ref, fg_ref,
                out_ref, act_s, *, tm):
    t = pl.program_id(1)
    nt = pl.num_programs(1)
    n = tm + 2 * GRID_W
    hm = hm_ref[0]
    hp = jnp.where(t > 0, hp_ref[0], jnp.zeros_like(hp_ref[0]))
    hn = jnp.where(t < nt - 1, hn_ref[0], jnp.zeros_like(hn_ref[0]))
    he = jnp.concatenate([hp, hm, hn], axis=0)
    gcol = lax.broadcasted_iota(jnp.int32, (n, 1), 0) & (GRID_W - 1)
    has_left = gcol != 0
    has_right = gcol != GRID_W - 1
    n_chunks = FF_HIDDEN // FFN_CW
    cols = lambda j: slice(j * FFN_CW, (j + 1) * FFN_CW)
    gate_cols = lambda j: slice(FF_HIDDEN + j * FFN_CW, FF_HIDDEN + (j + 1) * FFN_CW)
    up = lambda j: (_dot(he, wu_ref[:, cols(j)]), _dot(hm, wu_ref[:, gate_cols(j)]))
    acc = None
    nxt = up(0)
    for j in range(n_chunks):
        cs = cols(j)
        a, g = nxt
        if j + 1 < n_chunks:
            nxt = up(j + 1)
        taps = (jnp.where(has_left, pltpu.roll(a, 1, 0), 0.0), a,
                jnp.where(has_right, pltpu.roll(a, n - 1, 0), 0.0))
        conv = cb_ref[:, cs]
        for dr in range(3):
            for dc in range(3):
                conv = conv + cw_ref[3 * dr + dc: 3 * dr + dc + 1, cs] * taps[dc][GRID_W * dr: GRID_W * dr + tm]
        act_s[:, cs] = (_gelu_tanh(conv) * g).astype(BF16)
        if (j + 1) % FFN_DOWN_GROUP == 0 or j + 1 == n_chunks:
            gs = slice((j // FFN_DOWN_GROUP) * FFN_DOWN_GROUP * FFN_CW, (j + 1) * FFN_CW)
            part = _dot(act_s[:, gs], wd_ref[gs, :])
            acc = part if acc is None else acc + part
    x2 = x1_ref[0] + mod_ref[0, 5:6, :] * acc
    out_ref[0] = _rms_scale(x2) * fg_ref[...]


def _ffn(h2, x1, mod, wu, cw, cb, wd, fg):
    bsz, t_len, _ = x1.shape
    tm = FFN_TM
    rb = tm // GRID_W
    nrb = t_len // GRID_W
    tok = pl.BlockSpec((1, tm, D_MODEL), lambda b, t: (b, t, 0))
    return pl.pallas_call(
        functools.partial(_ffn_kernel, tm=tm),
        grid=(bsz, t_len // tm),
        in_specs=[tok,
                  pl.BlockSpec((1, GRID_W, D_MODEL), lambda b, t: (b, jnp.maximum(t * rb - 1, 0), 0)),
                  pl.BlockSpec((1, GRID_W, D_MODEL), lambda b, t: (b, jnp.minimum((t + 1) * rb, nrb - 1), 0)),
                  tok, pl.BlockSpec((1, 8, D_MODEL), lambda b, t: (b, 0, 0)),
                  _const_spec(wu.shape), _const_spec(cw.shape), _const_spec(cb.shape),
                  _const_spec(wd.shape), _const_spec(fg.shape)],
        out_specs=tok,
        out_shape=jax.ShapeDtypeStruct(x1.shape, F32),
        scratch_shapes=[pltpu.VMEM((tm, FF_HIDDEN), BF16)],
        compiler_params=pltpu.CompilerParams(dimension_semantics=("parallel", "parallel"),
                                             vmem_limit_bytes=VMEM_LIMIT),
        name="ffn",
    )(h2, h2, h2, x1, mod, wu, cw, cb, wd, fg)


def kernel(x, c, ctx, c_ctx, ada_w, ada_b, norm1_g, norm2_g, w_in, qk_conv_w, qk_conv_b, gate_b, mnorm_g,
           w_m_out, sc_conv_w, sc_conv_b, w_c_out, w_o, w_up, ff_conv_w, ff_conv_b, w_down, final_g):
    assert ada_w.shape[0] == 1, "single-layer block"
    bsz, t_lat, _ = x.shape
    t_ctx = ctx.shape[1]

    cc = jnp.zeros((2 * 8, D_MODEL), F32).at[:bsz].set(c).at[bsz].set(c_ctx)
    mod = _ada(cc, ada_w[0], ada_b[0][None, :])
    mod_x = jnp.pad(mod[:bsz].reshape(bsz, 6, D_MODEL), ((0, 0), (0, 2), (0, 0)))
    mod_c = jnp.pad(mod[bsz].reshape(1, 6, D_MODEL), ((0, 0), (0, 2), (0, 0)))

    cast = lambda a: a.astype(BF16)
    w_t = cast(jnp.swapaxes(w_in, 1, 2)[0])
    gb = gate_b[0][:, None]
    g1 = norm1_g[0][None, :]
    ctx_w = [w_t, gb, qk_conv_w[0], qk_conv_b[0][None, :]]
    lat_w = ctx_w + [sc_conv_w[0], sc_conv_b[0][None, :], cast(w_c_out[0])]

    k_l, vt_l, g_l, qt_l, ogt, gm, zc = _proj(x, mod_x, g1, lat_w, full=True, per_batch_mod=True)
    k_c, vt_c, g_c = _proj(ctx, mod_c, g1, ctx_w, full=False, per_batch_mod=False)

    gf = _gates(g_l, g_c)
    hgt = _mlstm(qt_l, k_l, vt_l, k_c, vt_c, gf, ogt, mnorm_g[0][None, :])

    x1, h2 = _merge(hgt, gm, zc, x, mod_x, norm2_g[0][None, :], cast(w_m_out[0]), cast(w_o[0]))

    out = _ffn(h2, x1, mod_x, cast(w_up[0]),
               ff_conv_w[0].reshape(9, FF_HIDDEN), ff_conv_b[0][None, :], cast(w_down[0]),
               final_g[None, :])
    return out
```

```python
import functools

import jax
import jax.numpy as jnp
from jax import lax
from jax.experimental import pallas as pl
from jax.experimental.pallas import tpu as pltpu

F32 = jnp.float32
BF16 = jnp.bfloat16

D_MODEL = 1024
N_HEADS = 8
HEAD_DIM = D_MODEL // N_HEADS
CHUNK = 128
GRID_W = 64
FF_HIDDEN = 2816
EPS = 1e-6
M_INIT = -1e30
N_GATES = 4 * N_HEADS

LANES = 128
BF16_ROWS = 16
VMEM_LIMIT = 56 * 1024 * 1024

PROJ_TM = 256
SEQ_HALO = BF16_ROWS
MERGE_TM = 1024
MERGE_SUB = 256
FFN_TM = 512
FFN_CW = 256
FFN_DOWN_GROUP = 4

R_ROWB, R_DECAY = 0, 2
C_BASE, C_PER_DIR = 4, 4
F_COLA, F_AINT, F_ENEG, F_WK = range(4)
N_FIELDS = 16


def _const_spec(shape):
    nd = len(shape)
    return pl.BlockSpec(shape, lambda *_: (0,) * nd, pipeline_mode=pl.Buffered(1))


def _sigmoid(v):
    return 1.0 / (1.0 + jnp.exp(-v))


def _rms_scale(v):
    return v * lax.rsqrt(jnp.mean(v * v, axis=-1, keepdims=True) + EPS)


def _dot(a, b):
    return jnp.dot(a, b, preferred_element_type=F32)


def _ada_kernel(c_ref, w_ref, b_ref, o_ref):
    cv = c_ref[...]
    s = cv * _sigmoid(cv)
    o_ref[...] = jnp.dot(s, w_ref[...], preferred_element_type=F32,
                         precision=lax.Precision.HIGHEST) + b_ref[...]


def _ada(cc, w, b):
    rows, n = cc.shape[0], w.shape[1]
    tn = 1024
    return pl.pallas_call(
        _ada_kernel,
        grid=(n // tn,),
        in_specs=[pl.BlockSpec((rows, D_MODEL), lambda j: (0, 0)),
                  pl.BlockSpec((D_MODEL, tn), lambda j: (0, j)),
                  pl.BlockSpec((1, tn), lambda j: (0, j))],
        out_specs=pl.BlockSpec((rows, tn), lambda j: (0, j)),
        out_shape=jax.ShapeDtypeStruct((rows, n), F32),
        name="ada",
    )(cc, w, b)


def _conv3_rows(p, w_ref, b_ref, tm):
    n = tm + 2 * SEQ_HALO
    lo, hi = SEQ_HALO, SEQ_HALO + tm
    left = pltpu.roll(p, 1, 0)[lo:hi]
    right = pltpu.roll(p, n - 1, 0)[lo:hi]
    return left * w_ref[0:1, :] + p[lo:hi] * w_ref[1:2, :] + right * w_ref[2:3, :] + b_ref[...]


def _store_head_tiles_t(ref, val, tm):
    for h in range(N_HEADS):
        for ci in range(tm // CHUNK):
            tile = val[ci * CHUNK:(ci + 1) * CHUNK, h * HEAD_DIM:(h + 1) * HEAD_DIM]
            ref[0, h, ci] = tile.T.astype(BF16)


W_QK = slice(0, 2 * D_MODEL)
W_V = slice(2 * D_MODEL, 3 * D_MODEL)
W_G = slice(3 * D_MODEL, 3 * D_MODEL + N_GATES)
W_CTX_ROWS = W_G.stop
W_O = slice(W_G.stop, W_G.stop + D_MODEL)
W_SB = slice(W_O.stop, W_O.stop + D_MODEL)
W_SCX = slice(W_SB.stop, W_SB.stop + 2 * D_MODEL)
W_MG = slice(W_SCX.stop, W_SCX.stop + 2 * D_MODEL)


def _dot_nt(a, b):
    return lax.dot_general(a, b, (((1,), (1,)), ((), ())), preferred_element_type=F32)


def _proj_kernel(*refs, tm, full):
    if full:
        (xm_ref, xp_ref, xn_ref, mod_ref, g1_ref, w_ref, gb_ref, qkw_ref, qkb_ref,
         scw_ref, scb_ref, wco_ref,
         k_ref, vt_ref, g_ref, qt_ref, ogt_ref, gm_ref, zc_ref) = refs
    else:
        (xm_ref, xp_ref, xn_ref, mod_ref, g1_ref, w_ref, gb_ref, qkw_ref, qkb_ref,
         k_ref, vt_ref, g_ref) = refs
    t = pl.program_id(1)
    nt = pl.num_programs(1)
    shift = mod_ref[0, 0:1, :]
    scale1 = 1.0 + mod_ref[0, 1:2, :]
    gain = g1_ref[...]

    def norm_mod(xv):
        return _rms_scale(xv) * gain * scale1 + shift

    hm = norm_mod(xm_ref[0])
    hp = jnp.where(t > 0, norm_mod(xp_ref[0]), 0.0)
    hn = jnp.where(t < nt - 1, norm_mod(xn_ref[0]), 0.0)
    hmb = hm.astype(BF16)
    he = jnp.concatenate([hp, hm, hn], axis=0).astype(BF16)

    pqk = _dot_nt(he, w_ref[W_QK, :])
    pv = _dot_nt(hmb, w_ref[W_V, :])
    qk = _conv3_rows(pqk, qkw_ref, qkb_ref, tm)
    qk = qk * _sigmoid(qk)
    k_ref[0] = (qk[:, D_MODEL:] * (HEAD_DIM ** -0.5)).astype(BF16)
    pg = _dot_nt(w_ref[W_G, :], hmb)
    if not full:
        _store_head_tiles_t(vt_ref, pv, tm)
        g_ref[0] = pg + gb_ref[...]
        return
    po = _dot_nt(hmb, w_ref[W_O, :])
    _store_head_tiles_t(qt_ref, qk[:, :D_MODEL], tm)
    _store_head_tiles_t(vt_ref, pv, tm)
    g_ref[0] = pg + gb_ref[...]
    pcx = _dot_nt(he, w_ref[W_SCX, :])
    _store_head_tiles_t(ogt_ref, _sigmoid(po), tm)
    psb = _dot_nt(hmb, w_ref[W_SB, :])
    cu = _conv3_rows(pcx[:, :D_MODEL] * pcx[:, D_MODEL:], scw_ref, scb_ref, tm)
    pm = _dot_nt(hmb, w_ref[W_MG, :])
    yc = _dot((psb * cu).astype(BF16), wco_ref[...])
    gm_ref[0] = _sigmoid(pm[:, :D_MODEL]).astype(BF16)
    zc_ref[0] = (_sigmoid(pm[:, D_MODEL:]) * yc).astype(BF16)


def _proj(xs, mod, g1, weights, *, full, per_batch_mod):
    bsz, t_len, _ = xs.shape
    tm = min(PROJ_TM, t_len)
    nt = t_len // tm
    hb = tm // SEQ_HALO
    nhb = t_len // SEQ_HALO
    x_specs = [
        pl.BlockSpec((1, tm, D_MODEL), lambda b, t: (b, t, 0)),
        pl.BlockSpec((1, SEQ_HALO, D_MODEL), lambda b, t: (b, jnp.maximum(t * hb - 1, 0), 0)),
        pl.BlockSpec((1, SEQ_HALO, D_MODEL), lambda b, t: (b, jnp.minimum((t + 1) * hb, nhb - 1), 0)),
    ]
    mod_spec = pl.BlockSpec((1, 8, D_MODEL), (lambda b, t: (b, 0, 0)) if per_batch_mod else (lambda b, t: (0, 0, 0)))
    w_rows = weights[0].shape[0] if full else W_CTX_ROWS
    w_specs = [_const_spec((w_rows, D_MODEL))] + [_const_spec(w.shape) for w in weights[1:]]
    tok = lambda n, dt: (pl.BlockSpec((1, tm, n), lambda b, t: (b, t, 0)),
                         jax.ShapeDtypeStruct((bsz, t_len, n), dt))
    tiles_t = (pl.BlockSpec((1, N_HEADS, tm // CHUNK, HEAD_DIM, CHUNK), lambda b, t: (b, 0, t, 0, 0)),
               jax.ShapeDtypeStruct((bsz, N_HEADS, t_len // CHUNK, HEAD_DIM, CHUNK), BF16))
    gates_t = (pl.BlockSpec((1, N_GATES, tm), lambda b, t: (b, 0, t)),
               jax.ShapeDtypeStruct((bsz, N_GATES, t_len), F32))
    outs = [tok(D_MODEL, BF16), tiles_t, gates_t]
    if full:
        outs += [tiles_t, tiles_t, tok(D_MODEL, BF16), tok(D_MODEL, BF16)]
    return pl.pallas_call(
        functools.partial(_proj_kernel, tm=tm, full=full),
        grid=(bsz, nt),
        in_specs=x_specs + [mod_spec, _const_spec(g1.shape)] + w_specs,
        out_specs=[o[0] for o in outs],
        out_shape=[o[1] for o in outs],
        compiler_params=pltpu.CompilerParams(dimension_semantics=("parallel", "parallel"),
                                             vmem_limit_bytes=VMEM_LIMIT),
        name="proj_full" if full else "proj_ctx",
    )(xs, xs, xs, mod, g1, *weights)


def _scan_lanes(v, combine, fill, forward):
    lane = lax.broadcasted_iota(jnp.int32, v.shape, 1)
    d = 1
    while d < LANES:
        if forward:
            shifted = jnp.where(lane >= d, pltpu.roll(v, d, 1), fill)
        else:
            shifted = jnp.where(lane < LANES - d, pltpu.roll(v, LANES - d, 1), fill)
        v = combine(v, shifted)
        d *= 2
    return v


def _gates_kernel(gl_ref, gc_ref, out_ref, *, n_ctx, n_lat):
    n_all = n_ctx + n_lat
    src = lax.broadcasted_iota(jnp.int32, (CHUNK, 2 * CHUNK), 0)
    dst = lax.broadcasted_iota(jnp.int32, (CHUNK, 2 * CHUNK), 1)

    def gate_rows(cg, first):
        if cg < n_ctx:
            return gc_ref[0, first:first + N_HEADS, cg * CHUNK:(cg + 1) * CHUNK]
        return gl_ref[0, first:first + N_HEADS, (cg - n_ctx) * CHUNK:(cg - n_ctx + 1) * CHUNK]

    for dirn in range(2):
        forward = dirn == 0
        if forward:
            order = list(range(n_all))
        else:
            order = list(range(n_ctx - 1, -1, -1)) + list(range(n_all - 1, n_ctx - 1, -1))
        last = LANES - 1 if forward else 0
        first_row = 2 * N_HEADS * dirn
        fg = jnp.concatenate([gate_rows(cg, first_row + N_HEADS) for cg in range(n_all)], axis=0)
        lf = jnp.minimum(fg, 0.0) - jnp.log1p(jnp.exp(-jnp.abs(fg)))
        prefix = (src <= dst) if forward else (src >= dst)
        sel = jnp.where((dst >= CHUNK) | prefix, 1.0, 0.0).astype(BF16)
        hi = lf.astype(BF16)
        rest = lf - hi.astype(F32)
        mid = rest.astype(BF16)
        low = (rest - mid.astype(F32)).astype(BF16)
        sums = _dot(hi, sel) + _dot(mid, sel) + _dot(low, sel)
        m = jnp.full((N_HEADS, 1), M_INIT, F32)
        for cg in order:
            rows = slice(cg * N_HEADS, (cg + 1) * N_HEADS)
            b = sums[rows, :CHUNK]
            b_end = sums[rows, CHUNK:CHUNK + 1]
            r = gate_rows(cg, first_row) - b
            cm = _scan_lanes(r, jnp.maximum, -jnp.inf, forward)
            mx = jnp.maximum(m, cm)
            mt = b + mx
            m_new = b_end + jnp.maximum(m, cm[:, last:last + 1])
            cbase = C_BASE + C_PER_DIR * dirn
            fields = {
                R_ROWB + dirn: r,
                R_DECAY + dirn: jnp.broadcast_to(jnp.exp(b_end + m - m_new), (N_HEADS, CHUNK)),
                cbase + F_COLA: -mx,
                cbase + F_AINT: jnp.exp(m - mx),
                cbase + F_ENEG: jnp.exp(-mt),
                cbase + F_WK: jnp.exp((b_end - m_new) + r),
            }
            for f, val in fields.items():
                for h in range(N_HEADS):
                    out_ref[0, h, cg, f:f + 1, :] = val[h:h + 1, :]
            m = m_new
    n_used = C_BASE + 2 * C_PER_DIR
    out_ref[0, :, :, n_used:, :] = jnp.zeros((N_HEADS, n_all, N_FIELDS - n_used, CHUNK), F32)


def _gates(g_lat, g_ctx):
    bsz, _, t_lat = g_lat.shape
    t_ctx = g_ctx.shape[2]
    n_all = (t_lat + t_ctx) // CHUNK
    shape = (bsz, N_HEADS, n_all, N_FIELDS, CHUNK)
    return pl.pallas_call(
        functools.partial(_gates_kernel, n_ctx=t_ctx // CHUNK, n_lat=t_lat // CHUNK),
        grid=(bsz,),
        in_specs=[pl.BlockSpec((1, N_GATES, t_lat), lambda b: (b, 0, 0)),
                  pl.BlockSpec((1, N_GATES, t_ctx), lambda b: (b, 0, 0))],
        out_specs=pl.BlockSpec((1,) + shape[1:], lambda b: (b, 0, 0, 0, 0)),
        out_shape=jax.ShapeDtypeStruct(shape, F32),
        compiler_params=pltpu.CompilerParams(dimension_semantics=("parallel",)),
        name="gates",
    )(g_lat, g_ctx)


STATE_ROWS = HEAD_DIM + BF16_ROWS
MLSTM_HEADS = 2


def _mlstm_kernel(qt_ref, kl_ref, vtl_ref, kc_ref, vtc_ref, gf_ref, ogt_ref, ng_ref, out_ref,
                  dz_s, zf_s, zb_s, z_s, p_s, *, n_ctx, n_lat):
    lane = lax.broadcasted_iota(jnp.int32, (CHUNK, CHUNK), 1)
    row = lax.broadcasted_iota(jnp.int32, (CHUNK, CHUNK), 0)
    tri = (row <= lane, row >= lane)
    tail_first = lax.broadcasted_iota(jnp.int32, (BF16_ROWS, CHUNK), 0) == 0
    ones_tail = jnp.where(tail_first, 1.0, 0.0).astype(BF16)

    for hh in range(MLSTM_HEADS):
        _mlstm_head(hh, qt_ref, kl_ref, vtl_ref, kc_ref, vtc_ref, gf_ref, ogt_ref, ng_ref, out_ref,
                    dz_s, zf_s, zb_s, z_s, p_s, tri, tail_first, ones_tail, n_ctx, n_lat)


def _mlstm_head(hh, qt_ref, kl_ref, vtl_ref, kc_ref, vtc_ref, gf_ref, ogt_ref, ng_ref, out_ref,
                dz_s, zf_s, zb_s, z_s, p_s, tri, tail_first, ones_tail, n_ctx, n_lat):
    hcols = slice(hh * HEAD_DIM, (hh + 1) * HEAD_DIM)

    def field(cg, dirn, f):
        j = C_BASE + C_PER_DIR * dirn + f
        return gf_ref[0, hh, cg, j:j + 1, :]

    def pass_a(cg, k, vt):
        vf = vt.astype(F32)
        parts = []
        for dirn in range(2):
            wk = field(cg, dirn, F_WK)
            parts += [(vf * wk).astype(BF16), jnp.where(tail_first, wk, 0.0).astype(BF16)]
        dz_s[cg] = _dot(jnp.concatenate(parts, axis=0), k)

    for cg in range(n_ctx):
        pass_a(cg, kc_ref[0, cg * CHUNK:(cg + 1) * CHUNK, hcols], vtc_ref[0, hh, cg])

    def pass_a_lat(i, carry):
        sl = pl.ds(pl.multiple_of(i * CHUNK, CHUNK), CHUNK)
        pass_a(n_ctx + i, kl_ref[0, sl, hcols], vtl_ref[0, hh, i])
        return carry

    lax.fori_loop(0, n_lat, pass_a_lat, 0, unroll=8)

    def advance(dirn, cg):
        dec = gf_ref[0, hh, cg, R_DECAY + dirn:R_DECAY + dirn + 1, :]
        z_s[dirn] = dec * z_s[dirn] + dz_s[cg, dirn * STATE_ROWS:(dirn + 1) * STATE_ROWS, :]

    z_s[...] = jnp.zeros_like(z_s)
    for cg in range(n_ctx):
        advance(0, cg)
    for cg in range(n_ctx - 1, -1, -1):
        advance(1, cg)

    def pass_b(i, carry):
        zf_s[i] = z_s[0].astype(BF16)
        advance(0, n_ctx + i)
        j = n_lat - 1 - i
        zb_s[j] = z_s[1].astype(BF16)
        advance(1, n_ctx + j)
        return carry

    lax.fori_loop(0, n_lat, pass_b, 0, unroll=2)

    gain = jnp.broadcast_to(ng_ref[:, hcols], (CHUNK, HEAD_DIM)).T

    def pass_c1(i, carry):
        cg = n_ctx + i
        sl = pl.ds(pl.multiple_of(i * CHUNK, CHUNK), CHUNK)
        st = _dot(kl_ref[0, sl, hcols], qt_ref[0, hh, i])
        rowb = gf_ref[0, hh, cg].T
        for dirn in range(2):
            dmat = rowb[:, R_ROWB + dirn:R_ROWB + dirn + 1] + field(cg, dirn, F_COLA)
            w = jnp.exp(jnp.where(tri[dirn], dmat, -jnp.inf))
            p_s[i, dirn] = (st * w).astype(BF16)
        return carry

    lax.fori_loop(0, n_lat, pass_c1, 0, unroll=16)

    def pass_c2(i, carry):
        cg = n_ctx + i
        qtf = qt_ref[0, hh, i].astype(F32)
        vaug = jnp.concatenate([vtl_ref[0, hh, i], ones_tail], axis=0)
        ht = None
        for dirn, z_ref in enumerate((zf_s, zb_s)):
            rhs = jnp.concatenate([(qtf * field(cg, dirn, F_AINT)).astype(BF16), p_s[i, dirn]], axis=0)
            res = _dot(jnp.concatenate([z_ref[i], vaug], axis=1), rhs)
            den = jnp.maximum(jnp.abs(res[HEAD_DIM:HEAD_DIM + 1, :]), field(cg, dirn, F_ENEG))
            hd = res[:HEAD_DIM, :] / den
            ht = hd if ht is None else ht + hd
        ht = ht * lax.rsqrt(jnp.mean(ht * ht, axis=0, keepdims=True) + EPS)
        out_ref[0, hh, i] = (ht * gain * ogt_ref[0, hh, i].astype(F32)).astype(BF16)
        return carry

    lax.fori_loop(0, n_lat, pass_c2, 0, unroll=16)


def _mlstm(qt_l, k_l, vt_l, k_c, vt_c, gf, ogt, ng):
    bsz, t_lat, _ = k_l.shape
    t_ctx = k_c.shape[1]
    n_lat, n_ctx = t_lat // CHUNK, t_ctx // CHUNK
    n_all = n_lat + n_ctx
    hb = MLSTM_HEADS
    head_lat = pl.BlockSpec((1, t_lat, hb * HEAD_DIM), lambda b, h: (b, 0, h))
    head_ctx = pl.BlockSpec((1, t_ctx, hb * HEAD_DIM), lambda b, h: (b, 0, h))
    tiles = lambda n: pl.BlockSpec((1, hb, n, HEAD_DIM, CHUNK), lambda b, h: (b, h, 0, 0, 0))
    return pl.pallas_call(
        functools.partial(_mlstm_kernel, n_ctx=n_ctx, n_lat=n_lat),
        grid=(bsz, N_HEADS // hb),
        in_specs=[tiles(n_lat), head_lat, tiles(n_lat), head_ctx, tiles(n_ctx),
                  pl.BlockSpec((1, hb, n_all, N_FIELDS, CHUNK), lambda b, h: (b, h, 0, 0, 0)),
                  tiles(n_lat),
                  pl.BlockSpec((1, hb * HEAD_DIM), lambda b, h: (0, h))],
        out_specs=tiles(n_lat),
        out_shape=jax.ShapeDtypeStruct(qt_l.shape, BF16),
        scratch_shapes=[pltpu.VMEM((n_all, 2 * STATE_ROWS, HEAD_DIM), F32),
                        pltpu.VMEM((n_lat, STATE_ROWS, HEAD_DIM), BF16),
                        pltpu.VMEM((n_lat, STATE_ROWS, HEAD_DIM), BF16),
                        pltpu.VMEM((2, STATE_ROWS, HEAD_DIM), F32),
                        pltpu.VMEM((n_lat, 2, CHUNK, CHUNK), BF16)],
        compiler_params=pltpu.CompilerParams(dimension_semantics=("parallel", "parallel"),
                                             vmem_limit_bytes=VMEM_LIMIT),
        name="mlstm",
    )(qt_l, k_l, vt_l, k_c, vt_c, gf, ogt, ng)


def _merge_kernel(hgt_ref, gm_ref, zc_ref, x_ref, mod_ref, g2_ref, wmo_ref, wo_ref, x1_ref, h2_ref):
    per_sub = MERGE_SUB // CHUNK
    subs = [slice(i * MERGE_SUB, (i + 1) * MERGE_SUB) for i in range(hgt_ref.shape[2] // per_sub)]
    hg = [jnp.concatenate(
        [jnp.concatenate([hgt_ref[0, h, ci].astype(F32).T for h in range(N_HEADS)], axis=1)
         for ci in range(i * per_sub, (i + 1) * per_sub)], axis=0).astype(BF16) for i in range(len(subs))]
    ym = [_dot(hg_i, wmo_ref[...]) for hg_i in hg]
    y = [_dot((gm_ref[0, sl, :].astype(F32) * ym_i + zc_ref[0, sl, :].astype(F32)).astype(BF16), wo_ref[...])
         for sl, ym_i in zip(subs, ym)]
    for sl, y_i in zip(subs, y):
        x1 = x_ref[0, sl, :] + mod_ref[0, 2:3, :] * y_i
        x1_ref[0, sl, :] = x1
        h2 = _rms_scale(x1) * g2_ref[...] * (1.0 + mod_ref[0, 4:5, :]) + mod_ref[0, 3:4, :]
        h2_ref[0, sl, :] = h2.astype(BF16)


def _merge(hgt, gm, zc, x, mod, g2, wmo, wo):
    bsz, t_len, _ = x.shape
    tm = MERGE_TM
    tok = pl.BlockSpec((1, tm, D_MODEL), lambda b, t: (b, t, 0))
    tiles = pl.BlockSpec((1, N_HEADS, tm // CHUNK, HEAD_DIM, CHUNK), lambda b, t: (b, 0, t, 0, 0))
    return pl.pallas_call(
        _merge_kernel,
        grid=(bsz, t_len // tm),
        in_specs=[tiles, tok, tok, tok, pl.BlockSpec((1, 8, D_MODEL), lambda b, t: (b, 0, 0)),
                  _const_spec(g2.shape), _const_spec(wmo.shape), _const_spec(wo.shape)],
        out_specs=[tok, tok],
        out_shape=[jax.ShapeDtypeStruct(x.shape, F32), jax.ShapeDtypeStruct(x.shape, BF16)],
        compiler_params=pltpu.CompilerParams(dimension_semantics=("parallel", "parallel"),
                                             vmem_limit_bytes=VMEM_LIMIT),
        name="merge",
    )(hgt, gm, zc, x, mod, g2, wmo, wo)


def _gelu_tanh(v):
    return 0.5 * v * (1.0 + jnp.tanh(0.7978845608028654 * (v + 0.044715 * (v * v * v))))


def _ffn_kernel(hm_ref, hp_ref, hn_ref, x1_ref, mod_ref, wu_ref, cw_ref, cb_ref, wd_ref, fg_ref,
                out_ref, act_s, *, tm):
    t = pl.program_id(1)
    nt = pl.num_programs(1)
    n = tm + 2 * GRID_W
    hm = hm_ref[0]
    hp = jnp.where(t > 0, hp_ref[0], jnp.zeros_like(hp_ref[0]))
    hn = jnp.where(t < nt - 1, hn_ref[0], jnp.zeros_like(hn_ref[0]))
    he = jnp.concatenate([hp, hm, hn], axis=0)
    gcol = lax.broadcasted_iota(jnp.int32, (n, 1), 0) & (GRID_W - 1)
    has_left = gcol != 0
    has_right = gcol != GRID_W - 1
    n_chunks = FF_HIDDEN // FFN_CW
    cols = lambda j: slice(j * FFN_CW, (j + 1) * FFN_CW)
    gate_cols = lambda j: slice(FF_HIDDEN + j * FFN_CW, FF_HIDDEN + (j + 1) * FFN_CW)
    up = lambda j: (_dot(he, wu_ref[:, cols(j)]), _dot(hm, wu_ref[:, gate_cols(j)]))
    acc = None
    nxt = up(0)
    for j in range(n_chunks):
        cs = cols(j)
        a, g = nxt
        if j + 1 < n_chunks:
            nxt = up(j + 1)
        taps = (jnp.where(has_left, pltpu.roll(a, 1, 0), 0.0), a,
                jnp.where(has_right, pltpu.roll(a, n - 1, 0), 0.0))
        conv = cb_ref[:, cs]
        for dr in range(3):
            for dc in range(3):
                conv = conv + cw_ref[3 * dr + dc: 3 * dr + dc + 1, cs] * taps[dc][GRID_W * dr: GRID_W * dr + tm]
        act_s[:, cs] = (_gelu_tanh(conv) * g).astype(BF16)
        if (j + 1) % FFN_DOWN_GROUP == 0 or j + 1 == n_chunks:
            gs = slice((j // FFN_DOWN_GROUP) * FFN_DOWN_GROUP * FFN_CW, (j + 1) * FFN_CW)
            part = _dot(act_s[:, gs], wd_ref[gs, :])
            acc = part if acc is None else acc + part
    x2 = x1_ref[0] + mod_ref[0, 5:6, :] * acc
    out_ref[0] = _rms_scale(x2) * fg_ref[...]


def _ffn(h2, x1, mod, wu, cw, cb, wd, fg):
    bsz, t_len, _ = x1.shape
    tm = FFN_TM
    rb = tm // GRID_W
    nrb = t_len // GRID_W
    tok = pl.BlockSpec((1, tm, D_MODEL), lambda b, t: (b, t, 0))
    return pl.pallas_call(
        functools.partial(_ffn_kernel, tm=tm),
        grid=(bsz, t_len // tm),
        in_specs=[tok,
                  pl.BlockSpec((1, GRID_W, D_MODEL), lambda b, t: (b, jnp.maximum(t * rb - 1, 0), 0)),
                  pl.BlockSpec((1, GRID_W, D_MODEL), lambda b, t: (b, jnp.minimum((t + 1) * rb, nrb - 1), 0)),
                  tok, pl.BlockSpec((1, 8, D_MODEL), lambda b, t: (b, 0, 0)),
                  _const_spec(wu.shape), _const_spec(cw.shape), _const_spec(cb.shape),
                  _const_spec(wd.shape), _const_spec(fg.shape)],
        out_specs=tok,
        out_shape=jax.ShapeDtypeStruct(x1.shape, F32),
        scratch_shapes=[pltpu.VMEM((tm, FF_HIDDEN), BF16)],
        compiler_params=pltpu.CompilerParams(dimension_semantics=("parallel", "parallel"),
                                             vmem_limit_bytes=VMEM_LIMIT),
        name="ffn",
    )(h2, h2, h2, x1, mod, wu, cw, cb, wd, fg)


def kernel(x, c, ctx, c_ctx, ada_w, ada_b, norm1_g, norm2_g, w_in, qk_conv_w, qk_conv_b, gate_b, mnorm_g,
           w_m_out, sc_conv_w, sc_conv_b, w_c_out, w_o, w_up, ff_conv_w, ff_conv_b, w_down, final_g):
    assert ada_w.shape[0] == 1, "single-layer block"
    bsz, t_lat, _ = x.shape
    t_ctx = ctx.shape[1]

    cc = jnp.zeros((2 * 8, D_MODEL), F32).at[:bsz].set(c).at[bsz].set(c_ctx)
    mod = _ada(cc, ada_w[0], ada_b[0][None, :])
    mod_x = jnp.pad(mod[:bsz].reshape(bsz, 6, D_MODEL), ((0, 0), (0, 2), (0, 0)))
    mod_c = jnp.pad(mod[bsz].reshape(1, 6, D_MODEL), ((0, 0), (0, 2), (0, 0)))

    cast = lambda a: a.astype(BF16)
    w_t = cast(jnp.swapaxes(w_in, 1, 2)[0])
    gb = gate_b[0][:, None]
    g1 = norm1_g[0][None, :]
    ctx_w = [w_t, gb, qk_conv_w[0], qk_conv_b[0][None, :]]
    lat_w = ctx_w + [sc_conv_w[0], sc_conv_b[0][None, :], cast(w_c_out[0])]

    k_l, vt_l, g_l, qt_l, ogt, gm, zc = _proj(x, mod_x, g1, lat_w, full=True, per_batch_mod=True)
    k_c, vt_c, g_c = _proj(ctx, mod_c, g1, ctx_w, full=False, per_batch_mod=False)

    gf = _gates(g_l, g_c)
    hgt = _mlstm(qt_l, k_l, vt_l, k_c, vt_c, gf, ogt, mnorm_g[0][None, :])

    x1, h2 = _merge(hgt, gm, zc, x, mod_x, norm2_g[0][None, :], cast(w_m_out[0]), cast(w_o[0]))

    out = _ffn(h2, x1, mod_x, cast(w_up[0]),
               ff_conv_w[0].reshape(9, FF_HIDDEN), ff_conv_b[0][None, :], cast(w_down[0]),
               final_g[None, :])
    return out
```

```python
import functools

import jax
import jax.numpy as jnp
from jax import lax
from jax.experimental import pallas as pl
from jax.experimental.pallas import tpu as pltpu

F32 = jnp.float32
BF16 = jnp.bfloat16

D_MODEL = 1024
N_HEADS = 8
HEAD_DIM = D_MODEL // N_HEADS
CHUNK = 128
GRID_W = 64
FF_HIDDEN = 2816
EPS = 1e-6
M_INIT = -1e30
N_GATES = 4 * N_HEADS

LANES = 128
BF16_ROWS = 16
VMEM_LIMIT = 56 * 1024 * 1024

PROJ_TM = 256
SEQ_HALO = BF16_ROWS
MERGE_TM = 1024
MERGE_SUB = 256
FFN_TM = 1024
FFN_CW = 256
FFN_DOWN_GROUP = 4

R_ROWB, R_DECAY = 0, 2
C_BASE, C_PER_DIR = 4, 4
F_COLA, F_AINT, F_ENEG, F_WK = range(4)
N_FIELDS = 16


def _const_spec(shape):
    nd = len(shape)
    return pl.BlockSpec(shape, lambda *_: (0,) * nd, pipeline_mode=pl.Buffered(1))


def _sigmoid(v):
    return 1.0 / (1.0 + jnp.exp(-v))


def _rms_scale(v):
    return v * lax.rsqrt(jnp.mean(v * v, axis=-1, keepdims=True) + EPS)


def _dot(a, b):
    return jnp.dot(a, b, preferred_element_type=F32)


def _ada_kernel(c_ref, w_ref, b_ref, o_ref):
    cv = c_ref[...]
    s = cv * _sigmoid(cv)
    w = w_ref[...]
    s_hi, w_hi = s.astype(BF16), w.astype(BF16)
    s_lo = (s - s_hi.astype(F32)).astype(BF16)
    w_lo = (w - w_hi.astype(F32)).astype(BF16)
    rows = s.shape[0]
    main = _dot(jnp.concatenate([s_hi, s_lo], axis=0), w_hi)
    o_ref[...] = main[:rows] + main[rows:] + _dot(s_hi, w_lo) + b_ref[...]


def _ada(cc, w, b):
    rows, n = cc.shape[0], w.shape[1]
    tn = 1024
    return pl.pallas_call(
        _ada_kernel,
        grid=(n // tn,),
        in_specs=[pl.BlockSpec((rows, D_MODEL), lambda j: (0, 0)),
                  pl.BlockSpec((D_MODEL, tn), lambda j: (0, j)),
                  pl.BlockSpec((1, tn), lambda j: (0, j))],
        out_specs=pl.BlockSpec((rows, tn), lambda j: (0, j)),
        out_shape=jax.ShapeDtypeStruct((rows, n), F32),
        name="ada",
    )(cc, w, b)


def _conv3_rows(p, w_ref, b_ref, tm):
    n = tm + 2 * SEQ_HALO
    lo, hi = SEQ_HALO, SEQ_HALO + tm
    left = pltpu.roll(p, 1, 0)[lo:hi]
    right = pltpu.roll(p, n - 1, 0)[lo:hi]
    return left * w_ref[0:1, :] + p[lo:hi] * w_ref[1:2, :] + right * w_ref[2:3, :] + b_ref[...]


def _store_head_tiles_t(ref, val, tm):
    for h in range(N_HEADS):
        for ci in range(tm // CHUNK):
            tile = val[ci * CHUNK:(ci + 1) * CHUNK, h * HEAD_DIM:(h + 1) * HEAD_DIM]
            ref[0, h, ci] = tile.T.astype(BF16)


W_QK = slice(0, 2 * D_MODEL)
W_V = slice(2 * D_MODEL, 3 * D_MODEL)
W_G = slice(3 * D_MODEL, 3 * D_MODEL + N_GATES)
W_CTX_ROWS = W_G.stop
W_O = slice(W_G.stop, W_G.stop + D_MODEL)
W_SB = slice(W_O.stop, W_O.stop + D_MODEL)
W_SCX = slice(W_SB.stop, W_SB.stop + 2 * D_MODEL)
W_MG = slice(W_SCX.stop, W_SCX.stop + 2 * D_MODEL)


def _dot_nt(a, b):
    return lax.dot_general(a, b, (((1,), (1,)), ((), ())), preferred_element_type=F32)


def _proj_kernel(*refs, tm, full):
    if full:
        (xm_ref, xp_ref, xn_ref, mod_ref, g1_ref, w_ref, gb_ref, qkw_ref, qkb_ref,
         scw_ref, scb_ref, wco_ref,
         k_ref, vt_ref, g_ref, qt_ref, ogt_ref, gm_ref, zc_ref) = refs
    else:
        (xm_ref, xp_ref, xn_ref, mod_ref, g1_ref, w_ref, gb_ref, qkw_ref, qkb_ref,
         k_ref, vt_ref, g_ref) = refs
    t = pl.program_id(1)
    nt = pl.num_programs(1)
    shift = mod_ref[0, 0:1, :]
    scale1 = 1.0 + mod_ref[0, 1:2, :]
    gain = g1_ref[...]

    def norm_mod(xv):
        return _rms_scale(xv) * gain * scale1 + shift

    hm = norm_mod(xm_ref[0])
    hp = jnp.where(t > 0, norm_mod(xp_ref[0]), 0.0)
    hn = jnp.where(t < nt - 1, norm_mod(xn_ref[0]), 0.0)
    hmb = hm.astype(BF16)
    he = jnp.concatenate([hp, hm, hn], axis=0).astype(BF16)

    pqk = _dot_nt(he, w_ref[W_QK, :])
    pv = _dot_nt(hmb, w_ref[W_V, :])
    qk = _conv3_rows(pqk, qkw_ref, qkb_ref, tm)
    qk = qk * _sigmoid(qk)
    k_ref[0] = (qk[:, D_MODEL:] * (HEAD_DIM ** -0.5)).astype(BF16)
    pg = _dot_nt(w_ref[W_G, :], hmb)
    if not full:
        _store_head_tiles_t(vt_ref, pv, tm)
        g_ref[0] = pg + gb_ref[...]
        return
    po = _dot_nt(hmb, w_ref[W_O, :])
    _store_head_tiles_t(qt_ref, qk[:, :D_MODEL], tm)
    _store_head_tiles_t(vt_ref, pv, tm)
    g_ref[0] = pg + gb_ref[...]
    pcx = _dot_nt(he, w_ref[W_SCX, :])
    _store_head_tiles_t(ogt_ref, _sigmoid(po), tm)
    psb = _dot_nt(hmb, w_ref[W_SB, :])
    cu = _conv3_rows(pcx[:, :D_MODEL] * pcx[:, D_MODEL:], scw_ref, scb_ref, tm)
    pm = _dot_nt(hmb, w_ref[W_MG, :])
    yc = _dot((psb * cu).astype(BF16), wco_ref[...])
    gm_ref[0] = _sigmoid(pm[:, :D_MODEL]).astype(BF16)
    zc_ref[0] = (_sigmoid(pm[:, D_MODEL:]) * yc).astype(BF16)


def _proj(xs, mod, g1, weights, *, full, per_batch_mod):
    bsz, t_len, _ = xs.shape
    tm = min(PROJ_TM, t_len)
    nt = t_len // tm
    hb = tm // SEQ_HALO
    nhb = t_len // SEQ_HALO
    x_specs = [
        pl.BlockSpec((1, tm, D_MODEL), lambda b, t: (b, t, 0)),
        pl.BlockSpec((1, SEQ_HALO, D_MODEL), lambda b, t: (b, jnp.maximum(t * hb - 1, 0), 0)),
        pl.BlockSpec((1, SEQ_HALO, D_MODEL), lambda b, t: (b, jnp.minimum((t + 1) * hb, nhb - 1), 0)),
    ]
    mod_spec = pl.BlockSpec((1, 8, D_MODEL), (lambda b, t: (b, 0, 0)) if per_batch_mod else (lambda b, t: (0, 0, 0)))
    w_rows = weights[0].shape[0] if full else W_CTX_ROWS
    w_specs = [_const_spec((w_rows, D_MODEL))] + [_const_spec(w.shape) for w in weights[1:]]
    tok = lambda n, dt: (pl.BlockSpec((1, tm, n), lambda b, t: (b, t, 0)),
                         jax.ShapeDtypeStruct((bsz, t_len, n), dt))
    tiles_t = (pl.BlockSpec((1, N_HEADS, tm // CHUNK, HEAD_DIM, CHUNK), lambda b, t: (b, 0, t, 0, 0)),
               jax.ShapeDtypeStruct((bsz, N_HEADS, t_len // CHUNK, HEAD_DIM, CHUNK), BF16))
    gates_t = (pl.BlockSpec((1, N_GATES, tm), lambda b, t: (b, 0, t)),
               jax.ShapeDtypeStruct((bsz, N_GATES, t_len), F32))
    outs = [tok(D_MODEL, BF16), tiles_t, gates_t]
    if full:
        outs += [tiles_t, tiles_t, tok(D_MODEL, BF16), tok(D_MODEL, BF16)]
    return pl.pallas_call(
        functools.partial(_proj_kernel, tm=tm, full=full),
        grid=(bsz, nt),
        in_specs=x_specs + [mod_spec, _const_spec(g1.shape)] + w_specs,
        out_specs=[o[0] for o in outs],
        out_shape=[o[1] for o in outs],
        compiler_params=pltpu.CompilerParams(dimension_semantics=("parallel", "parallel"),
                                             vmem_limit_bytes=VMEM_LIMIT),
        name="proj_full" if full else "proj_ctx",
    )(xs, xs, xs, mod, g1, *weights)


def _scan_lanes(v, combine, fill, forward):
    lane = lax.broadcasted_iota(jnp.int32, v.shape, 1)
    d = 1
    while d < LANES:
        if forward:
            shifted = jnp.where(lane >= d, pltpu.roll(v, d, 1), fill)
        else:
            shifted = jnp.where(lane < LANES - d, pltpu.roll(v, LANES - d, 1), fill)
        v = combine(v, shifted)
        d *= 2
    return v


def _gates_kernel(gl_ref, gc_ref, out_ref, *, n_ctx, n_lat):
    n_all = n_ctx + n_lat
    src = lax.broadcasted_iota(jnp.int32, (CHUNK, 2 * CHUNK), 0)
    dst = lax.broadcasted_iota(jnp.int32, (CHUNK, 2 * CHUNK), 1)

    def gate_rows(cg, first):
        if cg < n_ctx:
            return gc_ref[0, first:first + N_HEADS, cg * CHUNK:(cg + 1) * CHUNK]
        return gl_ref[0, first:first + N_HEADS, (cg - n_ctx) * CHUNK:(cg - n_ctx + 1) * CHUNK]

    for dirn in range(2):
        forward = dirn == 0
        if forward:
            order = list(range(n_all))
        else:
            order = list(range(n_ctx - 1, -1, -1)) + list(range(n_all - 1, n_ctx - 1, -1))
        last = LANES - 1 if forward else 0
        first_row = 2 * N_HEADS * dirn
        fg = jnp.concatenate([gate_rows(cg, first_row + N_HEADS) for cg in range(n_all)], axis=0)
        lf = jnp.minimum(fg, 0.0) - jnp.log1p(jnp.exp(-jnp.abs(fg)))
        prefix = (src <= dst) if forward else (src >= dst)
        sel = jnp.where((dst >= CHUNK) | prefix, 1.0, 0.0).astype(BF16)
        hi = lf.astype(BF16)
        rest = lf - hi.astype(F32)
        mid = rest.astype(BF16)
        low = (rest - mid.astype(F32)).astype(BF16)
        sums = _dot(hi, sel) + _dot(mid, sel) + _dot(low, sel)
        m = jnp.full((N_HEADS, 1), M_INIT, F32)
        for cg in order:
            rows = slice(cg * N_HEADS, (cg + 1) * N_HEADS)
            b = sums[rows, :CHUNK]
            b_end = sums[rows, CHUNK:CHUNK + 1]
            r = gate_rows(cg, first_row) - b
            cm = _scan_lanes(r, jnp.maximum, -jnp.inf, forward)
            mx = jnp.maximum(m, cm)
            mt = b + mx
            m_new = b_end + jnp.maximum(m, cm[:, last:last + 1])
            cbase = C_BASE + C_PER_DIR * dirn
            fields = {
                R_ROWB + dirn: r,
                R_DECAY + dirn: jnp.broadcast_to(jnp.exp(b_end + m - m_new), (N_HEADS, CHUNK)),
                cbase + F_COLA: -mx,
                cbase + F_AINT: jnp.exp(m - mx),
                cbase + F_ENEG: jnp.exp(-mt),
                cbase + F_WK: jnp.exp((b_end - m_new) + r),
            }
            for f, val in fields.items():
                for h in range(N_HEADS):
                    out_ref[0, h, cg, f:f + 1, :] = val[h:h + 1, :]
            m = m_new
    n_used = C_BASE + 2 * C_PER_DIR
    out_ref[0, :, :, n_used:, :] = jnp.zeros((N_HEADS, n_all, N_FIELDS - n_used, CHUNK), F32)


def _gates(g_lat, g_ctx):
    bsz, _, t_lat = g_lat.shape
    t_ctx = g_ctx.shape[2]
    n_all = (t_lat + t_ctx) // CHUNK
    shape = (bsz, N_HEADS, n_all, N_FIELDS, CHUNK)
    return pl.pallas_call(
        functools.partial(_gates_kernel, n_ctx=t_ctx // CHUNK, n_lat=t_lat // CHUNK),
        grid=(bsz,),
        in_specs=[pl.BlockSpec((1, N_GATES, t_lat), lambda b: (b, 0, 0)),
                  pl.BlockSpec((1, N_GATES, t_ctx), lambda b: (b, 0, 0))],
        out_specs=pl.BlockSpec((1,) + shape[1:], lambda b: (b, 0, 0, 0, 0)),
        out_shape=jax.ShapeDtypeStruct(shape, F32),
        compiler_params=pltpu.CompilerParams(dimension_semantics=("parallel",)),
        name="gates",
    )(g_lat, g_ctx)


STATE_ROWS = HEAD_DIM + BF16_ROWS
MLSTM_HEADS = 2


def _mlstm_kernel(qt_ref, kl_ref, vtl_ref, kc_ref, vtc_ref, gf_ref, ogt_ref, ng_ref, out_ref,
                  dz_s, zf_s, zb_s, z_s, p_s, *, n_ctx, n_lat):
    lane = lax.broadcasted_iota(jnp.int32, (CHUNK, CHUNK), 1)
    row = lax.broadcasted_iota(jnp.int32, (CHUNK, CHUNK), 0)
    tri = (row <= lane, row >= lane)
    tail_first = lax.broadcasted_iota(jnp.int32, (BF16_ROWS, CHUNK), 0) == 0
    ones_tail = jnp.where(tail_first, 1.0, 0.0).astype(BF16)

    for hh in range(MLSTM_HEADS):
        _mlstm_head(hh, qt_ref, kl_ref, vtl_ref, kc_ref, vtc_ref, gf_ref, ogt_ref, ng_ref, out_ref,
                    dz_s, zf_s, zb_s, z_s, p_s, tri, tail_first, ones_tail, n_ctx, n_lat)


def _mlstm_head(hh, qt_ref, kl_ref, vtl_ref, kc_ref, vtc_ref, gf_ref, ogt_ref, ng_ref, out_ref,
                dz_s, zf_s, zb_s, z_s, p_s, tri, tail_first, ones_tail, n_ctx, n_lat):
    hcols = slice(hh * HEAD_DIM, (hh + 1) * HEAD_DIM)

    def field(cg, dirn, f):
        j = C_BASE + C_PER_DIR * dirn + f
        return gf_ref[0, hh, cg, j:j + 1, :]

    def pass_a(cg, k, vt):
        vf = vt.astype(F32)
        parts = []
        for dirn in range(2):
            wk = field(cg, dirn, F_WK)
            parts += [(vf * wk).astype(BF16), jnp.where(tail_first, wk, 0.0).astype(BF16)]
        dz_s[cg] = _dot(jnp.concatenate(parts, axis=0), k)

    for cg in range(n_ctx):
        pass_a(cg, kc_ref[0, cg * CHUNK:(cg + 1) * CHUNK, hcols], vtc_ref[0, hh, cg])

    def pass_a_lat(i, carry):
        sl = pl.ds(pl.multiple_of(i * CHUNK, CHUNK), CHUNK)
        pass_a(n_ctx + i, kl_ref[0, sl, hcols], vtl_ref[0, hh, i])
        return carry

    lax.fori_loop(0, n_lat, pass_a_lat, 0, unroll=8)

    def advance(dirn, cg):
        dec = gf_ref[0, hh, cg, R_DECAY + dirn:R_DECAY + dirn + 1, :]
        z_s[dirn] = dec * z_s[dirn] + dz_s[cg, dirn * STATE_ROWS:(dirn + 1) * STATE_ROWS, :]

    z_s[...] = jnp.zeros_like(z_s)
    for cg in range(n_ctx):
        advance(0, cg)
    for cg in range(n_ctx - 1, -1, -1):
        advance(1, cg)

    def pass_b(i, carry):
        zf_s[i] = z_s[0].astype(BF16)
        advance(0, n_ctx + i)
        j = n_lat - 1 - i
        zb_s[j] = z_s[1].astype(BF16)
        advance(1, n_ctx + j)
        return carry

    lax.fori_loop(0, n_lat, pass_b, 0, unroll=2)

    gain = ng_ref[hh]

    def pass_c1(i, carry):
        cg = n_ctx + i
        sl = pl.ds(pl.multiple_of(i * CHUNK, CHUNK), CHUNK)
        st = _dot(kl_ref[0, sl, hcols], qt_ref[0, hh, i])
        rowb = gf_ref[0, hh, cg].T
        for dirn in range(2):
            dmat = rowb[:, R_ROWB + dirn:R_ROWB + dirn + 1] + field(cg, dirn, F_COLA)
            w = jnp.exp(jnp.where(tri[dirn], dmat, -jnp.inf))
            p_s[i, dirn] = (st * w).astype(BF16)
        return carry

    lax.fori_loop(0, n_lat, pass_c1, 0, unroll=16)

    def pass_c2(i, carry):
        cg = n_ctx + i
        qtf = qt_ref[0, hh, i].astype(F32)
        vaug = jnp.concatenate([vtl_ref[0, hh, i], ones_tail], axis=0)
        ht = None
        for dirn, z_ref in enumerate((zf_s, zb_s)):
            rhs = jnp.concatenate([(qtf * field(cg, dirn, F_AINT)).astype(BF16), p_s[i, dirn]], axis=0)
            res = _dot(jnp.concatenate([z_ref[i], vaug], axis=1), rhs)
            den = jnp.maximum(jnp.abs(res[HEAD_DIM:HEAD_DIM + 1, :]), field(cg, dirn, F_ENEG))
            hd = res[:HEAD_DIM, :] / den
            ht = hd if ht is None else ht + hd
        ht = ht * lax.rsqrt(jnp.mean(ht * ht, axis=0, keepdims=True) + EPS)
        out_ref[0, hh, i] = (ht * gain * ogt_ref[0, hh, i].astype(F32)).astype(BF16)
        return carry

    lax.fori_loop(0, n_lat, pass_c2, 0, unroll=16)


def _mlstm(qt_l, k_l, vt_l, k_c, vt_c, gf, ogt, ng):
    bsz, t_lat, _ = k_l.shape
    t_ctx = k_c.shape[1]
    n_lat, n_ctx = t_lat // CHUNK, t_ctx // CHUNK
    n_all = n_lat + n_ctx
    hb = MLSTM_HEADS
    head_lat = pl.BlockSpec((1, t_lat, hb * HEAD_DIM), lambda b, h: (b, 0, h))
    head_ctx = pl.BlockSpec((1, t_ctx, hb * HEAD_DIM), lambda b, h: (b, 0, h))
    tiles = lambda n: pl.BlockSpec((1, hb, n, HEAD_DIM, CHUNK), lambda b, h: (b, h, 0, 0, 0))
    return pl.pallas_call(
        functools.partial(_mlstm_kernel, n_ctx=n_ctx, n_lat=n_lat),
        grid=(bsz, N_HEADS // hb),
        in_specs=[tiles(n_lat), head_lat, tiles(n_lat), head_ctx, tiles(n_ctx),
                  pl.BlockSpec((1, hb, n_all, N_FIELDS, CHUNK), lambda b, h: (b, h, 0, 0, 0)),
                  tiles(n_lat),
                  pl.BlockSpec((hb, HEAD_DIM, 1), lambda b, h: (h, 0, 0))],
        out_specs=tiles(n_lat),
        out_shape=jax.ShapeDtypeStruct(qt_l.shape, BF16),
        scratch_shapes=[pltpu.VMEM((n_all, 2 * STATE_ROWS, HEAD_DIM), F32),
                        pltpu.VMEM((n_lat, STATE_ROWS, HEAD_DIM), BF16),
                        pltpu.VMEM((n_lat, STATE_ROWS, HEAD_DIM), BF16),
                        pltpu.VMEM((2, STATE_ROWS, HEAD_DIM), F32),
                        pltpu.VMEM((n_lat, 2, CHUNK, CHUNK), BF16)],
        compiler_params=pltpu.CompilerParams(dimension_semantics=("parallel", "parallel"),
                                             vmem_limit_bytes=VMEM_LIMIT),
        name="mlstm",
    )(qt_l, k_l, vt_l, k_c, vt_c, gf, ogt, ng)


def _merge_kernel(hgt_ref, gm_ref, zc_ref, x_ref, mod_ref, g2_ref, wmo_ref, wo_ref, x1_ref, h2_ref):
    per_sub = MERGE_SUB // CHUNK
    subs = [slice(i * MERGE_SUB, (i + 1) * MERGE_SUB) for i in range(hgt_ref.shape[2] // per_sub)]
    hg = [jnp.concatenate(
        [jnp.concatenate([hgt_ref[0, h, ci].astype(F32).T for h in range(N_HEADS)], axis=1)
         for ci in range(i * per_sub, (i + 1) * per_sub)], axis=0).astype(BF16) for i in range(len(subs))]
    ym = [_dot(hg_i, wmo_ref[...]) for hg_i in hg]
    y = [_dot((gm_ref[0, sl, :].astype(F32) * ym_i + zc_ref[0, sl, :].astype(F32)).astype(BF16), wo_ref[...])
         for sl, ym_i in zip(subs, ym)]
    for sl, y_i in zip(subs, y):
        x1 = x_ref[0, sl, :] + mod_ref[0, 2:3, :] * y_i
        x1_ref[0, sl, :] = x1
        h2 = _rms_scale(x1) * g2_ref[...] * (1.0 + mod_ref[0, 4:5, :]) + mod_ref[0, 3:4, :]
        h2_ref[0, sl, :] = h2.astype(BF16)


def _merge(hgt, gm, zc, x, mod, g2, wmo, wo):
    bsz, t_len, _ = x.shape
    tm = MERGE_TM
    tok = pl.BlockSpec((1, tm, D_MODEL), lambda b, t: (b, t, 0))
    tiles = pl.BlockSpec((1, N_HEADS, tm // CHUNK, HEAD_DIM, CHUNK), lambda b, t: (b, 0, t, 0, 0))
    return pl.pallas_call(
        _merge_kernel,
        grid=(bsz, t_len // tm),
        in_specs=[tiles, tok, tok, tok, pl.BlockSpec((1, 8, D_MODEL), lambda b, t: (b, 0, 0)),
                  _const_spec(g2.shape), _const_spec(wmo.shape), _const_spec(wo.shape)],
        out_specs=[tok, tok],
        out_shape=[jax.ShapeDtypeStruct(x.shape, F32), jax.ShapeDtypeStruct(x.shape, BF16)],
        compiler_params=pltpu.CompilerParams(dimension_semantics=("parallel", "parallel"),
                                             vmem_limit_bytes=VMEM_LIMIT),
        name="merge",
    )(hgt, gm, zc, x, mod, g2, wmo, wo)


def _gelu_tanh(v):
    return 0.5 * v * (1.0 + jnp.tanh(0.7978845608028654 * (v + 0.044715 * (v * v * v))))


def _ffn_kernel(hm_ref, hp_ref, hn_ref, x1_ref, mod_ref, wu_ref, cw_ref, cb_ref, wd_ref, fg_ref,
                out_ref, act_s, *, tm):
    t = pl.program_id(1)
    nt = pl.num_programs(1)
    n = tm + 2 * GRID_W
    hm = hm_ref[0]
    hp = jnp.where(t > 0, hp_ref[0], jnp.zeros_like(hp_ref[0]))
    hn = jnp.where(t < nt - 1, hn_ref[0], jnp.zeros_like(hn_ref[0]))
    he = jnp.concatenate([hp, hm, hn], axis=0)
    gcol = lax.broadcasted_iota(jnp.int32, (n, 1), 0) & (GRID_W - 1)
    has_left = gcol != 0
    has_right = gcol != GRID_W - 1
    n_chunks = FF_HIDDEN // FFN_CW
    cols = lambda j: slice(j * FFN_CW, (j + 1) * FFN_CW)
    gate_cols = lambda j: slice(FF_HIDDEN + j * FFN_CW, FF_HIDDEN + (j + 1) * FFN_CW)
    up = lambda j: (_dot(he, wu_ref[:, cols(j)]), _dot(hm, wu_ref[:, gate_cols(j)]))
    acc = None
    nxt = up(0)
    for j in range(n_chunks):
        cs = cols(j)
        a, g = nxt
        if j + 1 < n_chunks:
            nxt = up(j + 1)
        taps = (jnp.where(has_left, pltpu.roll(a, 1, 0), 0.0), a,
                jnp.where(has_right, pltpu.roll(a, n - 1, 0), 0.0))
        conv = cb_ref[:, cs]
        for dr in range(3):
            for dc in range(3):
                conv = conv + cw_ref[3 * dr + dc: 3 * dr + dc + 1, cs] * taps[dc][GRID_W * dr: GRID_W * dr + tm]
        act_s[:, cs] = (_gelu_tanh(conv) * g).astype(BF16)
        if (j + 1) % FFN_DOWN_GROUP == 0 or j + 1 == n_chunks:
            gs = slice((j // FFN_DOWN_GROUP) * FFN_DOWN_GROUP * FFN_CW, (j + 1) * FFN_CW)
            part = _dot(act_s[:, gs], wd_ref[gs, :])
            acc = part if acc is None else acc + part
    x2 = x1_ref[0] + mod_ref[0, 5:6, :] * acc
    out_ref[0] = _rms_scale(x2) * fg_ref[...]


def _ffn(h2, x1, mod, wu, cw, cb, wd, fg):
    bsz, t_len, _ = x1.shape
    tm = FFN_TM
    rb = tm // GRID_W
    nrb = t_len // GRID_W
    tok = pl.BlockSpec((1, tm, D_MODEL), lambda b, t: (b, t, 0))
    return pl.pallas_call(
        functools.partial(_ffn_kernel, tm=tm),
        grid=(bsz, t_len // tm),
        in_specs=[tok,
                  pl.BlockSpec((1, GRID_W, D_MODEL), lambda b, t: (b, jnp.maximum(t * rb - 1, 0), 0)),
                  pl.BlockSpec((1, GRID_W, D_MODEL), lambda b, t: (b, jnp.minimum((t + 1) * rb, nrb - 1), 0)),
                  tok, pl.BlockSpec((1, 8, D_MODEL), lambda b, t: (b, 0, 0)),
                  _const_spec(wu.shape), _const_spec(cw.shape), _const_spec(cb.shape),
                  _const_spec(wd.shape), _const_spec(fg.shape)],
        out_specs=tok,
        out_shape=jax.ShapeDtypeStruct(x1.shape, F32),
        scratch_shapes=[pltpu.VMEM((tm, FF_HIDDEN), BF16)],
        compiler_params=pltpu.CompilerParams(dimension_semantics=("parallel", "parallel"),
                                             vmem_limit_bytes=VMEM_LIMIT),
        name="ffn",
    )(h2, h2, h2, x1, mod, wu, cw, cb, wd, fg)


def kernel(x, c, ctx, c_ctx, ada_w, ada_b, norm1_g, norm2_g, w_in, qk_conv_w, qk_conv_b, gate_b, mnorm_g,
           w_m_out, sc_conv_w, sc_conv_b, w_c_out, w_o, w_up, ff_conv_w, ff_conv_b, w_down, final_g):
    assert ada_w.shape[0] == 1, "single-layer block"
    bsz, t_lat, _ = x.shape
    t_ctx = ctx.shape[1]

    cc = jnp.zeros((2 * 8, D_MODEL), F32).at[:bsz].set(c).at[bsz].set(c_ctx)
    mod = _ada(cc, ada_w[0], ada_b[0][None, :])
    mod_x = jnp.pad(mod[:bsz].reshape(bsz, 6, D_MODEL), ((0, 0), (0, 2), (0, 0)))
    mod_c = jnp.pad(mod[bsz].reshape(1, 6, D_MODEL), ((0, 0), (0, 2), (0, 0)))

    cast = lambda a: a.astype(BF16)
    w_t = cast(jnp.swapaxes(w_in, 1, 2)[0])
    gb = gate_b[0][:, None]
    g1 = norm1_g[0][None, :]
    ctx_w = [w_t, gb, qk_conv_w[0], qk_conv_b[0][None, :]]
    lat_w = ctx_w + [sc_conv_w[0], sc_conv_b[0][None, :], cast(w_c_out[0])]

    k_l, vt_l, g_l, qt_l, ogt, gm, zc = _proj(x, mod_x, g1, lat_w, full=True, per_batch_mod=True)
    k_c, vt_c, g_c = _proj(ctx, mod_c, g1, ctx_w, full=False, per_batch_mod=False)

    gf = _gates(g_l, g_c)
    hgt = _mlstm(qt_l, k_l, vt_l, k_c, vt_c, gf, ogt, mnorm_g[0].reshape(N_HEADS, HEAD_DIM, 1))

    x1, h2 = _merge(hgt, gm, zc, x, mod_x, norm2_g[0][None, :], cast(w_m_out[0]), cast(w_o[0]))

    out = _ffn(h2, x1, mod_x, cast(w_up[0]),
               ff_conv_w[0].reshape(9, FF_HIDDEN), ff_conv_b[0][None, :], cast(w_down[0]),
               final_g[None, :])
    return out
```

```python
import functools

import jax
import jax.numpy as jnp
from jax import lax
from jax.experimental import pallas as pl
from jax.experimental.pallas import tpu as pltpu

F32 = jnp.float32
BF16 = jnp.bfloat16

D_MODEL = 1024
N_HEADS = 8
HEAD_DIM = D_MODEL // N_HEADS
CHUNK = 128
GRID_W = 64
FF_HIDDEN = 2816
EPS = 1e-6
M_INIT = -1e30
N_GATES = 4 * N_HEADS

LANES = 128
BF16_ROWS = 16
VMEM_LIMIT = 56 * 1024 * 1024

PROJ_TM = 256
SEQ_HALO = BF16_ROWS
MERGE_TM = 1024
MERGE_SUB = 256
FFN_TM = 512
FFN_CW = 256
FFN_DOWN_GROUP = 4

R_ROWB, R_DECAY = 0, 2
C_BASE, C_PER_DIR = 4, 4
F_COLA, F_AINT, F_ENEG, F_WK = range(4)
N_FIELDS = 16


def _const_spec(shape):
    nd = len(shape)
    return pl.BlockSpec(shape, lambda *_: (0,) * nd, pipeline_mode=pl.Buffered(1))


def _sigmoid(v):
    return 0.5 * jnp.tanh(0.5 * v) + 0.5


def _silu(v):
    h = 0.5 * v
    return h + h * jnp.tanh(h)


def _rms_scale(v):
    return v * lax.rsqrt(jnp.mean(v * v, axis=-1, keepdims=True) + EPS)


def _dot(a, b):
    return jnp.dot(a, b, preferred_element_type=F32)


def _ada_kernel(c_ref, w_ref, b_ref, o_ref):
    cv = c_ref[...]
    s = _silu(cv)
    w = w_ref[...]
    s_hi, w_hi = s.astype(BF16), w.astype(BF16)
    s_lo = (s - s_hi.astype(F32)).astype(BF16)
    w_lo = (w - w_hi.astype(F32)).astype(BF16)
    rows = s.shape[0]
    main = _dot(jnp.concatenate([s_hi, s_lo], axis=0), w_hi)
    o_ref[...] = main[:rows] + main[rows:] + _dot(s_hi, w_lo) + b_ref[...]


def _ada(cc, w, b):
    rows, n = cc.shape[0], w.shape[1]
    tn = 1024
    return pl.pallas_call(
        _ada_kernel,
        grid=(n // tn,),
        in_specs=[pl.BlockSpec((rows, D_MODEL), lambda j: (0, 0)),
                  pl.BlockSpec((D_MODEL, tn), lambda j: (0, j)),
                  pl.BlockSpec((1, tn), lambda j: (0, j))],
        out_specs=pl.BlockSpec((rows, tn), lambda j: (0, j)),
        out_shape=jax.ShapeDtypeStruct((rows, n), F32),
        name="ada",
    )(cc, w, b)


def _conv3_rows(p, w_ref, b_ref, tm):
    n = tm + 2 * SEQ_HALO
    lo, hi = SEQ_HALO, SEQ_HALO + tm
    left = pltpu.roll(p, 1, 0)[lo:hi]
    right = pltpu.roll(p, n - 1, 0)[lo:hi]
    return left * w_ref[0:1, :] + p[lo:hi] * w_ref[1:2, :] + right * w_ref[2:3, :] + b_ref[...]


def _store_head_tiles_t(ref, val, tm):
    for h in range(N_HEADS):
        for ci in range(tm // CHUNK):
            tile = val[ci * CHUNK:(ci + 1) * CHUNK, h * HEAD_DIM:(h + 1) * HEAD_DIM]
            ref[0, h, ci] = tile.T.astype(BF16)


W_QK = slice(0, 2 * D_MODEL)
W_V = slice(2 * D_MODEL, 3 * D_MODEL)
W_G = slice(3 * D_MODEL, 3 * D_MODEL + N_GATES)
W_CTX_ROWS = W_G.stop
W_O = slice(W_G.stop, W_G.stop + D_MODEL)
W_SB = slice(W_O.stop, W_O.stop + D_MODEL)
W_SCX = slice(W_SB.stop, W_SB.stop + 2 * D_MODEL)
W_MG = slice(W_SCX.stop, W_SCX.stop + 2 * D_MODEL)


def _dot_nt(a, b):
    return lax.dot_general(a, b, (((1,), (1,)), ((), ())), preferred_element_type=F32)


def _proj_kernel(*refs, tm, full):
    if full:
        (xm_ref, xp_ref, xn_ref, mod_ref, g1_ref, w_ref, gb_ref, qkw_ref, qkb_ref,
         scw_ref, scb_ref, wco_ref,
         k_ref, vt_ref, g_ref, qt_ref, ogt_ref, gm_ref, zc_ref) = refs
    else:
        (xm_ref, xp_ref, xn_ref, mod_ref, g1_ref, w_ref, gb_ref, qkw_ref, qkb_ref,
         k_ref, vt_ref, g_ref) = refs
    t = pl.program_id(1)
    nt = pl.num_programs(1)
    shift = mod_ref[0, 0:1, :]
    scale1 = 1.0 + mod_ref[0, 1:2, :]
    gain = g1_ref[...]

    def norm_mod(xv):
        return _rms_scale(xv) * gain * scale1 + shift

    hm = norm_mod(xm_ref[0])
    hp = jnp.where(t > 0, norm_mod(xp_ref[0]), 0.0)
    hn = jnp.where(t < nt - 1, norm_mod(xn_ref[0]), 0.0)
    hmb = hm.astype(BF16)
    he = jnp.concatenate([hp, hm, hn], axis=0).astype(BF16)

    pqk = _dot_nt(he, w_ref[W_QK, :])
    pv = _dot_nt(hmb, w_ref[W_V, :])
    qk = _conv3_rows(pqk, qkw_ref, qkb_ref, tm)
    qk = _silu(qk)
    k_ref[0] = (qk[:, D_MODEL:] * (HEAD_DIM ** -0.5)).astype(BF16)
    pg = _dot_nt(w_ref[W_G, :], hmb)
    if not full:
        _store_head_tiles_t(vt_ref, pv, tm)
        g_ref[0] = pg + gb_ref[...]
        return
    po = _dot_nt(hmb, w_ref[W_O, :])
    _store_head_tiles_t(qt_ref, qk[:, :D_MODEL], tm)
    _store_head_tiles_t(vt_ref, pv, tm)
    g_ref[0] = pg + gb_ref[...]
    pcx = _dot_nt(he, w_ref[W_SCX, :])
    _store_head_tiles_t(ogt_ref, _sigmoid(po), tm)
    psb = _dot_nt(hmb, w_ref[W_SB, :])
    cu = _conv3_rows(pcx[:, :D_MODEL] * pcx[:, D_MODEL:], scw_ref, scb_ref, tm)
    pm = _dot_nt(hmb, w_ref[W_MG, :])
    yc = _dot((psb * cu).astype(BF16), wco_ref[...])
    gm_ref[0] = _sigmoid(pm[:, :D_MODEL]).astype(BF16)
    zc_ref[0] = (_sigmoid(pm[:, D_MODEL:]) * yc).astype(BF16)


def _proj(xs, mod, g1, weights, *, full, per_batch_mod):
    bsz, t_len, _ = xs.shape
    tm = min(PROJ_TM, t_len)
    nt = t_len // tm
    hb = tm // SEQ_HALO
    nhb = t_len // SEQ_HALO
    x_specs = [
        pl.BlockSpec((1, tm, D_MODEL), lambda b, t: (b, t, 0)),
        pl.BlockSpec((1, SEQ_HALO, D_MODEL), lambda b, t: (b, jnp.maximum(t * hb - 1, 0), 0)),
        pl.BlockSpec((1, SEQ_HALO, D_MODEL), lambda b, t: (b, jnp.minimum((t + 1) * hb, nhb - 1), 0)),
    ]
    mod_spec = pl.BlockSpec((1, 8, D_MODEL), (lambda b, t: (b, 0, 0)) if per_batch_mod else (lambda b, t: (0, 0, 0)))
    w_rows = weights[0].shape[0] if full else W_CTX_ROWS
    w_specs = [_const_spec((w_rows, D_MODEL))] + [_const_spec(w.shape) for w in weights[1:]]
    tok = lambda n, dt: (pl.BlockSpec((1, tm, n), lambda b, t: (b, t, 0)),
                         jax.ShapeDtypeStruct((bsz, t_len, n), dt))
    tiles_t = (pl.BlockSpec((1, N_HEADS, tm // CHUNK, HEAD_DIM, CHUNK), lambda b, t: (b, 0, t, 0, 0)),
               jax.ShapeDtypeStruct((bsz, N_HEADS, t_len // CHUNK, HEAD_DIM, CHUNK), BF16))
    gates_t = (pl.BlockSpec((1, N_GATES, tm), lambda b, t: (b, 0, t)),
               jax.ShapeDtypeStruct((bsz, N_GATES, t_len), F32))
    outs = [tok(D_MODEL, BF16), tiles_t, gates_t]
    if full:
        outs += [tiles_t, tiles_t, tok(D_MODEL, BF16), tok(D_MODEL, BF16)]
    return pl.pallas_call(
        functools.partial(_proj_kernel, tm=tm, full=full),
        grid=(bsz, nt),
        in_specs=x_specs + [mod_spec, _const_spec(g1.shape)] + w_specs,
        out_specs=[o[0] for o in outs],
        out_shape=[o[1] for o in outs],
        compiler_params=pltpu.CompilerParams(dimension_semantics=("parallel", "parallel"),
                                             vmem_limit_bytes=VMEM_LIMIT),
        name="proj_full" if full else "proj_ctx",
    )(xs, xs, xs, mod, g1, *weights)


def _scan_lanes(v, combine, fill, forward):
    lane = lax.broadcasted_iota(jnp.int32, v.shape, 1)
    d = 1
    while d < LANES:
        if forward:
            shifted = jnp.where(lane >= d, pltpu.roll(v, d, 1), fill)
        else:
            shifted = jnp.where(lane < LANES - d, pltpu.roll(v, LANES - d, 1), fill)
        v = combine(v, shifted)
        d *= 2
    return v


def _gates_kernel(gl_ref, gc_ref, out_ref, *, n_ctx, n_lat):
    n_all = n_ctx + n_lat
    src = lax.broadcasted_iota(jnp.int32, (CHUNK, 2 * CHUNK), 0)
    dst = lax.broadcasted_iota(jnp.int32, (CHUNK, 2 * CHUNK), 1)

    def gate_rows(cg, first):
        if cg < n_ctx:
            return gc_ref[0, first:first + N_HEADS, cg * CHUNK:(cg + 1) * CHUNK]
        return gl_ref[0, first:first + N_HEADS, (cg - n_ctx) * CHUNK:(cg - n_ctx + 1) * CHUNK]

    for dirn in range(2):
        forward = dirn == 0
        if forward:
            order = list(range(n_all))
        else:
            order = list(range(n_ctx - 1, -1, -1)) + list(range(n_all - 1, n_ctx - 1, -1))
        last = LANES - 1 if forward else 0
        first_row = 2 * N_HEADS * dirn
        fg = jnp.concatenate([gate_rows(cg, first_row + N_HEADS) for cg in range(n_all)], axis=0)
        lf = jnp.minimum(fg, 0.0) - jnp.log1p(jnp.exp(-jnp.abs(fg)))
        prefix = (src <= dst) if forward else (src >= dst)
        sel = jnp.where((dst >= CHUNK) | prefix, 1.0, 0.0).astype(BF16)
        hi = lf.astype(BF16)
        rest = lf - hi.astype(F32)
        mid = rest.astype(BF16)
        low = (rest - mid.astype(F32)).astype(BF16)
        sums = _dot(hi, sel) + _dot(mid, sel) + _dot(low, sel)
        m = jnp.full((N_HEADS, 1), M_INIT, F32)
        for cg in order:
            rows = slice(cg * N_HEADS, (cg + 1) * N_HEADS)
            b = sums[rows, :CHUNK]
            b_end = sums[rows, CHUNK:CHUNK + 1]
            r = gate_rows(cg, first_row) - b
            cm = _scan_lanes(r, jnp.maximum, -jnp.inf, forward)
            mx = jnp.maximum(m, cm)
            mt = b + mx
            m_new = b_end + jnp.maximum(m, cm[:, last:last + 1])
            cbase = C_BASE + C_PER_DIR * dirn
            fields = {
                R_ROWB + dirn: r,
                R_DECAY + dirn: jnp.broadcast_to(jnp.exp(b_end + m - m_new), (N_HEADS, CHUNK)),
                cbase + F_COLA: -mx,
                cbase + F_AINT: jnp.exp(m - mx),
                cbase + F_ENEG: jnp.exp(-mt),
                cbase + F_WK: jnp.exp((b_end - m_new) + r),
            }
            for f, val in fields.items():
                for h in range(N_HEADS):
                    out_ref[0, h, cg, f:f + 1, :] = val[h:h + 1, :]
            m = m_new
    n_used = C_BASE + 2 * C_PER_DIR
    out_ref[0, :, :, n_used:, :] = jnp.zeros((N_HEADS, n_all, N_FIELDS - n_used, CHUNK), F32)


def _gates(g_lat, g_ctx):
    bsz, _, t_lat = g_lat.shape
    t_ctx = g_ctx.shape[2]
    n_all = (t_lat + t_ctx) // CHUNK
    shape = (bsz, N_HEADS, n_all, N_FIELDS, CHUNK)
    return pl.pallas_call(
        functools.partial(_gates_kernel, n_ctx=t_ctx // CHUNK, n_lat=t_lat // CHUNK),
        grid=(bsz,),
        in_specs=[pl.BlockSpec((1, N_GATES, t_lat), lambda b: (b, 0, 0)),
                  pl.BlockSpec((1, N_GATES, t_ctx), lambda b: (b, 0, 0))],
        out_specs=pl.BlockSpec((1,) + shape[1:], lambda b: (b, 0, 0, 0, 0)),
        out_shape=jax.ShapeDtypeStruct(shape, F32),
        compiler_params=pltpu.CompilerParams(dimension_semantics=("parallel",)),
        name="gates",
    )(g_lat, g_ctx)


STATE_ROWS = HEAD_DIM + BF16_ROWS
MLSTM_HEADS = 2


def _mlstm_kernel(qt_ref, kl_ref, vtl_ref, kc_ref, vtc_ref, gf_ref, ogt_ref, ng_ref, out_ref,
                  dz_s, zf_s, zb_s, z_s, p_s, *, n_ctx, n_lat):
    lane = lax.broadcasted_iota(jnp.int32, (CHUNK, CHUNK), 1)
    row = lax.broadcasted_iota(jnp.int32, (CHUNK, CHUNK), 0)
    tri = (row <= lane, row >= lane)
    tail_first = lax.broadcasted_iota(jnp.int32, (BF16_ROWS, CHUNK), 0) == 0
    ones_tail = jnp.where(tail_first, 1.0, 0.0).astype(BF16)

    for hh in range(MLSTM_HEADS):
        _mlstm_head(hh, qt_ref, kl_ref, vtl_ref, kc_ref, vtc_ref, gf_ref, ogt_ref, ng_ref, out_ref,
                    dz_s, zf_s, zb_s, z_s, p_s, tri, tail_first, ones_tail, n_ctx, n_lat)


def _mlstm_head(hh, qt_ref, kl_ref, vtl_ref, kc_ref, vtc_ref, gf_ref, ogt_ref, ng_ref, out_ref,
                dz_s, zf_s, zb_s, z_s, p_s, tri, tail_first, ones_tail, n_ctx, n_lat):
    hcols = slice(hh * HEAD_DIM, (hh + 1) * HEAD_DIM)

    def field(cg, dirn, f):
        j = C_BASE + C_PER_DIR * dirn + f
        return gf_ref[0, hh, cg, j:j + 1, :]

    def pass_a(cg, k, vt):
        vf = vt.astype(F32)
        parts = []
        for dirn in range(2):
            wk = field(cg, dirn, F_WK)
            parts += [(vf * wk).astype(BF16), jnp.where(tail_first, wk, 0.0).astype(BF16)]
        dz_s[cg] = _dot(jnp.concatenate(parts, axis=0), k)

    for cg in range(n_ctx):
        pass_a(cg, kc_ref[0, cg * CHUNK:(cg + 1) * CHUNK, hcols], vtc_ref[0, hh, cg])

    def pass_a_lat(i, carry):
        sl = pl.ds(pl.multiple_of(i * CHUNK, CHUNK), CHUNK)
        pass_a(n_ctx + i, kl_ref[0, sl, hcols], vtl_ref[0, hh, i])
        return carry

    lax.fori_loop(0, n_lat, pass_a_lat, 0, unroll=8)

    def advance(dirn, cg):
        dec = gf_ref[0, hh, cg, R_DECAY + dirn:R_DECAY + dirn + 1, :]
        z_s[dirn] = dec * z_s[dirn] + dz_s[cg, dirn * STATE_ROWS:(dirn + 1) * STATE_ROWS, :]

    z_s[...] = jnp.zeros_like(z_s)
    for cg in range(n_ctx):
        advance(0, cg)
    for cg in range(n_ctx - 1, -1, -1):
        advance(1, cg)

    def pass_b(i, carry):
        zf_s[i] = z_s[0].astype(BF16)
        advance(0, n_ctx + i)
        j = n_lat - 1 - i
        zb_s[j] = z_s[1].astype(BF16)
        advance(1, n_ctx + j)
        return carry

    lax.fori_loop(0, n_lat, pass_b, 0, unroll=2)

    gain = jnp.broadcast_to(ng_ref[:, hcols], (CHUNK, HEAD_DIM)).T

    def pass_c1(i, carry):
        cg = n_ctx + i
        sl = pl.ds(pl.multiple_of(i * CHUNK, CHUNK), CHUNK)
        st = _dot(kl_ref[0, sl, hcols], qt_ref[0, hh, i])
        rowb = gf_ref[0, hh, cg].T
        for dirn in range(2):
            dmat = rowb[:, R_ROWB + dirn:R_ROWB + dirn + 1] + field(cg, dirn, F_COLA)
            w = jnp.exp(jnp.where(tri[dirn], dmat, -jnp.inf))
            p_s[i, dirn] = (st * w).astype(BF16)
        return carry

    lax.fori_loop(0, n_lat, pass_c1, 0, unroll=16)

    def pass_c2(i, carry):
        cg = n_ctx + i
        qtf = qt_ref[0, hh, i].astype(F32)
        vaug = jnp.concatenate([vtl_ref[0, hh, i], ones_tail], axis=0)
        ht = None
        for dirn, z_ref in enumerate((zf_s, zb_s)):
            rhs = jnp.concatenate([(qtf * field(cg, dirn, F_AINT)).astype(BF16), p_s[i, dirn]], axis=0)
            res = _dot(jnp.concatenate([z_ref[i], vaug], axis=1), rhs)
            den = jnp.maximum(jnp.abs(res[HEAD_DIM:HEAD_DIM + 1, :]), field(cg, dirn, F_ENEG))
            hd = res[:HEAD_DIM, :] / den
            ht = hd if ht is None else ht + hd
        ht = ht * lax.rsqrt(jnp.mean(ht * ht, axis=0, keepdims=True) + EPS)
        out_ref[0, hh, i] = (ht * gain * ogt_ref[0, hh, i].astype(F32)).astype(BF16)
        return carry

    lax.fori_loop(0, n_lat, pass_c2, 0, unroll=16)


def _mlstm(qt_l, k_l, vt_l, k_c, vt_c, gf, ogt, ng):
    bsz, t_lat, _ = k_l.shape
    t_ctx = k_c.shape[1]
    n_lat, n_ctx = t_lat // CHUNK, t_ctx // CHUNK
    n_all = n_lat + n_ctx
    hb = MLSTM_HEADS
    head_lat = pl.BlockSpec((1, t_lat, hb * HEAD_DIM), lambda b, h: (b, 0, h))
    head_ctx = pl.BlockSpec((1, t_ctx, hb * HEAD_DIM), lambda b, h: (b, 0, h))
    tiles = lambda n: pl.BlockSpec((1, hb, n, HEAD_DIM, CHUNK), lambda b, h: (b, h, 0, 0, 0))
    return pl.pallas_call(
        functools.partial(_mlstm_kernel, n_ctx=n_ctx, n_lat=n_lat),
        grid=(bsz, N_HEADS // hb),
        in_specs=[tiles(n_lat), head_lat, tiles(n_lat), head_ctx, tiles(n_ctx),
                  pl.BlockSpec((1, hb, n_all, N_FIELDS, CHUNK), lambda b, h: (b, h, 0, 0, 0)),
                  tiles(n_lat),
                  pl.BlockSpec((1, hb * HEAD_DIM), lambda b, h: (0, h))],
        out_specs=tiles(n_lat),
        out_shape=jax.ShapeDtypeStruct(qt_l.shape, BF16),
        scratch_shapes=[pltpu.VMEM((n_all, 2 * STATE_ROWS, HEAD_DIM), F32),
                        pltpu.VMEM((n_lat, STATE_ROWS, HEAD_DIM), BF16),
                        pltpu.VMEM((n_lat, STATE_ROWS, HEAD_DIM), BF16),
                        pltpu.VMEM((2, STATE_ROWS, HEAD_DIM), F32),
                        pltpu.VMEM((n_lat, 2, CHUNK, CHUNK), BF16)],
        compiler_params=pltpu.CompilerParams(dimension_semantics=("parallel", "parallel"),
                                             vmem_limit_bytes=VMEM_LIMIT),
        name="mlstm",
    )(qt_l, k_l, vt_l, k_c, vt_c, gf, ogt, ng)


def _merge_kernel(hgt_ref, gm_ref, zc_ref, x_ref, mod_ref, g2_ref, wmo_ref, wo_ref, x1_ref, h2_ref):
    per_sub = MERGE_SUB // CHUNK
    subs = [slice(i * MERGE_SUB, (i + 1) * MERGE_SUB) for i in range(hgt_ref.shape[2] // per_sub)]
    hg = [jnp.concatenate(
        [jnp.concatenate([hgt_ref[0, h, ci].astype(F32).T for h in range(N_HEADS)], axis=1)
         for ci in range(i * per_sub, (i + 1) * per_sub)], axis=0).astype(BF16) for i in range(len(subs))]
    ym = [_dot(hg_i, wmo_ref[...]) for hg_i in hg]
    y = [_dot((gm_ref[0, sl, :].astype(F32) * ym_i + zc_ref[0, sl, :].astype(F32)).astype(BF16), wo_ref[...])
         for sl, ym_i in zip(subs, ym)]
    for sl, y_i in zip(subs, y):
        x1 = x_ref[0, sl, :] + mod_ref[0, 2:3, :] * y_i
        x1_ref[0, sl, :] = x1
        h2 = _rms_scale(x1) * g2_ref[...] * (1.0 + mod_ref[0, 4:5, :]) + mod_ref[0, 3:4, :]
        h2_ref[0, sl, :] = h2.astype(BF16)


def _merge(hgt, gm, zc, x, mod, g2, wmo, wo):
    bsz, t_len, _ = x.shape
    tm = MERGE_TM
    tok = pl.BlockSpec((1, tm, D_MODEL), lambda b, t: (b, t, 0))
    tiles = pl.BlockSpec((1, N_HEADS, tm // CHUNK, HEAD_DIM, CHUNK), lambda b, t: (b, 0, t, 0, 0))
    return pl.pallas_call(
        _merge_kernel,
        grid=(bsz, t_len // tm),
        in_specs=[tiles, tok, tok, tok, pl.BlockSpec((1, 8, D_MODEL), lambda b, t: (b, 0, 0)),
                  _const_spec(g2.shape), _const_spec(wmo.shape), _const_spec(wo.shape)],
        out_specs=[tok, tok],
        out_shape=[jax.ShapeDtypeStruct(x.shape, F32), jax.ShapeDtypeStruct(x.shape, BF16)],
        compiler_params=pltpu.CompilerParams(dimension_semantics=("parallel", "parallel"),
                                             vmem_limit_bytes=VMEM_LIMIT),
        name="merge",
    )(hgt, gm, zc, x, mod, g2, wmo, wo)


def _gelu_tanh(v):
    return 0.5 * v * (1.0 + jnp.tanh(0.7978845608028654 * (v + 0.044715 * (v * v * v))))


def _ffn_kernel(hm_ref, hp_ref, hn_ref, x1_ref, mod_ref, wu_ref, cw_ref, cb_ref, wd_ref, fg_ref,
                out_ref, act_s, *, tm):
    t = pl.program_id(1)
    nt = pl.num_programs(1)
    n = tm + 2 * GRID_W
    hm = hm_ref[0]
    hp = jnp.where(t > 0, hp_ref[0], jnp.zeros_like(hp_ref[0]))
    hn = jnp.where(t < nt - 1, hn_ref[0], jnp.zeros_like(hn_ref[0]))
    he = jnp.concatenate([hp, hm, hn], axis=0)
    gcol = lax.broadcasted_iota(jnp.int32, (n, 1), 0) & (GRID_W - 1)
    has_left = gcol != 0
    has_right = gcol != GRID_W - 1
    n_chunks = FF_HIDDEN // FFN_CW
    cols = lambda j: slice(j * FFN_CW, (j + 1) * FFN_CW)
    gate_cols = lambda j: slice(FF_HIDDEN + j * FFN_CW, FF_HIDDEN + (j + 1) * FFN_CW)
    up = lambda j: (_dot(he, wu_ref[:, cols(j)]), _dot(hm, wu_ref[:, gate_cols(j)]))
    acc = None
    nxt = up(0)
    for j in range(n_chunks):
        cs = cols(j)
        a, g = nxt
        if j + 1 < n_chunks:
            nxt = up(j + 1)
        taps = (jnp.where(has_left, pltpu.roll(a, 1, 0), 0.0), a,
                jnp.where(has_right, pltpu.roll(a, n - 1, 0), 0.0))
        conv = cb_ref[:, cs]
        for dr in range(3):
            for dc in range(3):
                conv = conv + cw_ref[3 * dr + dc: 3 * dr + dc + 1, cs] * taps[dc][GRID_W * dr: GRID_W * dr + tm]
        act_s[:, cs] = (_gelu_tanh(conv) * g).astype(BF16)
        if (j + 1) % FFN_DOWN_GROUP == 0 or j + 1 == n_chunks:
            gs = slice((j // FFN_DOWN_GROUP) * FFN_DOWN_GROUP * FFN_CW, (j + 1) * FFN_CW)
            part = _dot(act_s[:, gs], wd_ref[gs, :])
            acc = part if acc is None else acc + part
    x2 = x1_ref[0] + mod_ref[0, 5:6, :] * acc
    out_ref[0] = _rms_scale(x2) * fg_ref[...]


def _ffn(h2, x1, mod, wu, cw, cb, wd, fg):
    bsz, t_len, _ = x1.shape
    tm = FFN_TM
    rb = tm // GRID_W
    nrb = t_len // GRID_W
    tok = pl.BlockSpec((1, tm, D_MODEL), lambda b, t: (b, t, 0))
    return pl.pallas_call(
        functools.partial(_ffn_kernel, tm=tm),
        grid=(bsz, t_len // tm),
        in_specs=[tok,
                  pl.BlockSpec((1, GRID_W, D_MODEL), lambda b, t: (b, jnp.maximum(t * rb - 1, 0), 0)),
                  pl.BlockSpec((1, GRID_W, D_MODEL), lambda b, t: (b, jnp.minimum((t + 1) * rb, nrb - 1), 0)),
                  tok, pl.BlockSpec((1, 8, D_MODEL), lambda b, t: (b, 0, 0)),
                  _const_spec(wu.shape), _const_spec(cw.shape), _const_spec(cb.shape),
                  _const_spec(wd.shape), _const_spec(fg.shape)],
        out_specs=tok,
        out_shape=jax.ShapeDtypeStruct(x1.shape, F32),
        scratch_shapes=[pltpu.VMEM((tm, FF_HIDDEN), BF16)],
        compiler_params=pltpu.CompilerParams(dimension_semantics=("parallel", "parallel"),
                                             vmem_limit_bytes=VMEM_LIMIT),
        name="ffn",
    )(h2, h2, h2, x1, mod, wu, cw, cb, wd, fg)


def kernel(x, c, ctx, c_ctx, ada_w, ada_b, norm1_g, norm2_g, w_in, qk_conv_w, qk_conv_b, gate_b, mnorm_g,
           w_m_out, sc_conv_w, sc_conv_b, w_c_out, w_o, w_up, ff_conv_w, ff_conv_b, w_down, final_g):
    assert ada_w.shape[0] == 1, "single-layer block"
    bsz, t_lat, _ = x.shape
    t_ctx = ctx.shape[1]

    cc = jnp.zeros((2 * 8, D_MODEL), F32).at[:bsz].set(c).at[bsz].set(c_ctx)
    mod = _ada(cc, ada_w[0], ada_b[0][None, :])
    mod_x = jnp.pad(mod[:bsz].reshape(bsz, 6, D_MODEL), ((0, 0), (0, 2), (0, 0)))
    mod_c = jnp.pad(mod[bsz].reshape(1, 6, D_MODEL), ((0, 0), (0, 2), (0, 0)))

    cast = lambda a: a.astype(BF16)
    w_t = cast(jnp.swapaxes(w_in, 1, 2)[0])
    gb = gate_b[0][:, None]
    g1 = norm1_g[0][None, :]
    ctx_w = [w_t, gb, qk_conv_w[0], qk_conv_b[0][None, :]]
    lat_w = ctx_w + [sc_conv_w[0], sc_conv_b[0][None, :], cast(w_c_out[0])]

    k_l, vt_l, g_l, qt_l, ogt, gm, zc = _proj(x, mod_x, g1, lat_w, full=True, per_batch_mod=True)
    k_c, vt_c, g_c = _proj(ctx, mod_c, g1, ctx_w, full=False, per_batch_mod=False)

    gf = _gates(g_l, g_c)
    hgt = _mlstm(qt_l, k_l, vt_l, k_c, vt_c, gf, ogt, mnorm_g[0][None, :])

    x1, h2 = _merge(hgt, gm, zc, x, mod_x, norm2_g[0][None, :], cast(w_m_out[0]), cast(w_o[0]))

    out = _ffn(h2, x1, mod_x, cast(w_up[0]),
               ff_conv_w[0].reshape(9, FF_HIDDEN), ff_conv_b[0][None, :], cast(w_down[0]),
               final_g[None, :])
    return out
```

```python
import functools

import jax
import jax.numpy as jnp
from jax import lax
from jax.experimental import pallas as pl
from jax.experimental.pallas import tpu as pltpu

F32 = jnp.float32
BF16 = jnp.bfloat16

D_MODEL = 1024
N_HEADS = 8
HEAD_DIM = D_MODEL // N_HEADS
CHUNK = 128
GRID_W = 64
FF_HIDDEN = 2816
EPS = 1e-6
M_INIT = -1e30
N_GATES = 4 * N_HEADS

LANES = 128
BF16_ROWS = 16
VMEM_LIMIT = 56 * 1024 * 1024

PROJ_TM = 256
SEQ_HALO = 8
MERGE_TM = 1024
MERGE_SUB = 256
FFN_TM = 512
FFN_CW = 256
FFN_DOWN_GROUP = 4

R_ROWB, R_DECAY = 0, 2
C_BASE, C_PER_DIR = 4, 4
F_COLA, F_AINT, F_ENEG, F_WK = range(4)
N_FIELDS = 16


def _const_spec(shape):
    nd = len(shape)
    return pl.BlockSpec(shape, lambda *_: (0,) * nd, pipeline_mode=pl.Buffered(1))


def _sigmoid(v):
    return 0.5 * jnp.tanh(0.5 * v) + 0.5


def _silu(v):
    h = 0.5 * v
    return h + h * jnp.tanh(h)


def _rms_scale(v):
    return v * lax.rsqrt(jnp.mean(v * v, axis=-1, keepdims=True) + EPS)


def _dot(a, b):
    return jnp.dot(a, b, preferred_element_type=F32)


def _ada_kernel(c_ref, w_ref, b_ref, o_ref):
    cv = c_ref[...]
    s = _silu(cv)
    w = w_ref[...]
    s_hi, w_hi = s.astype(BF16), w.astype(BF16)
    s_lo = (s - s_hi.astype(F32)).astype(BF16)
    w_lo = (w - w_hi.astype(F32)).astype(BF16)
    rows = s.shape[0]
    main = _dot(jnp.concatenate([s_hi, s_lo], axis=0), w_hi)
    o_ref[...] = main[:rows] + main[rows:] + _dot(s_hi, w_lo) + b_ref[...]


def _ada(cc, w, b):
    rows, n = cc.shape[0], w.shape[1]
    tn = 1024
    return pl.pallas_call(
        _ada_kernel,
        grid=(n // tn,),
        in_specs=[pl.BlockSpec((rows, D_MODEL), lambda j: (0, 0)),
                  pl.BlockSpec((D_MODEL, tn), lambda j: (0, j)),
                  pl.BlockSpec((1, tn), lambda j: (0, j))],
        out_specs=pl.BlockSpec((rows, tn), lambda j: (0, j)),
        out_shape=jax.ShapeDtypeStruct((rows, n), F32),
        name="ada",
    )(cc, w, b)


def _conv3_rows(p, w_ref, b_ref, tm):
    main = p[:tm]
    prev_row = p[tm + SEQ_HALO - 1:tm + SEQ_HALO]
    next_row = p[tm + SEQ_HALO:tm + SEQ_HALO + 1]
    sub = lax.broadcasted_iota(jnp.int32, (SEQ_HALO, 1), 0)
    down = pltpu.roll(main, 1, 0)
    up = pltpu.roll(main, tm - 1, 0)
    left = jnp.concatenate([jnp.where(sub == 0, prev_row, down[:SEQ_HALO]), down[SEQ_HALO:]], axis=0)
    right = jnp.concatenate([up[:tm - SEQ_HALO], jnp.where(sub == SEQ_HALO - 1, next_row, up[tm - SEQ_HALO:])],
                            axis=0)
    return left * w_ref[0:1, :] + main * w_ref[1:2, :] + right * w_ref[2:3, :] + b_ref[...]


def _store_head_tiles_t(ref, val, tm):
    for h in range(N_HEADS):
        for ci in range(tm // CHUNK):
            tile = val[ci * CHUNK:(ci + 1) * CHUNK, h * HEAD_DIM:(h + 1) * HEAD_DIM]
            ref[0, h, ci] = tile.T.astype(BF16)


W_QK = slice(0, 2 * D_MODEL)
W_V = slice(2 * D_MODEL, 3 * D_MODEL)
W_G = slice(3 * D_MODEL, 3 * D_MODEL + N_GATES)
W_CTX_ROWS = W_G.stop
W_O = slice(W_G.stop, W_G.stop + D_MODEL)
W_SB = slice(W_O.stop, W_O.stop + D_MODEL)
W_SCX = slice(W_SB.stop, W_SB.stop + 2 * D_MODEL)
W_MG = slice(W_SCX.stop, W_SCX.stop + 2 * D_MODEL)


def _dot_nt(a, b):
    return lax.dot_general(a, b, (((1,), (1,)), ((), ())), preferred_element_type=F32)


def _proj_kernel(*refs, tm, full):
    if full:
        (xm_ref, xp_ref, xn_ref, mod_ref, g1_ref, w_ref, gb_ref, qkw_ref, qkb_ref,
         scw_ref, scb_ref, wco_ref,
         k_ref, vt_ref, g_ref, qt_ref, ogt_ref, gm_ref, zc_ref) = refs
    else:
        (xm_ref, xp_ref, xn_ref, mod_ref, g1_ref, w_ref, gb_ref, qkw_ref, qkb_ref,
         k_ref, vt_ref, g_ref) = refs
    t = pl.program_id(1)
    nt = pl.num_programs(1)
    shift = mod_ref[0, 0:1, :]
    scale1 = 1.0 + mod_ref[0, 1:2, :]
    gain = g1_ref[...]

    def norm_mod(xv):
        return _rms_scale(xv) * gain * scale1 + shift

    hm = norm_mod(xm_ref[0])
    hp = jnp.where(t > 0, norm_mod(xp_ref[0]), 0.0)
    hn = jnp.where(t < nt - 1, norm_mod(xn_ref[0]), 0.0)
    hmb = hm.astype(BF16)
    he = jnp.concatenate([hm, hp, hn], axis=0).astype(BF16)

    pqk = _dot_nt(he, w_ref[W_QK, :])
    pv = _dot_nt(hmb, w_ref[W_V, :])
    qk = _conv3_rows(pqk, qkw_ref, qkb_ref, tm)
    qk = _silu(qk)
    k_ref[0] = (qk[:, D_MODEL:] * (HEAD_DIM ** -0.5)).astype(BF16)
    pg = _dot_nt(w_ref[W_G, :], hmb)
    if not full:
        _store_head_tiles_t(vt_ref, pv, tm)
        g_ref[0] = pg + gb_ref[...]
        return
    po = _dot_nt(hmb, w_ref[W_O, :])
    _store_head_tiles_t(qt_ref, qk[:, :D_MODEL], tm)
    _store_head_tiles_t(vt_ref, pv, tm)
    g_ref[0] = pg + gb_ref[...]
    pcx = _dot_nt(he, w_ref[W_SCX, :])
    _store_head_tiles_t(ogt_ref, _sigmoid(po), tm)
    psb = _dot_nt(hmb, w_ref[W_SB, :])
    cu = _conv3_rows(pcx[:, :D_MODEL] * pcx[:, D_MODEL:], scw_ref, scb_ref, tm)
    pm = _dot_nt(hmb, w_ref[W_MG, :])
    yc = _dot((psb * cu).astype(BF16), wco_ref[...])
    gm_ref[0] = _sigmoid(pm[:, :D_MODEL]).astype(BF16)
    zc_ref[0] = (_sigmoid(pm[:, D_MODEL:]) * yc).astype(BF16)


def _proj(xs, mod, g1, weights, *, full, per_batch_mod):
    bsz, t_len, _ = xs.shape
    tm = min(PROJ_TM, t_len)
    nt = t_len // tm
    hb = tm // SEQ_HALO
    nhb = t_len // SEQ_HALO
    x_specs = [
        pl.BlockSpec((1, tm, D_MODEL), lambda b, t: (b, t, 0)),
        pl.BlockSpec((1, SEQ_HALO, D_MODEL), lambda b, t: (b, jnp.maximum(t * hb - 1, 0), 0)),
        pl.BlockSpec((1, SEQ_HALO, D_MODEL), lambda b, t: (b, jnp.minimum((t + 1) * hb, nhb - 1), 0)),
    ]
    mod_spec = pl.BlockSpec((1, 8, D_MODEL), (lambda b, t: (b, 0, 0)) if per_batch_mod else (lambda b, t: (0, 0, 0)))
    w_rows = weights[0].shape[0] if full else W_CTX_ROWS
    w_specs = [_const_spec((w_rows, D_MODEL))] + [_const_spec(w.shape) for w in weights[1:]]
    tok = lambda n, dt: (pl.BlockSpec((1, tm, n), lambda b, t: (b, t, 0)),
                         jax.ShapeDtypeStruct((bsz, t_len, n), dt))
    tiles_t = (pl.BlockSpec((1, N_HEADS, tm // CHUNK, HEAD_DIM, CHUNK), lambda b, t: (b, 0, t, 0, 0)),
               jax.ShapeDtypeStruct((bsz, N_HEADS, t_len // CHUNK, HEAD_DIM, CHUNK), BF16))
    gates_t = (pl.BlockSpec((1, N_GATES, tm), lambda b, t: (b, 0, t)),
               jax.ShapeDtypeStruct((bsz, N_GATES, t_len), F32))
    outs = [tok(D_MODEL, BF16), tiles_t, gates_t]
    if full:
        outs += [tiles_t, tiles_t, tok(D_MODEL, BF16), tok(D_MODEL, BF16)]
    return pl.pallas_call(
        functools.partial(_proj_kernel, tm=tm, full=full),
        grid=(bsz, nt),
        in_specs=x_specs + [mod_spec, _const_spec(g1.shape)] + w_specs,
        out_specs=[o[0] for o in outs],
        out_shape=[o[1] for o in outs],
        compiler_params=pltpu.CompilerParams(dimension_semantics=("parallel", "parallel"),
                                             vmem_limit_bytes=VMEM_LIMIT),
        name="proj_full" if full else "proj_ctx",
    )(xs, xs, xs, mod, g1, *weights)


def _scan_lanes(v, combine, fill, forward):
    lane = lax.broadcasted_iota(jnp.int32, v.shape, 1)
    d = 1
    while d < LANES:
        if forward:
            shifted = jnp.where(lane >= d, pltpu.roll(v, d, 1), fill)
        else:
            shifted = jnp.where(lane < LANES - d, pltpu.roll(v, LANES - d, 1), fill)
        v = combine(v, shifted)
        d *= 2
    return v


def _gates_kernel(gl_ref, gc_ref, out_ref, *, n_ctx, n_lat):
    n_all = n_ctx + n_lat
    src = lax.broadcasted_iota(jnp.int32, (CHUNK, 2 * CHUNK), 0)
    dst = lax.broadcasted_iota(jnp.int32, (CHUNK, 2 * CHUNK), 1)

    def gate_rows(cg, first):
        if cg < n_ctx:
            return gc_ref[0, first:first + N_HEADS, cg * CHUNK:(cg + 1) * CHUNK]
        return gl_ref[0, first:first + N_HEADS, (cg - n_ctx) * CHUNK:(cg - n_ctx + 1) * CHUNK]

    for dirn in range(2):
        forward = dirn == 0
        if forward:
            order = list(range(n_all))
        else:
            order = list(range(n_ctx - 1, -1, -1)) + list(range(n_all - 1, n_ctx - 1, -1))
        last = LANES - 1 if forward else 0
        first_row = 2 * N_HEADS * dirn
        fg = jnp.concatenate([gate_rows(cg, first_row + N_HEADS) for cg in range(n_all)], axis=0)
        lf = jnp.minimum(fg, 0.0) - jnp.log1p(jnp.exp(-jnp.abs(fg)))
        prefix = (src <= dst) if forward else (src >= dst)
        sel = jnp.where((dst >= CHUNK) | prefix, 1.0, 0.0).astype(BF16)
        hi = lf.astype(BF16)
        rest = lf - hi.astype(F32)
        mid = rest.astype(BF16)
        low = (rest - mid.astype(F32)).astype(BF16)
        sums = _dot(hi, sel) + _dot(mid, sel) + _dot(low, sel)
        m = jnp.full((N_HEADS, 1), M_INIT, F32)
        for cg in order:
            rows = slice(cg * N_HEADS, (cg + 1) * N_HEADS)
            b = sums[rows, :CHUNK]
            b_end = sums[rows, CHUNK:CHUNK + 1]
            r = gate_rows(cg, first_row) - b
            cm = _scan_lanes(r, jnp.maximum, -jnp.inf, forward)
            mx = jnp.maximum(m, cm)
            mt = b + mx
            m_new = b_end + jnp.maximum(m, cm[:, last:last + 1])
            cbase = C_BASE + C_PER_DIR * dirn
            fields = {
                R_ROWB + dirn: r,
                R_DECAY + dirn: jnp.broadcast_to(jnp.exp(b_end + m - m_new), (N_HEADS, CHUNK)),
                cbase + F_COLA: -mx,
                cbase + F_AINT: jnp.exp(m - mx),
                cbase + F_ENEG: jnp.exp(-mt),
                cbase + F_WK: jnp.exp((b_end - m_new) + r),
            }
            for f, val in fields.items():
                for h in range(N_HEADS):
                    out_ref[0, h, cg, f:f + 1, :] = val[h:h + 1, :]
            m = m_new
    n_used = C_BASE + 2 * C_PER_DIR
    out_ref[0, :, :, n_used:, :] = jnp.zeros((N_HEADS, n_all, N_FIELDS - n_used, CHUNK), F32)


def _gates(g_lat, g_ctx):
    bsz, _, t_lat = g_lat.shape
    t_ctx = g_ctx.shape[2]
    n_all = (t_lat + t_ctx) // CHUNK
    shape = (bsz, N_HEADS, n_all, N_FIELDS, CHUNK)
    return pl.pallas_call(
        functools.partial(_gates_kernel, n_ctx=t_ctx // CHUNK, n_lat=t_lat // CHUNK),
        grid=(bsz,),
        in_specs=[pl.BlockSpec((1, N_GATES, t_lat), lambda b: (b, 0, 0)),
                  pl.BlockSpec((1, N_GATES, t_ctx), lambda b: (b, 0, 0))],
        out_specs=pl.BlockSpec((1,) + shape[1:], lambda b: (b, 0, 0, 0, 0)),
        out_shape=jax.ShapeDtypeStruct(shape, F32),
        compiler_params=pltpu.CompilerParams(dimension_semantics=("parallel",)),
        name="gates",
    )(g_lat, g_ctx)


STATE_ROWS = HEAD_DIM + BF16_ROWS
MLSTM_HEADS = 2


def _mlstm_kernel(qt_ref, kl_ref, vtl_ref, kc_ref, vtc_ref, gf_ref, ogt_ref, ng_ref, out_ref,
                  dz_s, zf_s, zb_s, z_s, p_s, *, n_ctx, n_lat):
    lane = lax.broadcasted_iota(jnp.int32, (CHUNK, CHUNK), 1)
    row = lax.broadcasted_iota(jnp.int32, (CHUNK, CHUNK), 0)
    tri = (row <= lane, row >= lane)
    tail_first = lax.broadcasted_iota(jnp.int32, (BF16_ROWS, CHUNK), 0) == 0
    ones_tail = jnp.where(tail_first, 1.0, 0.0).astype(BF16)

    for hh in range(MLSTM_HEADS):
        _mlstm_head(hh, qt_ref, kl_ref, vtl_ref, kc_ref, vtc_ref, gf_ref, ogt_ref, ng_ref, out_ref,
                    dz_s, zf_s, zb_s, z_s, p_s, tri, tail_first, ones_tail, n_ctx, n_lat)


def _mlstm_head(hh, qt_ref, kl_ref, vtl_ref, kc_ref, vtc_ref, gf_ref, ogt_ref, ng_ref, out_ref,
                dz_s, zf_s, zb_s, z_s, p_s, tri, tail_first, ones_tail, n_ctx, n_lat):
    hcols = slice(hh * HEAD_DIM, (hh + 1) * HEAD_DIM)

    def field(cg, dirn, f):
        j = C_BASE + C_PER_DIR * dirn + f
        return gf_ref[0, hh, cg, j:j + 1, :]

    def pass_a(cg, k, vt):
        vf = vt.astype(F32)
        parts = []
        for dirn in range(2):
            wk = field(cg, dirn, F_WK)
            parts += [(vf * wk).astype(BF16), jnp.where(tail_first, wk, 0.0).astype(BF16)]
        dz_s[cg] = _dot(jnp.concatenate(parts, axis=0), k)

    for cg in range(n_ctx):
        pass_a(cg, kc_ref[0, cg * CHUNK:(cg + 1) * CHUNK, hcols], vtc_ref[0, hh, cg])

    def pass_a_lat(i, carry):
        sl = pl.ds(pl.multiple_of(i * CHUNK, CHUNK), CHUNK)
        pass_a(n_ctx + i, kl_ref[0, sl, hcols], vtl_ref[0, hh, i])
        return carry

    lax.fori_loop(0, n_lat, pass_a_lat, 0, unroll=8)

    def advance(dirn, cg):
        dec = gf_ref[0, hh, cg, R_DECAY + dirn:R_DECAY + dirn + 1, :]
        z_s[dirn] = dec * z_s[dirn] + dz_s[cg, dirn * STATE_ROWS:(dirn + 1) * STATE_ROWS, :]

    z_s[...] = jnp.zeros_like(z_s)
    for cg in range(n_ctx):
        advance(0, cg)
    for cg in range(n_ctx - 1, -1, -1):
        advance(1, cg)

    def pass_b(i, carry):
        zf_s[i] = z_s[0].astype(BF16)
        advance(0, n_ctx + i)
        j = n_lat - 1 - i
        zb_s[j] = z_s[1].astype(BF16)
        advance(1, n_ctx + j)
        return carry

    lax.fori_loop(0, n_lat, pass_b, 0, unroll=2)

    gain = jnp.broadcast_to(ng_ref[:, hcols], (CHUNK, HEAD_DIM)).T

    def pass_c1(i, carry):
        cg = n_ctx + i
        sl = pl.ds(pl.multiple_of(i * CHUNK, CHUNK), CHUNK)
        st = _dot(kl_ref[0, sl, hcols], qt_ref[0, hh, i])
        rowb = gf_ref[0, hh, cg].T
        for dirn in range(2):
            dmat = rowb[:, R_ROWB + dirn:R_ROWB + dirn + 1] + field(cg, dirn, F_COLA)
            w = jnp.exp(jnp.where(tri[dirn], dmat, -jnp.inf))
            p_s[i, dirn] = (st * w).astype(BF16)
        return carry

    lax.fori_loop(0, n_lat, pass_c1, 0, unroll=16)

    def pass_c2(i, carry):
        cg = n_ctx + i
        qtf = qt_ref[0, hh, i].astype(F32)
        vaug = jnp.concatenate([vtl_ref[0, hh, i], ones_tail], axis=0)
        ht = None
        for dirn, z_ref in enumerate((zf_s, zb_s)):
            rhs = jnp.concatenate([(qtf * field(cg, dirn, F_AINT)).astype(BF16), p_s[i, dirn]], axis=0)
            res = _dot(jnp.concatenate([z_ref[i], vaug], axis=1), rhs)
            den = jnp.maximum(jnp.abs(res[HEAD_DIM:HEAD_DIM + 1, :]), field(cg, dirn, F_ENEG))
            hd = res[:HEAD_DIM, :] / den
            ht = hd if ht is None else ht + hd
        ht = ht * lax.rsqrt(jnp.mean(ht * ht, axis=0, keepdims=True) + EPS)
        out_ref[0, hh, i] = (ht * gain * ogt_ref[0, hh, i].astype(F32)).astype(BF16)
        return carry

    lax.fori_loop(0, n_lat, pass_c2, 0, unroll=16)


def _mlstm(qt_l, k_l, vt_l, k_c, vt_c, gf, ogt, ng):
    bsz, t_lat, _ = k_l.shape
    t_ctx = k_c.shape[1]
    n_lat, n_ctx = t_lat // CHUNK, t_ctx // CHUNK
    n_all = n_lat + n_ctx
    hb = MLSTM_HEADS
    head_lat = pl.BlockSpec((1, t_lat, hb * HEAD_DIM), lambda b, h: (b, 0, h))
    head_ctx = pl.BlockSpec((1, t_ctx, hb * HEAD_DIM), lambda b, h: (b, 0, h))
    tiles = lambda n: pl.BlockSpec((1, hb, n, HEAD_DIM, CHUNK), lambda b, h: (b, h, 0, 0, 0))
    return pl.pallas_call(
        functools.partial(_mlstm_kernel, n_ctx=n_ctx, n_lat=n_lat),
        grid=(bsz, N_HEADS // hb),
        in_specs=[tiles(n_lat), head_lat, tiles(n_lat), head_ctx, tiles(n_ctx),
                  pl.BlockSpec((1, hb, n_all, N_FIELDS, CHUNK), lambda b, h: (b, h, 0, 0, 0)),
                  tiles(n_lat),
                  pl.BlockSpec((1, hb * HEAD_DIM), lambda b, h: (0, h))],
        out_specs=tiles(n_lat),
        out_shape=jax.ShapeDtypeStruct(qt_l.shape, BF16),
        scratch_shapes=[pltpu.VMEM((n_all, 2 * STATE_ROWS, HEAD_DIM), F32),
                        pltpu.VMEM((n_lat, STATE_ROWS, HEAD_DIM), BF16),
                        pltpu.VMEM((n_lat, STATE_ROWS, HEAD_DIM), BF16),
                        pltpu.VMEM((2, STATE_ROWS, HEAD_DIM), F32),
                        pltpu.VMEM((n_lat, 2, CHUNK, CHUNK), BF16)],
        compiler_params=pltpu.CompilerParams(dimension_semantics=("parallel", "parallel"),
                                             vmem_limit_bytes=VMEM_LIMIT),
        name="mlstm",
    )(qt_l, k_l, vt_l, k_c, vt_c, gf, ogt, ng)


def _merge_kernel(hgt_ref, gm_ref, zc_ref, x_ref, mod_ref, g2_ref, wmo_ref, wo_ref, x1_ref, h2_ref):
    per_sub = MERGE_SUB // CHUNK
    subs = [slice(i * MERGE_SUB, (i + 1) * MERGE_SUB) for i in range(hgt_ref.shape[2] // per_sub)]
    hg = [jnp.concatenate(
        [jnp.concatenate([hgt_ref[0, h, ci].astype(F32).T for h in range(N_HEADS)], axis=1)
         for ci in range(i * per_sub, (i + 1) * per_sub)], axis=0).astype(BF16) for i in range(len(subs))]
    ym = [_dot(hg_i, wmo_ref[...]) for hg_i in hg]
    y = [_dot((gm_ref[0, sl, :].astype(F32) * ym_i + zc_ref[0, sl, :].astype(F32)).astype(BF16), wo_ref[...])
         for sl, ym_i in zip(subs, ym)]
    for sl, y_i in zip(subs, y):
        x1 = x_ref[0, sl, :] + mod_ref[0, 2:3, :] * y_i
        x1_ref[0, sl, :] = x1
        h2 = _rms_scale(x1) * g2_ref[...] * (1.0 + mod_ref[0, 4:5, :]) + mod_ref[0, 3:4, :]
        h2_ref[0, sl, :] = h2.astype(BF16)


def _merge(hgt, gm, zc, x, mod, g2, wmo, wo):
    bsz, t_len, _ = x.shape
    tm = MERGE_TM
    tok = pl.BlockSpec((1, tm, D_MODEL), lambda b, t: (b, t, 0))
    tiles = pl.BlockSpec((1, N_HEADS, tm // CHUNK, HEAD_DIM, CHUNK), lambda b, t: (b, 0, t, 0, 0))
    return pl.pallas_call(
        _merge_kernel,
        grid=(bsz, t_len // tm),
        in_specs=[tiles, tok, tok, tok, pl.BlockSpec((1, 8, D_MODEL), lambda b, t: (b, 0, 0)),
                  _const_spec(g2.shape), _const_spec(wmo.shape), _const_spec(wo.shape)],
        out_specs=[tok, tok],
        out_shape=[jax.ShapeDtypeStruct(x.shape, F32), jax.ShapeDtypeStruct(x.shape, BF16)],
        compiler_params=pltpu.CompilerParams(dimension_semantics=("parallel", "parallel"),
                                             vmem_limit_bytes=VMEM_LIMIT),
        name="merge",
    )(hgt, gm, zc, x, mod, g2, wmo, wo)


def _gelu_tanh(v):
    return 0.5 * v * (1.0 + jnp.tanh(0.7978845608028654 * (v + 0.044715 * (v * v * v))))


def _ffn_kernel(hm_ref, hp_ref, hn_ref, x1_ref, mod_ref, wu_ref, cw_ref, cb_ref, wd_ref, fg_ref,
                out_ref, act_s, *, tm):
    t = pl.program_id(1)
    nt = pl.num_programs(1)
    n = tm + 2 * GRID_W
    hm = hm_ref[0]
    hp = jnp.where(t > 0, hp_ref[0], jnp.zeros_like(hp_ref[0]))
    hn = jnp.where(t < nt - 1, hn_ref[0], jnp.zeros_like(hn_ref[0]))
    he = jnp.concatenate([hp, hm, hn], axis=0)
    gcol = lax.broadcasted_iota(jnp.int32, (n, 1), 0) & (GRID_W - 1)
    has_left = gcol != 0
    has_right = gcol != GRID_W - 1
    n_chunks = FF_HIDDEN // FFN_CW
    cols = lambda j: slice(j * FFN_CW, (j + 1) * FFN_CW)
    gate_cols = lambda j: slice(FF_HIDDEN + j * FFN_CW, FF_HIDDEN + (j + 1) * FFN_CW)
    up = lambda j: (_dot(he, wu_ref[:, cols(j)]), _dot(hm, wu_ref[:, gate_cols(j)]))
    acc = None
    nxt = up(0)
    for j in range(n_chunks):
        cs = cols(j)
        a, g = nxt
        if j + 1 < n_chunks:
            nxt = up(j + 1)
        taps = (jnp.where(has_left, pltpu.roll(a, 1, 0), 0.0), a,
                jnp.where(has_right, pltpu.roll(a, n - 1, 0), 0.0))
        conv = cb_ref[:, cs]
        for dr in range(3):
            for dc in range(3):
                conv = conv + cw_ref[3 * dr + dc: 3 * dr + dc + 1, cs] * taps[dc][GRID_W * dr: GRID_W * dr + tm]
        act_s[:, cs] = (_gelu_tanh(conv) * g).astype(BF16)
        if (j + 1) % FFN_DOWN_GROUP == 0 or j + 1 == n_chunks:
            gs = slice((j // FFN_DOWN_GROUP) * FFN_DOWN_GROUP * FFN_CW, (j + 1) * FFN_CW)
            part = _dot(act_s[:, gs], wd_ref[gs, :])
            acc = part if acc is None else acc + part
    x2 = x1_ref[0] + mod_ref[0, 5:6, :] * acc
    out_ref[0] = _rms_scale(x2) * fg_ref[...]


def _ffn(h2, x1, mod, wu, cw, cb, wd, fg):
    bsz, t_len, _ = x1.shape
    tm = FFN_TM
    rb = tm // GRID_W
    nrb = t_len // GRID_W
    tok = pl.BlockSpec((1, tm, D_MODEL), lambda b, t: (b, t, 0))
    return pl.pallas_call(
        functools.partial(_ffn_kernel, tm=tm),
        grid=(bsz, t_len // tm),
        in_specs=[tok,
                  pl.BlockSpec((1, GRID_W, D_MODEL), lambda b, t: (b, jnp.maximum(t * rb - 1, 0), 0)),
                  pl.BlockSpec((1, GRID_W, D_MODEL), lambda b, t: (b, jnp.minimum((t + 1) * rb, nrb - 1), 0)),
                  tok, pl.BlockSpec((1, 8, D_MODEL), lambda b, t: (b, 0, 0)),
                  _const_spec(wu.shape), _const_spec(cw.shape), _const_spec(cb.shape),
                  _const_spec(wd.shape), _const_spec(fg.shape)],
        out_specs=tok,
        out_shape=jax.ShapeDtypeStruct(x1.shape, F32),
        scratch_shapes=[pltpu.VMEM((tm, FF_HIDDEN), BF16)],
        compiler_params=pltpu.CompilerParams(dimension_semantics=("parallel", "parallel"),
                                             vmem_limit_bytes=VMEM_LIMIT),
        name="ffn",
    )(h2, h2, h2, x1, mod, wu, cw, cb, wd, fg)


def kernel(x, c, ctx, c_ctx, ada_w, ada_b, norm1_g, norm2_g, w_in, qk_conv_w, qk_conv_b, gate_b, mnorm_g,
           w_m_out, sc_conv_w, sc_conv_b, w_c_out, w_o, w_up, ff_conv_w, ff_conv_b, w_down, final_g):
    assert ada_w.shape[0] == 1, "single-layer block"
    bsz, t_lat, _ = x.shape
    t_ctx = ctx.shape[1]

    cc = jnp.zeros((2 * 8, D_MODEL), F32).at[:bsz].set(c).at[bsz].set(c_ctx)
    mod = _ada(cc, ada_w[0], ada_b[0][None, :])
    mod_x = jnp.pad(mod[:bsz].reshape(bsz, 6, D_MODEL), ((0, 0), (0, 2), (0, 0)))
    mod_c = jnp.pad(mod[bsz].reshape(1, 6, D_MODEL), ((0, 0), (0, 2), (0, 0)))

    cast = lambda a: a.astype(BF16)
    w_t = cast(jnp.swapaxes(w_in, 1, 2)[0])
    gb = gate_b[0][:, None]
    g1 = norm1_g[0][None, :]
    ctx_w = [w_t, gb, qk_conv_w[0], qk_conv_b[0][None, :]]
    lat_w = ctx_w + [sc_conv_w[0], sc_conv_b[0][None, :], cast(w_c_out[0])]

    k_l, vt_l, g_l, qt_l, ogt, gm, zc = _proj(x, mod_x, g1, lat_w, full=True, per_batch_mod=True)
    k_c, vt_c, g_c = _proj(ctx, mod_c, g1, ctx_w, full=False, per_batch_mod=False)

    gf = _gates(g_l, g_c)
    hgt = _mlstm(qt_l, k_l, vt_l, k_c, vt_c, gf, ogt, mnorm_g[0][None, :])

    x1, h2 = _merge(hgt, gm, zc, x, mod_x, norm2_g[0][None, :], cast(w_m_out[0]), cast(w_o[0]))

    out = _ffn(h2, x1, mod_x, cast(w_up[0]),
               ff_conv_w[0].reshape(9, FF_HIDDEN), ff_conv_b[0][None, :], cast(w_down[0]),
               final_g[None, :])
    return out
```

```python
import functools

import jax
import jax.numpy as jnp
from jax import lax
from jax.experimental import pallas as pl
from jax.experimental.pallas import tpu as pltpu

F32 = jnp.float32
BF16 = jnp.bfloat16

D_MODEL = 1024
N_HEADS = 8
HEAD_DIM = D_MODEL // N_HEADS
CHUNK = 128
GRID_W = 64
FF_HIDDEN = 2816
EPS = 1e-6
M_INIT = -1e30
N_GATES = 4 * N_HEADS

LANES = 128
BF16_ROWS = 16
VMEM_LIMIT = 56 * 1024 * 1024

PROJ_TM = 256
STAGE_SLOTS = 4
PROJ_STAGE_ROWS = 272
PROJ_STAGE_ROWS_CTX = 1552
SEQ_HALO = 8
MERGE_TM = 1024
MERGE_SUB = 256
FFN_TM = 512
FFN_CW = 256
FFN_DOWN_GROUP = 4
FFN_STAGE_ROWS_UP = 64
FFN_STAGE_ROWS_DOWN = 128

R_ROWB, R_DECAY = 0, 2
C_BASE, C_PER_DIR = 4, 4
F_COLA, F_AINT, F_ENEG, F_WK = range(4)
N_FIELDS = 16


def _const_spec(shape):
    nd = len(shape)
    return pl.BlockSpec(shape, lambda *_: (0,) * nd, pipeline_mode=pl.Buffered(1))


def _sigmoid(v):
    return 0.5 * jnp.tanh(0.5 * v) + 0.5


def _silu(v):
    h = 0.5 * v
    return h + h * jnp.tanh(h)


def _rms_scale(v):
    return v * lax.rsqrt(jnp.mean(v * v, axis=-1, keepdims=True) + EPS)


def _dot(a, b):
    return jnp.dot(a, b, preferred_element_type=F32)


def _ada_kernel(c_ref, w_ref, b_ref, o_ref):
    cv = c_ref[...]
    s = _silu(cv)
    w = w_ref[...]
    s_hi, w_hi = s.astype(BF16), w.astype(BF16)
    s_lo = (s - s_hi.astype(F32)).astype(BF16)
    w_lo = (w - w_hi.astype(F32)).astype(BF16)
    rows = s.shape[0]
    main = _dot(jnp.concatenate([s_hi, s_lo], axis=0), w_hi)
    o_ref[...] = main[:rows] + main[rows:] + _dot(s_hi, w_lo) + b_ref[...]


def _ada(cc, w, b):
    rows, n = cc.shape[0], w.shape[1]
    tn = 1024
    return pl.pallas_call(
        _ada_kernel,
        grid=(n // tn,),
        in_specs=[pl.BlockSpec((rows, D_MODEL), lambda j: (0, 0)),
                  pl.BlockSpec((D_MODEL, tn), lambda j: (0, j)),
                  pl.BlockSpec((1, tn), lambda j: (0, j))],
        out_specs=pl.BlockSpec((rows, tn), lambda j: (0, j)),
        out_shape=jax.ShapeDtypeStruct((rows, n), F32),
        name="ada",
    )(cc, w, b)


def _conv3_rows(p, w_ref, b_ref, tm):
    main = p[:tm]
    prev_row = p[tm + SEQ_HALO - 1:tm + SEQ_HALO]
    next_row = p[tm + SEQ_HALO:tm + SEQ_HALO + 1]
    sub = lax.broadcasted_iota(jnp.int32, (SEQ_HALO, 1), 0)
    down = pltpu.roll(main, 1, 0)
    up = pltpu.roll(main, tm - 1, 0)
    left = jnp.concatenate([jnp.where(sub == 0, prev_row, down[:SEQ_HALO]), down[SEQ_HALO:]], axis=0)
    right = jnp.concatenate([up[:tm - SEQ_HALO], jnp.where(sub == SEQ_HALO - 1, next_row, up[tm - SEQ_HALO:])],
                            axis=0)
    return left * w_ref[0:1, :] + main * w_ref[1:2, :] + right * w_ref[2:3, :] + b_ref[...]


def _store_head_tiles_t(ref, val, tm):
    for h in range(N_HEADS):
        for ci in range(tm // CHUNK):
            tile = val[ci * CHUNK:(ci + 1) * CHUNK, h * HEAD_DIM:(h + 1) * HEAD_DIM]
            ref[0, h, ci] = tile.T.astype(BF16)


W_QK = slice(0, 2 * D_MODEL)
W_V = slice(2 * D_MODEL, 3 * D_MODEL)
W_G = slice(3 * D_MODEL, 3 * D_MODEL + N_GATES)
W_CTX_ROWS = W_G.stop
W_O = slice(W_G.stop, W_G.stop + D_MODEL)
W_SB = slice(W_O.stop, W_O.stop + D_MODEL)
W_SCX = slice(W_SB.stop, W_SB.stop + 2 * D_MODEL)
W_MG = slice(W_SCX.stop, W_SCX.stop + 2 * D_MODEL)


def _dot_nt(a, b):
    return lax.dot_general(a, b, (((1,), (1,)), ((), ())), preferred_element_type=F32)


def _proj_kernel(*refs, tm, full):
    w_ref, stage_s, sem = refs[-3:]
    if full:
        (xm_ref, xp_ref, xn_ref, mod_ref, g1_ref, w_hbm, gb_ref, qkw_ref, qkb_ref,
         scw_ref, scb_ref, wco_ref,
         k_ref, vt_ref, g_ref, qt_ref, ogt_ref, gm_ref, zc_ref) = refs[:-3]
    else:
        (xm_ref, xp_ref, xn_ref, mod_ref, g1_ref, w_hbm, gb_ref, qkw_ref, qkb_ref,
         k_ref, vt_ref, g_ref) = refs[:-3]
    t = pl.program_id(1)
    nt = pl.num_programs(1)

    @pl.when((pl.program_id(0) == 0) & (t == 0))
    def _():
        rows = stage_s.shape[1]
        _stage_weight(w_hbm, w_ref, stage_s, sem, w_ref.shape[0] // rows,
                      lambda c: (pl.ds(c * rows, rows), slice(None)))

    shift = mod_ref[0, 0:1, :]
    scale1 = 1.0 + mod_ref[0, 1:2, :]
    gain = g1_ref[...]

    def norm_mod(xv):
        return _rms_scale(xv) * gain * scale1 + shift

    hm = norm_mod(xm_ref[0])
    hp = jnp.where(t > 0, norm_mod(xp_ref[0]), 0.0)
    hn = jnp.where(t < nt - 1, norm_mod(xn_ref[0]), 0.0)
    hmb = hm.astype(BF16)
    he = jnp.concatenate([hm, hp, hn], axis=0).astype(BF16)

    pqk = _dot_nt(he, w_ref[W_QK, :])
    pv = _dot_nt(hmb, w_ref[W_V, :])
    qk = _conv3_rows(pqk, qkw_ref, qkb_ref, tm)
    qk = _silu(qk)
    k_ref[0] = (qk[:, D_MODEL:] * (HEAD_DIM ** -0.5)).astype(BF16)
    pg = _dot_nt(w_ref[W_G, :], hmb)
    if not full:
        _store_head_tiles_t(vt_ref, pv, tm)
        g_ref[0] = pg + gb_ref[...]
        return
    po = _dot_nt(hmb, w_ref[W_O, :])
    _store_head_tiles_t(qt_ref, qk[:, :D_MODEL], tm)
    _store_head_tiles_t(vt_ref, pv, tm)
    g_ref[0] = pg + gb_ref[...]
    pcx = _dot_nt(he, w_ref[W_SCX, :])
    _store_head_tiles_t(ogt_ref, _sigmoid(po), tm)
    psb = _dot_nt(hmb, w_ref[W_SB, :])
    cu = _conv3_rows(pcx[:, :D_MODEL] * pcx[:, D_MODEL:], scw_ref, scb_ref, tm)
    pm = _dot_nt(hmb, w_ref[W_MG, :])
    yc = _dot((psb * cu).astype(BF16), wco_ref[...])
    gm_ref[0] = _sigmoid(pm[:, :D_MODEL]).astype(BF16)
    zc_ref[0] = (_sigmoid(pm[:, D_MODEL:]) * yc).astype(BF16)


def _proj(xs, mod, g1, weights, *, full, per_batch_mod):
    bsz, t_len, _ = xs.shape
    tm = min(PROJ_TM, t_len)
    nt = t_len // tm
    hb = tm // SEQ_HALO
    nhb = t_len // SEQ_HALO
    x_specs = [
        pl.BlockSpec((1, tm, D_MODEL), lambda b, t: (b, t, 0)),
        pl.BlockSpec((1, SEQ_HALO, D_MODEL), lambda b, t: (b, jnp.maximum(t * hb - 1, 0), 0)),
        pl.BlockSpec((1, SEQ_HALO, D_MODEL), lambda b, t: (b, jnp.minimum((t + 1) * hb, nhb - 1), 0)),
    ]
    mod_spec = pl.BlockSpec((1, 8, D_MODEL), (lambda b, t: (b, 0, 0)) if per_batch_mod else (lambda b, t: (0, 0, 0)))
    w_rows = weights[0].shape[1] if full else W_CTX_ROWS
    stage_rows, stage_slots = (PROJ_STAGE_ROWS, STAGE_SLOTS) if full else (PROJ_STAGE_ROWS_CTX, 2)
    assert w_rows % stage_rows == 0 and stage_rows % BF16_ROWS == 0
    w_specs = [pl.BlockSpec(memory_space=pl.ANY)] + [_const_spec(w.shape) for w in weights[1:]]
    tok = lambda n, dt: (pl.BlockSpec((1, tm, n), lambda b, t: (b, t, 0)),
                         jax.ShapeDtypeStruct((bsz, t_len, n), dt))
    tiles_t = (pl.BlockSpec((1, N_HEADS, tm // CHUNK, HEAD_DIM, CHUNK), lambda b, t: (b, 0, t, 0, 0)),
               jax.ShapeDtypeStruct((bsz, N_HEADS, t_len // CHUNK, HEAD_DIM, CHUNK), BF16))
    gates_t = (pl.BlockSpec((1, N_GATES, tm), lambda b, t: (b, 0, t)),
               jax.ShapeDtypeStruct((bsz, N_GATES, t_len), F32))
    outs = [tok(D_MODEL, BF16), tiles_t, gates_t]
    if full:
        outs += [tiles_t, tiles_t, tok(D_MODEL, BF16), tok(D_MODEL, BF16)]
    return pl.pallas_call(
        functools.partial(_proj_kernel, tm=tm, full=full),
        grid=(bsz, nt),
        in_specs=x_specs + [mod_spec, _const_spec(g1.shape)] + w_specs,
        out_specs=[o[0] for o in outs],
        out_shape=[o[1] for o in outs],
        scratch_shapes=[pltpu.VMEM((w_rows, D_MODEL), BF16),
                        pltpu.VMEM((stage_slots, stage_rows, D_MODEL), F32),
                        pltpu.SemaphoreType.DMA((stage_slots,))],
        compiler_params=pltpu.CompilerParams(dimension_semantics=("arbitrary", "arbitrary"),
                                             vmem_limit_bytes=VMEM_LIMIT),
        name="proj_full" if full else "proj_ctx",
    )(xs, xs, xs, mod, g1, *weights)


def _scan_lanes(v, combine, fill, forward):
    lane = lax.broadcasted_iota(jnp.int32, v.shape, 1)
    d = 1
    while d < LANES:
        if forward:
            shifted = jnp.where(lane >= d, pltpu.roll(v, d, 1), fill)
        else:
            shifted = jnp.where(lane < LANES - d, pltpu.roll(v, LANES - d, 1), fill)
        v = combine(v, shifted)
        d *= 2
    return v


def _gates_kernel(gl_ref, gc_ref, out_ref, *, n_ctx, n_lat):
    n_all = n_ctx + n_lat
    src = lax.broadcasted_iota(jnp.int32, (CHUNK, 2 * CHUNK), 0)
    dst = lax.broadcasted_iota(jnp.int32, (CHUNK, 2 * CHUNK), 1)

    def gate_rows(cg, first):
        if cg < n_ctx:
            return gc_ref[0, first:first + N_HEADS, cg * CHUNK:(cg + 1) * CHUNK]
        return gl_ref[0, first:first + N_HEADS, (cg - n_ctx) * CHUNK:(cg - n_ctx + 1) * CHUNK]

    for dirn in range(2):
        forward = dirn == 0
        if forward:
            order = list(range(n_all))
        else:
            order = list(range(n_ctx - 1, -1, -1)) + list(range(n_all - 1, n_ctx - 1, -1))
        last = LANES - 1 if forward else 0
        first_row = 2 * N_HEADS * dirn
        fg = jnp.concatenate([gate_rows(cg, first_row + N_HEADS) for cg in range(n_all)], axis=0)
        lf = jnp.minimum(fg, 0.0) - jnp.log1p(jnp.exp(-jnp.abs(fg)))
        prefix = (src <= dst) if forward else (src >= dst)
        sel = jnp.where((dst >= CHUNK) | prefix, 1.0, 0.0).astype(BF16)
        hi = lf.astype(BF16)
        rest = lf - hi.astype(F32)
        mid = rest.astype(BF16)
        low = (rest - mid.astype(F32)).astype(BF16)
        sums = _dot(hi, sel) + _dot(mid, sel) + _dot(low, sel)
        m = jnp.full((N_HEADS, 1), M_INIT, F32)
        for cg in order:
            rows = slice(cg * N_HEADS, (cg + 1) * N_HEADS)
            b = sums[rows, :CHUNK]
            b_end = sums[rows, CHUNK:CHUNK + 1]
            r = gate_rows(cg, first_row) - b
            cm = _scan_lanes(r, jnp.maximum, -jnp.inf, forward)
            mx = jnp.maximum(m, cm)
            mt = b + mx
            mx_end = jnp.maximum(m, cm[:, last:last + 1])
            cbase = C_BASE + C_PER_DIR * dirn
            fields = {
                R_ROWB + dirn: r,
                R_DECAY + dirn: jnp.broadcast_to(jnp.exp(m - mx_end), (N_HEADS, CHUNK)),
                cbase + F_COLA: -mx,
                cbase + F_AINT: jnp.exp(m - mx),
                cbase + F_ENEG: jnp.exp(-mt),
                cbase + F_WK: jnp.exp(r - mx_end),
            }
            m_new = b_end + mx_end
            for f, val in fields.items():
                for h in range(N_HEADS):
                    out_ref[0, h, cg, f:f + 1, :] = val[h:h + 1, :]
            m = m_new
    n_used = C_BASE + 2 * C_PER_DIR
    out_ref[0, :, :, n_used:, :] = jnp.zeros((N_HEADS, n_all, N_FIELDS - n_used, CHUNK), F32)


def _gates(g_lat, g_ctx):
    bsz, _, t_lat = g_lat.shape
    t_ctx = g_ctx.shape[2]
    n_all = (t_lat + t_ctx) // CHUNK
    shape = (bsz, N_HEADS, n_all, N_FIELDS, CHUNK)
    return pl.pallas_call(
        functools.partial(_gates_kernel, n_ctx=t_ctx // CHUNK, n_lat=t_lat // CHUNK),
        grid=(bsz,),
        in_specs=[pl.BlockSpec((1, N_GATES, t_lat), lambda b: (b, 0, 0)),
                  pl.BlockSpec((1, N_GATES, t_ctx), lambda b: (b, 0, 0))],
        out_specs=pl.BlockSpec((1,) + shape[1:], lambda b: (b, 0, 0, 0, 0)),
        out_shape=jax.ShapeDtypeStruct(shape, F32),
        compiler_params=pltpu.CompilerParams(dimension_semantics=("parallel",)),
        name="gates",
    )(g_lat, g_ctx)


STATE_ROWS = HEAD_DIM + BF16_ROWS
MLSTM_HEADS = 2


def _mlstm_kernel(qt_ref, kl_ref, vtl_ref, kc_ref, vtc_ref, gf_ref, ogt_ref, ng_ref, out_ref,
                  dz_s, zf_s, zb_s, z_s, p_s, *, n_ctx, n_lat):
    lane = lax.broadcasted_iota(jnp.int32, (CHUNK, CHUNK), 1)
    row = lax.broadcasted_iota(jnp.int32, (CHUNK, CHUNK), 0)
    tri = (row <= lane, row >= lane)
    tail_first = lax.broadcasted_iota(jnp.int32, (BF16_ROWS, CHUNK), 0) == 0
    ones_tail = jnp.where(tail_first, 1.0, 0.0).astype(BF16)

    for hh in range(MLSTM_HEADS):
        _mlstm_head(hh, qt_ref, kl_ref, vtl_ref, kc_ref, vtc_ref, gf_ref, ogt_ref, ng_ref, out_ref,
                    dz_s, zf_s, zb_s, z_s, p_s, tri, tail_first, ones_tail, n_ctx, n_lat)


def _mlstm_head(hh, qt_ref, kl_ref, vtl_ref, kc_ref, vtc_ref, gf_ref, ogt_ref, ng_ref, out_ref,
                dz_s, zf_s, zb_s, z_s, p_s, tri, tail_first, ones_tail, n_ctx, n_lat):
    hcols = slice(hh * HEAD_DIM, (hh + 1) * HEAD_DIM)

    def field(cg, dirn, f):
        j = C_BASE + C_PER_DIR * dirn + f
        return gf_ref[0, hh, cg, j:j + 1, :]

    def pass_a(cg, k, vt):
        vf = vt.astype(F32)
        parts = []
        for dirn in range(2):
            wk = field(cg, dirn, F_WK)
            parts += [(vf * wk).astype(BF16), jnp.where(tail_first, wk, 0.0).astype(BF16)]
        dz_s[cg] = _dot(jnp.concatenate(parts, axis=0), k)

    for cg in range(n_ctx):
        pass_a(cg, kc_ref[0, cg * CHUNK:(cg + 1) * CHUNK, hcols], vtc_ref[0, hh, cg])

    def pass_a_lat(i, carry):
        sl = pl.ds(pl.multiple_of(i * CHUNK, CHUNK), CHUNK)
        pass_a(n_ctx + i, kl_ref[0, sl, hcols], vtl_ref[0, hh, i])
        return carry

    lax.fori_loop(0, n_lat, pass_a_lat, 0, unroll=8)

    def advance(dirn, cg):
        dec = gf_ref[0, hh, cg, R_DECAY + dirn:R_DECAY + dirn + 1, :]
        z_s[dirn] = dec * z_s[dirn] + dz_s[cg, dirn * STATE_ROWS:(dirn + 1) * STATE_ROWS, :]

    z_s[...] = jnp.zeros_like(z_s)
    for cg in range(n_ctx):
        advance(0, cg)
    for cg in range(n_ctx - 1, -1, -1):
        advance(1, cg)

    def pass_b(i, carry):
        zf_s[i] = z_s[0].astype(BF16)
        advance(0, n_ctx + i)
        j = n_lat - 1 - i
        zb_s[j] = z_s[1].astype(BF16)
        advance(1, n_ctx + j)
        return carry

    lax.fori_loop(0, n_lat, pass_b, 0, unroll=2)

    gain = jnp.broadcast_to(ng_ref[:, hcols], (CHUNK, HEAD_DIM)).T

    def pass_c1(i, carry):
        cg = n_ctx + i
        sl = pl.ds(pl.multiple_of(i * CHUNK, CHUNK), CHUNK)
        st = _dot(kl_ref[0, sl, hcols], qt_ref[0, hh, i])
        rowb = gf_ref[0, hh, cg].T
        for dirn in range(2):
            dmat = rowb[:, R_ROWB + dirn:R_ROWB + dirn + 1] + field(cg, dirn, F_COLA)
            w = jnp.exp(jnp.where(tri[dirn], dmat, -jnp.inf))
            p_s[i, dirn] = (st * w).astype(BF16)
        return carry

    lax.fori_loop(0, n_lat, pass_c1, 0, unroll=16)

    def pass_c2(i, carry):
        cg = n_ctx + i
        qtf = qt_ref[0, hh, i].astype(F32)
        vaug = jnp.concatenate([vtl_ref[0, hh, i], ones_tail], axis=0)
        ht = None
        for dirn, z_ref in enumerate((zf_s, zb_s)):
            rhs = jnp.concatenate([(qtf * field(cg, dirn, F_AINT)).astype(BF16), p_s[i, dirn]], axis=0)
            res = _dot(jnp.concatenate([z_ref[i], vaug], axis=1), rhs)
            den = jnp.maximum(jnp.abs(res[HEAD_DIM:HEAD_DIM + 1, :]), field(cg, dirn, F_ENEG))
            hd = res[:HEAD_DIM, :] / den
            ht = hd if ht is None else ht + hd
        ht = ht * lax.rsqrt(jnp.mean(ht * ht, axis=0, keepdims=True) + EPS)
        out_ref[0, hh, i] = (ht * gain * ogt_ref[0, hh, i].astype(F32)).astype(BF16)
        return carry

    lax.fori_loop(0, n_lat, pass_c2, 0, unroll=16)


def _mlstm(qt_l, k_l, vt_l, k_c, vt_c, gf, ogt, ng):
    bsz, t_lat, _ = k_l.shape
    t_ctx = k_c.shape[1]
    n_lat, n_ctx = t_lat // CHUNK, t_ctx // CHUNK
    n_all = n_lat + n_ctx
    hb = MLSTM_HEADS
    head_lat = pl.BlockSpec((1, t_lat, hb * HEAD_DIM), lambda b, h: (b, 0, h))
    head_ctx = pl.BlockSpec((1, t_ctx, hb * HEAD_DIM), lambda b, h: (b, 0, h))
    tiles = lambda n: pl.BlockSpec((1, hb, n, HEAD_DIM, CHUNK), lambda b, h: (b, h, 0, 0, 0))
    return pl.pallas_call(
        functools.partial(_mlstm_kernel, n_ctx=n_ctx, n_lat=n_lat),
        grid=(bsz, N_HEADS // hb),
        in_specs=[tiles(n_lat), head_lat, tiles(n_lat), head_ctx, tiles(n_ctx),
                  pl.BlockSpec((1, hb, n_all, N_FIELDS, CHUNK), lambda b, h: (b, h, 0, 0, 0)),
                  tiles(n_lat),
                  pl.BlockSpec((1, hb * HEAD_DIM), lambda b, h: (0, h))],
        out_specs=tiles(n_lat),
        out_shape=jax.ShapeDtypeStruct(qt_l.shape, BF16),
        scratch_shapes=[pltpu.VMEM((n_all, 2 * STATE_ROWS, HEAD_DIM), F32),
                        pltpu.VMEM((n_lat, STATE_ROWS, HEAD_DIM), BF16),
                        pltpu.VMEM((n_lat, STATE_ROWS, HEAD_DIM), BF16),
                        pltpu.VMEM((2, STATE_ROWS, HEAD_DIM), F32),
                        pltpu.VMEM((n_lat, 2, CHUNK, CHUNK), BF16)],
        compiler_params=pltpu.CompilerParams(dimension_semantics=("parallel", "parallel"),
                                             vmem_limit_bytes=VMEM_LIMIT),
        name="mlstm",
    )(qt_l, k_l, vt_l, k_c, vt_c, gf, ogt, ng)


def _merge_kernel(hgt_ref, gm_ref, zc_ref, x_ref, mod_ref, g2_ref, wmo_ref, wo_ref, x1_ref, h2_ref):
    per_sub = MERGE_SUB // CHUNK
    subs = [slice(i * MERGE_SUB, (i + 1) * MERGE_SUB) for i in range(hgt_ref.shape[2] // per_sub)]
    hg = [jnp.concatenate(
        [jnp.concatenate([hgt_ref[0, h, ci].astype(F32).T for h in range(N_HEADS)], axis=1)
         for ci in range(i * per_sub, (i + 1) * per_sub)], axis=0).astype(BF16) for i in range(len(subs))]
    ym = [_dot(hg_i, wmo_ref[...]) for hg_i in hg]
    y = [_dot((gm_ref[0, sl, :].astype(F32) * ym_i + zc_ref[0, sl, :].astype(F32)).astype(BF16), wo_ref[...])
         for sl, ym_i in zip(subs, ym)]
    for sl, y_i in zip(subs, y):
        x1 = x_ref[0, sl, :] + mod_ref[0, 2:3, :] * y_i
        x1_ref[0, sl, :] = x1
        h2 = _rms_scale(x1) * g2_ref[...] * (1.0 + mod_ref[0, 4:5, :]) + mod_ref[0, 3:4, :]
        h2_ref[0, sl, :] = h2.astype(BF16)


def _merge(hgt, gm, zc, x, mod, g2, wmo, wo):
    bsz, t_len, _ = x.shape
    tm = MERGE_TM
    tok = pl.BlockSpec((1, tm, D_MODEL), lambda b, t: (b, t, 0))
    tiles = pl.BlockSpec((1, N_HEADS, tm // CHUNK, HEAD_DIM, CHUNK), lambda b, t: (b, 0, t, 0, 0))
    return pl.pallas_call(
        _merge_kernel,
        grid=(bsz, t_len // tm),
        in_specs=[tiles, tok, tok, tok, pl.BlockSpec((1, 8, D_MODEL), lambda b, t: (b, 0, 0)),
                  _const_spec(g2.shape), _const_spec(wmo.shape), _const_spec(wo.shape)],
        out_specs=[tok, tok],
        out_shape=[jax.ShapeDtypeStruct(x.shape, F32), jax.ShapeDtypeStruct(x.shape, BF16)],
        compiler_params=pltpu.CompilerParams(dimension_semantics=("parallel", "parallel"),
                                             vmem_limit_bytes=VMEM_LIMIT),
        name="merge",
    )(hgt, gm, zc, x, mod, g2, wmo, wo)


def _gelu_tanh(v):
    return 0.5 * v * (1.0 + jnp.tanh(0.7978845608028654 * (v + 0.044715 * (v * v * v))))


def _stage_weight(src_hbm, dst_s, stage_s, sem, n_parts, part):
    slots = stage_s.shape[0]

    def copy(c):
        return pltpu.make_async_copy(src_hbm.at[(0,) + part(c)], stage_s.at[c % slots], sem.at[c % slots])

    for c in range(min(slots, n_parts)):
        copy(c).start()
    for c in range(n_parts):
        copy(c).wait()
        dst_s[part(c)] = stage_s[c % slots].astype(BF16)
        if c + slots < n_parts:
            copy(c + slots).start()


def _ffn_kernel(hm_ref, hp_ref, hn_ref, x1_ref, mod_ref, wu_hbm, cw_ref, cb_ref, wd_hbm, fg_ref,
                out_ref, act_s, wu_ref, wd_ref, stage_u, stage_d, sem_u, sem_d, *, tm):
    t = pl.program_id(1)
    nt = pl.num_programs(1)

    @pl.when((pl.program_id(0) == 0) & (t == 0))
    def _():
        for src, dst, stage, sem in ((wu_hbm, wu_ref, stage_u, sem_u), (wd_hbm, wd_ref, stage_d, sem_d)):
            rows = stage.shape[1]
            _stage_weight(src, dst, stage, sem, dst.shape[0] // rows,
                          lambda c, rows=rows: (pl.ds(c * rows, rows), slice(None)))

    n = tm + 2 * GRID_W
    hm = hm_ref[0]
    hp = jnp.where(t > 0, hp_ref[0], jnp.zeros_like(hp_ref[0]))
    hn = jnp.where(t < nt - 1, hn_ref[0], jnp.zeros_like(hn_ref[0]))
    he = jnp.concatenate([hp, hm, hn], axis=0)
    gcol = lax.broadcasted_iota(jnp.int32, (n, 1), 0) & (GRID_W - 1)
    has_left = gcol != 0
    has_right = gcol != GRID_W - 1
    n_chunks = FF_HIDDEN // FFN_CW
    cols = lambda j: slice(j * FFN_CW, (j + 1) * FFN_CW)
    gate_cols = lambda j: slice(FF_HIDDEN + j * FFN_CW, FF_HIDDEN + (j + 1) * FFN_CW)
    up = lambda j: (_dot(he, wu_ref[:, cols(j)]), _dot(hm, wu_ref[:, gate_cols(j)]))
    acc = None
    nxt = up(0)
    for j in range(n_chunks):
        cs = cols(j)
        a, g = nxt
        if j + 1 < n_chunks:
            nxt = up(j + 1)
        taps = (jnp.where(has_left, pltpu.roll(a, 1, 0), 0.0), a,
                jnp.where(has_right, pltpu.roll(a, n - 1, 0), 0.0))
        conv = cb_ref[:, cs]
        for dr in range(3):
            for dc in range(3):
                conv = conv + cw_ref[3 * dr + dc: 3 * dr + dc + 1, cs] * taps[dc][GRID_W * dr: GRID_W * dr + tm]
        act_s[:, cs] = (_gelu_tanh(conv) * g).astype(BF16)
        if (j + 1) % FFN_DOWN_GROUP == 0 or j + 1 == n_chunks:
            gs = slice((j // FFN_DOWN_GROUP) * FFN_DOWN_GROUP * FFN_CW, (j + 1) * FFN_CW)
            part = _dot(act_s[:, gs], wd_ref[gs, :])
            acc = part if acc is None else acc + part
    x2 = x1_ref[0] + mod_ref[0, 5:6, :] * acc
    out_ref[0] = _rms_scale(x2) * fg_ref[...]


def _ffn(h2, x1, mod, wu, cw, cb, wd, fg):
    bsz, t_len, _ = x1.shape
    tm = FFN_TM
    rb = tm // GRID_W
    nrb = t_len // GRID_W
    tok = pl.BlockSpec((1, tm, D_MODEL), lambda b, t: (b, t, 0))
    return pl.pallas_call(
        functools.partial(_ffn_kernel, tm=tm),
        grid=(bsz, t_len // tm),
        in_specs=[tok,
                  pl.BlockSpec((1, GRID_W, D_MODEL), lambda b, t: (b, jnp.maximum(t * rb - 1, 0), 0)),
                  pl.BlockSpec((1, GRID_W, D_MODEL), lambda b, t: (b, jnp.minimum((t + 1) * rb, nrb - 1), 0)),
                  tok, pl.BlockSpec((1, 8, D_MODEL), lambda b, t: (b, 0, 0)),
                  pl.BlockSpec(memory_space=pl.ANY), _const_spec(cw.shape), _const_spec(cb.shape),
                  pl.BlockSpec(memory_space=pl.ANY), _const_spec(fg.shape)],
        out_specs=tok,
        out_shape=jax.ShapeDtypeStruct(x1.shape, F32),
        scratch_shapes=[pltpu.VMEM((tm, FF_HIDDEN), BF16),
                        pltpu.VMEM(wu.shape[1:], BF16),
                        pltpu.VMEM(wd.shape[1:], BF16),
                        pltpu.VMEM((STAGE_SLOTS, FFN_STAGE_ROWS_UP, wu.shape[2]), F32),
                        pltpu.VMEM((STAGE_SLOTS, FFN_STAGE_ROWS_DOWN, wd.shape[2]), F32),
                        pltpu.SemaphoreType.DMA((STAGE_SLOTS,)), pltpu.SemaphoreType.DMA((STAGE_SLOTS,))],
        compiler_params=pltpu.CompilerParams(dimension_semantics=("arbitrary", "arbitrary"),
                                             vmem_limit_bytes=VMEM_LIMIT),
        name="ffn",
    )(h2, h2, h2, x1, mod, wu, cw, cb, wd, fg)


def kernel(x, c, ctx, c_ctx, ada_w, ada_b, norm1_g, norm2_g, w_in, qk_conv_w, qk_conv_b, gate_b, mnorm_g,
           w_m_out, sc_conv_w, sc_conv_b, w_c_out, w_o, w_up, ff_conv_w, ff_conv_b, w_down, final_g):
    assert ada_w.shape[0] == 1, "single-layer block"
    bsz, t_lat, _ = x.shape
    t_ctx = ctx.shape[1]

    cc = jnp.zeros((2 * 8, D_MODEL), F32).at[:bsz].set(c).at[bsz].set(c_ctx)
    mod = _ada(cc, ada_w[0], ada_b[0][None, :])
    mod_x = jnp.pad(mod[:bsz].reshape(bsz, 6, D_MODEL), ((0, 0), (0, 2), (0, 0)))
    mod_c = jnp.pad(mod[bsz].reshape(1, 6, D_MODEL), ((0, 0), (0, 2), (0, 0)))

    cast = lambda a: a.astype(BF16)
    w_t = jnp.swapaxes(w_in, 1, 2)
    gb = gate_b[0][:, None]
    g1 = norm1_g[0][None, :]
    ctx_w = [w_t, gb, qk_conv_w[0], qk_conv_b[0][None, :]]
    lat_w = ctx_w + [sc_conv_w[0], sc_conv_b[0][None, :], cast(w_c_out[0])]

    k_l, vt_l, g_l, qt_l, ogt, gm, zc = _proj(x, mod_x, g1, lat_w, full=True, per_batch_mod=True)
    k_c, vt_c, g_c = _proj(ctx, mod_c, g1, ctx_w, full=False, per_batch_mod=False)

    gf = _gates(g_l, g_c)
    hgt = _mlstm(qt_l, k_l, vt_l, k_c, vt_c, gf, ogt, mnorm_g[0][None, :])

    x1, h2 = _merge(hgt, gm, zc, x, mod_x, norm2_g[0][None, :], cast(w_m_out[0]), cast(w_o[0]))

    out = _ffn(h2, x1, mod_x, w_up,
               ff_conv_w[0].reshape(9, FF_HIDDEN), ff_conv_b[0][None, :], w_down,
               final_g[None, :])
    return out
```

```python
import functools

import jax
import jax.numpy as jnp
from jax import lax
from jax.experimental import pallas as pl
from jax.experimental.pallas import tpu as pltpu

F32 = jnp.float32
BF16 = jnp.bfloat16

D_MODEL = 1024
N_HEADS = 8
HEAD_DIM = D_MODEL // N_HEADS
CHUNK = 128
GRID_W = 64
FF_HIDDEN = 2816
EPS = 1e-6
M_INIT = -1e30
N_GATES = 4 * N_HEADS

LANES = 128
BF16_ROWS = 16
VMEM_LIMIT = 56 * 1024 * 1024

PROJ_TM = 256
STAGE_SLOTS = 8
PROJ_STAGE_ROWS = 272
PROJ_STAGE_ROWS_CTX = 1552
SEQ_HALO = 8
MERGE_TM = 1024
MERGE_SUB = 256
FFN_TM = 512
FFN_CW = 256
FFN_DOWN_GROUP = 4
FFN_STAGE_ROWS_UP = 64
FFN_STAGE_ROWS_DOWN = 128

R_ROWB, R_DECAY = 0, 2
C_BASE, C_PER_DIR = 4, 4
F_COLA, F_AINT, F_ENEG, F_WK = range(4)
N_FIELDS = 16


def _const_spec(shape):
    nd = len(shape)
    return pl.BlockSpec(shape, lambda *_: (0,) * nd, pipeline_mode=pl.Buffered(1))


def _sigmoid(v):
    return 0.5 * jnp.tanh(0.5 * v) + 0.5


def _silu(v):
    h = 0.5 * v
    return h + h * jnp.tanh(h)


def _rms_scale(v):
    return v * lax.rsqrt(jnp.mean(v * v, axis=-1, keepdims=True) + EPS)


def _dot(a, b):
    return jnp.dot(a, b, preferred_element_type=F32)


def _ada_kernel(c_ref, w_ref, b_ref, o_ref):
    cv = c_ref[...]
    s = _silu(cv)
    w = w_ref[...]
    s_hi, w_hi = s.astype(BF16), w.astype(BF16)
    s_lo = (s - s_hi.astype(F32)).astype(BF16)
    w_lo = (w - w_hi.astype(F32)).astype(BF16)
    rows = s.shape[0]
    main = _dot(jnp.concatenate([s_hi, s_lo], axis=0), w_hi)
    o_ref[...] = main[:rows] + main[rows:] + _dot(s_hi, w_lo) + b_ref[...]


def _ada(cc, w, b):
    rows, n = cc.shape[0], w.shape[1]
    tn = 1024
    return pl.pallas_call(
        _ada_kernel,
        grid=(n // tn,),
        in_specs=[pl.BlockSpec((rows, D_MODEL), lambda j: (0, 0)),
                  pl.BlockSpec((D_MODEL, tn), lambda j: (0, j)),
                  pl.BlockSpec((1, tn), lambda j: (0, j))],
        out_specs=pl.BlockSpec((rows, tn), lambda j: (0, j)),
        out_shape=jax.ShapeDtypeStruct((rows, n), F32),
        name="ada",
    )(cc, w, b)


def _conv3_rows(p, w_ref, b_ref, tm):
    main = p[:tm]
    prev_row = p[tm + SEQ_HALO - 1:tm + SEQ_HALO]
    next_row = p[tm + SEQ_HALO:tm + SEQ_HALO + 1]
    sub = lax.broadcasted_iota(jnp.int32, (SEQ_HALO, 1), 0)
    down = pltpu.roll(main, 1, 0)
    up = pltpu.roll(main, tm - 1, 0)
    left = jnp.concatenate([jnp.where(sub == 0, prev_row, down[:SEQ_HALO]), down[SEQ_HALO:]], axis=0)
    right = jnp.concatenate([up[:tm - SEQ_HALO], jnp.where(sub == SEQ_HALO - 1, next_row, up[tm - SEQ_HALO:])],
                            axis=0)
    return left * w_ref[0:1, :] + main * w_ref[1:2, :] + right * w_ref[2:3, :] + b_ref[...]


def _store_head_tiles_t(ref, val, tm):
    for h in range(N_HEADS):
        for ci in range(tm // CHUNK):
            tile = val[ci * CHUNK:(ci + 1) * CHUNK, h * HEAD_DIM:(h + 1) * HEAD_DIM]
            ref[0, h, ci] = tile.T.astype(BF16)


W_QK = slice(0, 2 * D_MODEL)
W_V = slice(2 * D_MODEL, 3 * D_MODEL)
W_G = slice(3 * D_MODEL, 3 * D_MODEL + N_GATES)
W_CTX_ROWS = W_G.stop
W_O = slice(W_G.stop, W_G.stop + D_MODEL)
W_SB = slice(W_O.stop, W_O.stop + D_MODEL)
W_SCX = slice(W_SB.stop, W_SB.stop + 2 * D_MODEL)
W_MG = slice(W_SCX.stop, W_SCX.stop + 2 * D_MODEL)


def _dot_nt(a, b):
    return lax.dot_general(a, b, (((1,), (1,)), ((), ())), preferred_element_type=F32)


def _proj_kernel(*refs, tm, full):
    w_ref, stage_s, sem = refs[-3:]
    if full:
        (xm_ref, xp_ref, xn_ref, mod_ref, g1_ref, w_hbm, gb_ref, qkw_ref, qkb_ref,
         scw_ref, scb_ref, wco_ref,
         k_ref, vt_ref, g_ref, qt_ref, ogt_ref, gm_ref, zc_ref) = refs[:-3]
    else:
        (xm_ref, xp_ref, xn_ref, mod_ref, g1_ref, w_hbm, gb_ref, qkw_ref, qkb_ref,
         k_ref, vt_ref, g_ref) = refs[:-3]
    t = pl.program_id(1)
    nt = pl.num_programs(1)

    @pl.when((pl.program_id(0) == 0) & (t == 0))
    def _():
        rows = stage_s.shape[1]
        _stage_weight(w_hbm, w_ref, stage_s, sem, w_ref.shape[0] // rows,
                      lambda c: (pl.ds(c * rows, rows), slice(None)))

    shift = mod_ref[0, 0:1, :]
    scale1 = 1.0 + mod_ref[0, 1:2, :]
    gain = g1_ref[...]

    def norm_mod(xv):
        return _rms_scale(xv) * gain * scale1 + shift

    hm = norm_mod(xm_ref[0])
    hp = jnp.where(t > 0, norm_mod(xp_ref[0]), 0.0)
    hn = jnp.where(t < nt - 1, norm_mod(xn_ref[0]), 0.0)
    hmb = hm.astype(BF16)
    he = jnp.concatenate([hm, hp, hn], axis=0).astype(BF16)

    pqk = _dot_nt(he, w_ref[W_QK, :])
    pv = _dot_nt(hmb, w_ref[W_V, :])
    qk = _conv3_rows(pqk, qkw_ref, qkb_ref, tm)
    qk = _silu(qk)
    k_ref[0] = (qk[:, D_MODEL:] * (HEAD_DIM ** -0.5)).astype(BF16)
    pg = _dot_nt(w_ref[W_G, :], hmb)
    if not full:
        _store_head_tiles_t(vt_ref, pv, tm)
        g_ref[0] = pg + gb_ref[...]
        return
    po = _dot_nt(hmb, w_ref[W_O, :])
    _store_head_tiles_t(qt_ref, qk[:, :D_MODEL], tm)
    _store_head_tiles_t(vt_ref, pv, tm)
    g_ref[0] = pg + gb_ref[...]
    pcx = _dot_nt(he, w_ref[W_SCX, :])
    _store_head_tiles_t(ogt_ref, _sigmoid(po), tm)
    psb = _dot_nt(hmb, w_ref[W_SB, :])
    cu = _conv3_rows(pcx[:, :D_MODEL] * pcx[:, D_MODEL:], scw_ref, scb_ref, tm)
    pm = _dot_nt(hmb, w_ref[W_MG, :])
    yc = _dot((psb * cu).astype(BF16), wco_ref[...])
    gm_ref[0] = _sigmoid(pm[:, :D_MODEL]).astype(BF16)
    zc_ref[0] = (_sigmoid(pm[:, D_MODEL:]) * yc).astype(BF16)


def _proj(xs, mod, g1, weights, *, full, per_batch_mod):
    bsz, t_len, _ = xs.shape
    tm = min(PROJ_TM, t_len)
    nt = t_len // tm
    hb = tm // SEQ_HALO
    nhb = t_len // SEQ_HALO
    x_specs = [
        pl.BlockSpec((1, tm, D_MODEL), lambda b, t: (b, t, 0)),
        pl.BlockSpec((1, SEQ_HALO, D_MODEL), lambda b, t: (b, jnp.maximum(t * hb - 1, 0), 0)),
        pl.BlockSpec((1, SEQ_HALO, D_MODEL), lambda b, t: (b, jnp.minimum((t + 1) * hb, nhb - 1), 0)),
    ]
    mod_spec = pl.BlockSpec((1, 8, D_MODEL), (lambda b, t: (b, 0, 0)) if per_batch_mod else (lambda b, t: (0, 0, 0)))
    w_rows = weights[0].shape[1] if full else W_CTX_ROWS
    stage_rows, stage_slots = (PROJ_STAGE_ROWS, STAGE_SLOTS) if full else (PROJ_STAGE_ROWS_CTX, 2)
    assert w_rows % stage_rows == 0 and stage_rows % BF16_ROWS == 0
    w_specs = [pl.BlockSpec(memory_space=pl.ANY)] + [_const_spec(w.shape) for w in weights[1:]]
    tok = lambda n, dt: (pl.BlockSpec((1, tm, n), lambda b, t: (b, t, 0)),
                         jax.ShapeDtypeStruct((bsz, t_len, n), dt))
    tiles_t = (pl.BlockSpec((1, N_HEADS, tm // CHUNK, HEAD_DIM, CHUNK), lambda b, t: (b, 0, t, 0, 0)),
               jax.ShapeDtypeStruct((bsz, N_HEADS, t_len // CHUNK, HEAD_DIM, CHUNK), BF16))
    gates_t = (pl.BlockSpec((1, N_GATES, tm), lambda b, t: (b, 0, t)),
               jax.ShapeDtypeStruct((bsz, N_GATES, t_len), F32))
    outs = [tok(D_MODEL, BF16), tiles_t, gates_t]
    if full:
        outs += [tiles_t, tiles_t, tok(D_MODEL, BF16), tok(D_MODEL, BF16)]
    return pl.pallas_call(
        functools.partial(_proj_kernel, tm=tm, full=full),
        grid=(bsz, nt),
        in_specs=x_specs + [mod_spec, _const_spec(g1.shape)] + w_specs,
        out_specs=[o[0] for o in outs],
        out_shape=[o[1] for o in outs],
        scratch_shapes=[pltpu.VMEM((w_rows, D_MODEL), BF16),
                        pltpu.VMEM((stage_slots, stage_rows, D_MODEL), F32),
                        pltpu.SemaphoreType.DMA((stage_slots,))],
        compiler_params=pltpu.CompilerParams(dimension_semantics=("arbitrary", "arbitrary"),
                                             vmem_limit_bytes=VMEM_LIMIT),
        name="proj_full" if full else "proj_ctx",
    )(xs, xs, xs, mod, g1, *weights)


def _scan_lanes(v, combine, fill, forward):
    lane = lax.broadcasted_iota(jnp.int32, v.shape, 1)
    d = 1
    while d < LANES:
        if forward:
            shifted = jnp.where(lane >= d, pltpu.roll(v, d, 1), fill)
        else:
            shifted = jnp.where(lane < LANES - d, pltpu.roll(v, LANES - d, 1), fill)
        v = combine(v, shifted)
        d *= 2
    return v


def _gates_kernel(gl_ref, gc_ref, out_ref, *, n_ctx, n_lat):
    n_all = n_ctx + n_lat
    src = lax.broadcasted_iota(jnp.int32, (CHUNK, 2 * CHUNK), 0)
    dst = lax.broadcasted_iota(jnp.int32, (CHUNK, 2 * CHUNK), 1)

    def gate_rows(cg, first):
        if cg < n_ctx:
            return gc_ref[0, first:first + N_HEADS, cg * CHUNK:(cg + 1) * CHUNK]
        return gl_ref[0, first:first + N_HEADS, (cg - n_ctx) * CHUNK:(cg - n_ctx + 1) * CHUNK]

    for dirn in range(2):
        forward = dirn == 0
        if forward:
            order = list(range(n_all))
        else:
            order = list(range(n_ctx - 1, -1, -1)) + list(range(n_all - 1, n_ctx - 1, -1))
        last = LANES - 1 if forward else 0
        first_row = 2 * N_HEADS * dirn
        fg = jnp.concatenate([gate_rows(cg, first_row + N_HEADS) for cg in range(n_all)], axis=0)
        lf = jnp.minimum(fg, 0.0) - jnp.log1p(jnp.exp(-jnp.abs(fg)))
        prefix = (src <= dst) if forward else (src >= dst)
        sel = jnp.where((dst >= CHUNK) | prefix, 1.0, 0.0).astype(BF16)
        hi = lf.astype(BF16)
        rest = lf - hi.astype(F32)
        mid = rest.astype(BF16)
        low = (rest - mid.astype(F32)).astype(BF16)
        sums = _dot(hi, sel) + _dot(mid, sel) + _dot(low, sel)
        m = jnp.full((N_HEADS, 1), M_INIT, F32)
        for cg in order:
            rows = slice(cg * N_HEADS, (cg + 1) * N_HEADS)
            b = sums[rows, :CHUNK]
            b_end = sums[rows, CHUNK:CHUNK + 1]
            r = gate_rows(cg, first_row) - b
            cm = _scan_lanes(r, jnp.maximum, -jnp.inf, forward)
            mx = jnp.maximum(m, cm)
            mt = b + mx
            mx_end = jnp.maximum(m, cm[:, last:last + 1])
            cbase = C_BASE + C_PER_DIR * dirn
            fields = {
                R_ROWB + dirn: r,
                R_DECAY + dirn: jnp.broadcast_to(jnp.exp(m - mx_end), (N_HEADS, CHUNK)),
                cbase + F_COLA: -mx,
                cbase + F_AINT: jnp.exp(m - mx),
                cbase + F_ENEG: jnp.exp(-mt),
                cbase + F_WK: jnp.exp(r - mx_end),
            }
            m_new = b_end + mx_end
            for f, val in fields.items():
                for h in range(N_HEADS):
                    out_ref[0, h, cg, f:f + 1, :] = val[h:h + 1, :]
            m = m_new
    n_used = C_BASE + 2 * C_PER_DIR
    out_ref[0, :, :, n_used:, :] = jnp.zeros((N_HEADS, n_all, N_FIELDS - n_used, CHUNK), F32)


def _gates(g_lat, g_ctx):
    bsz, _, t_lat = g_lat.shape
    t_ctx = g_ctx.shape[2]
    n_all = (t_lat + t_ctx) // CHUNK
    shape = (bsz, N_HEADS, n_all, N_FIELDS, CHUNK)
    return pl.pallas_call(
        functools.partial(_gates_kernel, n_ctx=t_ctx // CHUNK, n_lat=t_lat // CHUNK),
        grid=(bsz,),
        in_specs=[pl.BlockSpec((1, N_GATES, t_lat), lambda b: (b, 0, 0)),
                  pl.BlockSpec((1, N_GATES, t_ctx), lambda b: (b, 0, 0))],
        out_specs=pl.BlockSpec((1,) + shape[1:], lambda b: (b, 0, 0, 0, 0)),
        out_shape=jax.ShapeDtypeStruct(shape, F32),
        compiler_params=pltpu.CompilerParams(dimension_semantics=("parallel",)),
        name="gates",
    )(g_lat, g_ctx)


STATE_ROWS = HEAD_DIM + BF16_ROWS
MLSTM_HEADS = 2


def _mlstm_kernel(qt_ref, kl_ref, vtl_ref, kc_ref, vtc_ref, gf_ref, ogt_ref, ng_ref, out_ref,
                  dz_s, zf_s, zb_s, z_s, p_s, *, n_ctx, n_lat):
    lane = lax.broadcasted_iota(jnp.int32, (CHUNK, CHUNK), 1)
    row = lax.broadcasted_iota(jnp.int32, (CHUNK, CHUNK), 0)
    tri = (row <= lane, row >= lane)
    tail_first = lax.broadcasted_iota(jnp.int32, (BF16_ROWS, CHUNK), 0) == 0
    ones_tail = jnp.where(tail_first, 1.0, 0.0).astype(BF16)

    for hh in range(MLSTM_HEADS):
        _mlstm_head(hh, qt_ref, kl_ref, vtl_ref, kc_ref, vtc_ref, gf_ref, ogt_ref, ng_ref, out_ref,
                    dz_s, zf_s, zb_s, z_s, p_s, tri, tail_first, ones_tail, n_ctx, n_lat)


def _mlstm_head(hh, qt_ref, kl_ref, vtl_ref, kc_ref, vtc_ref, gf_ref, ogt_ref, ng_ref, out_ref,
                dz_s, zf_s, zb_s, z_s, p_s, tri, tail_first, ones_tail, n_ctx, n_lat):
    hcols = slice(hh * HEAD_DIM, (hh + 1) * HEAD_DIM)

    def field(cg, dirn, f):
        j = C_BASE + C_PER_DIR * dirn + f
        return gf_ref[0, hh, cg, j:j + 1, :]

    def pass_a(cg, k, vt):
        vf = vt.astype(F32)
        parts = []
        for dirn in range(2):
            wk = field(cg, dirn, F_WK)
            parts += [(vf * wk).astype(BF16), jnp.where(tail_first, wk, 0.0).astype(BF16)]
        dz_s[cg] = _dot(jnp.concatenate(parts, axis=0), k)

    for cg in range(n_ctx):
        pass_a(cg, kc_ref[0, cg * CHUNK:(cg + 1) * CHUNK, hcols], vtc_ref[0, hh, cg])

    def pass_a_lat(i, carry):
        sl = pl.ds(pl.multiple_of(i * CHUNK, CHUNK), CHUNK)
        pass_a(n_ctx + i, kl_ref[0, sl, hcols], vtl_ref[0, hh, i])
        return carry

    lax.fori_loop(0, n_lat, pass_a_lat, 0, unroll=8)

    def advance(dirn, cg):
        dec = gf_ref[0, hh, cg, R_DECAY + dirn:R_DECAY + dirn + 1, :]
        z_s[dirn] = dec * z_s[dirn] + dz_s[cg, dirn * STATE_ROWS:(dirn + 1) * STATE_ROWS, :]

    z_s[...] = jnp.zeros_like(z_s)
    for cg in range(n_ctx):
        advance(0, cg)
    for cg in range(n_ctx - 1, -1, -1):
        advance(1, cg)

    def pass_b(i, carry):
        zf_s[i] = z_s[0].astype(BF16)
        advance(0, n_ctx + i)
        j = n_lat - 1 - i
        zb_s[j] = z_s[1].astype(BF16)
        advance(1, n_ctx + j)
        return carry

    lax.fori_loop(0, n_lat, pass_b, 0, unroll=2)

    gain = jnp.broadcast_to(ng_ref[:, hcols], (CHUNK, HEAD_DIM)).T

    def pass_c1(i, carry):
        cg = n_ctx + i
        sl = pl.ds(pl.multiple_of(i * CHUNK, CHUNK), CHUNK)
        st = _dot(kl_ref[0, sl, hcols], qt_ref[0, hh, i])
        rowb = gf_ref[0, hh, cg].T
        for dirn in range(2):
            dmat = rowb[:, R_ROWB + dirn:R_ROWB + dirn + 1] + field(cg, dirn, F_COLA)
            w = jnp.exp(jnp.where(tri[dirn], dmat, -jnp.inf))
            p_s[i, dirn] = (st * w).astype(BF16)
        return carry

    lax.fori_loop(0, n_lat, pass_c1, 0, unroll=16)

    def pass_c2(i, carry):
        cg = n_ctx + i
        qtf = qt_ref[0, hh, i].astype(F32)
        vaug = jnp.concatenate([vtl_ref[0, hh, i], ones_tail], axis=0)
        ht = None
        for dirn, z_ref in enumerate((zf_s, zb_s)):
            rhs = jnp.concatenate([(qtf * field(cg, dirn, F_AINT)).astype(BF16), p_s[i, dirn]], axis=0)
            res = _dot(jnp.concatenate([z_ref[i], vaug], axis=1), rhs)
            den = jnp.maximum(jnp.abs(res[HEAD_DIM:HEAD_DIM + 1, :]), field(cg, dirn, F_ENEG))
            hd = res[:HEAD_DIM, :] / den
            ht = hd if ht is None else ht + hd
        ht = ht * lax.rsqrt(jnp.mean(ht * ht, axis=0, keepdims=True) + EPS)
        out_ref[0, hh, i] = (ht * gain * ogt_ref[0, hh, i].astype(F32)).astype(BF16)
        return carry

    lax.fori_loop(0, n_lat, pass_c2, 0, unroll=16)


def _mlstm(qt_l, k_l, vt_l, k_c, vt_c, gf, ogt, ng):
    bsz, t_lat, _ = k_l.shape
    t_ctx = k_c.shape[1]
    n_lat, n_ctx = t_lat // CHUNK, t_ctx // CHUNK
    n_all = n_lat + n_ctx
    hb = MLSTM_HEADS
    head_lat = pl.BlockSpec((1, t_lat, hb * HEAD_DIM), lambda b, h: (b, 0, h))
    head_ctx = pl.BlockSpec((1, t_ctx, hb * HEAD_DIM), lambda b, h: (b, 0, h))
    tiles = lambda n: pl.BlockSpec((1, hb, n, HEAD_DIM, CHUNK), lambda b, h: (b, h, 0, 0, 0))
    return pl.pallas_call(
        functools.partial(_mlstm_kernel, n_ctx=n_ctx, n_lat=n_lat),
        grid=(bsz, N_HEADS // hb),
        in_specs=[tiles(n_lat), head_lat, tiles(n_lat), head_ctx, tiles(n_ctx),
                  pl.BlockSpec((1, hb, n_all, N_FIELDS, CHUNK), lambda b, h: (b, h, 0, 0, 0)),
                  tiles(n_lat),
                  pl.BlockSpec((1, hb * HEAD_DIM), lambda b, h: (0, h))],
        out_specs=tiles(n_lat),
        out_shape=jax.ShapeDtypeStruct(qt_l.shape, BF16),
        scratch_shapes=[pltpu.VMEM((n_all, 2 * STATE_ROWS, HEAD_DIM), F32),
                        pltpu.VMEM((n_lat, STATE_ROWS, HEAD_DIM), BF16),
                        pltpu.VMEM((n_lat, STATE_ROWS, HEAD_DIM), BF16),
                        pltpu.VMEM((2, STATE_ROWS, HEAD_DIM), F32),
                        pltpu.VMEM((n_lat, 2, CHUNK, CHUNK), BF16)],
        compiler_params=pltpu.CompilerParams(dimension_semantics=("parallel", "parallel"),
                                             vmem_limit_bytes=VMEM_LIMIT),
        name="mlstm",
    )(qt_l, k_l, vt_l, k_c, vt_c, gf, ogt, ng)


def _merge_kernel(hgt_ref, gm_ref, zc_ref, x_ref, mod_ref, g2_ref, wmo_ref, wo_ref, x1_ref, h2_ref):
    per_sub = MERGE_SUB // CHUNK
    subs = [slice(i * MERGE_SUB, (i + 1) * MERGE_SUB) for i in range(hgt_ref.shape[2] // per_sub)]
    hg = [jnp.concatenate(
        [jnp.concatenate([hgt_ref[0, h, ci].astype(F32).T for h in range(N_HEADS)], axis=1)
         for ci in range(i * per_sub, (i + 1) * per_sub)], axis=0).astype(BF16) for i in range(len(subs))]
    ym = [_dot(hg_i, wmo_ref[...]) for hg_i in hg]
    y = [_dot((gm_ref[0, sl, :].astype(F32) * ym_i + zc_ref[0, sl, :].astype(F32)).astype(BF16), wo_ref[...])
         for sl, ym_i in zip(subs, ym)]
    for sl, y_i in zip(subs, y):
        x1 = x_ref[0, sl, :] + mod_ref[0, 2:3, :] * y_i
        x1_ref[0, sl, :] = x1
        h2 = _rms_scale(x1) * g2_ref[...] * (1.0 + mod_ref[0, 4:5, :]) + mod_ref[0, 3:4, :]
        h2_ref[0, sl, :] = h2.astype(BF16)


def _merge(hgt, gm, zc, x, mod, g2, wmo, wo):
    bsz, t_len, _ = x.shape
    tm = MERGE_TM
    tok = pl.BlockSpec((1, tm, D_MODEL), lambda b, t: (b, t, 0))
    tiles = pl.BlockSpec((1, N_HEADS, tm // CHUNK, HEAD_DIM, CHUNK), lambda b, t: (b, 0, t, 0, 0))
    return pl.pallas_call(
        _merge_kernel,
        grid=(bsz, t_len // tm),
        in_specs=[tiles, tok, tok, tok, pl.BlockSpec((1, 8, D_MODEL), lambda b, t: (b, 0, 0)),
                  _const_spec(g2.shape), _const_spec(wmo.shape), _const_spec(wo.shape)],
        out_specs=[tok, tok],
        out_shape=[jax.ShapeDtypeStruct(x.shape, F32), jax.ShapeDtypeStruct(x.shape, BF16)],
        compiler_params=pltpu.CompilerParams(dimension_semantics=("parallel", "parallel"),
                                             vmem_limit_bytes=VMEM_LIMIT),
        name="merge",
    )(hgt, gm, zc, x, mod, g2, wmo, wo)


def _gelu_tanh(v):
    return 0.5 * v * (1.0 + jnp.tanh(0.7978845608028654 * (v + 0.044715 * (v * v * v))))


def _stage_weight(src_hbm, dst_s, stage_s, sem, n_parts, part):
    slots = stage_s.shape[0]

    def copy(c):
        return pltpu.make_async_copy(src_hbm.at[(0,) + part(c)], stage_s.at[c % slots], sem.at[c % slots])

    for c in range(min(slots, n_parts)):
        copy(c).start()
    for c in range(n_parts):
        copy(c).wait()
        dst_s[part(c)] = stage_s[c % slots].astype(BF16)
        if c + slots < n_parts:
            copy(c + slots).start()


def _ffn_kernel(hm_ref, hp_ref, hn_ref, x1_ref, mod_ref, wu_hbm, cw_ref, cb_ref, wd_hbm, fg_ref,
                out_ref, act_s, wu_ref, wd_ref, stage_u, stage_d, sem_u, sem_d, *, tm):
    t = pl.program_id(1)
    nt = pl.num_programs(1)

    @pl.when((pl.program_id(0) == 0) & (t == 0))
    def _():
        for src, dst, stage, sem in ((wu_hbm, wu_ref, stage_u, sem_u), (wd_hbm, wd_ref, stage_d, sem_d)):
            rows = stage.shape[1]
            _stage_weight(src, dst, stage, sem, dst.shape[0] // rows,
                          lambda c, rows=rows: (pl.ds(c * rows, rows), slice(None)))

    n = tm + 2 * GRID_W
    hm = hm_ref[0]
    hp = jnp.where(t > 0, hp_ref[0], jnp.zeros_like(hp_ref[0]))
    hn = jnp.where(t < nt - 1, hn_ref[0], jnp.zeros_like(hn_ref[0]))
    he = jnp.concatenate([hp, hm, hn], axis=0)
    gcol = lax.broadcasted_iota(jnp.int32, (n, 1), 0) & (GRID_W - 1)
    has_left = gcol != 0
    has_right = gcol != GRID_W - 1
    n_chunks = FF_HIDDEN // FFN_CW
    cols = lambda j: slice(j * FFN_CW, (j + 1) * FFN_CW)
    gate_cols = lambda j: slice(FF_HIDDEN + j * FFN_CW, FF_HIDDEN + (j + 1) * FFN_CW)
    up = lambda j: (_dot(he, wu_ref[:, cols(j)]), _dot(hm, wu_ref[:, gate_cols(j)]))
    acc = None
    nxt = up(0)
    for j in range(n_chunks):
        cs = cols(j)
        a, g = nxt
        if j + 1 < n_chunks:
            nxt = up(j + 1)
        taps = (jnp.where(has_left, pltpu.roll(a, 1, 0), 0.0), a,
                jnp.where(has_right, pltpu.roll(a, n - 1, 0), 0.0))
        conv = cb_ref[:, cs]
        for dr in range(3):
            for dc in range(3):
                conv = conv + cw_ref[3 * dr + dc: 3 * dr + dc + 1, cs] * taps[dc][GRID_W * dr: GRID_W * dr + tm]
        act_s[:, cs] = (_gelu_tanh(conv) * g).astype(BF16)
        if (j + 1) % FFN_DOWN_GROUP == 0 or j + 1 == n_chunks:
            gs = slice((j // FFN_DOWN_GROUP) * FFN_DOWN_GROUP * FFN_CW, (j + 1) * FFN_CW)
            part = _dot(act_s[:, gs], wd_ref[gs, :])
            acc = part if acc is None else acc + part
    x2 = x1_ref[0] + mod_ref[0, 5:6, :] * acc
    out_ref[0] = _rms_scale(x2) * fg_ref[...]


def _ffn(h2, x1, mod, wu, cw, cb, wd, fg):
    bsz, t_len, _ = x1.shape
    tm = FFN_TM
    rb = tm // GRID_W
    nrb = t_len // GRID_W
    tok = pl.BlockSpec((1, tm, D_MODEL), lambda b, t: (b, t, 0))
    return pl.pallas_call(
        functools.partial(_ffn_kernel, tm=tm),
        grid=(bsz, t_len // tm),
        in_specs=[tok,
                  pl.BlockSpec((1, GRID_W, D_MODEL), lambda b, t: (b, jnp.maximum(t * rb - 1, 0), 0)),
                  pl.BlockSpec((1, GRID_W, D_MODEL), lambda b, t: (b, jnp.minimum((t + 1) * rb, nrb - 1), 0)),
                  tok, pl.BlockSpec((1, 8, D_MODEL), lambda b, t: (b, 0, 0)),
                  pl.BlockSpec(memory_space=pl.ANY), _const_spec(cw.shape), _const_spec(cb.shape),
                  pl.BlockSpec(memory_space=pl.ANY), _const_spec(fg.shape)],
        out_specs=tok,
        out_shape=jax.ShapeDtypeStruct(x1.shape, F32),
        scratch_shapes=[pltpu.VMEM((tm, FF_HIDDEN), BF16),
                        pltpu.VMEM(wu.shape[1:], BF16),
                        pltpu.VMEM(wd.shape[1:], BF16),
                        pltpu.VMEM((STAGE_SLOTS, FFN_STAGE_ROWS_UP, wu.shape[2]), F32),
                        pltpu.VMEM((STAGE_SLOTS, FFN_STAGE_ROWS_DOWN, wd.shape[2]), F32),
                        pltpu.SemaphoreType.DMA((STAGE_SLOTS,)), pltpu.SemaphoreType.DMA((STAGE_SLOTS,))],
        compiler_params=pltpu.CompilerParams(dimension_semantics=("arbitrary", "arbitrary"),
                                             vmem_limit_bytes=VMEM_LIMIT),
        name="ffn",
    )(h2, h2, h2, x1, mod, wu, cw, cb, wd, fg)


def kernel(x, c, ctx, c_ctx, ada_w, ada_b, norm1_g, norm2_g, w_in, qk_conv_w, qk_conv_b, gate_b, mnorm_g,
           w_m_out, sc_conv_w, sc_conv_b, w_c_out, w_o, w_up, ff_conv_w, ff_conv_b, w_down, final_g):
    assert ada_w.shape[0] == 1, "single-layer block"
    bsz, t_lat, _ = x.shape
    t_ctx = ctx.shape[1]

    cc = jnp.zeros((2 * 8, D_MODEL), F32).at[:bsz].set(c).at[bsz].set(c_ctx)
    mod = _ada(cc, ada_w[0], ada_b[0][None, :])
    mod_x = jnp.pad(mod[:bsz].reshape(bsz, 6, D_MODEL), ((0, 0), (0, 2), (0, 0)))
    mod_c = jnp.pad(mod[bsz].reshape(1, 6, D_MODEL), ((0, 0), (0, 2), (0, 0)))

    cast = lambda a: a.astype(BF16)
    w_t = jnp.swapaxes(w_in, 1, 2)
    gb = gate_b[0][:, None]
    g1 = norm1_g[0][None, :]
    ctx_w = [w_t, gb, qk_conv_w[0], qk_conv_b[0][None, :]]
    lat_w = ctx_w + [sc_conv_w[0], sc_conv_b[0][None, :], cast(w_c_out[0])]

    k_l, vt_l, g_l, qt_l, ogt, gm, zc = _proj(x, mod_x, g1, lat_w, full=True, per_batch_mod=True)
    k_c, vt_c, g_c = _proj(ctx, mod_c, g1, ctx_w, full=False, per_batch_mod=False)

    gf = _gates(g_l, g_c)
    hgt = _mlstm(qt_l, k_l, vt_l, k_c, vt_c, gf, ogt, mnorm_g[0][None, :])

    x1, h2 = _merge(hgt, gm, zc, x, mod_x, norm2_g[0][None, :], cast(w_m_out[0]), cast(w_o[0]))

    out = _ffn(h2, x1, mod_x, w_up,
               ff_conv_w[0].reshape(9, FF_HIDDEN), ff_conv_b[0][None, :], w_down,
               final_g[None, :])
    return out
```

```python
import functools

import jax
import jax.numpy as jnp
from jax import lax
from jax.experimental import pallas as pl
from jax.experimental.pallas import tpu as pltpu

F32 = jnp.float32
BF16 = jnp.bfloat16

D_MODEL = 1024
N_HEADS = 8
HEAD_DIM = D_MODEL // N_HEADS
CHUNK = 128
GRID_W = 64
FF_HIDDEN = 2816
EPS = 1e-6
M_INIT = -1e30
N_GATES = 4 * N_HEADS

LANES = 128
BF16_ROWS = 16
VMEM_LIMIT = 56 * 1024 * 1024

PROJ_TM = 256
STAGE_SLOTS = 4
PROJ_STAGE_ROWS = 272
PROJ_STAGE_ROWS_CTX = 1552
SEQ_HALO = 8
MERGE_TM = 1024
MERGE_SUB = 256
FFN_TM = 512
FFN_CW = 256
FFN_DOWN_GROUP = 4
FFN_STAGE_ROWS_UP = 64
FFN_STAGE_ROWS_DOWN = 128

R_ROWB, R_DECAY = 0, 2
C_BASE, C_PER_DIR = 4, 4
F_COLA, F_AINT, F_ENEG, F_WK = range(4)
N_FIELDS = 16


def _const_spec(shape):
    nd = len(shape)
    return pl.BlockSpec(shape, lambda *_: (0,) * nd, pipeline_mode=pl.Buffered(1))


def _sigmoid(v):
    return 0.5 * jnp.tanh(0.5 * v) + 0.5


def _silu(v):
    h = 0.5 * v
    return h + h * jnp.tanh(h)


def _rms_scale(v):
    return v * lax.rsqrt(jnp.mean(v * v, axis=-1, keepdims=True) + EPS)


def _dot(a, b):
    return jnp.dot(a, b, preferred_element_type=F32)


def _ada_kernel(c_ref, w_ref, b_ref, o_ref):
    cv = c_ref[...]
    s = _silu(cv)
    w = w_ref[...]
    s_hi, w_hi = s.astype(BF16), w.astype(BF16)
    s_lo = (s - s_hi.astype(F32)).astype(BF16)
    w_lo = (w - w_hi.astype(F32)).astype(BF16)
    rows = s.shape[0]
    main = _dot(jnp.concatenate([s_hi, s_lo], axis=0), w_hi)
    o_ref[...] = main[:rows] + main[rows:] + _dot(s_hi, w_lo) + b_ref[...]


def _ada(cc, w, b):
    rows, n = cc.shape[0], w.shape[1]
    tn = 1024
    return pl.pallas_call(
        _ada_kernel,
        grid=(n // tn,),
        in_specs=[pl.BlockSpec((rows, D_MODEL), lambda j: (0, 0)),
                  pl.BlockSpec((D_MODEL, tn), lambda j: (0, j)),
                  pl.BlockSpec((1, tn), lambda j: (0, j))],
        out_specs=pl.BlockSpec((rows, tn), lambda j: (0, j)),
        out_shape=jax.ShapeDtypeStruct((rows, n), F32),
        name="ada",
    )(cc, w, b)


def _conv3_rows(p, w_ref, b_ref, tm):
    main = p[:tm]
    prev_row = p[tm + SEQ_HALO - 1:tm + SEQ_HALO]
    next_row = p[tm + SEQ_HALO:tm + SEQ_HALO + 1]
    sub = lax.broadcasted_iota(jnp.int32, (SEQ_HALO, 1), 0)
    down = pltpu.roll(main, 1, 0)
    up = pltpu.roll(main, tm - 1, 0)
    left = jnp.concatenate([jnp.where(sub == 0, prev_row, down[:SEQ_HALO]), down[SEQ_HALO:]], axis=0)
    right = jnp.concatenate([up[:tm - SEQ_HALO], jnp.where(sub == SEQ_HALO - 1, next_row, up[tm - SEQ_HALO:])],
                            axis=0)
    return left * w_ref[0:1, :] + main * w_ref[1:2, :] + right * w_ref[2:3, :] + b_ref[...]


def _store_head_tiles_t(ref, val, tm):
    for h in range(N_HEADS):
        for ci in range(tm // CHUNK):
            tile = val[ci * CHUNK:(ci + 1) * CHUNK, h * HEAD_DIM:(h + 1) * HEAD_DIM]
            ref[0, h, ci] = tile.T.astype(BF16)


W_QK = slice(0, 2 * D_MODEL)
W_V = slice(2 * D_MODEL, 3 * D_MODEL)
W_G = slice(3 * D_MODEL, 3 * D_MODEL + N_GATES)
W_CTX_ROWS = W_G.stop
W_O = slice(W_G.stop, W_G.stop + D_MODEL)
W_SB = slice(W_O.stop, W_O.stop + D_MODEL)
W_SCX = slice(W_SB.stop, W_SB.stop + 2 * D_MODEL)
W_MG = slice(W_SCX.stop, W_SCX.stop + 2 * D_MODEL)


def _dot_nt(a, b):
    return lax.dot_general(a, b, (((1,), (1,)), ((), ())), preferred_element_type=F32)


def _proj_kernel(*refs, tm, full):
    w_ref, stage_s, sem = refs[-3:]
    if full:
        (xm_ref, xp_ref, xn_ref, mod_ref, g1_ref, w_hbm, gb_ref, qkw_ref, qkb_ref,
         scw_ref, scb_ref, wco_ref,
         k_ref, vt_ref, g_ref, qt_ref, ogt_ref, gm_ref, zc_ref) = refs[:-3]
    else:
        (xm_ref, xp_ref, xn_ref, mod_ref, g1_ref, w_hbm, gb_ref, qkw_ref, qkb_ref,
         k_ref, vt_ref, g_ref) = refs[:-3]
    t = pl.program_id(1)
    nt = pl.num_programs(1)

    @pl.when((pl.program_id(0) == 0) & (t == 0))
    def _():
        rows = stage_s.shape[1]
        _stage_weight(w_hbm, w_ref, stage_s, sem, w_ref.shape[0] // rows,
                      lambda c: (pl.ds(c * rows, rows), slice(None)))

    shift = mod_ref[0, 0:1, :]
    scale1 = 1.0 + mod_ref[0, 1:2, :]
    gain = g1_ref[...]

    def norm_mod(xv):
        return _rms_scale(xv) * gain * scale1 + shift

    hm = norm_mod(xm_ref[0])
    hp = jnp.where(t > 0, norm_mod(xp_ref[0]), 0.0)
    hn = jnp.where(t < nt - 1, norm_mod(xn_ref[0]), 0.0)
    hmb = hm.astype(BF16)
    he = jnp.concatenate([hm, hp, hn], axis=0).astype(BF16)

    pqk = _dot_nt(he, w_ref[W_QK, :])
    pv = _dot_nt(hmb, w_ref[W_V, :])
    qk = _conv3_rows(pqk, qkw_ref, qkb_ref, tm)
    qk = _silu(qk)
    k_ref[0] = (qk[:, D_MODEL:] * (HEAD_DIM ** -0.5)).astype(BF16)
    pg = _dot_nt(w_ref[W_G, :], hmb)
    if not full:
        _store_head_tiles_t(vt_ref, pv, tm)
        g_ref[0] = pg + gb_ref[...]
        return
    po = _dot_nt(hmb, w_ref[W_O, :])
    _store_head_tiles_t(qt_ref, qk[:, :D_MODEL], tm)
    _store_head_tiles_t(vt_ref, pv, tm)
    g_ref[0] = pg + gb_ref[...]
    pcx = _dot_nt(he, w_ref[W_SCX, :])
    _store_head_tiles_t(ogt_ref, _sigmoid(po), tm)
    psb = _dot_nt(hmb, w_ref[W_SB, :])
    cu = _conv3_rows(pcx[:, :D_MODEL] * pcx[:, D_MODEL:], scw_ref, scb_ref, tm)
    pm = _dot_nt(hmb, w_ref[W_MG, :])
    yc = _dot((psb * cu).astype(BF16), wco_ref[...])
    gm_ref[0] = _sigmoid(pm[:, :D_MODEL]).astype(BF16)
    zc_ref[0] = (_sigmoid(pm[:, D_MODEL:]) * yc).astype(BF16)


def _proj(xs, mod, g1, weights, *, full, per_batch_mod):
    bsz, t_len, _ = xs.shape
    tm = min(PROJ_TM, t_len)
    nt = t_len // tm
    hb = tm // SEQ_HALO
    nhb = t_len // SEQ_HALO
    x_specs = [
        pl.BlockSpec((1, tm, D_MODEL), lambda b, t: (b, t, 0)),
        pl.BlockSpec((1, SEQ_HALO, D_MODEL), lambda b, t: (b, jnp.maximum(t * hb - 1, 0), 0)),
        pl.BlockSpec((1, SEQ_HALO, D_MODEL), lambda b, t: (b, jnp.minimum((t + 1) * hb, nhb - 1), 0)),
    ]
    mod_spec = pl.BlockSpec((1, 8, D_MODEL), (lambda b, t: (b, 0, 0)) if per_batch_mod else (lambda b, t: (0, 0, 0)))
    w_rows = weights[0].shape[1] if full else W_CTX_ROWS
    stage_rows, stage_slots = (PROJ_STAGE_ROWS, STAGE_SLOTS) if full else (PROJ_STAGE_ROWS_CTX, 2)
    assert w_rows % stage_rows == 0 and stage_rows % BF16_ROWS == 0
    w_specs = [pl.BlockSpec(memory_space=pl.ANY)] + [_const_spec(w.shape) for w in weights[1:]]
    tok = lambda n, dt: (pl.BlockSpec((1, tm, n), lambda b, t: (b, t, 0)),
                         jax.ShapeDtypeStruct((bsz, t_len, n), dt))
    tiles_t = (pl.BlockSpec((1, N_HEADS, tm // CHUNK, HEAD_DIM, CHUNK), lambda b, t: (b, 0, t, 0, 0)),
               jax.ShapeDtypeStruct((bsz, N_HEADS, t_len // CHUNK, HEAD_DIM, CHUNK), BF16))
    gates_t = (pl.BlockSpec((1, N_GATES, tm), lambda b, t: (b, 0, t)),
               jax.ShapeDtypeStruct((bsz, N_GATES, t_len), F32))
    outs = [tok(D_MODEL, BF16), tiles_t, gates_t]
    if full:
        outs += [tiles_t, tiles_t, tok(D_MODEL, BF16), tok(D_MODEL, BF16)]
    return pl.pallas_call(
        functools.partial(_proj_kernel, tm=tm, full=full),
        grid=(bsz, nt),
        in_specs=x_specs + [mod_spec, _const_spec(g1.shape)] + w_specs,
        out_specs=[o[0] for o in outs],
        out_shape=[o[1] for o in outs],
        scratch_shapes=[pltpu.VMEM((w_rows, D_MODEL), BF16),
                        pltpu.VMEM((stage_slots, stage_rows, D_MODEL), F32),
                        pltpu.SemaphoreType.DMA((stage_slots,))],
        compiler_params=pltpu.CompilerParams(dimension_semantics=("arbitrary", "arbitrary"),
                                             vmem_limit_bytes=VMEM_LIMIT),
        name="proj_full" if full else "proj_ctx",
    )(xs, xs, xs, mod, g1, *weights)


def _scan_lanes(v, combine, fill, forward):
    lane = lax.broadcasted_iota(jnp.int32, v.shape, 1)
    d = 1
    while d < LANES:
        if forward:
            shifted = jnp.where(lane >= d, pltpu.roll(v, d, 1), fill)
        else:
            shifted = jnp.where(lane < LANES - d, pltpu.roll(v, LANES - d, 1), fill)
        v = combine(v, shifted)
        d *= 2
    return v


def _gates_kernel(gl_ref, gc_ref, out_ref, *, n_ctx, n_lat):
    n_all = n_ctx + n_lat
    src = lax.broadcasted_iota(jnp.int32, (CHUNK, 2 * CHUNK), 0)
    dst = lax.broadcasted_iota(jnp.int32, (CHUNK, 2 * CHUNK), 1)

    def gate_rows(cg, first):
        if cg < n_ctx:
            return gc_ref[0, first:first + N_HEADS, cg * CHUNK:(cg + 1) * CHUNK]
        return gl_ref[0, first:first + N_HEADS, (cg - n_ctx) * CHUNK:(cg - n_ctx + 1) * CHUNK]

    for dirn in range(2):
        forward = dirn == 0
        if forward:
            order = list(range(n_all))
        else:
            order = list(range(n_ctx - 1, -1, -1)) + list(range(n_all - 1, n_ctx - 1, -1))
        last = LANES - 1 if forward else 0
        first_row = 2 * N_HEADS * dirn
        fg = jnp.concatenate([gate_rows(cg, first_row + N_HEADS) for cg in range(n_all)], axis=0)
        lf = jnp.minimum(fg, 0.0) - jnp.log1p(jnp.exp(-jnp.abs(fg)))
        prefix = (src <= dst) if forward else (src >= dst)
        sel = jnp.where((dst >= CHUNK) | prefix, 1.0, 0.0).astype(BF16)
        hi = lf.astype(BF16)
        rest = lf - hi.astype(F32)
        mid = rest.astype(BF16)
        low = (rest - mid.astype(F32)).astype(BF16)
        sums = _dot(hi, sel) + _dot(mid, sel) + _dot(low, sel)
        m = jnp.full((N_HEADS, 1), M_INIT, F32)
        for cg in order:
            rows = slice(cg * N_HEADS, (cg + 1) * N_HEADS)
            b = sums[rows, :CHUNK]
            b_end = sums[rows, CHUNK:CHUNK + 1]
            r = gate_rows(cg, first_row) - b
            cm = _scan_lanes(r, jnp.maximum, -jnp.inf, forward)
            mx = jnp.maximum(m, cm)
            mt = b + mx
            mx_end = jnp.maximum(m, cm[:, last:last + 1])
            cbase = C_BASE + C_PER_DIR * dirn
            fields = {
                R_ROWB + dirn: r,
                R_DECAY + dirn: jnp.broadcast_to(jnp.exp(m - mx_end), (N_HEADS, CHUNK)),
                cbase + F_COLA: -mx,
                cbase + F_AINT: jnp.exp(m - mx),
                cbase + F_ENEG: jnp.exp(-mt),
                cbase + F_WK: jnp.exp(r - mx_end),
            }
            m_new = b_end + mx_end
            for f, val in fields.items():
                for h in range(N_HEADS):
                    out_ref[0, h, cg, f:f + 1, :] = val[h:h + 1, :]
            m = m_new
    n_used = C_BASE + 2 * C_PER_DIR
    out_ref[0, :, :, n_used:, :] = jnp.zeros((N_HEADS, n_all, N_FIELDS - n_used, CHUNK), F32)


def _gates(g_lat, g_ctx):
    bsz, _, t_lat = g_lat.shape
    t_ctx = g_ctx.shape[2]
    n_all = (t_lat + t_ctx) // CHUNK
    shape = (bsz, N_HEADS, n_all, N_FIELDS, CHUNK)
    return pl.pallas_call(
        functools.partial(_gates_kernel, n_ctx=t_ctx // CHUNK, n_lat=t_lat // CHUNK),
        grid=(bsz,),
        in_specs=[pl.BlockSpec((1, N_GATES, t_lat), lambda b: (b, 0, 0)),
                  pl.BlockSpec((1, N_GATES, t_ctx), lambda b: (b, 0, 0))],
        out_specs=pl.BlockSpec((1,) + shape[1:], lambda b: (b, 0, 0, 0, 0)),
        out_shape=jax.ShapeDtypeStruct(shape, F32),
        compiler_params=pltpu.CompilerParams(dimension_semantics=("parallel",)),
        name="gates",
    )(g_lat, g_ctx)


STATE_ROWS = HEAD_DIM + BF16_ROWS
MLSTM_HEADS = 4


def _mlstm_kernel(qt_ref, kl_ref, vtl_ref, kc_ref, vtc_ref, gf_ref, ogt_ref, ng_ref, out_ref,
                  dz_s, zf_s, zb_s, z_s, p_s, *, n_ctx, n_lat):
    lane = lax.broadcasted_iota(jnp.int32, (CHUNK, CHUNK), 1)
    row = lax.broadcasted_iota(jnp.int32, (CHUNK, CHUNK), 0)
    tri = (row <= lane, row >= lane)
    tail_first = lax.broadcasted_iota(jnp.int32, (BF16_ROWS, CHUNK), 0) == 0
    ones_tail = jnp.where(tail_first, 1.0, 0.0).astype(BF16)

    for hh in range(MLSTM_HEADS):
        _mlstm_head(hh, qt_ref, kl_ref, vtl_ref, kc_ref, vtc_ref, gf_ref, ogt_ref, ng_ref, out_ref,
                    dz_s, zf_s, zb_s, z_s, p_s, tri, tail_first, ones_tail, n_ctx, n_lat)


def _mlstm_head(hh, qt_ref, kl_ref, vtl_ref, kc_ref, vtc_ref, gf_ref, ogt_ref, ng_ref, out_ref,
                dz_s, zf_s, zb_s, z_s, p_s, tri, tail_first, ones_tail, n_ctx, n_lat):
    hcols = slice(hh * HEAD_DIM, (hh + 1) * HEAD_DIM)

    def field(cg, dirn, f):
        j = C_BASE + C_PER_DIR * dirn + f
        return gf_ref[0, hh, cg, j:j + 1, :]

    def pass_a(cg, k, vt):
        vf = vt.astype(F32)
        parts = []
        for dirn in range(2):
            wk = field(cg, dirn, F_WK)
            parts += [(vf * wk).astype(BF16), jnp.where(tail_first, wk, 0.0).astype(BF16)]
        dz_s[cg] = _dot(jnp.concatenate(parts, axis=0), k)

    for cg in range(n_ctx):
        pass_a(cg, kc_ref[0, cg * CHUNK:(cg + 1) * CHUNK, hcols], vtc_ref[0, hh, cg])

    def pass_a_lat(i, carry):
        sl = pl.ds(pl.multiple_of(i * CHUNK, CHUNK), CHUNK)
        pass_a(n_ctx + i, kl_ref[0, sl, hcols], vtl_ref[0, hh, i])
        return carry

    lax.fori_loop(0, n_lat, pass_a_lat, 0, unroll=8)

    def advance(dirn, cg):
        dec = gf_ref[0, hh, cg, R_DECAY + dirn:R_DECAY + dirn + 1, :]
        z_s[dirn] = dec * z_s[dirn] + dz_s[cg, dirn * STATE_ROWS:(dirn + 1) * STATE_ROWS, :]

    z_s[...] = jnp.zeros_like(z_s)
    for cg in range(n_ctx):
        advance(0, cg)
    for cg in range(n_ctx - 1, -1, -1):
        advance(1, cg)

    def pass_b(i, carry):
        zf_s[i] = z_s[0].astype(BF16)
        advance(0, n_ctx + i)
        j = n_lat - 1 - i
        zb_s[j] = z_s[1].astype(BF16)
        advance(1, n_ctx + j)
        return carry

    lax.fori_loop(0, n_lat, pass_b, 0, unroll=2)

    gain = jnp.broadcast_to(ng_ref[:, hcols], (CHUNK, HEAD_DIM)).T

    def pass_c1(i, carry):
        cg = n_ctx + i
        sl = pl.ds(pl.multiple_of(i * CHUNK, CHUNK), CHUNK)
        st = _dot(kl_ref[0, sl, hcols], qt_ref[0, hh, i])
        rowb = gf_ref[0, hh, cg].T
        for dirn in range(2):
            dmat = rowb[:, R_ROWB + dirn:R_ROWB + dirn + 1] + field(cg, dirn, F_COLA)
            w = jnp.exp(jnp.where(tri[dirn], dmat, -jnp.inf))
            p_s[i, dirn] = (st * w).astype(BF16)
        return carry

    lax.fori_loop(0, n_lat, pass_c1, 0, unroll=16)

    def pass_c2(i, carry):
        cg = n_ctx + i
        qtf = qt_ref[0, hh, i].astype(F32)
        vaug = jnp.concatenate([vtl_ref[0, hh, i], ones_tail], axis=0)
        ht = None
        for dirn, z_ref in enumerate((zf_s, zb_s)):
            rhs = jnp.concatenate([(qtf * field(cg, dirn, F_AINT)).astype(BF16), p_s[i, dirn]], axis=0)
            res = _dot(jnp.concatenate([z_ref[i], vaug], axis=1), rhs)
            den = jnp.maximum(jnp.abs(res[HEAD_DIM:HEAD_DIM + 1, :]), field(cg, dirn, F_ENEG))
            hd = res[:HEAD_DIM, :] / den
            ht = hd if ht is None else ht + hd
        ht = ht * lax.rsqrt(jnp.mean(ht * ht, axis=0, keepdims=True) + EPS)
        out_ref[0, hh, i] = (ht * gain * ogt_ref[0, hh, i].astype(F32)).astype(BF16)
        return carry

    lax.fori_loop(0, n_lat, pass_c2, 0, unroll=16)


def _mlstm(qt_l, k_l, vt_l, k_c, vt_c, gf, ogt, ng):
    bsz, t_lat, _ = k_l.shape
    t_ctx = k_c.shape[1]
    n_lat, n_ctx = t_lat // CHUNK, t_ctx // CHUNK
    n_all = n_lat + n_ctx
    hb = MLSTM_HEADS
    head_lat = pl.BlockSpec((1, t_lat, hb * HEAD_DIM), lambda b, h: (b, 0, h))
    head_ctx = pl.BlockSpec((1, t_ctx, hb * HEAD_DIM), lambda b, h: (b, 0, h))
    tiles = lambda n: pl.BlockSpec((1, hb, n, HEAD_DIM, CHUNK), lambda b, h: (b, h, 0, 0, 0))
    return pl.pallas_call(
        functools.partial(_mlstm_kernel, n_ctx=n_ctx, n_lat=n_lat),
        grid=(bsz, N_HEADS // hb),
        in_specs=[tiles(n_lat), head_lat, tiles(n_lat), head_ctx, tiles(n_ctx),
                  pl.BlockSpec((1, hb, n_all, N_FIELDS, CHUNK), lambda b, h: (b, h, 0, 0, 0)),
                  tiles(n_lat),
                  pl.BlockSpec((1, hb * HEAD_DIM), lambda b, h: (0, h))],
        out_specs=tiles(n_lat),
        out_shape=jax.ShapeDtypeStruct(qt_l.shape, BF16),
        scratch_shapes=[pltpu.VMEM((n_all, 2 * STATE_ROWS, HEAD_DIM), F32),
                        pltpu.VMEM((n_lat, STATE_ROWS, HEAD_DIM), BF16),
                        pltpu.VMEM((n_lat, STATE_ROWS, HEAD_DIM), BF16),
                        pltpu.VMEM((2, STATE_ROWS, HEAD_DIM), F32),
                        pltpu.VMEM((n_lat, 2, CHUNK, CHUNK), BF16)],
        compiler_params=pltpu.CompilerParams(dimension_semantics=("parallel", "parallel"),
                                             vmem_limit_bytes=VMEM_LIMIT),
        name="mlstm",
    )(qt_l, k_l, vt_l, k_c, vt_c, gf, ogt, ng)


def _merge_kernel(hgt_ref, gm_ref, zc_ref, x_ref, mod_ref, g2_ref, wmo_ref, wo_ref, x1_ref, h2_ref):
    per_sub = MERGE_SUB // CHUNK
    subs = [slice(i * MERGE_SUB, (i + 1) * MERGE_SUB) for i in range(hgt_ref.shape[2] // per_sub)]
    hg = [jnp.concatenate(
        [jnp.concatenate([hgt_ref[0, h, ci].astype(F32).T for h in range(N_HEADS)], axis=1)
         for ci in range(i * per_sub, (i + 1) * per_sub)], axis=0).astype(BF16) for i in range(len(subs))]
    ym = [_dot(hg_i, wmo_ref[...]) for hg_i in hg]
    y = [_dot((gm_ref[0, sl, :].astype(F32) * ym_i + zc_ref[0, sl, :].astype(F32)).astype(BF16), wo_ref[...])
         for sl, ym_i in zip(subs, ym)]
    for sl, y_i in zip(subs, y):
        x1 = x_ref[0, sl, :] + mod_ref[0, 2:3, :] * y_i
        x1_ref[0, sl, :] = x1
        h2 = _rms_scale(x1) * g2_ref[...] * (1.0 + mod_ref[0, 4:5, :]) + mod_ref[0, 3:4, :]
        h2_ref[0, sl, :] = h2.astype(BF16)


def _merge(hgt, gm, zc, x, mod, g2, wmo, wo):
    bsz, t_len, _ = x.shape
    tm = MERGE_TM
    tok = pl.BlockSpec((1, tm, D_MODEL), lambda b, t: (b, t, 0))
    tiles = pl.BlockSpec((1, N_HEADS, tm // CHUNK, HEAD_DIM, CHUNK), lambda b, t: (b, 0, t, 0, 0))
    return pl.pallas_call(
        _merge_kernel,
        grid=(bsz, t_len // tm),
        in_specs=[tiles, tok, tok, tok, pl.BlockSpec((1, 8, D_MODEL), lambda b, t: (b, 0, 0)),
                  _const_spec(g2.shape), _const_spec(wmo.shape), _const_spec(wo.shape)],
        out_specs=[tok, tok],
        out_shape=[jax.ShapeDtypeStruct(x.shape, F32), jax.ShapeDtypeStruct(x.shape, BF16)],
        compiler_params=pltpu.CompilerParams(dimension_semantics=("parallel", "parallel"),
                                             vmem_limit_bytes=VMEM_LIMIT),
        name="merge",
    )(hgt, gm, zc, x, mod, g2, wmo, wo)


def _gelu_tanh(v):
    return 0.5 * v * (1.0 + jnp.tanh(0.7978845608028654 * (v + 0.044715 * (v * v * v))))


def _stage_weight(src_hbm, dst_s, stage_s, sem, n_parts, part):
    slots = stage_s.shape[0]

    def copy(c):
        return pltpu.make_async_copy(src_hbm.at[(0,) + part(c)], stage_s.at[c % slots], sem.at[c % slots])

    for c in range(min(slots, n_parts)):
        copy(c).start()
    for c in range(n_parts):
        copy(c).wait()
        dst_s[part(c)] = stage_s[c % slots].astype(BF16)
        if c + slots < n_parts:
            copy(c + slots).start()


def _ffn_kernel(hm_ref, hp_ref, hn_ref, x1_ref, mod_ref, wu_hbm, cw_ref, cb_ref, wd_hbm, fg_ref,
                out_ref, act_s, wu_ref, wd_ref, stage_u, stage_d, sem_u, sem_d, *, tm):
    t = pl.program_id(1)
    nt = pl.num_programs(1)

    @pl.when((pl.program_id(0) == 0) & (t == 0))
    def _():
        for src, dst, stage, sem in ((wu_hbm, wu_ref, stage_u, sem_u), (wd_hbm, wd_ref, stage_d, sem_d)):
            rows = stage.shape[1]
            _stage_weight(src, dst, stage, sem, dst.shape[0] // rows,
                          lambda c, rows=rows: (pl.ds(c * rows, rows), slice(None)))

    n = tm + 2 * GRID_W
    hm = hm_ref[0]
    hp = jnp.where(t > 0, hp_ref[0], jnp.zeros_like(hp_ref[0]))
    hn = jnp.where(t < nt - 1, hn_ref[0], jnp.zeros_like(hn_ref[0]))
    he = jnp.concatenate([hp, hm, hn], axis=0)
    gcol = lax.broadcasted_iota(jnp.int32, (n, 1), 0) & (GRID_W - 1)
    has_left = gcol != 0
    has_right = gcol != GRID_W - 1
    n_chunks = FF_HIDDEN // FFN_CW
    cols = lambda j: slice(j * FFN_CW, (j + 1) * FFN_CW)
    gate_cols = lambda j: slice(FF_HIDDEN + j * FFN_CW, FF_HIDDEN + (j + 1) * FFN_CW)
    up = lambda j: (_dot(he, wu_ref[:, cols(j)]), _dot(hm, wu_ref[:, gate_cols(j)]))
    acc = None
    nxt = up(0)
    for j in range(n_chunks):
        cs = cols(j)
        a, g = nxt
        if j + 1 < n_chunks:
            nxt = up(j + 1)
        taps = (jnp.where(has_left, pltpu.roll(a, 1, 0), 0.0), a,
                jnp.where(has_right, pltpu.roll(a, n - 1, 0), 0.0))
        conv = cb_ref[:, cs]
        for dr in range(3):
            for dc in range(3):
                conv = conv + cw_ref[3 * dr + dc: 3 * dr + dc + 1, cs] * taps[dc][GRID_W * dr: GRID_W * dr + tm]
        act_s[:, cs] = (_gelu_tanh(conv) * g).astype(BF16)
        if (j + 1) % FFN_DOWN_GROUP == 0 or j + 1 == n_chunks:
            gs = slice((j // FFN_DOWN_GROUP) * FFN_DOWN_GROUP * FFN_CW, (j + 1) * FFN_CW)
            part = _dot(act_s[:, gs], wd_ref[gs, :])
            acc = part if acc is None else acc + part
    x2 = x1_ref[0] + mod_ref[0, 5:6, :] * acc
    out_ref[0] = _rms_scale(x2) * fg_ref[...]


def _ffn(h2, x1, mod, wu, cw, cb, wd, fg):
    bsz, t_len, _ = x1.shape
    tm = FFN_TM
    rb = tm // GRID_W
    nrb = t_len // GRID_W
    tok = pl.BlockSpec((1, tm, D_MODEL), lambda b, t: (b, t, 0))
    return pl.pallas_call(
        functools.partial(_ffn_kernel, tm=tm),
        grid=(bsz, t_len // tm),
        in_specs=[tok,
                  pl.BlockSpec((1, GRID_W, D_MODEL), lambda b, t: (b, jnp.maximum(t * rb - 1, 0), 0)),
                  pl.BlockSpec((1, GRID_W, D_MODEL), lambda b, t: (b, jnp.minimum((t + 1) * rb, nrb - 1), 0)),
                  tok, pl.BlockSpec((1, 8, D_MODEL), lambda b, t: (b, 0, 0)),
                  pl.BlockSpec(memory_space=pl.ANY), _const_spec(cw.shape), _const_spec(cb.shape),
                  pl.BlockSpec(memory_space=pl.ANY), _const_spec(fg.shape)],
        out_specs=tok,
        out_shape=jax.ShapeDtypeStruct(x1.shape, F32),
        scratch_shapes=[pltpu.VMEM((tm, FF_HIDDEN), BF16),
                        pltpu.VMEM(wu.shape[1:], BF16),
                        pltpu.VMEM(wd.shape[1:], BF16),
                        pltpu.VMEM((STAGE_SLOTS, FFN_STAGE_ROWS_UP, wu.shape[2]), F32),
                        pltpu.VMEM((STAGE_SLOTS, FFN_STAGE_ROWS_DOWN, wd.shape[2]), F32),
                        pltpu.SemaphoreType.DMA((STAGE_SLOTS,)), pltpu.SemaphoreType.DMA((STAGE_SLOTS,))],
        compiler_params=pltpu.CompilerParams(dimension_semantics=("arbitrary", "arbitrary"),
                                             vmem_limit_bytes=VMEM_LIMIT),
        name="ffn",
    )(h2, h2, h2, x1, mod, wu, cw, cb, wd, fg)


def kernel(x, c, ctx, c_ctx, ada_w, ada_b, norm1_g, norm2_g, w_in, qk_conv_w, qk_conv_b, gate_b, mnorm_g,
           w_m_out, sc_conv_w, sc_conv_b, w_c_out, w_o, w_up, ff_conv_w, ff_conv_b, w_down, final_g):
    assert ada_w.shape[0] == 1, "single-layer block"
    bsz, t_lat, _ = x.shape
    t_ctx = ctx.shape[1]

    cc = jnp.zeros((2 * 8, D_MODEL), F32).at[:bsz].set(c).at[bsz].set(c_ctx)
    mod = _ada(cc, ada_w[0], ada_b[0][None, :])
    mod_x = jnp.pad(mod[:bsz].reshape(bsz, 6, D_MODEL), ((0, 0), (0, 2), (0, 0)))
    mod_c = jnp.pad(mod[bsz].reshape(1, 6, D_MODEL), ((0, 0), (0, 2), (0, 0)))

    cast = lambda a: a.astype(BF16)
    w_t = jnp.swapaxes(w_in, 1, 2)
    gb = gate_b[0][:, None]
    g1 = norm1_g[0][None, :]
    ctx_w = [w_t, gb, qk_conv_w[0], qk_conv_b[0][None, :]]
    lat_w = ctx_w + [sc_conv_w[0], sc_conv_b[0][None, :], cast(w_c_out[0])]

    k_l, vt_l, g_l, qt_l, ogt, gm, zc = _proj(x, mod_x, g1, lat_w, full=True, per_batch_mod=True)
    k_c, vt_c, g_c = _proj(ctx, mod_c, g1, ctx_w, full=False, per_batch_mod=False)

    gf = _gates(g_l, g_c)
    hgt = _mlstm(qt_l, k_l, vt_l, k_c, vt_c, gf, ogt, mnorm_g[0][None, :])

    x1, h2 = _merge(hgt, gm, zc, x, mod_x, norm2_g[0][None, :], cast(w_m_out[0]), cast(w_o[0]))

    out = _ffn(h2, x1, mod_x, w_up,
               ff_conv_w[0].reshape(9, FF_HIDDEN), ff_conv_b[0][None, :], w_down,
               final_g[None, :])
    return out
```

```python
import functools

import jax
import jax.numpy as jnp
from jax import lax
from jax.experimental import pallas as pl
from jax.experimental.pallas import tpu as pltpu

F32 = jnp.float32
BF16 = jnp.bfloat16

D_MODEL = 1024
N_HEADS = 8
HEAD_DIM = D_MODEL // N_HEADS
CHUNK = 128
GRID_W = 64
FF_HIDDEN = 2816
EPS = 1e-6
M_INIT = -1e30
N_GATES = 4 * N_HEADS

LANES = 128
BF16_ROWS = 16
VMEM_LIMIT = 56 * 1024 * 1024

PROJ_TM = 256
STAGE_SLOTS = 4
PROJ_STAGE_ROWS = 272
PROJ_STAGE_ROWS_CTX = 1552
SEQ_HALO = 8
MERGE_TM = 1024
MERGE_SUB = 256
FFN_TM = 512
FFN_CW = 256
FFN_DOWN_GROUP = 4
FFN_STAGE_ROWS_UP = 64
FFN_STAGE_ROWS_DOWN = 128

R_ROWB, R_DECAY = 0, 2
C_BASE, C_PER_DIR = 4, 4
F_COLA, F_AINT, F_ENEG, F_WK = range(4)
N_FIELDS = 16


def _const_spec(shape):
    nd = len(shape)
    return pl.BlockSpec(shape, lambda *_: (0,) * nd, pipeline_mode=pl.Buffered(1))


def _sigmoid(v):
    return 0.5 * jnp.tanh(0.5 * v) + 0.5


def _silu(v):
    h = 0.5 * v
    return h + h * jnp.tanh(h)


def _rms_scale(v):
    return v * lax.rsqrt(jnp.mean(v * v, axis=-1, keepdims=True) + EPS)


def _dot(a, b):
    return jnp.dot(a, b, preferred_element_type=F32)


def _ada_kernel(c_ref, w_ref, b_ref, o_ref):
    cv = c_ref[...]
    s = _silu(cv)
    w = w_ref[...]
    s_hi, w_hi = s.astype(BF16), w.astype(BF16)
    s_lo = (s - s_hi.astype(F32)).astype(BF16)
    w_lo = (w - w_hi.astype(F32)).astype(BF16)
    rows = s.shape[0]
    main = _dot(jnp.concatenate([s_hi, s_lo], axis=0), w_hi)
    o_ref[...] = main[:rows] + main[rows:] + _dot(s_hi, w_lo) + b_ref[...]


def _ada(cc, w, b):
    rows, n = cc.shape[0], w.shape[1]
    tn = 1024
    return pl.pallas_call(
        _ada_kernel,
        grid=(n // tn,),
        in_specs=[pl.BlockSpec((rows, D_MODEL), lambda j: (0, 0)),
                  pl.BlockSpec((D_MODEL, tn), lambda j: (0, j)),
                  pl.BlockSpec((1, tn), lambda j: (0, j))],
        out_specs=pl.BlockSpec((rows, tn), lambda j: (0, j)),
        out_shape=jax.ShapeDtypeStruct((rows, n), F32),
        name="ada",
    )(cc, w, b)


def _conv3_rows(p, w_ref, b_ref, tm):
    main = p[:tm]
    prev_row = p[tm + SEQ_HALO - 1:tm + SEQ_HALO]
    next_row = p[tm + SEQ_HALO:tm + SEQ_HALO + 1]
    sub = lax.broadcasted_iota(jnp.int32, (SEQ_HALO, 1), 0)
    down = pltpu.roll(main, 1, 0)
    up = pltpu.roll(main, tm - 1, 0)
    left = jnp.concatenate([jnp.where(sub == 0, prev_row, down[:SEQ_HALO]), down[SEQ_HALO:]], axis=0)
    right = jnp.concatenate([up[:tm - SEQ_HALO], jnp.where(sub == SEQ_HALO - 1, next_row, up[tm - SEQ_HALO:])],
                            axis=0)
    return left * w_ref[0:1, :] + main * w_ref[1:2, :] + right * w_ref[2:3, :] + b_ref[...]


def _store_head_tiles_t(ref, val, tm):
    for h in range(N_HEADS):
        for ci in range(tm // CHUNK):
            tile = val[ci * CHUNK:(ci + 1) * CHUNK, h * HEAD_DIM:(h + 1) * HEAD_DIM]
            ref[0, h, ci] = tile.T.astype(BF16)


W_QK = slice(0, 2 * D_MODEL)
W_V = slice(2 * D_MODEL, 3 * D_MODEL)
W_G = slice(3 * D_MODEL, 3 * D_MODEL + N_GATES)
W_CTX_ROWS = W_G.stop
W_O = slice(W_G.stop, W_G.stop + D_MODEL)
W_SB = slice(W_O.stop, W_O.stop + D_MODEL)
W_SCX = slice(W_SB.stop, W_SB.stop + 2 * D_MODEL)
W_MG = slice(W_SCX.stop, W_SCX.stop + 2 * D_MODEL)


def _dot_nt(a, b):
    return lax.dot_general(a, b, (((1,), (1,)), ((), ())), preferred_element_type=F32)


def _proj_kernel(*refs, tm, full):
    w_ref, stage_s, sem = refs[-3:]
    if full:
        (xm_ref, xp_ref, xn_ref, mod_ref, g1_ref, w_hbm, gb_ref, qkw_ref, qkb_ref,
         scw_ref, scb_ref, wco_ref,
         k_ref, vt_ref, g_ref, qt_ref, ogt_ref, gm_ref, zc_ref) = refs[:-3]
    else:
        (xm_ref, xp_ref, xn_ref, mod_ref, g1_ref, w_hbm, gb_ref, qkw_ref, qkb_ref,
         k_ref, vt_ref, g_ref) = refs[:-3]
    t = pl.program_id(1)
    nt = pl.num_programs(1)

    @pl.when((pl.program_id(0) == 0) & (t == 0))
    def _():
        rows = stage_s.shape[1]
        _stage_weight(w_hbm, w_ref, stage_s, sem, w_ref.shape[0] // rows,
                      lambda c: (pl.ds(c * rows, rows), slice(None)))

    shift = mod_ref[0, 0:1, :]
    scale1 = 1.0 + mod_ref[0, 1:2, :]
    gain = g1_ref[...]

    def norm_mod(xv):
        return _rms_scale(xv) * gain * scale1 + shift

    hm = norm_mod(xm_ref[0])
    hp = jnp.where(t > 0, norm_mod(xp_ref[0]), 0.0)
    hn = jnp.where(t < nt - 1, norm_mod(xn_ref[0]), 0.0)
    hmb = hm.astype(BF16)
    he = jnp.concatenate([hm, hp, hn], axis=0).astype(BF16)

    pqk = _dot_nt(he, w_ref[W_QK, :])
    pv = _dot_nt(hmb, w_ref[W_V, :])
    qk = _conv3_rows(pqk, qkw_ref, qkb_ref, tm)
    qk = _silu(qk)
    k_ref[0] = (qk[:, D_MODEL:] * (HEAD_DIM ** -0.5)).astype(BF16)
    pg = _dot_nt(w_ref[W_G, :], hmb)
    if not full:
        _store_head_tiles_t(vt_ref, pv, tm)
        g_ref[0] = pg + gb_ref[...]
        return
    po = _dot_nt(hmb, w_ref[W_O, :])
    _store_head_tiles_t(qt_ref, qk[:, :D_MODEL], tm)
    _store_head_tiles_t(vt_ref, pv, tm)
    g_ref[0] = pg + gb_ref[...]
    pcx = _dot_nt(he, w_ref[W_SCX, :])
    _store_head_tiles_t(ogt_ref, _sigmoid(po), tm)
    psb = _dot_nt(hmb, w_ref[W_SB, :])
    cu = _conv3_rows(pcx[:, :D_MODEL] * pcx[:, D_MODEL:], scw_ref, scb_ref, tm)
    pm = _dot_nt(hmb, w_ref[W_MG, :])
    yc = _dot((psb * cu).astype(BF16), wco_ref[...])
    gm_ref[0] = _sigmoid(pm[:, :D_MODEL]).astype(BF16)
    zc_ref[0] = (_sigmoid(pm[:, D_MODEL:]) * yc).astype(BF16)


def _proj(xs, mod, g1, weights, *, full, per_batch_mod):
    bsz, t_len, _ = xs.shape
    tm = min(PROJ_TM, t_len)
    nt = t_len // tm
    hb = tm // SEQ_HALO
    nhb = t_len // SEQ_HALO
    x_specs = [
        pl.BlockSpec((1, tm, D_MODEL), lambda b, t: (b, t, 0)),
        pl.BlockSpec((1, SEQ_HALO, D_MODEL), lambda b, t: (b, jnp.maximum(t * hb - 1, 0), 0)),
        pl.BlockSpec((1, SEQ_HALO, D_MODEL), lambda b, t: (b, jnp.minimum((t + 1) * hb, nhb - 1), 0)),
    ]
    mod_spec = pl.BlockSpec((1, 8, D_MODEL), (lambda b, t: (b, 0, 0)) if per_batch_mod else (lambda b, t: (0, 0, 0)))
    w_rows = weights[0].shape[1] if full else W_CTX_ROWS
    stage_rows, stage_slots = (PROJ_STAGE_ROWS, STAGE_SLOTS) if full else (PROJ_STAGE_ROWS_CTX, 2)
    assert w_rows % stage_rows == 0 and stage_rows % BF16_ROWS == 0
    w_specs = [pl.BlockSpec(memory_space=pl.ANY)] + [_const_spec(w.shape) for w in weights[1:]]
    tok = lambda n, dt: (pl.BlockSpec((1, tm, n), lambda b, t: (b, t, 0)),
                         jax.ShapeDtypeStruct((bsz, t_len, n), dt))
    tiles_t = (pl.BlockSpec((1, N_HEADS, tm // CHUNK, HEAD_DIM, CHUNK), lambda b, t: (b, 0, t, 0, 0)),
               jax.ShapeDtypeStruct((bsz, N_HEADS, t_len // CHUNK, HEAD_DIM, CHUNK), BF16))
    gates_t = (pl.BlockSpec((1, N_GATES, tm), lambda b, t: (b, 0, t)),
               jax.ShapeDtypeStruct((bsz, N_GATES, t_len), F32))
    outs = [tok(D_MODEL, BF16), tiles_t, gates_t]
    if full:
        outs += [tiles_t, tiles_t, tok(D_MODEL, BF16), tok(D_MODEL, BF16)]
    return pl.pallas_call(
        functools.partial(_proj_kernel, tm=tm, full=full),
        grid=(bsz, nt),
        in_specs=x_specs + [mod_spec, _const_spec(g1.shape)] + w_specs,
        out_specs=[o[0] for o in outs],
        out_shape=[o[1] for o in outs],
        scratch_shapes=[pltpu.VMEM((w_rows, D_MODEL), BF16),
                        pltpu.VMEM((stage_slots, stage_rows, D_MODEL), F32),
                        pltpu.SemaphoreType.DMA((stage_slots,))],
        compiler_params=pltpu.CompilerParams(dimension_semantics=("arbitrary", "arbitrary"),
                                             vmem_limit_bytes=VMEM_LIMIT),
        name="proj_full" if full else "proj_ctx",
    )(xs, xs, xs, mod, g1, *weights)


def _scan_lanes(v, combine, fill, forward):
    lane = lax.broadcasted_iota(jnp.int32, v.shape, 1)
    d = 1
    while d < LANES:
        if forward:
            shifted = jnp.where(lane >= d, pltpu.roll(v, d, 1), fill)
        else:
            shifted = jnp.where(lane < LANES - d, pltpu.roll(v, LANES - d, 1), fill)
        v = combine(v, shifted)
        d *= 2
    return v


def _gates_kernel(gl_ref, gc_ref, out_ref, *, n_ctx, n_lat):
    n_all = n_ctx + n_lat
    src = lax.broadcasted_iota(jnp.int32, (CHUNK, 2 * CHUNK), 0)
    dst = lax.broadcasted_iota(jnp.int32, (CHUNK, 2 * CHUNK), 1)

    def gate_rows(cg, first):
        if cg < n_ctx:
            return gc_ref[0, first:first + N_HEADS, cg * CHUNK:(cg + 1) * CHUNK]
        return gl_ref[0, first:first + N_HEADS, (cg - n_ctx) * CHUNK:(cg - n_ctx + 1) * CHUNK]

    for dirn in range(2):
        forward = dirn == 0
        if forward:
            order = list(range(n_all))
        else:
            order = list(range(n_ctx - 1, -1, -1)) + list(range(n_all - 1, n_ctx - 1, -1))
        last = LANES - 1 if forward else 0
        first_row = 2 * N_HEADS * dirn
        fg = jnp.concatenate([gate_rows(cg, first_row + N_HEADS) for cg in range(n_all)], axis=0)
        lf = jnp.minimum(fg, 0.0) - jnp.log1p(jnp.exp(-jnp.abs(fg)))
        prefix = (src <= dst) if forward else (src >= dst)
        sel = jnp.where((dst >= CHUNK) | prefix, 1.0, 0.0).astype(BF16)
        hi = lf.astype(BF16)
        rest = lf - hi.astype(F32)
        mid = rest.astype(BF16)
        low = (rest - mid.astype(F32)).astype(BF16)
        sums = _dot(hi, sel) + _dot(mid, sel) + _dot(low, sel)
        m = jnp.full((N_HEADS, 1), M_INIT, F32)
        for cg in order:
            rows = slice(cg * N_HEADS, (cg + 1) * N_HEADS)
            b = sums[rows, :CHUNK]
            b_end = sums[rows, CHUNK:CHUNK + 1]
            r = gate_rows(cg, first_row) - b
            cm = _scan_lanes(r, jnp.maximum, -jnp.inf, forward)
            mx = jnp.maximum(m, cm)
            mt = b + mx
            mx_end = jnp.maximum(m, cm[:, last:last + 1])
            cbase = C_BASE + C_PER_DIR * dirn
            fields = {
                R_ROWB + dirn: r,
                R_DECAY + dirn: jnp.broadcast_to(jnp.exp(m - mx_end), (N_HEADS, CHUNK)),
                cbase + F_COLA: -mx,
                cbase + F_AINT: jnp.exp(m - mx),
                cbase + F_ENEG: jnp.exp(-mt),
                cbase + F_WK: jnp.exp(r - mx_end),
            }
            m_new = b_end + mx_end
            for f, val in fields.items():
                for h in range(N_HEADS):
                    out_ref[0, h, cg, f:f + 1, :] = val[h:h + 1, :]
            m = m_new
    n_used = C_BASE + 2 * C_PER_DIR
    out_ref[0, :, :, n_used:, :] = jnp.zeros((N_HEADS, n_all, N_FIELDS - n_used, CHUNK), F32)


def _gates(g_lat, g_ctx):
    bsz, _, t_lat = g_lat.shape
    t_ctx = g_ctx.shape[2]
    n_all = (t_lat + t_ctx) // CHUNK
    shape = (bsz, N_HEADS, n_all, N_FIELDS, CHUNK)
    return pl.pallas_call(
        functools.partial(_gates_kernel, n_ctx=t_ctx // CHUNK, n_lat=t_lat // CHUNK),
        grid=(bsz,),
        in_specs=[pl.BlockSpec((1, N_GATES, t_lat), lambda b: (b, 0, 0)),
                  pl.BlockSpec((1, N_GATES, t_ctx), lambda b: (b, 0, 0))],
        out_specs=pl.BlockSpec((1,) + shape[1:], lambda b: (b, 0, 0, 0, 0)),
        out_shape=jax.ShapeDtypeStruct(shape, F32),
        compiler_params=pltpu.CompilerParams(dimension_semantics=("parallel",)),
        name="gates",
    )(g_lat, g_ctx)


STATE_ROWS = HEAD_DIM + BF16_ROWS
MLSTM_HEADS = 2


def _mlstm_kernel(qt_ref, kl_ref, vtl_ref, kc_ref, vtc_ref, gf_ref, ogt_ref, ng_ref, out_ref,
                  dz_s, zf_s, zb_s, z_s, p_s, *, n_ctx, n_lat):
    lane = lax.broadcasted_iota(jnp.int32, (CHUNK, CHUNK), 1)
    row = lax.broadcasted_iota(jnp.int32, (CHUNK, CHUNK), 0)
    tri = (row <= lane, row >= lane)
    tail_first = lax.broadcasted_iota(jnp.int32, (BF16_ROWS, CHUNK), 0) == 0
    ones_tail = jnp.where(tail_first, 1.0, 0.0).astype(BF16)

    for hh in range(MLSTM_HEADS):
        _mlstm_head(hh, qt_ref, kl_ref, vtl_ref, kc_ref, vtc_ref, gf_ref, ogt_ref, ng_ref, out_ref,
                    dz_s, zf_s, zb_s, z_s, p_s, tri, tail_first, ones_tail, n_ctx, n_lat)


def _mlstm_head(hh, qt_ref, kl_ref, vtl_ref, kc_ref, vtc_ref, gf_ref, ogt_ref, ng_ref, out_ref,
                dz_s, zf_s, zb_s, z_s, p_s, tri, tail_first, ones_tail, n_ctx, n_lat):
    hcols = slice(hh * HEAD_DIM, (hh + 1) * HEAD_DIM)

    def field(cg, dirn, f):
        j = C_BASE + C_PER_DIR * dirn + f
        return gf_ref[0, hh, cg, j:j + 1, :]

    def pass_a(cg, k, vt):
        vf = vt.astype(F32)
        parts = []
        for dirn in range(2):
            wk = field(cg, dirn, F_WK)
            parts += [(vf * wk).astype(BF16), jnp.where(tail_first, wk, 0.0).astype(BF16)]
        dz_s[cg] = _dot(jnp.concatenate(parts, axis=0), k)

    for cg in range(n_ctx):
        pass_a(cg, kc_ref[0, cg * CHUNK:(cg + 1) * CHUNK, hcols], vtc_ref[0, hh, cg])

    def pass_a_lat(i, carry):
        sl = pl.ds(pl.multiple_of(i * CHUNK, CHUNK), CHUNK)
        pass_a(n_ctx + i, kl_ref[0, sl, hcols], vtl_ref[0, hh, i])
        return carry

    lax.fori_loop(0, n_lat, pass_a_lat, 0, unroll=16)

    def advance(dirn, cg):
        dec = gf_ref[0, hh, cg, R_DECAY + dirn:R_DECAY + dirn + 1, :]
        z_s[dirn] = dec * z_s[dirn] + dz_s[cg, dirn * STATE_ROWS:(dirn + 1) * STATE_ROWS, :]

    z_s[...] = jnp.zeros_like(z_s)
    for cg in range(n_ctx):
        advance(0, cg)
    for cg in range(n_ctx - 1, -1, -1):
        advance(1, cg)

    def pass_b(i, carry):
        zf_s[i] = z_s[0].astype(BF16)
        advance(0, n_ctx + i)
        j = n_lat - 1 - i
        zb_s[j] = z_s[1].astype(BF16)
        advance(1, n_ctx + j)
        return carry

    lax.fori_loop(0, n_lat, pass_b, 0, unroll=4)

    gain = jnp.broadcast_to(ng_ref[:, hcols], (CHUNK, HEAD_DIM)).T

    def pass_c1(i, carry):
        cg = n_ctx + i
        sl = pl.ds(pl.multiple_of(i * CHUNK, CHUNK), CHUNK)
        st = _dot(kl_ref[0, sl, hcols], qt_ref[0, hh, i])
        rowb = gf_ref[0, hh, cg].T
        for dirn in range(2):
            dmat = rowb[:, R_ROWB + dirn:R_ROWB + dirn + 1] + field(cg, dirn, F_COLA)
            w = jnp.exp(jnp.where(tri[dirn], dmat, -jnp.inf))
            p_s[i, dirn] = (st * w).astype(BF16)
        return carry

    lax.fori_loop(0, n_lat, pass_c1, 0, unroll=16)

    def pass_c2(i, carry):
        cg = n_ctx + i
        qtf = qt_ref[0, hh, i].astype(F32)
        vaug = jnp.concatenate([vtl_ref[0, hh, i], ones_tail], axis=0)
        ht = None
        for dirn, z_ref in enumerate((zf_s, zb_s)):
            rhs = jnp.concatenate([(qtf * field(cg, dirn, F_AINT)).astype(BF16), p_s[i, dirn]], axis=0)
            res = _dot(jnp.concatenate([z_ref[i], vaug], axis=1), rhs)
            den = jnp.maximum(jnp.abs(res[HEAD_DIM:HEAD_DIM + 1, :]), field(cg, dirn, F_ENEG))
            hd = res[:HEAD_DIM, :] / den
            ht = hd if ht is None else ht + hd
        ht = ht * lax.rsqrt(jnp.mean(ht * ht, axis=0, keepdims=True) + EPS)
        out_ref[0, hh, i] = (ht * gain * ogt_ref[0, hh, i].astype(F32)).astype(BF16)
        return carry

    lax.fori_loop(0, n_lat, pass_c2, 0, unroll=16)


def _mlstm(qt_l, k_l, vt_l, k_c, vt_c, gf, ogt, ng):
    bsz, t_lat, _ = k_l.shape
    t_ctx = k_c.shape[1]
    n_lat, n_ctx = t_lat // CHUNK, t_ctx // CHUNK
    n_all = n_lat + n_ctx
    hb = MLSTM_HEADS
    head_lat = pl.BlockSpec((1, t_lat, hb * HEAD_DIM), lambda b, h: (b, 0, h))
    head_ctx = pl.BlockSpec((1, t_ctx, hb * HEAD_DIM), lambda b, h: (b, 0, h))
    tiles = lambda n: pl.BlockSpec((1, hb, n, HEAD_DIM, CHUNK), lambda b, h: (b, h, 0, 0, 0))
    return pl.pallas_call(
        functools.partial(_mlstm_kernel, n_ctx=n_ctx, n_lat=n_lat),
        grid=(bsz, N_HEADS // hb),
        in_specs=[tiles(n_lat), head_lat, tiles(n_lat), head_ctx, tiles(n_ctx),
                  pl.BlockSpec((1, hb, n_all, N_FIELDS, CHUNK), lambda b, h: (b, h, 0, 0, 0)),
                  tiles(n_lat),
                  pl.BlockSpec((1, hb * HEAD_DIM), lambda b, h: (0, h))],
        out_specs=tiles(n_lat),
        out_shape=jax.ShapeDtypeStruct(qt_l.shape, BF16),
        scratch_shapes=[pltpu.VMEM((n_all, 2 * STATE_ROWS, HEAD_DIM), F32),
                        pltpu.VMEM((n_lat, STATE_ROWS, HEAD_DIM), BF16),
                        pltpu.VMEM((n_lat, STATE_ROWS, HEAD_DIM), BF16),
                        pltpu.VMEM((2, STATE_ROWS, HEAD_DIM), F32),
                        pltpu.VMEM((n_lat, 2, CHUNK, CHUNK), BF16)],
        compiler_params=pltpu.CompilerParams(dimension_semantics=("parallel", "parallel"),
                                             vmem_limit_bytes=VMEM_LIMIT),
        name="mlstm",
    )(qt_l, k_l, vt_l, k_c, vt_c, gf, ogt, ng)


def _merge_kernel(hgt_ref, gm_ref, zc_ref, x_ref, mod_ref, g2_ref, wmo_ref, wo_ref, x1_ref, h2_ref):
    per_sub = MERGE_SUB // CHUNK
    subs = [slice(i * MERGE_SUB, (i + 1) * MERGE_SUB) for i in range(hgt_ref.shape[2] // per_sub)]
    hg = [jnp.concatenate(
        [jnp.concatenate([hgt_ref[0, h, ci].astype(F32).T for h in range(N_HEADS)], axis=1)
         for ci in range(i * per_sub, (i + 1) * per_sub)], axis=0).astype(BF16) for i in range(len(subs))]
    ym = [_dot(hg_i, wmo_ref[...]) for hg_i in hg]
    y = [_dot((gm_ref[0, sl, :].astype(F32) * ym_i + zc_ref[0, sl, :].astype(F32)).astype(BF16), wo_ref[...])
         for sl, ym_i in zip(subs, ym)]
    for sl, y_i in zip(subs, y):
        x1 = x_ref[0, sl, :] + mod_ref[0, 2:3, :] * y_i
        x1_ref[0, sl, :] = x1
        h2 = _rms_scale(x1) * g2_ref[...] * (1.0 + mod_ref[0, 4:5, :]) + mod_ref[0, 3:4, :]
        h2_ref[0, sl, :] = h2.astype(BF16)


def _merge(hgt, gm, zc, x, mod, g2, wmo, wo):
    bsz, t_len, _ = x.shape
    tm = MERGE_TM
    tok = pl.BlockSpec((1, tm, D_MODEL), lambda b, t: (b, t, 0))
    tiles = pl.BlockSpec((1, N_HEADS, tm // CHUNK, HEAD_DIM, CHUNK), lambda b, t: (b, 0, t, 0, 0))
    return pl.pallas_call(
        _merge_kernel,
        grid=(bsz, t_len // tm),
        in_specs=[tiles, tok, tok, tok, pl.BlockSpec((1, 8, D_MODEL), lambda b, t: (b, 0, 0)),
                  _const_spec(g2.shape), _const_spec(wmo.shape), _const_spec(wo.shape)],
        out_specs=[tok, tok],
        out_shape=[jax.ShapeDtypeStruct(x.shape, F32), jax.ShapeDtypeStruct(x.shape, BF16)],
        compiler_params=pltpu.CompilerParams(dimension_semantics=("parallel", "parallel"),
                                             vmem_limit_bytes=VMEM_LIMIT),
        name="merge",
    )(hgt, gm, zc, x, mod, g2, wmo, wo)


def _gelu_tanh(v):
    return 0.5 * v * (1.0 + jnp.tanh(0.7978845608028654 * (v + 0.044715 * (v * v * v))))


def _stage_weight(src_hbm, dst_s, stage_s, sem, n_parts, part):
    slots = stage_s.shape[0]

    def copy(c):
        return pltpu.make_async_copy(src_hbm.at[(0,) + part(c)], stage_s.at[c % slots], sem.at[c % slots])

    for c in range(min(slots, n_parts)):
        copy(c).start()
    for c in range(n_parts):
        copy(c).wait()
        dst_s[part(c)] = stage_s[c % slots].astype(BF16)
        if c + slots < n_parts:
            copy(c + slots).start()


def _ffn_kernel(hm_ref, hp_ref, hn_ref, x1_ref, mod_ref, wu_hbm, cw_ref, cb_ref, wd_hbm, fg_ref,
                out_ref, act_s, wu_ref, wd_ref, stage_u, stage_d, sem_u, sem_d, *, tm):
    t = pl.program_id(1)
    nt = pl.num_programs(1)

    @pl.when((pl.program_id(0) == 0) & (t == 0))
    def _():
        for src, dst, stage, sem in ((wu_hbm, wu_ref, stage_u, sem_u), (wd_hbm, wd_ref, stage_d, sem_d)):
            rows = stage.shape[1]
            _stage_weight(src, dst, stage, sem, dst.shape[0] // rows,
                          lambda c, rows=rows: (pl.ds(c * rows, rows), slice(None)))

    n = tm + 2 * GRID_W
    hm = hm_ref[0]
    hp = jnp.where(t > 0, hp_ref[0], jnp.zeros_like(hp_ref[0]))
    hn = jnp.where(t < nt - 1, hn_ref[0], jnp.zeros_like(hn_ref[0]))
    he = jnp.concatenate([hp, hm, hn], axis=0)
    gcol = lax.broadcasted_iota(jnp.int32, (n, 1), 0) & (GRID_W - 1)
    has_left = gcol != 0
    has_right = gcol != GRID_W - 1
    n_chunks = FF_HIDDEN // FFN_CW
    cols = lambda j: slice(j * FFN_CW, (j + 1) * FFN_CW)
    gate_cols = lambda j: slice(FF_HIDDEN + j * FFN_CW, FF_HIDDEN + (j + 1) * FFN_CW)
    up = lambda j: (_dot(he, wu_ref[:, cols(j)]), _dot(hm, wu_ref[:, gate_cols(j)]))
    acc = None
    nxt = up(0)
    for j in range(n_chunks):
        cs = cols(j)
        a, g = nxt
        if j + 1 < n_chunks:
            nxt = up(j + 1)
        taps = (jnp.where(has_left, pltpu.roll(a, 1, 0), 0.0), a,
                jnp.where(has_right, pltpu.roll(a, n - 1, 0), 0.0))
        conv = cb_ref[:, cs]
        for dr in range(3):
            for dc in range(3):
                conv = conv + cw_ref[3 * dr + dc: 3 * dr + dc + 1, cs] * taps[dc][GRID_W * dr: GRID_W * dr + tm]
        act_s[:, cs] = (_gelu_tanh(conv) * g).astype(BF16)
        if (j + 1) % FFN_DOWN_GROUP == 0 or j + 1 == n_chunks:
            gs = slice((j // FFN_DOWN_GROUP) * FFN_DOWN_GROUP * FFN_CW, (j + 1) * FFN_CW)
            part = _dot(act_s[:, gs], wd_ref[gs, :])
            acc = part if acc is None else acc + part
    x2 = x1_ref[0] + mod_ref[0, 5:6, :] * acc
    out_ref[0] = _rms_scale(x2) * fg_ref[...]


def _ffn(h2, x1, mod, wu, cw, cb, wd, fg):
    bsz, t_len, _ = x1.shape
    tm = FFN_TM
    rb = tm // GRID_W
    nrb = t_len // GRID_W
    tok = pl.BlockSpec((1, tm, D_MODEL), lambda b, t: (b, t, 0))
    return pl.pallas_call(
        functools.partial(_ffn_kernel, tm=tm),
        grid=(bsz, t_len // tm),
        in_specs=[tok,
                  pl.BlockSpec((1, GRID_W, D_MODEL), lambda b, t: (b, jnp.maximum(t * rb - 1, 0), 0)),
                  pl.BlockSpec((1, GRID_W, D_MODEL), lambda b, t: (b, jnp.minimum((t + 1) * rb, nrb - 1), 0)),
                  tok, pl.BlockSpec((1, 8, D_MODEL), lambda b, t: (b, 0, 0)),
                  pl.BlockSpec(memory_space=pl.ANY), _const_spec(cw.shape), _const_spec(cb.shape),
                  pl.BlockSpec(memory_space=pl.ANY), _const_spec(fg.shape)],
        out_specs=tok,
        out_shape=jax.ShapeDtypeStruct(x1.shape, F32),
        scratch_shapes=[pltpu.VMEM((tm, FF_HIDDEN), BF16),
                        pltpu.VMEM(wu.shape[1:], BF16),
                        pltpu.VMEM(wd.shape[1:], BF16),
                        pltpu.VMEM((STAGE_SLOTS, FFN_STAGE_ROWS_UP, wu.shape[2]), F32),
                        pltpu.VMEM((STAGE_SLOTS, FFN_STAGE_ROWS_DOWN, wd.shape[2]), F32),
                        pltpu.SemaphoreType.DMA((STAGE_SLOTS,)), pltpu.SemaphoreType.DMA((STAGE_SLOTS,))],
        compiler_params=pltpu.CompilerParams(dimension_semantics=("arbitrary", "arbitrary"),
                                             vmem_limit_bytes=VMEM_LIMIT),
        name="ffn",
    )(h2, h2, h2, x1, mod, wu, cw, cb, wd, fg)


def kernel(x, c, ctx, c_ctx, ada_w, ada_b, norm1_g, norm2_g, w_in, qk_conv_w, qk_conv_b, gate_b, mnorm_g,
           w_m_out, sc_conv_w, sc_conv_b, w_c_out, w_o, w_up, ff_conv_w, ff_conv_b, w_down, final_g):
    assert ada_w.shape[0] == 1, "single-layer block"
    bsz, t_lat, _ = x.shape
    t_ctx = ctx.shape[1]

    cc = jnp.zeros((2 * 8, D_MODEL), F32).at[:bsz].set(c).at[bsz].set(c_ctx)
    mod = _ada(cc, ada_w[0], ada_b[0][None, :])
    mod_x = jnp.pad(mod[:bsz].reshape(bsz, 6, D_MODEL), ((0, 0), (0, 2), (0, 0)))
    mod_c = jnp.pad(mod[bsz].reshape(1, 6, D_MODEL), ((0, 0), (0, 2), (0, 0)))

    cast = lambda a: a.astype(BF16)
    w_t = jnp.swapaxes(w_in, 1, 2)
    gb = gate_b[0][:, None]
    g1 = norm1_g[0][None, :]
    ctx_w = [w_t, gb, qk_conv_w[0], qk_conv_b[0][None, :]]
    lat_w = ctx_w + [sc_conv_w[0], sc_conv_b[0][None, :], cast(w_c_out[0])]

    k_l, vt_l, g_l, qt_l, ogt, gm, zc = _proj(x, mod_x, g1, lat_w, full=True, per_batch_mod=True)
    k_c, vt_c, g_c = _proj(ctx, mod_c, g1, ctx_w, full=False, per_batch_mod=False)

    gf = _gates(g_l, g_c)
    hgt = _mlstm(qt_l, k_l, vt_l, k_c, vt_c, gf, ogt, mnorm_g[0][None, :])

    x1, h2 = _merge(hgt, gm, zc, x, mod_x, norm2_g[0][None, :], cast(w_m_out[0]), cast(w_o[0]))

    out = _ffn(h2, x1, mod_x, w_up,
               ff_conv_w[0].reshape(9, FF_HIDDEN), ff_conv_b[0][None, :], w_down,
               final_g[None, :])
    return out
```

```python
import functools

import jax
import jax.numpy as jnp
from jax import lax
from jax.experimental import pallas as pl
from jax.experimental.pallas import tpu as pltpu

F32 = jnp.float32
BF16 = jnp.bfloat16

D_MODEL = 1024
N_HEADS = 8
HEAD_DIM = D_MODEL // N_HEADS
CHUNK = 128
GRID_W = 64
FF_HIDDEN = 2816
EPS = 1e-6
M_INIT = -1e30
N_GATES = 4 * N_HEADS

LANES = 128
BF16_ROWS = 16
VMEM_LIMIT = 56 * 1024 * 1024

PROJ_TM = 256
STAGE_SLOTS = 4
PROJ_STAGE_ROWS = 272
PROJ_STAGE_ROWS_CTX = 1552
SEQ_HALO = 8
MERGE_TM = 1024
MERGE_SUB = 256
FFN_TM = 512
FFN_CW = 256
FFN_DOWN_GROUP = 4
FFN_STAGE_ROWS_UP = 64
FFN_STAGE_ROWS_DOWN = 128

R_ROWB, R_DECAY = 0, 2
C_BASE, C_PER_DIR = 4, 4
F_COLA, F_AINT, F_ENEG, F_WK = range(4)
N_FIELDS = 16


def _const_spec(shape):
    nd = len(shape)
    return pl.BlockSpec(shape, lambda *_: (0,) * nd, pipeline_mode=pl.Buffered(1))


def _sigmoid(v):
    return 0.5 * jnp.tanh(0.5 * v) + 0.5


def _silu(v):
    h = 0.5 * v
    return h + h * jnp.tanh(h)


def _rms_scale(v):
    return v * lax.rsqrt(jnp.mean(v * v, axis=-1, keepdims=True) + EPS)


def _dot(a, b):
    return jnp.dot(a, b, preferred_element_type=F32)


def _ada_kernel(c_ref, w_ref, b_ref, o_ref):
    cv = c_ref[...]
    s = _silu(cv)
    w = w_ref[...]
    s_hi, w_hi = s.astype(BF16), w.astype(BF16)
    s_lo = (s - s_hi.astype(F32)).astype(BF16)
    w_lo = (w - w_hi.astype(F32)).astype(BF16)
    rows = s.shape[0]
    main = _dot(jnp.concatenate([s_hi, s_lo], axis=0), w_hi)
    o_ref[...] = main[:rows] + main[rows:] + _dot(s_hi, w_lo) + b_ref[...]


def _ada(cc, w, b):
    rows, n = cc.shape[0], w.shape[1]
    tn = 1024
    return pl.pallas_call(
        _ada_kernel,
        grid=(n // tn,),
        in_specs=[pl.BlockSpec((rows, D_MODEL), lambda j: (0, 0)),
                  pl.BlockSpec((D_MODEL, tn), lambda j: (0, j)),
                  pl.BlockSpec((1, tn), lambda j: (0, j))],
        out_specs=pl.BlockSpec((rows, tn), lambda j: (0, j)),
        out_shape=jax.ShapeDtypeStruct((rows, n), F32),
        name="ada",
    )(cc, w, b)


def _conv3_rows(p, w_ref, b_ref, tm):
    main = p[:tm]
    prev_row = p[tm + SEQ_HALO - 1:tm + SEQ_HALO]
    next_row = p[tm + SEQ_HALO:tm + SEQ_HALO + 1]
    sub = lax.broadcasted_iota(jnp.int32, (SEQ_HALO, 1), 0)
    down = pltpu.roll(main, 1, 0)
    up = pltpu.roll(main, tm - 1, 0)
    left = jnp.concatenate([jnp.where(sub == 0, prev_row, down[:SEQ_HALO]), down[SEQ_HALO:]], axis=0)
    right = jnp.concatenate([up[:tm - SEQ_HALO], jnp.where(sub == SEQ_HALO - 1, next_row, up[tm - SEQ_HALO:])],
                            axis=0)
    return left * w_ref[0:1, :] + main * w_ref[1:2, :] + right * w_ref[2:3, :] + b_ref[...]


def _store_head_tiles_t(ref, val, tm):
    for h in range(N_HEADS):
        for ci in range(tm // CHUNK):
            tile = val[ci * CHUNK:(ci + 1) * CHUNK, h * HEAD_DIM:(h + 1) * HEAD_DIM]
            ref[0, h, ci] = tile.T.astype(BF16)


W_QK = slice(0, 2 * D_MODEL)
W_V = slice(2 * D_MODEL, 3 * D_MODEL)
W_G = slice(3 * D_MODEL, 3 * D_MODEL + N_GATES)
W_CTX_ROWS = W_G.stop
W_O = slice(W_G.stop, W_G.stop + D_MODEL)
W_SB = slice(W_O.stop, W_O.stop + D_MODEL)
W_SCX = slice(W_SB.stop, W_SB.stop + 2 * D_MODEL)
W_MG = slice(W_SCX.stop, W_SCX.stop + 2 * D_MODEL)


def _dot_nt(a, b):
    return lax.dot_general(a, b, (((1,), (1,)), ((), ())), preferred_element_type=F32)


def _proj_kernel(*refs, tm, full):
    w_ref, stage_s, sem = refs[-3:]
    if full:
        (xm_ref, xp_ref, xn_ref, mod_ref, g1_ref, w_hbm, gb_ref, qkw_ref, qkb_ref,
         scw_ref, scb_ref, wco_ref,
         k_ref, vt_ref, g_ref, qt_ref, ogt_ref, gm_ref, zc_ref) = refs[:-3]
    else:
        (xm_ref, xp_ref, xn_ref, mod_ref, g1_ref, w_hbm, gb_ref, qkw_ref, qkb_ref,
         k_ref, vt_ref, g_ref) = refs[:-3]
    t = pl.program_id(1)
    nt = pl.num_programs(1)

    @pl.when((pl.program_id(0) == 0) & (t == 0))
    def _():
        rows = stage_s.shape[1]
        _stage_weight(w_hbm, w_ref, stage_s, sem, w_ref.shape[0] // rows,
                      lambda c: (pl.ds(c * rows, rows), slice(None)))

    shift = mod_ref[0, 0:1, :]
    scale1 = 1.0 + mod_ref[0, 1:2, :]
    gain = g1_ref[...]

    def norm_mod(xv):
        return _rms_scale(xv) * gain * scale1 + shift

    hm = norm_mod(xm_ref[0])
    hp = jnp.where(t > 0, norm_mod(xp_ref[0]), 0.0)
    hn = jnp.where(t < nt - 1, norm_mod(xn_ref[0]), 0.0)
    hmb = hm.astype(BF16)
    he = jnp.concatenate([hm, hp, hn], axis=0).astype(BF16)

    pqk = _dot_nt(he, w_ref[W_QK, :])
    pv = _dot_nt(hmb, w_ref[W_V, :])
    qk = _conv3_rows(pqk, qkw_ref, qkb_ref, tm)
    qk = _silu(qk)
    k_ref[0] = (qk[:, D_MODEL:] * (HEAD_DIM ** -0.5)).astype(BF16)
    pg = _dot_nt(w_ref[W_G, :], hmb)
    if not full:
        _store_head_tiles_t(vt_ref, pv, tm)
        g_ref[0] = pg + gb_ref[...]
        return
    po = _dot_nt(hmb, w_ref[W_O, :])
    _store_head_tiles_t(qt_ref, qk[:, :D_MODEL], tm)
    _store_head_tiles_t(vt_ref, pv, tm)
    g_ref[0] = pg + gb_ref[...]
    pcx = _dot_nt(he, w_ref[W_SCX, :])
    _store_head_tiles_t(ogt_ref, _sigmoid(po), tm)
    psb = _dot_nt(hmb, w_ref[W_SB, :])
    cu = _conv3_rows(pcx[:, :D_MODEL] * pcx[:, D_MODEL:], scw_ref, scb_ref, tm)
    pm = _dot_nt(hmb, w_ref[W_MG, :])
    yc = _dot((psb * cu).astype(BF16), wco_ref[...])
    gm_ref[0] = _sigmoid(pm[:, :D_MODEL]).astype(BF16)
    zc_ref[0] = (_sigmoid(pm[:, D_MODEL:]) * yc).astype(BF16)


def _proj(xs, mod, g1, weights, *, full, per_batch_mod):
    bsz, t_len, _ = xs.shape
    tm = min(PROJ_TM, t_len)
    nt = t_len // tm
    hb = tm // SEQ_HALO
    nhb = t_len // SEQ_HALO
    x_specs = [
        pl.BlockSpec((1, tm, D_MODEL), lambda b, t: (b, t, 0)),
        pl.BlockSpec((1, SEQ_HALO, D_MODEL), lambda b, t: (b, jnp.maximum(t * hb - 1, 0), 0)),
        pl.BlockSpec((1, SEQ_HALO, D_MODEL), lambda b, t: (b, jnp.minimum((t + 1) * hb, nhb - 1), 0)),
    ]
    mod_spec = pl.BlockSpec((1, 8, D_MODEL), (lambda b, t: (b, 0, 0)) if per_batch_mod else (lambda b, t: (0, 0, 0)))
    w_rows = weights[0].shape[1] if full else W_CTX_ROWS
    stage_rows, stage_slots = (PROJ_STAGE_ROWS, STAGE_SLOTS) if full else (PROJ_STAGE_ROWS_CTX, 2)
    assert w_rows % stage_rows == 0 and stage_rows % BF16_ROWS == 0
    w_specs = [pl.BlockSpec(memory_space=pl.ANY)] + [_const_spec(w.shape) for w in weights[1:]]
    tok = lambda n, dt: (pl.BlockSpec((1, tm, n), lambda b, t: (b, t, 0)),
                         jax.ShapeDtypeStruct((bsz, t_len, n), dt))
    tiles_t = (pl.BlockSpec((1, N_HEADS, tm // CHUNK, HEAD_DIM, CHUNK), lambda b, t: (b, 0, t, 0, 0)),
               jax.ShapeDtypeStruct((bsz, N_HEADS, t_len // CHUNK, HEAD_DIM, CHUNK), BF16))
    gates_t = (pl.BlockSpec((1, N_GATES, tm), lambda b, t: (b, 0, t)),
               jax.ShapeDtypeStruct((bsz, N_GATES, t_len), F32))
    outs = [tok(D_MODEL, BF16), tiles_t, gates_t]
    if full:
        outs += [tiles_t, tiles_t, tok(D_MODEL, BF16), tok(D_MODEL, BF16)]
    return pl.pallas_call(
        functools.partial(_proj_kernel, tm=tm, full=full),
        grid=(bsz, nt),
        in_specs=x_specs + [mod_spec, _const_spec(g1.shape)] + w_specs,
        out_specs=[o[0] for o in outs],
        out_shape=[o[1] for o in outs],
        scratch_shapes=[pltpu.VMEM((w_rows, D_MODEL), BF16),
                        pltpu.VMEM((stage_slots, stage_rows, D_MODEL), F32),
                        pltpu.SemaphoreType.DMA((stage_slots,))],
        compiler_params=pltpu.CompilerParams(dimension_semantics=("arbitrary", "arbitrary"),
                                             vmem_limit_bytes=VMEM_LIMIT),
        name="proj_full" if full else "proj_ctx",
    )(xs, xs, xs, mod, g1, *weights)


def _scan_lanes(v, combine, fill, forward):
    lane = lax.broadcasted_iota(jnp.int32, v.shape, 1)
    d = 1
    while d < LANES:
        if forward:
            shifted = jnp.where(lane >= d, pltpu.roll(v, d, 1), fill)
        else:
            shifted = jnp.where(lane < LANES - d, pltpu.roll(v, LANES - d, 1), fill)
        v = combine(v, shifted)
        d *= 2
    return v


def _gates_kernel(gl_ref, gc_ref, out_ref, *, n_ctx, n_lat):
    n_all = n_ctx + n_lat
    src = lax.broadcasted_iota(jnp.int32, (CHUNK, 2 * CHUNK), 0)
    dst = lax.broadcasted_iota(jnp.int32, (CHUNK, 2 * CHUNK), 1)

    def gate_rows(cg, first):
        if cg < n_ctx:
            return gc_ref[0, first:first + N_HEADS, cg * CHUNK:(cg + 1) * CHUNK]
        return gl_ref[0, first:first + N_HEADS, (cg - n_ctx) * CHUNK:(cg - n_ctx + 1) * CHUNK]

    for dirn in range(2):
        forward = dirn == 0
        if forward:
            order = list(range(n_all))
        else:
            order = list(range(n_ctx - 1, -1, -1)) + list(range(n_all - 1, n_ctx - 1, -1))
        last = LANES - 1 if forward else 0
        first_row = 2 * N_HEADS * dirn
        fg = jnp.concatenate([gate_rows(cg, first_row + N_HEADS) for cg in range(n_all)], axis=0)
        lf = jnp.minimum(fg, 0.0) - jnp.log1p(jnp.exp(-jnp.abs(fg)))
        prefix = (src <= dst) if forward else (src >= dst)
        sel = jnp.where((dst >= CHUNK) | prefix, 1.0, 0.0).astype(BF16)
        hi = lf.astype(BF16)
        rest = lf - hi.astype(F32)
        mid = rest.astype(BF16)
        low = (rest - mid.astype(F32)).astype(BF16)
        sums = _dot(hi, sel) + _dot(mid, sel) + _dot(low, sel)
        m = jnp.full((N_HEADS, 1), M_INIT, F32)
        for cg in order:
            rows = slice(cg * N_HEADS, (cg + 1) * N_HEADS)
            b = sums[rows, :CHUNK]
            b_end = sums[rows, CHUNK:CHUNK + 1]
            r = gate_rows(cg, first_row) - b
            cm = _scan_lanes(r, jnp.maximum, -jnp.inf, forward)
            mx = jnp.maximum(m, cm)
            mt = b + mx
            mx_end = jnp.maximum(m, cm[:, last:last + 1])
            cbase = C_BASE + C_PER_DIR * dirn
            fields = {
                R_ROWB + dirn: r,
                R_DECAY + dirn: jnp.broadcast_to(jnp.exp(m - mx_end), (N_HEADS, CHUNK)),
                cbase + F_COLA: -mx,
                cbase + F_AINT: jnp.exp(m - mx),
                cbase + F_ENEG: jnp.exp(-mt),
                cbase + F_WK: jnp.exp(r - mx_end),
            }
            m_new = b_end + mx_end
            for f, val in fields.items():
                for h in range(N_HEADS):
                    out_ref[0, h, cg, f:f + 1, :] = val[h:h + 1, :]
            m = m_new
    n_used = C_BASE + 2 * C_PER_DIR
    out_ref[0, :, :, n_used:, :] = jnp.zeros((N_HEADS, n_all, N_FIELDS - n_used, CHUNK), F32)


def _gates(g_lat, g_ctx):
    bsz, _, t_lat = g_lat.shape
    t_ctx = g_ctx.shape[2]
    n_all = (t_lat + t_ctx) // CHUNK
    shape = (bsz, N_HEADS, n_all, N_FIELDS, CHUNK)
    return pl.pallas_call(
        functools.partial(_gates_kernel, n_ctx=t_ctx // CHUNK, n_lat=t_lat // CHUNK),
        grid=(bsz,),
        in_specs=[pl.BlockSpec((1, N_GATES, t_lat), lambda b: (b, 0, 0)),
                  pl.BlockSpec((1, N_GATES, t_ctx), lambda b: (b, 0, 0))],
        out_specs=pl.BlockSpec((1,) + shape[1:], lambda b: (b, 0, 0, 0, 0)),
        out_shape=jax.ShapeDtypeStruct(shape, F32),
        compiler_params=pltpu.CompilerParams(dimension_semantics=("parallel",)),
        name="gates",
    )(g_lat, g_ctx)


STATE_ROWS = HEAD_DIM + BF16_ROWS
MLSTM_HEADS = 2


def _mlstm_kernel(qt_ref, kl_ref, vtl_ref, kc_ref, vtc_ref, gf_ref, ogt_ref, ng_ref, out_ref,
                  dz_s, zf_s, zb_s, z_s, p_s, *, n_ctx, n_lat):
    lane = lax.broadcasted_iota(jnp.int32, (CHUNK, CHUNK), 1)
    row = lax.broadcasted_iota(jnp.int32, (CHUNK, CHUNK), 0)
    tri = (row <= lane, row >= lane)
    tail_first = lax.broadcasted_iota(jnp.int32, (BF16_ROWS, CHUNK), 0) == 0
    ones_tail = jnp.where(tail_first, 1.0, 0.0).astype(BF16)

    for hh in range(MLSTM_HEADS):
        _mlstm_head(hh, qt_ref, kl_ref, vtl_ref, kc_ref, vtc_ref, gf_ref, ogt_ref, ng_ref, out_ref,
                    dz_s, zf_s, zb_s, z_s, p_s, tri, tail_first, ones_tail, n_ctx, n_lat)


def _mlstm_head(hh, qt_ref, kl_ref, vtl_ref, kc_ref, vtc_ref, gf_ref, ogt_ref, ng_ref, out_ref,
                dz_s, zf_s, zb_s, z_s, p_s, tri, tail_first, ones_tail, n_ctx, n_lat):
    hcols = slice(hh * HEAD_DIM, (hh + 1) * HEAD_DIM)

    def field(cg, dirn, f):
        j = C_BASE + C_PER_DIR * dirn + f
        return gf_ref[0, hh, cg, j:j + 1, :]

    def pass_a(cg, k, vt):
        vf = vt.astype(F32)
        parts = []
        for dirn in range(2):
            wk = field(cg, dirn, F_WK)
            parts += [(vf * wk).astype(BF16), jnp.where(tail_first, wk, 0.0).astype(BF16)]
        dz_s[cg] = _dot(jnp.concatenate(parts, axis=0), k)

    for cg in range(n_ctx):
        pass_a(cg, kc_ref[0, cg * CHUNK:(cg + 1) * CHUNK, hcols], vtc_ref[0, hh, cg])

    def pass_a_lat(i, carry):
        sl = pl.ds(pl.multiple_of(i * CHUNK, CHUNK), CHUNK)
        pass_a(n_ctx + i, kl_ref[0, sl, hcols], vtl_ref[0, hh, i])
        return carry

    lax.fori_loop(0, n_lat, pass_a_lat, 0, unroll=16)

    def advance(dirn, cg):
        dec = gf_ref[0, hh, cg, R_DECAY + dirn:R_DECAY + dirn + 1, :]
        z_s[dirn] = dec * z_s[dirn] + dz_s[cg, dirn * STATE_ROWS:(dirn + 1) * STATE_ROWS, :]

    z_s[...] = jnp.zeros_like(z_s)
    for cg in range(n_ctx):
        advance(0, cg)
    for cg in range(n_ctx - 1, -1, -1):
        advance(1, cg)

    def pass_b(i, carry):
        zf_s[i] = z_s[0].astype(BF16)
        advance(0, n_ctx + i)
        j = n_lat - 1 - i
        zb_s[j] = z_s[1].astype(BF16)
        advance(1, n_ctx + j)
        return carry

    lax.fori_loop(0, n_lat, pass_b, 0, unroll=16)

    gain = jnp.broadcast_to(ng_ref[:, hcols], (CHUNK, HEAD_DIM)).T

    def pass_c1(i, carry):
        cg = n_ctx + i
        sl = pl.ds(pl.multiple_of(i * CHUNK, CHUNK), CHUNK)
        st = _dot(kl_ref[0, sl, hcols], qt_ref[0, hh, i])
        rowb = gf_ref[0, hh, cg].T
        for dirn in range(2):
            dmat = rowb[:, R_ROWB + dirn:R_ROWB + dirn + 1] + field(cg, dirn, F_COLA)
            w = jnp.exp(jnp.where(tri[dirn], dmat, -jnp.inf))
            p_s[i, dirn] = (st * w).astype(BF16)
        return carry

    lax.fori_loop(0, n_lat, pass_c1, 0, unroll=16)

    def pass_c2(i, carry):
        cg = n_ctx + i
        qtf = qt_ref[0, hh, i].astype(F32)
        vaug = jnp.concatenate([vtl_ref[0, hh, i], ones_tail], axis=0)
        ht = None
        for dirn, z_ref in enumerate((zf_s, zb_s)):
            rhs = jnp.concatenate([(qtf * field(cg, dirn, F_AINT)).astype(BF16), p_s[i, dirn]], axis=0)
            res = _dot(jnp.concatenate([z_ref[i], vaug], axis=1), rhs)
            den = jnp.maximum(jnp.abs(res[HEAD_DIM:HEAD_DIM + 1, :]), field(cg, dirn, F_ENEG))
            hd = res[:HEAD_DIM, :] / den
            ht = hd if ht is None else ht + hd
        ht = ht * lax.rsqrt(jnp.mean(ht * ht, axis=0, keepdims=True) + EPS)
        out_ref[0, hh, i] = (ht * gain * ogt_ref[0, hh, i].astype(F32)).astype(BF16)
        return carry

    lax.fori_loop(0, n_lat, pass_c2, 0, unroll=16)


def _mlstm(qt_l, k_l, vt_l, k_c, vt_c, gf, ogt, ng):
    bsz, t_lat, _ = k_l.shape
    t_ctx = k_c.shape[1]
    n_lat, n_ctx = t_lat // CHUNK, t_ctx // CHUNK
    n_all = n_lat + n_ctx
    hb = MLSTM_HEADS
    head_lat = pl.BlockSpec((1, t_lat, hb * HEAD_DIM), lambda b, h: (b, 0, h))
    head_ctx = pl.BlockSpec((1, t_ctx, hb * HEAD_DIM), lambda b, h: (b, 0, h))
    tiles = lambda n: pl.BlockSpec((1, hb, n, HEAD_DIM, CHUNK), lambda b, h: (b, h, 0, 0, 0))
    return pl.pallas_call(
        functools.partial(_mlstm_kernel, n_ctx=n_ctx, n_lat=n_lat),
        grid=(bsz, N_HEADS // hb),
        in_specs=[tiles(n_lat), head_lat, tiles(n_lat), head_ctx, tiles(n_ctx),
                  pl.BlockSpec((1, hb, n_all, N_FIELDS, CHUNK), lambda b, h: (b, h, 0, 0, 0)),
                  tiles(n_lat),
                  pl.BlockSpec((1, hb * HEAD_DIM), lambda b, h: (0, h))],
        out_specs=tiles(n_lat),
        out_shape=jax.ShapeDtypeStruct(qt_l.shape, BF16),
        scratch_shapes=[pltpu.VMEM((n_all, 2 * STATE_ROWS, HEAD_DIM), F32),
                        pltpu.VMEM((n_lat, STATE_ROWS, HEAD_DIM), BF16),
                        pltpu.VMEM((n_lat, STATE_ROWS, HEAD_DIM), BF16),
                        pltpu.VMEM((2, STATE_ROWS, HEAD_DIM), F32),
                        pltpu.VMEM((n_lat, 2, CHUNK, CHUNK), BF16)],
        compiler_params=pltpu.CompilerParams(dimension_semantics=("parallel", "parallel"),
                                             vmem_limit_bytes=VMEM_LIMIT),
        name="mlstm",
    )(qt_l, k_l, vt_l, k_c, vt_c, gf, ogt, ng)


def _merge_kernel(hgt_ref, gm_ref, zc_ref, x_ref, mod_ref, g2_ref, wmo_ref, wo_ref, x1_ref, h2_ref):
    per_sub = MERGE_SUB // CHUNK
    subs = [slice(i * MERGE_SUB, (i + 1) * MERGE_SUB) for i in range(hgt_ref.shape[2] // per_sub)]
    hg = [jnp.concatenate(
        [jnp.concatenate([hgt_ref[0, h, ci].astype(F32).T for h in range(N_HEADS)], axis=1)
         for ci in range(i * per_sub, (i + 1) * per_sub)], axis=0).astype(BF16) for i in range(len(subs))]
    ym = [_dot(hg_i, wmo_ref[...]) for hg_i in hg]
    y = [_dot((gm_ref[0, sl, :].astype(F32) * ym_i + zc_ref[0, sl, :].astype(F32)).astype(BF16), wo_ref[...])
         for sl, ym_i in zip(subs, ym)]
    for sl, y_i in zip(subs, y):
        x1 = x_ref[0, sl, :] + mod_ref[0, 2:3, :] * y_i
        x1_ref[0, sl, :] = x1
        h2 = _rms_scale(x1) * g2_ref[...] * (1.0 + mod_ref[0, 4:5, :]) + mod_ref[0, 3:4, :]
        h2_ref[0, sl, :] = h2.astype(BF16)


def _merge(hgt, gm, zc, x, mod, g2, wmo, wo):
    bsz, t_len, _ = x.shape
    tm = MERGE_TM
    tok = pl.BlockSpec((1, tm, D_MODEL), lambda b, t: (b, t, 0))
    tiles = pl.BlockSpec((1, N_HEADS, tm // CHUNK, HEAD_DIM, CHUNK), lambda b, t: (b, 0, t, 0, 0))
    return pl.pallas_call(
        _merge_kernel,
        grid=(bsz, t_len // tm),
        in_specs=[tiles, tok, tok, tok, pl.BlockSpec((1, 8, D_MODEL), lambda b, t: (b, 0, 0)),
                  _const_spec(g2.shape), _const_spec(wmo.shape), _const_spec(wo.shape)],
        out_specs=[tok, tok],
        out_shape=[jax.ShapeDtypeStruct(x.shape, F32), jax.ShapeDtypeStruct(x.shape, BF16)],
        compiler_params=pltpu.CompilerParams(dimension_semantics=("parallel", "parallel"),
                                             vmem_limit_bytes=VMEM_LIMIT),
        name="merge",
    )(hgt, gm, zc, x, mod, g2, wmo, wo)


def _gelu_tanh(v):
    return 0.5 * v * (1.0 + jnp.tanh(0.7978845608028654 * (v + 0.044715 * (v * v * v))))


def _stage_weight(src_hbm, dst_s, stage_s, sem, n_parts, part):
    slots = stage_s.shape[0]

    def copy(c):
        return pltpu.make_async_copy(src_hbm.at[(0,) + part(c)], stage_s.at[c % slots], sem.at[c % slots])

    for c in range(min(slots, n_parts)):
        copy(c).start()
    for c in range(n_parts):
        copy(c).wait()
        dst_s[part(c)] = stage_s[c % slots].astype(BF16)
        if c + slots < n_parts:
            copy(c + slots).start()


def _ffn_kernel(hm_ref, hp_ref, hn_ref, x1_ref, mod_ref, wu_hbm, cw_ref, cb_ref, wd_hbm, fg_ref,
                out_ref, act_s, wu_ref, wd_ref, stage_u, stage_d, sem_u, sem_d, *, tm):
    t = pl.program_id(1)
    nt = pl.num_programs(1)

    @pl.when((pl.program_id(0) == 0) & (t == 0))
    def _():
        for src, dst, stage, sem in ((wu_hbm, wu_ref, stage_u, sem_u), (wd_hbm, wd_ref, stage_d, sem_d)):
            rows = stage.shape[1]
            _stage_weight(src, dst, stage, sem, dst.shape[0] // rows,
                          lambda c, rows=rows: (pl.ds(c * rows, rows), slice(None)))

    n = tm + 2 * GRID_W
    hm = hm_ref[0]
    hp = jnp.where(t > 0, hp_ref[0], jnp.zeros_like(hp_ref[0]))
    hn = jnp.where(t < nt - 1, hn_ref[0], jnp.zeros_like(hn_ref[0]))
    he = jnp.concatenate([hp, hm, hn], axis=0)
    gcol = lax.broadcasted_iota(jnp.int32, (n, 1), 0) & (GRID_W - 1)
    has_left = gcol != 0
    has_right = gcol != GRID_W - 1
    n_chunks = FF_HIDDEN // FFN_CW
    cols = lambda j: slice(j * FFN_CW, (j + 1) * FFN_CW)
    gate_cols = lambda j: slice(FF_HIDDEN + j * FFN_CW, FF_HIDDEN + (j + 1) * FFN_CW)
    up = lambda j: (_dot(he, wu_ref[:, cols(j)]), _dot(hm, wu_ref[:, gate_cols(j)]))
    acc = None
    nxt = up(0)
    for j in range(n_chunks):
        cs = cols(j)
        a, g = nxt
        if j + 1 < n_chunks:
            nxt = up(j + 1)
        taps = (jnp.where(has_left, pltpu.roll(a, 1, 0), 0.0), a,
                jnp.where(has_right, pltpu.roll(a, n - 1, 0), 0.0))
        conv = cb_ref[:, cs]
        for dr in range(3):
            for dc in range(3):
                conv = conv + cw_ref[3 * dr + dc: 3 * dr + dc + 1, cs] * taps[dc][GRID_W * dr: GRID_W * dr + tm]
        act_s[:, cs] = (_gelu_tanh(conv) * g).astype(BF16)
        if (j + 1) % FFN_DOWN_GROUP == 0 or j + 1 == n_chunks:
            gs = slice((j // FFN_DOWN_GROUP) * FFN_DOWN_GROUP * FFN_CW, (j + 1) * FFN_CW)
            part = _dot(act_s[:, gs], wd_ref[gs, :])
            acc = part if acc is None else acc + part
    x2 = x1_ref[0] + mod_ref[0, 5:6, :] * acc
    out_ref[0] = _rms_scale(x2) * fg_ref[...]


def _ffn(h2, x1, mod, wu, cw, cb, wd, fg):
    bsz, t_len, _ = x1.shape
    tm = FFN_TM
    rb = tm // GRID_W
    nrb = t_len // GRID_W
    tok = pl.BlockSpec((1, tm, D_MODEL), lambda b, t: (b, t, 0))
    return pl.pallas_call(
        functools.partial(_ffn_kernel, tm=tm),
        grid=(bsz, t_len // tm),
        in_specs=[tok,
                  pl.BlockSpec((1, GRID_W, D_MODEL), lambda b, t: (b, jnp.maximum(t * rb - 1, 0), 0)),
                  pl.BlockSpec((1, GRID_W, D_MODEL), lambda b, t: (b, jnp.minimum((t + 1) * rb, nrb - 1), 0)),
                  tok, pl.BlockSpec((1, 8, D_MODEL), lambda b, t: (b, 0, 0)),
                  pl.BlockSpec(memory_space=pl.ANY), _const_spec(cw.shape), _const_spec(cb.shape),
                  pl.BlockSpec(memory_space=pl.ANY), _const_spec(fg.shape)],
        out_specs=tok,
        out_shape=jax.ShapeDtypeStruct(x1.shape, F32),
        scratch_shapes=[pltpu.VMEM((tm, FF_HIDDEN), BF16),
                        pltpu.VMEM(wu.shape[1:], BF16),
                        pltpu.VMEM(wd.shape[1:], BF16),
                        pltpu.VMEM((STAGE_SLOTS, FFN_STAGE_ROWS_UP, wu.shape[2]), F32),
                        pltpu.VMEM((STAGE_SLOTS, FFN_STAGE_ROWS_DOWN, wd.shape[2]), F32),
                        pltpu.SemaphoreType.DMA((STAGE_SLOTS,)), pltpu.SemaphoreType.DMA((STAGE_SLOTS,))],
        compiler_params=pltpu.CompilerParams(dimension_semantics=("arbitrary", "arbitrary"),
                                             vmem_limit_bytes=VMEM_LIMIT),
        name="ffn",
    )(h2, h2, h2, x1, mod, wu, cw, cb, wd, fg)


def kernel(x, c, ctx, c_ctx, ada_w, ada_b, norm1_g, norm2_g, w_in, qk_conv_w, qk_conv_b, gate_b, mnorm_g,
           w_m_out, sc_conv_w, sc_conv_b, w_c_out, w_o, w_up, ff_conv_w, ff_conv_b, w_down, final_g):
    assert ada_w.shape[0] == 1, "single-layer block"
    bsz, t_lat, _ = x.shape
    t_ctx = ctx.shape[1]

    cc = jnp.zeros((2 * 8, D_MODEL), F32).at[:bsz].set(c).at[bsz].set(c_ctx)
    mod = _ada(cc, ada_w[0], ada_b[0][None, :])
    mod_x = jnp.pad(mod[:bsz].reshape(bsz, 6, D_MODEL), ((0, 0), (0, 2), (0, 0)))
    mod_c = jnp.pad(mod[bsz].reshape(1, 6, D_MODEL), ((0, 0), (0, 2), (0, 0)))

    cast = lambda a: a.astype(BF16)
    w_t = jnp.swapaxes(w_in, 1, 2)
    gb = gate_b[0][:, None]
    g1 = norm1_g[0][None, :]
    ctx_w = [w_t, gb, qk_conv_w[0], qk_conv_b[0][None, :]]
    lat_w = ctx_w + [sc_conv_w[0], sc_conv_b[0][None, :], cast(w_c_out[0])]

    k_l, vt_l, g_l, qt_l, ogt, gm, zc = _proj(x, mod_x, g1, lat_w, full=True, per_batch_mod=True)
    k_c, vt_c, g_c = _proj(ctx, mod_c, g1, ctx_w, full=False, per_batch_mod=False)

    gf = _gates(g_l, g_c)
    hgt = _mlstm(qt_l, k_l, vt_l, k_c, vt_c, gf, ogt, mnorm_g[0][None, :])

    x1, h2 = _merge(hgt, gm, zc, x, mod_x, norm2_g[0][None, :], cast(w_m_out[0]), cast(w_o[0]))

    out = _ffn(h2, x1, mod_x, w_up,
               ff_conv_w[0].reshape(9, FF_HIDDEN), ff_conv_b[0][None, :], w_down,
               final_g[None, :])
    return out
```

```python
import functools

import jax
import jax.numpy as jnp
from jax import lax
from jax.experimental import pallas as pl
from jax.experimental.pallas import tpu as pltpu

F32 = jnp.float32
BF16 = jnp.bfloat16

D_MODEL = 1024
N_HEADS = 8
HEAD_DIM = D_MODEL // N_HEADS
CHUNK = 128
GRID_W = 64
FF_HIDDEN = 2816
EPS = 1e-6
M_INIT = -1e30
N_GATES = 4 * N_HEADS

LANES = 128
BF16_ROWS = 16
VMEM_LIMIT = 56 * 1024 * 1024

PROJ_TM = 512
STAGE_SLOTS = 4
PROJ_STAGE_ROWS = 272
PROJ_STAGE_ROWS_CTX = 1552
SEQ_HALO = 8
MERGE_TM = 1024
MERGE_SUB = 256
FFN_TM = 512
FFN_CW = 256
FFN_DOWN_GROUP = 4
FFN_STAGE_ROWS_UP = 64
FFN_STAGE_ROWS_DOWN = 128

R_ROWB, R_DECAY = 0, 2
C_BASE, C_PER_DIR = 4, 4
F_COLA, F_AINT, F_ENEG, F_WK = range(4)
N_FIELDS = 16


def _const_spec(shape):
    nd = len(shape)
    return pl.BlockSpec(shape, lambda *_: (0,) * nd, pipeline_mode=pl.Buffered(1))


def _sigmoid(v):
    return 0.5 * jnp.tanh(0.5 * v) + 0.5


def _silu(v):
    h = 0.5 * v
    return h + h * jnp.tanh(h)


def _rms_scale(v):
    return v * lax.rsqrt(jnp.mean(v * v, axis=-1, keepdims=True) + EPS)


def _dot(a, b):
    return jnp.dot(a, b, preferred_element_type=F32)


def _ada_kernel(c_ref, w_ref, b_ref, o_ref):
    cv = c_ref[...]
    s = _silu(cv)
    w = w_ref[...]
    s_hi, w_hi = s.astype(BF16), w.astype(BF16)
    s_lo = (s - s_hi.astype(F32)).astype(BF16)
    w_lo = (w - w_hi.astype(F32)).astype(BF16)
    rows = s.shape[0]
    main = _dot(jnp.concatenate([s_hi, s_lo], axis=0), w_hi)
    o_ref[...] = main[:rows] + main[rows:] + _dot(s_hi, w_lo) + b_ref[...]


def _ada(cc, w, b):
    rows, n = cc.shape[0], w.shape[1]
    tn = 1024
    return pl.pallas_call(
        _ada_kernel,
        grid=(n // tn,),
        in_specs=[pl.BlockSpec((rows, D_MODEL), lambda j: (0, 0)),
                  pl.BlockSpec((D_MODEL, tn), lambda j: (0, j)),
                  pl.BlockSpec((1, tn), lambda j: (0, j))],
        out_specs=pl.BlockSpec((rows, tn), lambda j: (0, j)),
        out_shape=jax.ShapeDtypeStruct((rows, n), F32),
        name="ada",
    )(cc, w, b)


def _conv3_rows(p, w_ref, b_ref, tm):
    main = p[:tm]
    prev_row = p[tm + SEQ_HALO - 1:tm + SEQ_HALO]
    next_row = p[tm + SEQ_HALO:tm + SEQ_HALO + 1]
    sub = lax.broadcasted_iota(jnp.int32, (SEQ_HALO, 1), 0)
    down = pltpu.roll(main, 1, 0)
    up = pltpu.roll(main, tm - 1, 0)
    left = jnp.concatenate([jnp.where(sub == 0, prev_row, down[:SEQ_HALO]), down[SEQ_HALO:]], axis=0)
    right = jnp.concatenate([up[:tm - SEQ_HALO], jnp.where(sub == SEQ_HALO - 1, next_row, up[tm - SEQ_HALO:])],
                            axis=0)
    return left * w_ref[0:1, :] + main * w_ref[1:2, :] + right * w_ref[2:3, :] + b_ref[...]


def _store_head_tiles_t(ref, val, tm):
    for h in range(N_HEADS):
        for ci in range(tm // CHUNK):
            tile = val[ci * CHUNK:(ci + 1) * CHUNK, h * HEAD_DIM:(h + 1) * HEAD_DIM]
            ref[0, h, ci] = tile.T.astype(BF16)


W_QK = slice(0, 2 * D_MODEL)
W_V = slice(2 * D_MODEL, 3 * D_MODEL)
W_G = slice(3 * D_MODEL, 3 * D_MODEL + N_GATES)
W_CTX_ROWS = W_G.stop
W_O = slice(W_G.stop, W_G.stop + D_MODEL)
W_SB = slice(W_O.stop, W_O.stop + D_MODEL)
W_SCX = slice(W_SB.stop, W_SB.stop + 2 * D_MODEL)
W_MG = slice(W_SCX.stop, W_SCX.stop + 2 * D_MODEL)


def _dot_nt(a, b):
    return lax.dot_general(a, b, (((1,), (1,)), ((), ())), preferred_element_type=F32)


def _proj_kernel(*refs, tm, full):
    w_ref, stage_s, sem = refs[-3:]
    if full:
        (xm_ref, xp_ref, xn_ref, mod_ref, g1_ref, w_hbm, gb_ref, qkw_ref, qkb_ref,
         scw_ref, scb_ref, wco_ref,
         k_ref, vt_ref, g_ref, qt_ref, ogt_ref, gm_ref, zc_ref) = refs[:-3]
    else:
        (xm_ref, xp_ref, xn_ref, mod_ref, g1_ref, w_hbm, gb_ref, qkw_ref, qkb_ref,
         k_ref, vt_ref, g_ref) = refs[:-3]
    t = pl.program_id(1)
    nt = pl.num_programs(1)

    @pl.when((pl.program_id(0) == 0) & (t == 0))
    def _():
        rows = stage_s.shape[1]
        _stage_weight(w_hbm, w_ref, stage_s, sem, w_ref.shape[0] // rows,
                      lambda c: (pl.ds(c * rows, rows), slice(None)))

    shift = mod_ref[0, 0:1, :]
    scale1 = 1.0 + mod_ref[0, 1:2, :]
    gain = g1_ref[...]

    def norm_mod(xv):
        return _rms_scale(xv) * gain * scale1 + shift

    hm = norm_mod(xm_ref[0])
    hp = jnp.where(t > 0, norm_mod(xp_ref[0]), 0.0)
    hn = jnp.where(t < nt - 1, norm_mod(xn_ref[0]), 0.0)
    hmb = hm.astype(BF16)
    he = jnp.concatenate([hm, hp, hn], axis=0).astype(BF16)

    pqk = _dot_nt(he, w_ref[W_QK, :])
    pv = _dot_nt(hmb, w_ref[W_V, :])
    qk = _conv3_rows(pqk, qkw_ref, qkb_ref, tm)
    qk = _silu(qk)
    k_ref[0] = (qk[:, D_MODEL:] * (HEAD_DIM ** -0.5)).astype(BF16)
    pg = _dot_nt(w_ref[W_G, :], hmb)
    if not full:
        _store_head_tiles_t(vt_ref, pv, tm)
        g_ref[0] = pg + gb_ref[...]
        return
    po = _dot_nt(hmb, w_ref[W_O, :])
    _store_head_tiles_t(qt_ref, qk[:, :D_MODEL], tm)
    _store_head_tiles_t(vt_ref, pv, tm)
    g_ref[0] = pg + gb_ref[...]
    pcx = _dot_nt(he, w_ref[W_SCX, :])
    _store_head_tiles_t(ogt_ref, _sigmoid(po), tm)
    psb = _dot_nt(hmb, w_ref[W_SB, :])
    cu = _conv3_rows(pcx[:, :D_MODEL] * pcx[:, D_MODEL:], scw_ref, scb_ref, tm)
    pm = _dot_nt(hmb, w_ref[W_MG, :])
    yc = _dot((psb * cu).astype(BF16), wco_ref[...])
    gm_ref[0] = _sigmoid(pm[:, :D_MODEL]).astype(BF16)
    zc_ref[0] = (_sigmoid(pm[:, D_MODEL:]) * yc).astype(BF16)


def _proj(xs, mod, g1, weights, *, full, per_batch_mod):
    bsz, t_len, _ = xs.shape
    tm = min(PROJ_TM, t_len)
    nt = t_len // tm
    hb = tm // SEQ_HALO
    nhb = t_len // SEQ_HALO
    x_specs = [
        pl.BlockSpec((1, tm, D_MODEL), lambda b, t: (b, t, 0)),
        pl.BlockSpec((1, SEQ_HALO, D_MODEL), lambda b, t: (b, jnp.maximum(t * hb - 1, 0), 0)),
        pl.BlockSpec((1, SEQ_HALO, D_MODEL), lambda b, t: (b, jnp.minimum((t + 1) * hb, nhb - 1), 0)),
    ]
    mod_spec = pl.BlockSpec((1, 8, D_MODEL), (lambda b, t: (b, 0, 0)) if per_batch_mod else (lambda b, t: (0, 0, 0)))
    w_rows = weights[0].shape[1] if full else W_CTX_ROWS
    stage_rows, stage_slots = (PROJ_STAGE_ROWS, STAGE_SLOTS) if full else (PROJ_STAGE_ROWS_CTX, 2)
    assert w_rows % stage_rows == 0 and stage_rows % BF16_ROWS == 0
    w_specs = [pl.BlockSpec(memory_space=pl.ANY)] + [_const_spec(w.shape) for w in weights[1:]]
    tok = lambda n, dt: (pl.BlockSpec((1, tm, n), lambda b, t: (b, t, 0)),
                         jax.ShapeDtypeStruct((bsz, t_len, n), dt))
    tiles_t = (pl.BlockSpec((1, N_HEADS, tm // CHUNK, HEAD_DIM, CHUNK), lambda b, t: (b, 0, t, 0, 0)),
               jax.ShapeDtypeStruct((bsz, N_HEADS, t_len // CHUNK, HEAD_DIM, CHUNK), BF16))
    gates_t = (pl.BlockSpec((1, N_GATES, tm), lambda b, t: (b, 0, t)),
               jax.ShapeDtypeStruct((bsz, N_GATES, t_len), F32))
    outs = [tok(D_MODEL, BF16), tiles_t, gates_t]
    if full:
        outs += [tiles_t, tiles_t, tok(D_MODEL, BF16), tok(D_MODEL, BF16)]
    return pl.pallas_call(
        functools.partial(_proj_kernel, tm=tm, full=full),
        grid=(bsz, nt),
        in_specs=x_specs + [mod_spec, _const_spec(g1.shape)] + w_specs,
        out_specs=[o[0] for o in outs],
        out_shape=[o[1] for o in outs],
        scratch_shapes=[pltpu.VMEM((w_rows, D_MODEL), BF16),
                        pltpu.VMEM((stage_slots, stage_rows, D_MODEL), F32),
                        pltpu.SemaphoreType.DMA((stage_slots,))],
        compiler_params=pltpu.CompilerParams(dimension_semantics=("arbitrary", "arbitrary"),
                                             vmem_limit_bytes=VMEM_LIMIT),
        name="proj_full" if full else "proj_ctx",
    )(xs, xs, xs, mod, g1, *weights)


def _scan_lanes(v, combine, fill, forward):
    lane = lax.broadcasted_iota(jnp.int32, v.shape, 1)
    d = 1
    while d < LANES:
        if forward:
            shifted = jnp.where(lane >= d, pltpu.roll(v, d, 1), fill)
        else:
            shifted = jnp.where(lane < LANES - d, pltpu.roll(v, LANES - d, 1), fill)
        v = combine(v, shifted)
        d *= 2
    return v


def _gates_kernel(gl_ref, gc_ref, out_ref, *, n_ctx, n_lat):
    n_all = n_ctx + n_lat
    src = lax.broadcasted_iota(jnp.int32, (CHUNK, 2 * CHUNK), 0)
    dst = lax.broadcasted_iota(jnp.int32, (CHUNK, 2 * CHUNK), 1)

    def gate_rows(cg, first):
        if cg < n_ctx:
            return gc_ref[0, first:first + N_HEADS, cg * CHUNK:(cg + 1) * CHUNK]
        return gl_ref[0, first:first + N_HEADS, (cg - n_ctx) * CHUNK:(cg - n_ctx + 1) * CHUNK]

    for dirn in range(2):
        forward = dirn == 0
        if forward:
            order = list(range(n_all))
        else:
            order = list(range(n_ctx - 1, -1, -1)) + list(range(n_all - 1, n_ctx - 1, -1))
        last = LANES - 1 if forward else 0
        first_row = 2 * N_HEADS * dirn
        fg = jnp.concatenate([gate_rows(cg, first_row + N_HEADS) for cg in range(n_all)], axis=0)
        lf = jnp.minimum(fg, 0.0) - jnp.log1p(jnp.exp(-jnp.abs(fg)))
        prefix = (src <= dst) if forward else (src >= dst)
        sel = jnp.where((dst >= CHUNK) | prefix, 1.0, 0.0).astype(BF16)
        hi = lf.astype(BF16)
        rest = lf - hi.astype(F32)
        mid = rest.astype(BF16)
        low = (rest - mid.astype(F32)).astype(BF16)
        sums = _dot(hi, sel) + _dot(mid, sel) + _dot(low, sel)
        m = jnp.full((N_HEADS, 1), M_INIT, F32)
        for cg in order:
            rows = slice(cg * N_HEADS, (cg + 1) * N_HEADS)
            b = sums[rows, :CHUNK]
            b_end = sums[rows, CHUNK:CHUNK + 1]
            r = gate_rows(cg, first_row) - b
            cm = _scan_lanes(r, jnp.maximum, -jnp.inf, forward)
            mx = jnp.maximum(m, cm)
            mt = b + mx
            mx_end = jnp.maximum(m, cm[:, last:last + 1])
            cbase = C_BASE + C_PER_DIR * dirn
            fields = {
                R_ROWB + dirn: r,
                R_DECAY + dirn: jnp.broadcast_to(jnp.exp(m - mx_end), (N_HEADS, CHUNK)),
                cbase + F_COLA: -mx,
                cbase + F_AINT: jnp.exp(m - mx),
                cbase + F_ENEG: jnp.exp(-mt),
                cbase + F_WK: jnp.exp(r - mx_end),
            }
            m_new = b_end + mx_end
            for f, val in fields.items():
                for h in range(N_HEADS):
                    out_ref[0, h, cg, f:f + 1, :] = val[h:h + 1, :]
            m = m_new
    n_used = C_BASE + 2 * C_PER_DIR
    out_ref[0, :, :, n_used:, :] = jnp.zeros((N_HEADS, n_all, N_FIELDS - n_used, CHUNK), F32)


def _gates(g_lat, g_ctx):
    bsz, _, t_lat = g_lat.shape
    t_ctx = g_ctx.shape[2]
    n_all = (t_lat + t_ctx) // CHUNK
    shape = (bsz, N_HEADS, n_all, N_FIELDS, CHUNK)
    return pl.pallas_call(
        functools.partial(_gates_kernel, n_ctx=t_ctx // CHUNK, n_lat=t_lat // CHUNK),
        grid=(bsz,),
        in_specs=[pl.BlockSpec((1, N_GATES, t_lat), lambda b: (b, 0, 0)),
                  pl.BlockSpec((1, N_GATES, t_ctx), lambda b: (b, 0, 0))],
        out_specs=pl.BlockSpec((1,) + shape[1:], lambda b: (b, 0, 0, 0, 0)),
        out_shape=jax.ShapeDtypeStruct(shape, F32),
        compiler_params=pltpu.CompilerParams(dimension_semantics=("parallel",)),
        name="gates",
    )(g_lat, g_ctx)


STATE_ROWS = HEAD_DIM + BF16_ROWS
MLSTM_HEADS = 2


def _mlstm_kernel(qt_ref, kl_ref, vtl_ref, kc_ref, vtc_ref, gf_ref, ogt_ref, ng_ref, out_ref,
                  dz_s, zf_s, zb_s, z_s, p_s, *, n_ctx, n_lat):
    lane = lax.broadcasted_iota(jnp.int32, (CHUNK, CHUNK), 1)
    row = lax.broadcasted_iota(jnp.int32, (CHUNK, CHUNK), 0)
    tri = (row <= lane, row >= lane)
    tail_first = lax.broadcasted_iota(jnp.int32, (BF16_ROWS, CHUNK), 0) == 0
    ones_tail = jnp.where(tail_first, 1.0, 0.0).astype(BF16)

    for hh in range(MLSTM_HEADS):
        _mlstm_head(hh, qt_ref, kl_ref, vtl_ref, kc_ref, vtc_ref, gf_ref, ogt_ref, ng_ref, out_ref,
                    dz_s, zf_s, zb_s, z_s, p_s, tri, tail_first, ones_tail, n_ctx, n_lat)


def _mlstm_head(hh, qt_ref, kl_ref, vtl_ref, kc_ref, vtc_ref, gf_ref, ogt_ref, ng_ref, out_ref,
                dz_s, zf_s, zb_s, z_s, p_s, tri, tail_first, ones_tail, n_ctx, n_lat):
    hcols = slice(hh * HEAD_DIM, (hh + 1) * HEAD_DIM)

    def field(cg, dirn, f):
        j = C_BASE + C_PER_DIR * dirn + f
        return gf_ref[0, hh, cg, j:j + 1, :]

    def pass_a(cg, k, vt):
        vf = vt.astype(F32)
        parts = []
        for dirn in range(2):
            wk = field(cg, dirn, F_WK)
            parts += [(vf * wk).astype(BF16), jnp.where(tail_first, wk, 0.0).astype(BF16)]
        dz_s[cg] = _dot(jnp.concatenate(parts, axis=0), k)

    for cg in range(n_ctx):
        pass_a(cg, kc_ref[0, cg * CHUNK:(cg + 1) * CHUNK, hcols], vtc_ref[0, hh, cg])

    def pass_a_lat(i, carry):
        sl = pl.ds(pl.multiple_of(i * CHUNK, CHUNK), CHUNK)
        pass_a(n_ctx + i, kl_ref[0, sl, hcols], vtl_ref[0, hh, i])
        return carry

    lax.fori_loop(0, n_lat, pass_a_lat, 0, unroll=16)

    def advance(dirn, cg):
        dec = gf_ref[0, hh, cg, R_DECAY + dirn:R_DECAY + dirn + 1, :]
        z_s[dirn] = dec * z_s[dirn] + dz_s[cg, dirn * STATE_ROWS:(dirn + 1) * STATE_ROWS, :]

    z_s[...] = jnp.zeros_like(z_s)
    for cg in range(n_ctx):
        advance(0, cg)
    for cg in range(n_ctx - 1, -1, -1):
        advance(1, cg)

    def pass_b(i, carry):
        zf_s[i] = z_s[0].astype(BF16)
        advance(0, n_ctx + i)
        j = n_lat - 1 - i
        zb_s[j] = z_s[1].astype(BF16)
        advance(1, n_ctx + j)
        return carry

    lax.fori_loop(0, n_lat, pass_b, 0, unroll=16)

    gain = jnp.broadcast_to(ng_ref[:, hcols], (CHUNK, HEAD_DIM)).T

    def pass_c1(i, carry):
        cg = n_ctx + i
        sl = pl.ds(pl.multiple_of(i * CHUNK, CHUNK), CHUNK)
        st = _dot(kl_ref[0, sl, hcols], qt_ref[0, hh, i])
        rowb = gf_ref[0, hh, cg].T
        for dirn in range(2):
            dmat = rowb[:, R_ROWB + dirn:R_ROWB + dirn + 1] + field(cg, dirn, F_COLA)
            w = jnp.exp(jnp.where(tri[dirn], dmat, -jnp.inf))
            p_s[i, dirn] = (st * w).astype(BF16)
        return carry

    lax.fori_loop(0, n_lat, pass_c1, 0, unroll=16)

    def pass_c2(i, carry):
        cg = n_ctx + i
        qtf = qt_ref[0, hh, i].astype(F32)
        vaug = jnp.concatenate([vtl_ref[0, hh, i], ones_tail], axis=0)
        ht = None
        for dirn, z_ref in enumerate((zf_s, zb_s)):
            rhs = jnp.concatenate([(qtf * field(cg, dirn, F_AINT)).astype(BF16), p_s[i, dirn]], axis=0)
            res = _dot(jnp.concatenate([z_ref[i], vaug], axis=1), rhs)
            den = jnp.maximum(jnp.abs(res[HEAD_DIM:HEAD_DIM + 1, :]), field(cg, dirn, F_ENEG))
            hd = res[:HEAD_DIM, :] / den
            ht = hd if ht is None else ht + hd
        ht = ht * lax.rsqrt(jnp.mean(ht * ht, axis=0, keepdims=True) + EPS)
        out_ref[0, hh, i] = (ht * gain * ogt_ref[0, hh, i].astype(F32)).astype(BF16)
        return carry

    lax.fori_loop(0, n_lat, pass_c2, 0, unroll=16)


def _mlstm(qt_l, k_l, vt_l, k_c, vt_c, gf, ogt, ng):
    bsz, t_lat, _ = k_l.shape
    t_ctx = k_c.shape[1]
    n_lat, n_ctx = t_lat // CHUNK, t_ctx // CHUNK
    n_all = n_lat + n_ctx
    hb = MLSTM_HEADS
    head_lat = pl.BlockSpec((1, t_lat, hb * HEAD_DIM), lambda b, h: (b, 0, h))
    head_ctx = pl.BlockSpec((1, t_ctx, hb * HEAD_DIM), lambda b, h: (b, 0, h))
    tiles = lambda n: pl.BlockSpec((1, hb, n, HEAD_DIM, CHUNK), lambda b, h: (b, h, 0, 0, 0))
    return pl.pallas_call(
        functools.partial(_mlstm_kernel, n_ctx=n_ctx, n_lat=n_lat),
        grid=(bsz, N_HEADS // hb),
        in_specs=[tiles(n_lat), head_lat, tiles(n_lat), head_ctx, tiles(n_ctx),
                  pl.BlockSpec((1, hb, n_all, N_FIELDS, CHUNK), lambda b, h: (b, h, 0, 0, 0)),
                  tiles(n_lat),
                  pl.BlockSpec((1, hb * HEAD_DIM), lambda b, h: (0, h))],
        out_specs=tiles(n_lat),
        out_shape=jax.ShapeDtypeStruct(qt_l.shape, BF16),
        scratch_shapes=[pltpu.VMEM((n_all, 2 * STATE_ROWS, HEAD_DIM), F32),
                        pltpu.VMEM((n_lat, STATE_ROWS, HEAD_DIM), BF16),
                        pltpu.VMEM((n_lat, STATE_ROWS, HEAD_DIM), BF16),
                        pltpu.VMEM((2, STATE_ROWS, HEAD_DIM), F32),
                        pltpu.VMEM((n_lat, 2, CHUNK, CHUNK), BF16)],
        compiler_params=pltpu.CompilerParams(dimension_semantics=("parallel", "parallel"),
                                             vmem_limit_bytes=VMEM_LIMIT),
        name="mlstm",
    )(qt_l, k_l, vt_l, k_c, vt_c, gf, ogt, ng)


def _merge_kernel(hgt_ref, gm_ref, zc_ref, x_ref, mod_ref, g2_ref, wmo_ref, wo_ref, x1_ref, h2_ref):
    per_sub = MERGE_SUB // CHUNK
    subs = [slice(i * MERGE_SUB, (i + 1) * MERGE_SUB) for i in range(hgt_ref.shape[2] // per_sub)]
    hg = [jnp.concatenate(
        [jnp.concatenate([hgt_ref[0, h, ci].astype(F32).T for h in range(N_HEADS)], axis=1)
         for ci in range(i * per_sub, (i + 1) * per_sub)], axis=0).astype(BF16) for i in range(len(subs))]
    ym = [_dot(hg_i, wmo_ref[...]) for hg_i in hg]
    y = [_dot((gm_ref[0, sl, :].astype(F32) * ym_i + zc_ref[0, sl, :].astype(F32)).astype(BF16), wo_ref[...])
         for sl, ym_i in zip(subs, ym)]
    for sl, y_i in zip(subs, y):
        x1 = x_ref[0, sl, :] + mod_ref[0, 2:3, :] * y_i
        x1_ref[0, sl, :] = x1
        h2 = _rms_scale(x1) * g2_ref[...] * (1.0 + mod_ref[0, 4:5, :]) + mod_ref[0, 3:4, :]
        h2_ref[0, sl, :] = h2.astype(BF16)


def _merge(hgt, gm, zc, x, mod, g2, wmo, wo):
    bsz, t_len, _ = x.shape
    tm = MERGE_TM
    tok = pl.BlockSpec((1, tm, D_MODEL), lambda b, t: (b, t, 0))
    tiles = pl.BlockSpec((1, N_HEADS, tm // CHUNK, HEAD_DIM, CHUNK), lambda b, t: (b, 0, t, 0, 0))
    return pl.pallas_call(
        _merge_kernel,
        grid=(bsz, t_len // tm),
        in_specs=[tiles, tok, tok, tok, pl.BlockSpec((1, 8, D_MODEL), lambda b, t: (b, 0, 0)),
                  _const_spec(g2.shape), _const_spec(wmo.shape), _const_spec(wo.shape)],
        out_specs=[tok, tok],
        out_shape=[jax.ShapeDtypeStruct(x.shape, F32), jax.ShapeDtypeStruct(x.shape, BF16)],
        compiler_params=pltpu.CompilerParams(dimension_semantics=("parallel", "parallel"),
                                             vmem_limit_bytes=VMEM_LIMIT),
        name="merge",
    )(hgt, gm, zc, x, mod, g2, wmo, wo)


def _gelu_tanh(v):
    return 0.5 * v * (1.0 + jnp.tanh(0.7978845608028654 * (v + 0.044715 * (v * v * v))))


def _stage_weight(src_hbm, dst_s, stage_s, sem, n_parts, part):
    slots = stage_s.shape[0]

    def copy(c):
        return pltpu.make_async_copy(src_hbm.at[(0,) + part(c)], stage_s.at[c % slots], sem.at[c % slots])

    for c in range(min(slots, n_parts)):
        copy(c).start()
    for c in range(n_parts):
        copy(c).wait()
        dst_s[part(c)] = stage_s[c % slots].astype(BF16)
        if c + slots < n_parts:
            copy(c + slots).start()


def _ffn_kernel(hm_ref, hp_ref, hn_ref, x1_ref, mod_ref, wu_hbm, cw_ref, cb_ref, wd_hbm, fg_ref,
                out_ref, act_s, wu_ref, wd_ref, stage_u, stage_d, sem_u, sem_d, *, tm):
    t = pl.program_id(1)
    nt = pl.num_programs(1)

    @pl.when((pl.program_id(0) == 0) & (t == 0))
    def _():
        for src, dst, stage, sem in ((wu_hbm, wu_ref, stage_u, sem_u), (wd_hbm, wd_ref, stage_d, sem_d)):
            rows = stage.shape[1]
            _stage_weight(src, dst, stage, sem, dst.shape[0] // rows,
                          lambda c, rows=rows: (pl.ds(c * rows, rows), slice(None)))

    n = tm + 2 * GRID_W
    hm = hm_ref[0]
    hp = jnp.where(t > 0, hp_ref[0], jnp.zeros_like(hp_ref[0]))
    hn = jnp.where(t < nt - 1, hn_ref[0], jnp.zeros_like(hn_ref[0]))
    he = jnp.concatenate([hp, hm, hn], axis=0)
    gcol = lax.broadcasted_iota(jnp.int32, (n, 1), 0) & (GRID_W - 1)
    has_left = gcol != 0
    has_right = gcol != GRID_W - 1
    n_chunks = FF_HIDDEN // FFN_CW
    cols = lambda j: slice(j * FFN_CW, (j + 1) * FFN_CW)
    gate_cols = lambda j: slice(FF_HIDDEN + j * FFN_CW, FF_HIDDEN + (j + 1) * FFN_CW)
    up = lambda j: (_dot(he, wu_ref[:, cols(j)]), _dot(hm, wu_ref[:, gate_cols(j)]))
    acc = None
    nxt = up(0)
    for j in range(n_chunks):
        cs = cols(j)
        a, g = nxt
        if j + 1 < n_chunks:
            nxt = up(j + 1)
        taps = (jnp.where(has_left, pltpu.roll(a, 1, 0), 0.0), a,
                jnp.where(has_right, pltpu.roll(a, n - 1, 0), 0.0))
        conv = cb_ref[:, cs]
        for dr in range(3):
            for dc in range(3):
                conv = conv + cw_ref[3 * dr + dc: 3 * dr + dc + 1, cs] * taps[dc][GRID_W * dr: GRID_W * dr + tm]
        act_s[:, cs] = (_gelu_tanh(conv) * g).astype(BF16)
        if (j + 1) % FFN_DOWN_GROUP == 0 or j + 1 == n_chunks:
            gs = slice((j // FFN_DOWN_GROUP) * FFN_DOWN_GROUP * FFN_CW, (j + 1) * FFN_CW)
            part = _dot(act_s[:, gs], wd_ref[gs, :])
            acc = part if acc is None else acc + part
    x2 = x1_ref[0] + mod_ref[0, 5:6, :] * acc
    out_ref[0] = _rms_scale(x2) * fg_ref[...]


def _ffn(h2, x1, mod, wu, cw, cb, wd, fg):
    bsz, t_len, _ = x1.shape
    tm = FFN_TM
    rb = tm // GRID_W
    nrb = t_len // GRID_W
    tok = pl.BlockSpec((1, tm, D_MODEL), lambda b, t: (b, t, 0))
    return pl.pallas_call(
        functools.partial(_ffn_kernel, tm=tm),
        grid=(bsz, t_len // tm),
        in_specs=[tok,
                  pl.BlockSpec((1, GRID_W, D_MODEL), lambda b, t: (b, jnp.maximum(t * rb - 1, 0), 0)),
                  pl.BlockSpec((1, GRID_W, D_MODEL), lambda b, t: (b, jnp.minimum((t + 1) * rb, nrb - 1), 0)),
                  tok, pl.BlockSpec((1, 8, D_MODEL), lambda b, t: (b, 0, 0)),
                  pl.BlockSpec(memory_space=pl.ANY), _const_spec(cw.shape), _const_spec(cb.shape),
                  pl.BlockSpec(memory_space=pl.ANY), _const_spec(fg.shape)],
        out_specs=tok,
        out_shape=jax.ShapeDtypeStruct(x1.shape, F32),
        scratch_shapes=[pltpu.VMEM((tm, FF_HIDDEN), BF16),
                        pltpu.VMEM(wu.shape[1:], BF16),
                        pltpu.VMEM(wd.shape[1:], BF16),
                        pltpu.VMEM((STAGE_SLOTS, FFN_STAGE_ROWS_UP, wu.shape[2]), F32),
                        pltpu.VMEM((STAGE_SLOTS, FFN_STAGE_ROWS_DOWN, wd.shape[2]), F32),
                        pltpu.SemaphoreType.DMA((STAGE_SLOTS,)), pltpu.SemaphoreType.DMA((STAGE_SLOTS,))],
        compiler_params=pltpu.CompilerParams(dimension_semantics=("arbitrary", "arbitrary"),
                                             vmem_limit_bytes=VMEM_LIMIT),
        name="ffn",
    )(h2, h2, h2, x1, mod, wu, cw, cb, wd, fg)


def kernel(x, c, ctx, c_ctx, ada_w, ada_b, norm1_g, norm2_g, w_in, qk_conv_w, qk_conv_b, gate_b, mnorm_g,
           w_m_out, sc_conv_w, sc_conv_b, w_c_out, w_o, w_up, ff_conv_w, ff_conv_b, w_down, final_g):
    assert ada_w.shape[0] == 1, "single-layer block"
    bsz, t_lat, _ = x.shape
    t_ctx = ctx.shape[1]

    cc = jnp.zeros((2 * 8, D_MODEL), F32).at[:bsz].set(c).at[bsz].set(c_ctx)
    mod = _ada(cc, ada_w[0], ada_b[0][None, :])
    mod_x = jnp.pad(mod[:bsz].reshape(bsz, 6, D_MODEL), ((0, 0), (0, 2), (0, 0)))
    mod_c = jnp.pad(mod[bsz].reshape(1, 6, D_MODEL), ((0, 0), (0, 2), (0, 0)))

    cast = lambda a: a.astype(BF16)
    w_t = jnp.swapaxes(w_in, 1, 2)
    gb = gate_b[0][:, None]
    g1 = norm1_g[0][None, :]
    ctx_w = [w_t, gb, qk_conv_w[0], qk_conv_b[0][None, :]]
    lat_w = ctx_w + [sc_conv_w[0], sc_conv_b[0][None, :], cast(w_c_out[0])]

    k_l, vt_l, g_l, qt_l, ogt, gm, zc = _proj(x, mod_x, g1, lat_w, full=True, per_batch_mod=True)
    k_c, vt_c, g_c = _proj(ctx, mod_c, g1, ctx_w, full=False, per_batch_mod=False)

    gf = _gates(g_l, g_c)
    hgt = _mlstm(qt_l, k_l, vt_l, k_c, vt_c, gf, ogt, mnorm_g[0][None, :])

    x1, h2 = _merge(hgt, gm, zc, x, mod_x, norm2_g[0][None, :], cast(w_m_out[0]), cast(w_o[0]))

    out = _ffn(h2, x1, mod_x, w_up,
               ff_conv_w[0].reshape(9, FF_HIDDEN), ff_conv_b[0][None, :], w_down,
               final_g[None, :])
    return out
```

```python
import functools

import jax
import jax.numpy as jnp
from jax import lax
from jax.experimental import pallas as pl
from jax.experimental.pallas import tpu as pltpu

F32 = jnp.float32
BF16 = jnp.bfloat16

D_MODEL = 1024
N_HEADS = 8
HEAD_DIM = D_MODEL // N_HEADS
CHUNK = 128
GRID_W = 64
FF_HIDDEN = 2816
EPS = 1e-6
M_INIT = -1e30
N_GATES = 4 * N_HEADS

LANES = 128
BF16_ROWS = 16
VMEM_LIMIT = 56 * 1024 * 1024

PROJ_TM = 256
STAGE_SLOTS = 4
PROJ_STAGE_ROWS = 272
PROJ_STAGE_ROWS_CTX = 1552
SEQ_HALO = 8
MERGE_TM = 1024
MERGE_SUB = 256
FFN_TM = 512
FFN_CW = 256
FFN_DOWN_GROUP = 4
FFN_STAGE_ROWS_UP = 64
FFN_STAGE_ROWS_DOWN = 128

R_ROWB, R_DECAY = 0, 2
C_BASE, C_PER_DIR = 4, 4
F_COLA, F_AINT, F_ENEG, F_WK = range(4)
N_FIELDS = 16


def _const_spec(shape):
    nd = len(shape)
    return pl.BlockSpec(shape, lambda *_: (0,) * nd, pipeline_mode=pl.Buffered(1))


def _sigmoid(v):
    return 0.5 * jnp.tanh(0.5 * v) + 0.5


def _silu(v):
    h = 0.5 * v
    return h + h * jnp.tanh(h)


def _rms_scale(v):
    return v * lax.rsqrt(jnp.mean(v * v, axis=-1, keepdims=True) + EPS)


def _dot(a, b):
    return jnp.dot(a, b, preferred_element_type=F32)


def _ada_kernel(c_ref, w_ref, b_ref, o_ref):
    cv = c_ref[...]
    s = _silu(cv)
    w = w_ref[...]
    s_hi, w_hi = s.astype(BF16), w.astype(BF16)
    s_lo = (s - s_hi.astype(F32)).astype(BF16)
    w_lo = (w - w_hi.astype(F32)).astype(BF16)
    rows = s.shape[0]
    main = _dot(jnp.concatenate([s_hi, s_lo], axis=0), w_hi)
    o_ref[...] = main[:rows] + main[rows:] + _dot(s_hi, w_lo) + b_ref[...]


def _ada(cc, w, b):
    rows, n = cc.shape[0], w.shape[1]
    tn = 1024
    return pl.pallas_call(
        _ada_kernel,
        grid=(n // tn,),
        in_specs=[pl.BlockSpec((rows, D_MODEL), lambda j: (0, 0)),
                  pl.BlockSpec((D_MODEL, tn), lambda j: (0, j)),
                  pl.BlockSpec((1, tn), lambda j: (0, j))],
        out_specs=pl.BlockSpec((rows, tn), lambda j: (0, j)),
        out_shape=jax.ShapeDtypeStruct((rows, n), F32),
        name="ada",
    )(cc, w, b)


def _conv3_rows(p, w_ref, b_ref, tm):
    main = p[:tm]
    prev_row = p[tm + SEQ_HALO - 1:tm + SEQ_HALO]
    next_row = p[tm + SEQ_HALO:tm + SEQ_HALO + 1]
    sub = lax.broadcasted_iota(jnp.int32, (SEQ_HALO, 1), 0)
    down = pltpu.roll(main, 1, 0)
    up = pltpu.roll(main, tm - 1, 0)
    left = jnp.concatenate([jnp.where(sub == 0, prev_row, down[:SEQ_HALO]), down[SEQ_HALO:]], axis=0)
    right = jnp.concatenate([up[:tm - SEQ_HALO], jnp.where(sub == SEQ_HALO - 1, next_row, up[tm - SEQ_HALO:])],
                            axis=0)
    return left * w_ref[0:1, :] + main * w_ref[1:2, :] + right * w_ref[2:3, :] + b_ref[...]


def _store_head_tiles_t(ref, val, tm):
    for h in range(N_HEADS):
        for ci in range(tm // CHUNK):
            tile = val[ci * CHUNK:(ci + 1) * CHUNK, h * HEAD_DIM:(h + 1) * HEAD_DIM]
            ref[0, h, ci] = tile.T.astype(BF16)


W_QK = slice(0, 2 * D_MODEL)
W_V = slice(2 * D_MODEL, 3 * D_MODEL)
W_G = slice(3 * D_MODEL, 3 * D_MODEL + N_GATES)
W_CTX_ROWS = W_G.stop
W_O = slice(W_G.stop, W_G.stop + D_MODEL)
W_SB = slice(W_O.stop, W_O.stop + D_MODEL)
W_SCX = slice(W_SB.stop, W_SB.stop + 2 * D_MODEL)
W_MG = slice(W_SCX.stop, W_SCX.stop + 2 * D_MODEL)


def _dot_nt(a, b):
    return lax.dot_general(a, b, (((1,), (1,)), ((), ())), preferred_element_type=F32)


def _proj_kernel(*refs, tm, full):
    w_ref, stage_s, sem = refs[-3:]
    if full:
        (xm_ref, xp_ref, xn_ref, mod_ref, g1_ref, w_hbm, gb_ref, qkw_ref, qkb_ref,
         scw_ref, scb_ref, wco_ref,
         k_ref, vt_ref, g_ref, qt_ref, ogt_ref, gm_ref, zc_ref) = refs[:-3]
    else:
        (xm_ref, xp_ref, xn_ref, mod_ref, g1_ref, w_hbm, gb_ref, qkw_ref, qkb_ref,
         k_ref, vt_ref, g_ref) = refs[:-3]
    t = pl.program_id(1)
    nt = pl.num_programs(1)

    @pl.when((pl.program_id(0) == 0) & (t == 0))
    def _():
        rows = stage_s.shape[1]
        _stage_weight(w_hbm, w_ref, stage_s, sem, w_ref.shape[0] // rows,
                      lambda c: (pl.ds(c * rows, rows), slice(None)))

    shift = mod_ref[0, 0:1, :]
    scale1 = 1.0 + mod_ref[0, 1:2, :]
    gain = g1_ref[...]

    def norm_mod(xv):
        return _rms_scale(xv) * gain * scale1 + shift

    hm = norm_mod(xm_ref[0])
    hp = jnp.where(t > 0, norm_mod(xp_ref[0]), 0.0)
    hn = jnp.where(t < nt - 1, norm_mod(xn_ref[0]), 0.0)
    hmb = hm.astype(BF16)
    he = jnp.concatenate([hm, hp, hn], axis=0).astype(BF16)

    pqk = _dot_nt(he, w_ref[W_QK, :])
    pv = _dot_nt(hmb, w_ref[W_V, :])
    qk = _conv3_rows(pqk, qkw_ref, qkb_ref, tm)
    qk = _silu(qk)
    k_ref[0] = (qk[:, D_MODEL:] * (HEAD_DIM ** -0.5)).astype(BF16)
    pg = _dot_nt(w_ref[W_G, :], hmb)
    if not full:
        _store_head_tiles_t(vt_ref, pv, tm)
        g_ref[0] = pg + gb_ref[...]
        return
    po = _dot_nt(hmb, w_ref[W_O, :])
    _store_head_tiles_t(qt_ref, qk[:, :D_MODEL], tm)
    _store_head_tiles_t(vt_ref, pv, tm)
    g_ref[0] = pg + gb_ref[...]
    pcx = _dot_nt(he, w_ref[W_SCX, :])
    _store_head_tiles_t(ogt_ref, _sigmoid(po), tm)
    psb = _dot_nt(hmb, w_ref[W_SB, :])
    cu = _conv3_rows(pcx[:, :D_MODEL] * pcx[:, D_MODEL:], scw_ref, scb_ref, tm)
    pm = _dot_nt(hmb, w_ref[W_MG, :])
    yc = _dot((psb * cu).astype(BF16), wco_ref[...])
    gm_ref[0] = _sigmoid(pm[:, :D_MODEL]).astype(BF16)
    zc_ref[0] = (_sigmoid(pm[:, D_MODEL:]) * yc).astype(BF16)


def _proj(xs, mod, g1, weights, *, full, per_batch_mod):
    bsz, t_len, _ = xs.shape
    tm = min(PROJ_TM, t_len)
    nt = t_len // tm
    hb = tm // SEQ_HALO
    nhb = t_len // SEQ_HALO
    x_specs = [
        pl.BlockSpec((1, tm, D_MODEL), lambda b, t: (b, t, 0)),
        pl.BlockSpec((1, SEQ_HALO, D_MODEL), lambda b, t: (b, jnp.maximum(t * hb - 1, 0), 0)),
        pl.BlockSpec((1, SEQ_HALO, D_MODEL), lambda b, t: (b, jnp.minimum((t + 1) * hb, nhb - 1), 0)),
    ]
    mod_spec = pl.BlockSpec((1, 8, D_MODEL), (lambda b, t: (b, 0, 0)) if per_batch_mod else (lambda b, t: (0, 0, 0)))
    w_rows = weights[0].shape[1] if full else W_CTX_ROWS
    stage_rows, stage_slots = (PROJ_STAGE_ROWS, STAGE_SLOTS) if full else (PROJ_STAGE_ROWS_CTX, 2)
    assert w_rows % stage_rows == 0 and stage_rows % BF16_ROWS == 0
    w_specs = [pl.BlockSpec(memory_space=pl.ANY)] + [_const_spec(w.shape) for w in weights[1:]]
    tok = lambda n, dt: (pl.BlockSpec((1, tm, n), lambda b, t: (b, t, 0)),
                         jax.ShapeDtypeStruct((bsz, t_len, n), dt))
    tiles_t = (pl.BlockSpec((1, N_HEADS, tm // CHUNK, HEAD_DIM, CHUNK), lambda b, t: (b, 0, t, 0, 0)),
               jax.ShapeDtypeStruct((bsz, N_HEADS, t_len // CHUNK, HEAD_DIM, CHUNK), BF16))
    gates_t = (pl.BlockSpec((1, N_GATES, tm), lambda b, t: (b, 0, t)),
               jax.ShapeDtypeStruct((bsz, N_GATES, t_len), F32))
    outs = [tok(D_MODEL, BF16), tiles_t, gates_t]
    if full:
        outs += [tiles_t, tiles_t, tok(D_MODEL, BF16), tok(D_MODEL, BF16)]
    return pl.pallas_call(
        functools.partial(_proj_kernel, tm=tm, full=full),
        grid=(bsz, nt),
        in_specs=x_specs + [mod_spec, _const_spec(g1.shape)] + w_specs,
        out_specs=[o[0] for o in outs],
        out_shape=[o[1] for o in outs],
        scratch_shapes=[pltpu.VMEM((w_rows, D_MODEL), BF16),
                        pltpu.VMEM((stage_slots, stage_rows, D_MODEL), F32),
                        pltpu.SemaphoreType.DMA((stage_slots,))],
        compiler_params=pltpu.CompilerParams(dimension_semantics=("arbitrary", "arbitrary"),
                                             vmem_limit_bytes=VMEM_LIMIT),
        name="proj_full" if full else "proj_ctx",
    )(xs, xs, xs, mod, g1, *weights)


def _scan_lanes(v, combine, fill, forward):
    lane = lax.broadcasted_iota(jnp.int32, v.shape, 1)
    d = 1
    while d < LANES:
        if forward:
            shifted = jnp.where(lane >= d, pltpu.roll(v, d, 1), fill)
        else:
            shifted = jnp.where(lane < LANES - d, pltpu.roll(v, LANES - d, 1), fill)
        v = combine(v, shifted)
        d *= 2
    return v


def _gates_kernel(gl_ref, gc_ref, out_ref, *, n_ctx, n_lat):
    n_all = n_ctx + n_lat
    src = lax.broadcasted_iota(jnp.int32, (CHUNK, 2 * CHUNK), 0)
    dst = lax.broadcasted_iota(jnp.int32, (CHUNK, 2 * CHUNK), 1)

    def gate_rows(cg, first):
        if cg < n_ctx:
            return gc_ref[0, first:first + N_HEADS, cg * CHUNK:(cg + 1) * CHUNK]
        return gl_ref[0, first:first + N_HEADS, (cg - n_ctx) * CHUNK:(cg - n_ctx + 1) * CHUNK]

    for dirn in range(2):
        forward = dirn == 0
        if forward:
            order = list(range(n_all))
        else:
            order = list(range(n_ctx - 1, -1, -1)) + list(range(n_all - 1, n_ctx - 1, -1))
        last = LANES - 1 if forward else 0
        first_row = 2 * N_HEADS * dirn
        fg = jnp.concatenate([gate_rows(cg, first_row + N_HEADS) for cg in range(n_all)], axis=0)
        lf = jnp.minimum(fg, 0.0) - jnp.log1p(jnp.exp(-jnp.abs(fg)))
        prefix = (src <= dst) if forward else (src >= dst)
        sel = jnp.where((dst >= CHUNK) | prefix, 1.0, 0.0).astype(BF16)
        hi = lf.astype(BF16)
        rest = lf - hi.astype(F32)
        mid = rest.astype(BF16)
        low = (rest - mid.astype(F32)).astype(BF16)
        sums = _dot(hi, sel) + _dot(mid, sel) + _dot(low, sel)
        m = jnp.full((N_HEADS, 1), M_INIT, F32)
        for cg in order:
            rows = slice(cg * N_HEADS, (cg + 1) * N_HEADS)
            b = sums[rows, :CHUNK]
            b_end = sums[rows, CHUNK:CHUNK + 1]
            r = gate_rows(cg, first_row) - b
            cm = _scan_lanes(r, jnp.maximum, -jnp.inf, forward)
            mx = jnp.maximum(m, cm)
            mt = b + mx
            mx_end = jnp.maximum(m, cm[:, last:last + 1])
            cbase = C_BASE + C_PER_DIR * dirn
            fields = {
                R_ROWB + dirn: r,
                R_DECAY + dirn: jnp.broadcast_to(jnp.exp(m - mx_end), (N_HEADS, CHUNK)),
                cbase + F_COLA: -mx,
                cbase + F_AINT: jnp.exp(m - mx),
                cbase + F_ENEG: jnp.exp(-mt),
                cbase + F_WK: jnp.exp(r - mx_end),
            }
            m_new = b_end + mx_end
            for f, val in fields.items():
                for h in range(N_HEADS):
                    out_ref[0, h, cg, f:f + 1, :] = val[h:h + 1, :]
            m = m_new
    n_used = C_BASE + 2 * C_PER_DIR
    out_ref[0, :, :, n_used:, :] = jnp.zeros((N_HEADS, n_all, N_FIELDS - n_used, CHUNK), F32)


def _gates(g_lat, g_ctx):
    bsz, _, t_lat = g_lat.shape
    t_ctx = g_ctx.shape[2]
    n_all = (t_lat + t_ctx) // CHUNK
    shape = (bsz, N_HEADS, n_all, N_FIELDS, CHUNK)
    return pl.pallas_call(
        functools.partial(_gates_kernel, n_ctx=t_ctx // CHUNK, n_lat=t_lat // CHUNK),
        grid=(bsz,),
        in_specs=[pl.BlockSpec((1, N_GATES, t_lat), lambda b: (b, 0, 0)),
                  pl.BlockSpec((1, N_GATES, t_ctx), lambda b: (b, 0, 0))],
        out_specs=pl.BlockSpec((1,) + shape[1:], lambda b: (b, 0, 0, 0, 0)),
        out_shape=jax.ShapeDtypeStruct(shape, F32),
        compiler_params=pltpu.CompilerParams(dimension_semantics=("parallel",)),
        name="gates",
    )(g_lat, g_ctx)


STATE_ROWS = HEAD_DIM + BF16_ROWS
MLSTM_HEADS = 4


def _mlstm_kernel(qt_ref, kl_ref, vtl_ref, kc_ref, vtc_ref, gf_ref, ogt_ref, ng_ref, out_ref,
                  dz_s, zf_s, zb_s, z_s, p_s, *, n_ctx, n_lat):
    lane = lax.broadcasted_iota(jnp.int32, (CHUNK, CHUNK), 1)
    row = lax.broadcasted_iota(jnp.int32, (CHUNK, CHUNK), 0)
    tri = (row <= lane, row >= lane)
    tail_first = lax.broadcasted_iota(jnp.int32, (BF16_ROWS, CHUNK), 0) == 0
    ones_tail = jnp.where(tail_first, 1.0, 0.0).astype(BF16)

    for hh in range(MLSTM_HEADS):
        _mlstm_head(hh, qt_ref, kl_ref, vtl_ref, kc_ref, vtc_ref, gf_ref, ogt_ref, ng_ref, out_ref,
                    dz_s, zf_s, zb_s, z_s, p_s, tri, tail_first, ones_tail, n_ctx, n_lat)


def _mlstm_head(hh, qt_ref, kl_ref, vtl_ref, kc_ref, vtc_ref, gf_ref, ogt_ref, ng_ref, out_ref,
                dz_s, zf_s, zb_s, z_s, p_s, tri, tail_first, ones_tail, n_ctx, n_lat):
    hcols = slice(hh * HEAD_DIM, (hh + 1) * HEAD_DIM)

    def field(cg, dirn, f):
        j = C_BASE + C_PER_DIR * dirn + f
        return gf_ref[0, hh, cg, j:j + 1, :]

    def pass_a(cg, k, vt):
        vf = vt.astype(F32)
        parts = []
        for dirn in range(2):
            wk = field(cg, dirn, F_WK)
            parts += [(vf * wk).astype(BF16), jnp.where(tail_first, wk, 0.0).astype(BF16)]
        dz_s[cg] = _dot(jnp.concatenate(parts, axis=0), k)

    for cg in range(n_ctx):
        pass_a(cg, kc_ref[0, cg * CHUNK:(cg + 1) * CHUNK, hcols], vtc_ref[0, hh, cg])

    def pass_a_lat(i, carry):
        sl = pl.ds(pl.multiple_of(i * CHUNK, CHUNK), CHUNK)
        pass_a(n_ctx + i, kl_ref[0, sl, hcols], vtl_ref[0, hh, i])
        return carry

    lax.fori_loop(0, n_lat, pass_a_lat, 0, unroll=16)

    def advance(dirn, cg):
        dec = gf_ref[0, hh, cg, R_DECAY + dirn:R_DECAY + dirn + 1, :]
        z_s[dirn] = dec * z_s[dirn] + dz_s[cg, dirn * STATE_ROWS:(dirn + 1) * STATE_ROWS, :]

    z_s[...] = jnp.zeros_like(z_s)
    for cg in range(n_ctx):
        advance(0, cg)
    for cg in range(n_ctx - 1, -1, -1):
        advance(1, cg)

    def pass_b(i, carry):
        zf_s[i] = z_s[0].astype(BF16)
        advance(0, n_ctx + i)
        j = n_lat - 1 - i
        zb_s[j] = z_s[1].astype(BF16)
        advance(1, n_ctx + j)
        return carry

    lax.fori_loop(0, n_lat, pass_b, 0, unroll=16)

    gain = jnp.broadcast_to(ng_ref[:, hcols], (CHUNK, HEAD_DIM)).T

    def pass_c1(i, carry):
        cg = n_ctx + i
        sl = pl.ds(pl.multiple_of(i * CHUNK, CHUNK), CHUNK)
        st = _dot(kl_ref[0, sl, hcols], qt_ref[0, hh, i])
        rowb = gf_ref[0, hh, cg].T
        for dirn in range(2):
            dmat = rowb[:, R_ROWB + dirn:R_ROWB + dirn + 1] + field(cg, dirn, F_COLA)
            w = jnp.exp(jnp.where(tri[dirn], dmat, -jnp.inf))
            p_s[i, dirn] = (st * w).astype(BF16)
        return carry

    lax.fori_loop(0, n_lat, pass_c1, 0, unroll=16)

    def pass_c2(i, carry):
        cg = n_ctx + i
        qtf = qt_ref[0, hh, i].astype(F32)
        vaug = jnp.concatenate([vtl_ref[0, hh, i], ones_tail], axis=0)
        ht = None
        for dirn, z_ref in enumerate((zf_s, zb_s)):
            rhs = jnp.concatenate([(qtf * field(cg, dirn, F_AINT)).astype(BF16), p_s[i, dirn]], axis=0)
            res = _dot(jnp.concatenate([z_ref[i], vaug], axis=1), rhs)
            den = jnp.maximum(jnp.abs(res[HEAD_DIM:HEAD_DIM + 1, :]), field(cg, dirn, F_ENEG))
            hd = res[:HEAD_DIM, :] / den
            ht = hd if ht is None else ht + hd
        ht = ht * lax.rsqrt(jnp.mean(ht * ht, axis=0, keepdims=True) + EPS)
        out_ref[0, hh, i] = (ht * gain * ogt_ref[0, hh, i].astype(F32)).astype(BF16)
        return carry

    lax.fori_loop(0, n_lat, pass_c2, 0, unroll=16)


def _mlstm(qt_l, k_l, vt_l, k_c, vt_c, gf, ogt, ng):
    bsz, t_lat, _ = k_l.shape
    t_ctx = k_c.shape[1]
    n_lat, n_ctx = t_lat // CHUNK, t_ctx // CHUNK
    n_all = n_lat + n_ctx
    hb = MLSTM_HEADS
    head_lat = pl.BlockSpec((1, t_lat, hb * HEAD_DIM), lambda b, h: (b, 0, h))
    head_ctx = pl.BlockSpec((1, t_ctx, hb * HEAD_DIM), lambda b, h: (b, 0, h))
    tiles = lambda n: pl.BlockSpec((1, hb, n, HEAD_DIM, CHUNK), lambda b, h: (b, h, 0, 0, 0))
    return pl.pallas_call(
        functools.partial(_mlstm_kernel, n_ctx=n_ctx, n_lat=n_lat),
        grid=(bsz, N_HEADS // hb),
        in_specs=[tiles(n_lat), head_lat, tiles(n_lat), head_ctx, tiles(n_ctx),
                  pl.BlockSpec((1, hb, n_all, N_FIELDS, CHUNK), lambda b, h: (b, h, 0, 0, 0)),
                  tiles(n_lat),
                  pl.BlockSpec((1, hb * HEAD_DIM), lambda b, h: (0, h))],
        out_specs=tiles(n_lat),
        out_shape=jax.ShapeDtypeStruct(qt_l.shape, BF16),
        scratch_shapes=[pltpu.VMEM((n_all, 2 * STATE_ROWS, HEAD_DIM), F32),
                        pltpu.VMEM((n_lat, STATE_ROWS, HEAD_DIM), BF16),
                        pltpu.VMEM((n_lat, STATE_ROWS, HEAD_DIM), BF16),
                        pltpu.VMEM((2, STATE_ROWS, HEAD_DIM), F32),
                        pltpu.VMEM((n_lat, 2, CHUNK, CHUNK), BF16)],
        compiler_params=pltpu.CompilerParams(dimension_semantics=("parallel", "parallel"),
                                             vmem_limit_bytes=VMEM_LIMIT),
        name="mlstm",
    )(qt_l, k_l, vt_l, k_c, vt_c, gf, ogt, ng)


def _merge_kernel(hgt_ref, gm_ref, zc_ref, x_ref, mod_ref, g2_ref, wmo_ref, wo_ref, x1_ref, h2_ref):
    per_sub = MERGE_SUB // CHUNK
    subs = [slice(i * MERGE_SUB, (i + 1) * MERGE_SUB) for i in range(hgt_ref.shape[2] // per_sub)]
    hg = [jnp.concatenate(
        [jnp.concatenate([hgt_ref[0, h, ci].astype(F32).T for h in range(N_HEADS)], axis=1)
         for ci in range(i * per_sub, (i + 1) * per_sub)], axis=0).astype(BF16) for i in range(len(subs))]
    ym = [_dot(hg_i, wmo_ref[...]) for hg_i in hg]
    y = [_dot((gm_ref[0, sl, :].astype(F32) * ym_i + zc_ref[0, sl, :].astype(F32)).astype(BF16), wo_ref[...])
         for sl, ym_i in zip(subs, ym)]
    for sl, y_i in zip(subs, y):
        x1 = x_ref[0, sl, :] + mod_ref[0, 2:3, :] * y_i
        x1_ref[0, sl, :] = x1
        h2 = _rms_scale(x1) * g2_ref[...] * (1.0 + mod_ref[0, 4:5, :]) + mod_ref[0, 3:4, :]
        h2_ref[0, sl, :] = h2.astype(BF16)


def _merge(hgt, gm, zc, x, mod, g2, wmo, wo):
    bsz, t_len, _ = x.shape
    tm = MERGE_TM
    tok = pl.BlockSpec((1, tm, D_MODEL), lambda b, t: (b, t, 0))
    tiles = pl.BlockSpec((1, N_HEADS, tm // CHUNK, HEAD_DIM, CHUNK), lambda b, t: (b, 0, t, 0, 0))
    return pl.pallas_call(
        _merge_kernel,
        grid=(bsz, t_len // tm),
        in_specs=[tiles, tok, tok, tok, pl.BlockSpec((1, 8, D_MODEL), lambda b, t: (b, 0, 0)),
                  _const_spec(g2.shape), _const_spec(wmo.shape), _const_spec(wo.shape)],
        out_specs=[tok, tok],
        out_shape=[jax.ShapeDtypeStruct(x.shape, F32), jax.ShapeDtypeStruct(x.shape, BF16)],
        compiler_params=pltpu.CompilerParams(dimension_semantics=("parallel", "parallel"),
                                             vmem_limit_bytes=VMEM_LIMIT),
        name="merge",
    )(hgt, gm, zc, x, mod, g2, wmo, wo)


def _gelu_tanh(v):
    return 0.5 * v * (1.0 + jnp.tanh(0.7978845608028654 * (v + 0.044715 * (v * v * v))))


def _stage_weight(src_hbm, dst_s, stage_s, sem, n_parts, part):
    slots = stage_s.shape[0]

    def copy(c):
        return pltpu.make_async_copy(src_hbm.at[(0,) + part(c)], stage_s.at[c % slots], sem.at[c % slots])

    for c in range(min(slots, n_parts)):
        copy(c).start()
    for c in range(n_parts):
        copy(c).wait()
        dst_s[part(c)] = stage_s[c % slots].astype(BF16)
        if c + slots < n_parts:
            copy(c + slots).start()


def _ffn_kernel(hm_ref, hp_ref, hn_ref, x1_ref, mod_ref, wu_hbm, cw_ref, cb_ref, wd_hbm, fg_ref,
                out_ref, act_s, wu_ref, wd_ref, stage_u, stage_d, sem_u, sem_d, *, tm):
    t = pl.program_id(1)
    nt = pl.num_programs(1)

    @pl.when((pl.program_id(0) == 0) & (t == 0))
    def _():
        for src, dst, stage, sem in ((wu_hbm, wu_ref, stage_u, sem_u), (wd_hbm, wd_ref, stage_d, sem_d)):
            rows = stage.shape[1]
            _stage_weight(src, dst, stage, sem, dst.shape[0] // rows,
                          lambda c, rows=rows: (pl.ds(c * rows, rows), slice(None)))

    n = tm + 2 * GRID_W
    hm = hm_ref[0]
    hp = jnp.where(t > 0, hp_ref[0], jnp.zeros_like(hp_ref[0]))
    hn = jnp.where(t < nt - 1, hn_ref[0], jnp.zeros_like(hn_ref[0]))
    he = jnp.concatenate([hp, hm, hn], axis=0)
    gcol = lax.broadcasted_iota(jnp.int32, (n, 1), 0) & (GRID_W - 1)
    has_left = gcol != 0
    has_right = gcol != GRID_W - 1
    n_chunks = FF_HIDDEN // FFN_CW
    cols = lambda j: slice(j * FFN_CW, (j + 1) * FFN_CW)
    gate_cols = lambda j: slice(FF_HIDDEN + j * FFN_CW, FF_HIDDEN + (j + 1) * FFN_CW)
    up = lambda j: (_dot(he, wu_ref[:, cols(j)]), _dot(hm, wu_ref[:, gate_cols(j)]))
    acc = None
    nxt = up(0)
    for j in range(n_chunks):
        cs = cols(j)
        a, g = nxt
        if j + 1 < n_chunks:
            nxt = up(j + 1)
        taps = (jnp.where(has_left, pltpu.roll(a, 1, 0), 0.0), a,
                jnp.where(has_right, pltpu.roll(a, n - 1, 0), 0.0))
        conv = cb_ref[:, cs]
        for dr in range(3):
            for dc in range(3):
                conv = conv + cw_ref[3 * dr + dc: 3 * dr + dc + 1, cs] * taps[dc][GRID_W * dr: GRID_W * dr + tm]
        act_s[:, cs] = (_gelu_tanh(conv) * g).astype(BF16)
        if (j + 1) % FFN_DOWN_GROUP == 0 or j + 1 == n_chunks:
            gs = slice((j // FFN_DOWN_GROUP) * FFN_DOWN_GROUP * FFN_CW, (j + 1) * FFN_CW)
            part = _dot(act_s[:, gs], wd_ref[gs, :])
            acc = part if acc is None else acc + part
    x2 = x1_ref[0] + mod_ref[0, 5:6, :] * acc
    out_ref[0] = _rms_scale(x2) * fg_ref[...]


def _ffn(h2, x1, mod, wu, cw, cb, wd, fg):
    bsz, t_len, _ = x1.shape
    tm = FFN_TM
    rb = tm // GRID_W
    nrb = t_len // GRID_W
    tok = pl.BlockSpec((1, tm, D_MODEL), lambda b, t: (b, t, 0))
    return pl.pallas_call(
        functools.partial(_ffn_kernel, tm=tm),
        grid=(bsz, t_len // tm),
        in_specs=[tok,
                  pl.BlockSpec((1, GRID_W, D_MODEL), lambda b, t: (b, jnp.maximum(t * rb - 1, 0), 0)),
                  pl.BlockSpec((1, GRID_W, D_MODEL), lambda b, t: (b, jnp.minimum((t + 1) * rb, nrb - 1), 0)),
                  tok, pl.BlockSpec((1, 8, D_MODEL), lambda b, t: (b, 0, 0)),
                  pl.BlockSpec(memory_space=pl.ANY), _const_spec(cw.shape), _const_spec(cb.shape),
                  pl.BlockSpec(memory_space=pl.ANY), _const_spec(fg.shape)],
        out_specs=tok,
        out_shape=jax.ShapeDtypeStruct(x1.shape, F32),
        scratch_shapes=[pltpu.VMEM((tm, FF_HIDDEN), BF16),
                        pltpu.VMEM(wu.shape[1:], BF16),
                        pltpu.VMEM(wd.shape[1:], BF16),
                        pltpu.VMEM((STAGE_SLOTS, FFN_STAGE_ROWS_UP, wu.shape[2]), F32),
                        pltpu.VMEM((STAGE_SLOTS, FFN_STAGE_ROWS_DOWN, wd.shape[2]), F32),
                        pltpu.SemaphoreType.DMA((STAGE_SLOTS,)), pltpu.SemaphoreType.DMA((STAGE_SLOTS,))],
        compiler_params=pltpu.CompilerParams(dimension_semantics=("arbitrary", "arbitrary"),
                                             vmem_limit_bytes=VMEM_LIMIT),
        name="ffn",
    )(h2, h2, h2, x1, mod, wu, cw, cb, wd, fg)


def kernel(x, c, ctx, c_ctx, ada_w, ada_b, norm1_g, norm2_g, w_in, qk_conv_w, qk_conv_b, gate_b, mnorm_g,
           w_m_out, sc_conv_w, sc_conv_b, w_c_out, w_o, w_up, ff_conv_w, ff_conv_b, w_down, final_g):
    assert ada_w.shape[0] == 1, "single-layer block"
    bsz, t_lat, _ = x.shape
    t_ctx = ctx.shape[1]

    cc = jnp.zeros((2 * 8, D_MODEL), F32).at[:bsz].set(c).at[bsz].set(c_ctx)
    mod = _ada(cc, ada_w[0], ada_b[0][None, :])
    mod_x = jnp.pad(mod[:bsz].reshape(bsz, 6, D_MODEL), ((0, 0), (0, 2), (0, 0)))
    mod_c = jnp.pad(mod[bsz].reshape(1, 6, D_MODEL), ((0, 0), (0, 2), (0, 0)))

    cast = lambda a: a.astype(BF16)
    w_t = jnp.swapaxes(w_in, 1, 2)
    gb = gate_b[0][:, None]
    g1 = norm1_g[0][None, :]
    ctx_w = [w_t, gb, qk_conv_w[0], qk_conv_b[0][None, :]]
    lat_w = ctx_w + [sc_conv_w[0], sc_conv_b[0][None, :], cast(w_c_out[0])]

    k_l, vt_l, g_l, qt_l, ogt, gm, zc = _proj(x, mod_x, g1, lat_w, full=True, per_batch_mod=True)
    k_c, vt_c, g_c = _proj(ctx, mod_c, g1, ctx_w, full=False, per_batch_mod=False)

    gf = _gates(g_l, g_c)
    hgt = _mlstm(qt_l, k_l, vt_l, k_c, vt_c, gf, ogt, mnorm_g[0][None, :])

    x1, h2 = _merge(hgt, gm, zc, x, mod_x, norm2_g[0][None, :], cast(w_m_out[0]), cast(w_o[0]))

    out = _ffn(h2, x1, mod_x, w_up,
               ff_conv_w[0].reshape(9, FF_HIDDEN), ff_conv_b[0][None, :], w_down,
               final_g[None, :])
    return out
```

```python
import functools

import jax
import jax.numpy as jnp
from jax import lax
from jax.experimental import pallas as pl
from jax.experimental.pallas import tpu as pltpu

F32 = jnp.float32
BF16 = jnp.bfloat16

D_MODEL = 1024
N_HEADS = 8
HEAD_DIM = D_MODEL // N_HEADS
CHUNK = 128
GRID_W = 64
FF_HIDDEN = 2816
EPS = 1e-6
M_INIT = -1e30
N_GATES = 4 * N_HEADS

LANES = 128
BF16_ROWS = 16
VMEM_LIMIT = 56 * 1024 * 1024

PROJ_TM = 256
STAGE_SLOTS = 4
PROJ_STAGE_ROWS = 272
PROJ_STAGE_ROWS_CTX = 1552
SEQ_HALO = 8
MERGE_TM = 1024
MERGE_SUB = 256
FFN_TM = 512
FFN_CW = 256
FFN_DOWN_GROUP = 4
FFN_STAGE_ROWS_UP = 64
FFN_STAGE_ROWS_DOWN = 128

R_ROWB, R_DECAY = 0, 2
C_BASE, C_PER_DIR = 4, 4
F_COLA, F_AINT, F_ENEG, F_WK = range(4)
N_FIELDS = 16


def _const_spec(shape):
    nd = len(shape)
    return pl.BlockSpec(shape, lambda *_: (0,) * nd, pipeline_mode=pl.Buffered(1))


def _sigmoid(v):
    return 0.5 * jnp.tanh(0.5 * v) + 0.5


def _silu(v):
    h = 0.5 * v
    return h + h * jnp.tanh(h)


def _rms_scale(v):
    return v * lax.rsqrt(jnp.mean(v * v, axis=-1, keepdims=True) + EPS)


def _dot(a, b):
    return jnp.dot(a, b, preferred_element_type=F32)


def _ada_kernel(c_ref, w_ref, b_ref, o_ref):
    cv = c_ref[...]
    s = _silu(cv)
    w = w_ref[...]
    s_hi, w_hi = s.astype(BF16), w.astype(BF16)
    s_lo = (s - s_hi.astype(F32)).astype(BF16)
    w_lo = (w - w_hi.astype(F32)).astype(BF16)
    rows = s.shape[0]
    main = _dot(jnp.concatenate([s_hi, s_lo], axis=0), w_hi)
    o_ref[...] = main[:rows] + main[rows:] + _dot(s_hi, w_lo) + b_ref[...]


def _ada(cc, w, b):
    rows, n = cc.shape[0], w.shape[1]
    tn = 1024
    return pl.pallas_call(
        _ada_kernel,
        grid=(n // tn,),
        in_specs=[pl.BlockSpec((rows, D_MODEL), lambda j: (0, 0)),
                  pl.BlockSpec((D_MODEL, tn), lambda j: (0, j)),
                  pl.BlockSpec((1, tn), lambda j: (0, j))],
        out_specs=pl.BlockSpec((rows, tn), lambda j: (0, j)),
        out_shape=jax.ShapeDtypeStruct((rows, n), F32),
        name="ada",
    )(cc, w, b)


def _conv3_rows(p, w_ref, b_ref, tm):
    main = p[:tm]
    prev_row = p[tm + SEQ_HALO - 1:tm + SEQ_HALO]
    next_row = p[tm + SEQ_HALO:tm + SEQ_HALO + 1]
    sub = lax.broadcasted_iota(jnp.int32, (SEQ_HALO, 1), 0)
    down = pltpu.roll(main, 1, 0)
    up = pltpu.roll(main, tm - 1, 0)
    left = jnp.concatenate([jnp.where(sub == 0, prev_row, down[:SEQ_HALO]), down[SEQ_HALO:]], axis=0)
    right = jnp.concatenate([up[:tm - SEQ_HALO], jnp.where(sub == SEQ_HALO - 1, next_row, up[tm - SEQ_HALO:])],
                            axis=0)
    return left * w_ref[0:1, :] + main * w_ref[1:2, :] + right * w_ref[2:3, :] + b_ref[...]


def _store_head_tiles_t(ref, val, tm):
    for h in range(N_HEADS):
        for ci in range(tm // CHUNK):
            tile = val[ci * CHUNK:(ci + 1) * CHUNK, h * HEAD_DIM:(h + 1) * HEAD_DIM]
            ref[0, h, ci] = tile.T.astype(BF16)


W_QK = slice(0, 2 * D_MODEL)
W_V = slice(2 * D_MODEL, 3 * D_MODEL)
W_G = slice(3 * D_MODEL, 3 * D_MODEL + N_GATES)
W_CTX_ROWS = W_G.stop
W_O = slice(W_G.stop, W_G.stop + D_MODEL)
W_SB = slice(W_O.stop, W_O.stop + D_MODEL)
W_SCX = slice(W_SB.stop, W_SB.stop + 2 * D_MODEL)
W_MG = slice(W_SCX.stop, W_SCX.stop + 2 * D_MODEL)


def _dot_nt(a, b):
    return lax.dot_general(a, b, (((1,), (1,)), ((), ())), preferred_element_type=F32)


def _proj_kernel(*refs, tm, full):
    w_ref, stage_s, sem = refs[-3:]
    if full:
        (xm_ref, xp_ref, xn_ref, mod_ref, g1_ref, w_hbm, gb_ref, qkw_ref, qkb_ref,
         scw_ref, scb_ref, wco_ref,
         k_ref, vt_ref, g_ref, qt_ref, ogt_ref, gm_ref, zc_ref) = refs[:-3]
    else:
        (xm_ref, xp_ref, xn_ref, mod_ref, g1_ref, w_hbm, gb_ref, qkw_ref, qkb_ref,
         k_ref, vt_ref, g_ref) = refs[:-3]
    t = pl.program_id(1)
    nt = pl.num_programs(1)

    @pl.when((pl.program_id(0) == 0) & (t == 0))
    def _():
        rows = stage_s.shape[1]
        _stage_weight(w_hbm, w_ref, stage_s, sem, w_ref.shape[0] // rows,
                      lambda c: (pl.ds(c * rows, rows), slice(None)))

    shift = mod_ref[0, 0:1, :]
    scale1 = 1.0 + mod_ref[0, 1:2, :]
    gain = g1_ref[...]

    def norm_mod(xv):
        return _rms_scale(xv) * gain * scale1 + shift

    hm = norm_mod(xm_ref[0])
    hp = jnp.where(t > 0, norm_mod(xp_ref[0]), 0.0)
    hn = jnp.where(t < nt - 1, norm_mod(xn_ref[0]), 0.0)
    hmb = hm.astype(BF16)
    he = jnp.concatenate([hm, hp, hn], axis=0).astype(BF16)

    pqk = _dot_nt(he, w_ref[W_QK, :])
    pv = _dot_nt(hmb, w_ref[W_V, :])
    qk = _conv3_rows(pqk, qkw_ref, qkb_ref, tm)
    qk = _silu(qk)
    k_ref[0] = (qk[:, D_MODEL:] * (HEAD_DIM ** -0.5)).astype(BF16)
    pg = _dot_nt(w_ref[W_G, :], hmb)
    if not full:
        _store_head_tiles_t(vt_ref, pv, tm)
        g_ref[0] = pg + gb_ref[...]
        return
    po = _dot_nt(hmb, w_ref[W_O, :])
    _store_head_tiles_t(qt_ref, qk[:, :D_MODEL], tm)
    _store_head_tiles_t(vt_ref, pv, tm)
    g_ref[0] = pg + gb_ref[...]
    pcx = _dot_nt(he, w_ref[W_SCX, :])
    _store_head_tiles_t(ogt_ref, _sigmoid(po), tm)
    psb = _dot_nt(hmb, w_ref[W_SB, :])
    cu = _conv3_rows(pcx[:, :D_MODEL] * pcx[:, D_MODEL:], scw_ref, scb_ref, tm)
    pm = _dot_nt(hmb, w_ref[W_MG, :])
    yc = _dot((psb * cu).astype(BF16), wco_ref[...])
    gm_ref[0] = _sigmoid(pm[:, :D_MODEL]).astype(BF16)
    zc_ref[0] = (_sigmoid(pm[:, D_MODEL:]) * yc).astype(BF16)


def _proj(xs, mod, g1, weights, *, full, per_batch_mod):
    bsz, t_len, _ = xs.shape
    tm = min(PROJ_TM, t_len)
    nt = t_len // tm
    hb = tm // SEQ_HALO
    nhb = t_len // SEQ_HALO
    x_specs = [
        pl.BlockSpec((1, tm, D_MODEL), lambda b, t: (b, t, 0)),
        pl.BlockSpec((1, SEQ_HALO, D_MODEL), lambda b, t: (b, jnp.maximum(t * hb - 1, 0), 0)),
        pl.BlockSpec((1, SEQ_HALO, D_MODEL), lambda b, t: (b, jnp.minimum((t + 1) * hb, nhb - 1), 0)),
    ]
    mod_spec = pl.BlockSpec((1, 8, D_MODEL), (lambda b, t: (b, 0, 0)) if per_batch_mod else (lambda b, t: (0, 0, 0)))
    w_rows = weights[0].shape[1] if full else W_CTX_ROWS
    stage_rows, stage_slots = (PROJ_STAGE_ROWS, STAGE_SLOTS) if full else (PROJ_STAGE_ROWS_CTX, 2)
    assert w_rows % stage_rows == 0 and stage_rows % BF16_ROWS == 0
    w_specs = [pl.BlockSpec(memory_space=pl.ANY)] + [_const_spec(w.shape) for w in weights[1:]]
    tok = lambda n, dt: (pl.BlockSpec((1, tm, n), lambda b, t: (b, t, 0)),
                         jax.ShapeDtypeStruct((bsz, t_len, n), dt))
    tiles_t = (pl.BlockSpec((1, N_HEADS, tm // CHUNK, HEAD_DIM, CHUNK), lambda b, t: (b, 0, t, 0, 0)),
               jax.ShapeDtypeStruct((bsz, N_HEADS, t_len // CHUNK, HEAD_DIM, CHUNK), BF16))
    gates_t = (pl.BlockSpec((1, N_GATES, tm), lambda b, t: (b, 0, t)),
               jax.ShapeDtypeStruct((bsz, N_GATES, t_len), F32))
    outs = [tok(D_MODEL, BF16), tiles_t, gates_t]
    if full:
        outs += [tiles_t, tiles_t, tok(D_MODEL, BF16), tok(D_MODEL, BF16)]
    return pl.pallas_call(
        functools.partial(_proj_kernel, tm=tm, full=full),
        grid=(bsz, nt),
        in_specs=x_specs + [mod_spec, _const_spec(g1.shape)] + w_specs,
        out_specs=[o[0] for o in outs],
        out_shape=[o[1] for o in outs],
        scratch_shapes=[pltpu.VMEM((w_rows, D_MODEL), BF16),
                        pltpu.VMEM((stage_slots, stage_rows, D_MODEL), F32),
                        pltpu.SemaphoreType.DMA((stage_slots,))],
        compiler_params=pltpu.CompilerParams(dimension_semantics=("arbitrary", "arbitrary"),
                                             vmem_limit_bytes=VMEM_LIMIT),
        name="proj_full" if full else "proj_ctx",
    )(xs, xs, xs, mod, g1, *weights)


def _scan_lanes(v, combine, fill, forward):
    lane = lax.broadcasted_iota(jnp.int32, v.shape, 1)
    d = 1
    while d < LANES:
        if forward:
            shifted = jnp.where(lane >= d, pltpu.roll(v, d, 1), fill)
        else:
            shifted = jnp.where(lane < LANES - d, pltpu.roll(v, LANES - d, 1), fill)
        v = combine(v, shifted)
        d *= 2
    return v


def _gates_kernel(gl_ref, gc_ref, out_ref, *, n_ctx, n_lat):
    n_all = n_ctx + n_lat
    src = lax.broadcasted_iota(jnp.int32, (CHUNK, 2 * CHUNK), 0)
    dst = lax.broadcasted_iota(jnp.int32, (CHUNK, 2 * CHUNK), 1)

    def gate_rows(cg, first):
        if cg < n_ctx:
            return gc_ref[0, first:first + N_HEADS, cg * CHUNK:(cg + 1) * CHUNK]
        return gl_ref[0, first:first + N_HEADS, (cg - n_ctx) * CHUNK:(cg - n_ctx + 1) * CHUNK]

    for dirn in range(2):
        forward = dirn == 0
        if forward:
            order = list(range(n_all))
        else:
            order = list(range(n_ctx - 1, -1, -1)) + list(range(n_all - 1, n_ctx - 1, -1))
        last = LANES - 1 if forward else 0
        first_row = 2 * N_HEADS * dirn
        fg = jnp.concatenate([gate_rows(cg, first_row + N_HEADS) for cg in range(n_all)], axis=0)
        lf = jnp.minimum(fg, 0.0) - jnp.log1p(jnp.exp(-jnp.abs(fg)))
        prefix = (src <= dst) if forward else (src >= dst)
        sel = jnp.where((dst >= CHUNK) | prefix, 1.0, 0.0).astype(BF16)
        hi = lf.astype(BF16)
        rest = lf - hi.astype(F32)
        mid = rest.astype(BF16)
        low = (rest - mid.astype(F32)).astype(BF16)
        sums = _dot(hi, sel) + _dot(mid, sel) + _dot(low, sel)
        m = jnp.full((N_HEADS, 1), M_INIT, F32)
        for cg in order:
            rows = slice(cg * N_HEADS, (cg + 1) * N_HEADS)
            b = sums[rows, :CHUNK]
            b_end = sums[rows, CHUNK:CHUNK + 1]
            r = gate_rows(cg, first_row) - b
            cm = _scan_lanes(r, jnp.maximum, -jnp.inf, forward)
            mx = jnp.maximum(m, cm)
            mt = b + mx
            mx_end = jnp.maximum(m, cm[:, last:last + 1])
            cbase = C_BASE + C_PER_DIR * dirn
            fields = {
                R_ROWB + dirn: r,
                R_DECAY + dirn: jnp.broadcast_to(jnp.exp(m - mx_end), (N_HEADS, CHUNK)),
                cbase + F_COLA: -mx,
                cbase + F_AINT: jnp.exp(m - mx),
                cbase + F_ENEG: jnp.exp(-mt),
                cbase + F_WK: jnp.exp(r - mx_end),
            }
            m_new = b_end + mx_end
            for f, val in fields.items():
                for h in range(N_HEADS):
                    out_ref[0, h, cg, f:f + 1, :] = val[h:h + 1, :]
            m = m_new
    n_used = C_BASE + 2 * C_PER_DIR
    out_ref[0, :, :, n_used:, :] = jnp.zeros((N_HEADS, n_all, N_FIELDS - n_used, CHUNK), F32)


def _gates(g_lat, g_ctx):
    bsz, _, t_lat = g_lat.shape
    t_ctx = g_ctx.shape[2]
    n_all = (t_lat + t_ctx) // CHUNK
    shape = (bsz, N_HEADS, n_all, N_FIELDS, CHUNK)
    return pl.pallas_call(
        functools.partial(_gates_kernel, n_ctx=t_ctx // CHUNK, n_lat=t_lat // CHUNK),
        grid=(bsz,),
        in_specs=[pl.BlockSpec((1, N_GATES, t_lat), lambda b: (b, 0, 0)),
                  pl.BlockSpec((1, N_GATES, t_ctx), lambda b: (b, 0, 0))],
        out_specs=pl.BlockSpec((1,) + shape[1:], lambda b: (b, 0, 0, 0, 0)),
        out_shape=jax.ShapeDtypeStruct(shape, F32),
        compiler_params=pltpu.CompilerParams(dimension_semantics=("parallel",)),
        name="gates",
    )(g_lat, g_ctx)


STATE_ROWS = HEAD_DIM + BF16_ROWS
MLSTM_HEADS = 2


def _mlstm_kernel(qt_ref, kl_ref, vtl_ref, kc_ref, vtc_ref, gf_ref, ogt_ref, ng_ref, out_ref,
                  dz_s, zf_s, zb_s, z_s, p_s, *, n_ctx, n_lat):
    lane = lax.broadcasted_iota(jnp.int32, (CHUNK, CHUNK), 1)
    row = lax.broadcasted_iota(jnp.int32, (CHUNK, CHUNK), 0)
    tri = (row <= lane, row >= lane)
    tail_first = lax.broadcasted_iota(jnp.int32, (BF16_ROWS, CHUNK), 0) == 0
    ones_tail = jnp.where(tail_first, 1.0, 0.0).astype(BF16)

    for hh in range(MLSTM_HEADS):
        _mlstm_head(hh, qt_ref, kl_ref, vtl_ref, kc_ref, vtc_ref, gf_ref, ogt_ref, ng_ref, out_ref,
                    dz_s, zf_s, zb_s, z_s, p_s, tri, tail_first, ones_tail, n_ctx, n_lat)


def _mlstm_head(hh, qt_ref, kl_ref, vtl_ref, kc_ref, vtc_ref, gf_ref, ogt_ref, ng_ref, out_ref,
                dz_s, zf_s, zb_s, z_s, p_s, tri, tail_first, ones_tail, n_ctx, n_lat):
    hcols = slice(hh * HEAD_DIM, (hh + 1) * HEAD_DIM)

    def field(cg, dirn, f):
        j = C_BASE + C_PER_DIR * dirn + f
        return gf_ref[0, hh, cg, j:j + 1, :]

    def pass_a(cg, k, vt):
        vf = vt.astype(F32)
        parts = []
        for dirn in range(2):
            wk = field(cg, dirn, F_WK)
            parts += [(vf * wk).astype(BF16), jnp.where(tail_first, wk, 0.0).astype(BF16)]
        dz_s[cg] = _dot(jnp.concatenate(parts, axis=0), k)

    for cg in range(n_ctx):
        pass_a(cg, kc_ref[0, cg * CHUNK:(cg + 1) * CHUNK, hcols], vtc_ref[0, hh, cg])

    def pass_a_lat(i, carry):
        sl = pl.ds(pl.multiple_of(i * CHUNK, CHUNK), CHUNK)
        pass_a(n_ctx + i, kl_ref[0, sl, hcols], vtl_ref[0, hh, i])
        return carry

    lax.fori_loop(0, n_lat, pass_a_lat, 0, unroll=16)

    def advance(dirn, cg):
        dec = gf_ref[0, hh, cg, R_DECAY + dirn:R_DECAY + dirn + 1, :]
        z_s[dirn] = dec * z_s[dirn] + dz_s[cg, dirn * STATE_ROWS:(dirn + 1) * STATE_ROWS, :]

    z_s[...] = jnp.zeros_like(z_s)
    for cg in range(n_ctx):
        advance(0, cg)
    for cg in range(n_ctx - 1, -1, -1):
        advance(1, cg)

    def pass_b(i, carry):
        zf_s[i] = z_s[0].astype(BF16)
        advance(0, n_ctx + i)
        j = n_lat - 1 - i
        zb_s[j] = z_s[1].astype(BF16)
        advance(1, n_ctx + j)
        return carry

    lax.fori_loop(0, n_lat, pass_b, 0, unroll=16)

    gain = jnp.broadcast_to(ng_ref[:, hcols], (CHUNK, HEAD_DIM)).T

    def pass_c1(i, carry):
        cg = n_ctx + i
        sl = pl.ds(pl.multiple_of(i * CHUNK, CHUNK), CHUNK)
        st = _dot(kl_ref[0, sl, hcols], qt_ref[0, hh, i])
        rowb = gf_ref[0, hh, cg].T
        for dirn in range(2):
            dmat = rowb[:, R_ROWB + dirn:R_ROWB + dirn + 1] + field(cg, dirn, F_COLA)
            w = jnp.exp(jnp.where(tri[dirn], dmat, -jnp.inf))
            p_s[i, dirn] = (st * w).astype(BF16)
        return carry

    lax.fori_loop(0, n_lat, pass_c1, 0, unroll=16)

    def pass_c2(i, carry):
        cg = n_ctx + i
        qtf = qt_ref[0, hh, i].astype(F32)
        vaug = jnp.concatenate([vtl_ref[0, hh, i], ones_tail], axis=0)
        ht = None
        for dirn, z_ref in enumerate((zf_s, zb_s)):
            rhs = jnp.concatenate([(qtf * field(cg, dirn, F_AINT)).astype(BF16), p_s[i, dirn]], axis=0)
            res = _dot(jnp.concatenate([z_ref[i], vaug], axis=1), rhs)
            den = jnp.maximum(jnp.abs(res[HEAD_DIM:HEAD_DIM + 1, :]), field(cg, dirn, F_ENEG))
            hd = res[:HEAD_DIM, :] / den
            ht = hd if ht is None else ht + hd
        ht = ht * lax.rsqrt(jnp.mean(ht * ht, axis=0, keepdims=True) + EPS)
        out_ref[0, hh, i] = (ht * gain * ogt_ref[0, hh, i].astype(F32)).astype(BF16)
        return carry

    lax.fori_loop(0, n_lat, pass_c2, 0, unroll=16)


def _mlstm(qt_l, k_l, vt_l, k_c, vt_c, gf, ogt, ng):
    bsz, t_lat, _ = k_l.shape
    t_ctx = k_c.shape[1]
    n_lat, n_ctx = t_lat // CHUNK, t_ctx // CHUNK
    n_all = n_lat + n_ctx
    hb = MLSTM_HEADS
    head_lat = pl.BlockSpec((1, t_lat, hb * HEAD_DIM), lambda b, h: (b, 0, h))
    head_ctx = pl.BlockSpec((1, t_ctx, hb * HEAD_DIM), lambda b, h: (b, 0, h))
    tiles = lambda n: pl.BlockSpec((1, hb, n, HEAD_DIM, CHUNK), lambda b, h: (b, h, 0, 0, 0))
    return pl.pallas_call(
        functools.partial(_mlstm_kernel, n_ctx=n_ctx, n_lat=n_lat),
        grid=(bsz, N_HEADS // hb),
        in_specs=[tiles(n_lat), head_lat, tiles(n_lat), head_ctx, tiles(n_ctx),
                  pl.BlockSpec((1, hb, n_all, N_FIELDS, CHUNK), lambda b, h: (b, h, 0, 0, 0)),
                  tiles(n_lat),
                  pl.BlockSpec((1, hb * HEAD_DIM), lambda b, h: (0, h))],
        out_specs=tiles(n_lat),
        out_shape=jax.ShapeDtypeStruct(qt_l.shape, BF16),
        scratch_shapes=[pltpu.VMEM((n_all, 2 * STATE_ROWS, HEAD_DIM), F32),
                        pltpu.VMEM((n_lat, STATE_ROWS, HEAD_DIM), BF16),
                        pltpu.VMEM((n_lat, STATE_ROWS, HEAD_DIM), BF16),
                        pltpu.VMEM((2, STATE_ROWS, HEAD_DIM), F32),
                        pltpu.VMEM((n_lat, 2, CHUNK, CHUNK), BF16)],
        compiler_params=pltpu.CompilerParams(dimension_semantics=("parallel", "parallel"),
                                             vmem_limit_bytes=VMEM_LIMIT),
        name="mlstm",
    )(qt_l, k_l, vt_l, k_c, vt_c, gf, ogt, ng)


def _merge_kernel(hgt_ref, gm_ref, zc_ref, x_ref, mod_ref, g2_ref, wmo_ref, wo_ref, x1_ref, h2_ref):
    per_sub = MERGE_SUB // CHUNK
    subs = [slice(i * MERGE_SUB, (i + 1) * MERGE_SUB) for i in range(hgt_ref.shape[2] // per_sub)]
    hg = [jnp.concatenate(
        [jnp.concatenate([hgt_ref[0, h, ci].astype(F32).T for h in range(N_HEADS)], axis=1)
         for ci in range(i * per_sub, (i + 1) * per_sub)], axis=0).astype(BF16) for i in range(len(subs))]
    ym = [_dot(hg_i, wmo_ref[...]) for hg_i in hg]
    y = [_dot((gm_ref[0, sl, :].astype(F32) * ym_i + zc_ref[0, sl, :].astype(F32)).astype(BF16), wo_ref[...])
         for sl, ym_i in zip(subs, ym)]
    for sl, y_i in zip(subs, y):
        x1 = x_ref[0, sl, :] + mod_ref[0, 2:3, :] * y_i
        x1_ref[0, sl, :] = x1
        h2 = _rms_scale(x1) * g2_ref[...] * (1.0 + mod_ref[0, 4:5, :]) + mod_ref[0, 3:4, :]
        h2_ref[0, sl, :] = h2.astype(BF16)


def _merge(hgt, gm, zc, x, mod, g2, wmo, wo):
    bsz, t_len, _ = x.shape
    tm = MERGE_TM
    tok = pl.BlockSpec((1, tm, D_MODEL), lambda b, t: (b, t, 0))
    tiles = pl.BlockSpec((1, N_HEADS, tm // CHUNK, HEAD_DIM, CHUNK), lambda b, t: (b, 0, t, 0, 0))
    return pl.pallas_call(
        _merge_kernel,
        grid=(bsz, t_len // tm),
        in_specs=[tiles, tok, tok, tok, pl.BlockSpec((1, 8, D_MODEL), lambda b, t: (b, 0, 0)),
                  _const_spec(g2.shape), _const_spec(wmo.shape), _const_spec(wo.shape)],
        out_specs=[tok, tok],
        out_shape=[jax.ShapeDtypeStruct(x.shape, F32), jax.ShapeDtypeStruct(x.shape, BF16)],
        compiler_params=pltpu.CompilerParams(dimension_semantics=("parallel", "parallel"),
                                             vmem_limit_bytes=VMEM_LIMIT),
        name="merge",
    )(hgt, gm, zc, x, mod, g2, wmo, wo)


def _gelu_tanh(v):
    return 0.5 * v * (1.0 + jnp.tanh(0.7978845608028654 * (v + 0.044715 * (v * v * v))))


def _stage_weight(src_hbm, dst_s, stage_s, sem, n_parts, part):
    slots = stage_s.shape[0]

    def copy(c):
        return pltpu.make_async_copy(src_hbm.at[(0,) + part(c)], stage_s.at[c % slots], sem.at[c % slots])

    for c in range(min(slots, n_parts)):
        copy(c).start()
    for c in range(n_parts):
        copy(c).wait()
        dst_s[part(c)] = stage_s[c % slots].astype(BF16)
        if c + slots < n_parts:
            copy(c + slots).start()


def _ffn_kernel(hm_ref, hp_ref, hn_ref, x1_ref, mod_ref, wu_hbm, cw_ref, cb_ref, wd_hbm, fg_ref,
                out_ref, act_s, wu_ref, wd_ref, stage_u, stage_d, sem_u, sem_d, *, tm):
    t = pl.program_id(1)
    nt = pl.num_programs(1)

    @pl.when((pl.program_id(0) == 0) & (t == 0))
    def _():
        for src, dst, stage, sem in ((wu_hbm, wu_ref, stage_u, sem_u), (wd_hbm, wd_ref, stage_d, sem_d)):
            rows = stage.shape[1]
            _stage_weight(src, dst, stage, sem, dst.shape[0] // rows,
                          lambda c, rows=rows: (pl.ds(c * rows, rows), slice(None)))

    n = tm + 2 * GRID_W
    hm = hm_ref[0]
    hp = jnp.where(t > 0, hp_ref[0], jnp.zeros_like(hp_ref[0]))
    hn = jnp.where(t < nt - 1, hn_ref[0], jnp.zeros_like(hn_ref[0]))
    he = jnp.concatenate([hp, hm, hn], axis=0)
    gcol = lax.broadcasted_iota(jnp.int32, (n, 1), 0) & (GRID_W - 1)
    has_left = gcol != 0
    has_right = gcol != GRID_W - 1
    n_chunks = FF_HIDDEN // FFN_CW
    cols = lambda j: slice(j * FFN_CW, (j + 1) * FFN_CW)
    gate_cols = lambda j: slice(FF_HIDDEN + j * FFN_CW, FF_HIDDEN + (j + 1) * FFN_CW)
    up = lambda j: (_dot(he, wu_ref[:, cols(j)]), _dot(hm, wu_ref[:, gate_cols(j)]))
    acc = None
    nxt = up(0)
    for j in range(n_chunks):
        cs = cols(j)
        a, g = nxt
        if j + 1 < n_chunks:
            nxt = up(j + 1)
        for half in range(FFN_CW // LANES):
            hs = slice(half * LANES, (half + 1) * LANES)
            ws = slice(cs.start + half * LANES, cs.start + (half + 1) * LANES)
            ah = a[:, hs]
            taps = (jnp.where(has_left, pltpu.roll(ah, 1, 0), 0.0), ah,
                    jnp.where(has_right, pltpu.roll(ah, n - 1, 0), 0.0))
            conv = cb_ref[:, ws]
            for dr in range(3):
                for dc in range(3):
                    conv = conv + cw_ref[3 * dr + dc: 3 * dr + dc + 1, ws] * taps[dc][GRID_W * dr: GRID_W * dr + tm]
            act_s[:, ws] = (_gelu_tanh(conv) * g[:, hs]).astype(BF16)
        if (j + 1) % FFN_DOWN_GROUP == 0 or j + 1 == n_chunks:
            gs = slice((j // FFN_DOWN_GROUP) * FFN_DOWN_GROUP * FFN_CW, (j + 1) * FFN_CW)
            part = _dot(act_s[:, gs], wd_ref[gs, :])
            acc = part if acc is None else acc + part
    x2 = x1_ref[0] + mod_ref[0, 5:6, :] * acc
    out_ref[0] = _rms_scale(x2) * fg_ref[...]


def _ffn(h2, x1, mod, wu, cw, cb, wd, fg):
    bsz, t_len, _ = x1.shape
    tm = FFN_TM
    rb = tm // GRID_W
    nrb = t_len // GRID_W
    tok = pl.BlockSpec((1, tm, D_MODEL), lambda b, t: (b, t, 0))
    return pl.pallas_call(
        functools.partial(_ffn_kernel, tm=tm),
        grid=(bsz, t_len // tm),
        in_specs=[tok,
                  pl.BlockSpec((1, GRID_W, D_MODEL), lambda b, t: (b, jnp.maximum(t * rb - 1, 0), 0)),
                  pl.BlockSpec((1, GRID_W, D_MODEL), lambda b, t: (b, jnp.minimum((t + 1) * rb, nrb - 1), 0)),
                  tok, pl.BlockSpec((1, 8, D_MODEL), lambda b, t: (b, 0, 0)),
                  pl.BlockSpec(memory_space=pl.ANY), _const_spec(cw.shape), _const_spec(cb.shape),
                  pl.BlockSpec(memory_space=pl.ANY), _const_spec(fg.shape)],
        out_specs=tok,
        out_shape=jax.ShapeDtypeStruct(x1.shape, F32),
        scratch_shapes=[pltpu.VMEM((tm, FF_HIDDEN), BF16),
                        pltpu.VMEM(wu.shape[1:], BF16),
                        pltpu.VMEM(wd.shape[1:], BF16),
                        pltpu.VMEM((STAGE_SLOTS, FFN_STAGE_ROWS_UP, wu.shape[2]), F32),
                        pltpu.VMEM((STAGE_SLOTS, FFN_STAGE_ROWS_DOWN, wd.shape[2]), F32),
                        pltpu.SemaphoreType.DMA((STAGE_SLOTS,)), pltpu.SemaphoreType.DMA((STAGE_SLOTS,))],
        compiler_params=pltpu.CompilerParams(dimension_semantics=("arbitrary", "arbitrary"),
                                             vmem_limit_bytes=VMEM_LIMIT),
        name="ffn",
    )(h2, h2, h2, x1, mod, wu, cw, cb, wd, fg)


def kernel(x, c, ctx, c_ctx, ada_w, ada_b, norm1_g, norm2_g, w_in, qk_conv_w, qk_conv_b, gate_b, mnorm_g,
           w_m_out, sc_conv_w, sc_conv_b, w_c_out, w_o, w_up, ff_conv_w, ff_conv_b, w_down, final_g):
    assert ada_w.shape[0] == 1, "single-layer block"
    bsz, t_lat, _ = x.shape
    t_ctx = ctx.shape[1]

    cc = jnp.zeros((2 * 8, D_MODEL), F32).at[:bsz].set(c).at[bsz].set(c_ctx)
    mod = _ada(cc, ada_w[0], ada_b[0][None, :])
    mod_x = jnp.pad(mod[:bsz].reshape(bsz, 6, D_MODEL), ((0, 0), (0, 2), (0, 0)))
    mod_c = jnp.pad(mod[bsz].reshape(1, 6, D_MODEL), ((0, 0), (0, 2), (0, 0)))

    cast = lambda a: a.astype(BF16)
    w_t = jnp.swapaxes(w_in, 1, 2)
    gb = gate_b[0][:, None]
    g1 = norm1_g[0][None, :]
    ctx_w = [w_t, gb, qk_conv_w[0], qk_conv_b[0][None, :]]
    lat_w = ctx_w + [sc_conv_w[0], sc_conv_b[0][None, :], cast(w_c_out[0])]

    k_l, vt_l, g_l, qt_l, ogt, gm, zc = _proj(x, mod_x, g1, lat_w, full=True, per_batch_mod=True)
    k_c, vt_c, g_c = _proj(ctx, mod_c, g1, ctx_w, full=False, per_batch_mod=False)

    gf = _gates(g_l, g_c)
    hgt = _mlstm(qt_l, k_l, vt_l, k_c, vt_c, gf, ogt, mnorm_g[0][None, :])

    x1, h2 = _merge(hgt, gm, zc, x, mod_x, norm2_g[0][None, :], cast(w_m_out[0]), cast(w_o[0]))

    out = _ffn(h2, x1, mod_x, w_up,
               ff_conv_w[0].reshape(9, FF_HIDDEN), ff_conv_b[0][None, :], w_down,
               final_g[None, :])
    return out
```
